```python
import math
import jax
import jax.numpy as jnp
from jax import lax
import numpy as np

D_MODEL = 1024
BATCH = 8
SEQ = 4096
DEPTH = 1

D_MIX = D_MODEL
HGRN_WIDTH = D_MIX // 2
RET_WIDTH = D_MIX - HGRN_WIDTH
HGRN_HEADS = 4
HGRN_DK = HGRN_WIDTH // HGRN_HEADS
HGRN_DV = HGRN_WIDTH // HGRN_HEADS
RET_HEADS = 4
RET_DK = RET_WIDTH // RET_HEADS
RET_DV = RET_WIDTH // RET_HEADS
CHUNK = 64
ROPE_BASE = 10000.0
PROJ_SIZES = (HGRN_HEADS * HGRN_DK, HGRN_HEADS * HGRN_DK, HGRN_HEADS * HGRN_DV, HGRN_HEADS * HGRN_DV,
              RET_HEADS * RET_DK, RET_HEADS * RET_DK, RET_HEADS * RET_DV, RET_HEADS * RET_DV)
D_PROJ = sum(PROJ_SIZES)
N_GROUPS = 4
EXPERTS_PER_GROUP = 8
N_EXPERTS = N_GROUPS * EXPERTS_PER_GROUP
TOP_K = 2
D_EXPERT = D_MODEL // 2
MOE_BLOCK = 128
DEEPNORM_ALPHA = (2.0 * DEPTH) ** 0.25
DEEPNORM_BETA = (8.0 * DEPTH) ** -0.25
LN_EPS = 1e-5

kernel_name = "hymba_style_hgrn2_retention_hmoe_deepnorm_adaln"


def _layer_norm(x, w=None, b=None):
    xf = x.astype(jnp.float32)
    mu = jnp.mean(xf, axis=-1, keepdims=True)
    var = jnp.mean(jnp.square(xf - mu), axis=-1, keepdims=True)
    y = (xf - mu) * lax.rsqrt(var + LN_EPS)
    if w is not None:
        y = y * w.astype(jnp.float32) + b.astype(jnp.float32)
    return y.astype(x.dtype)


def _head_rms_norm(o, w):
    H, d = o.shape[2], o.shape[3]
    o = o * lax.rsqrt(jnp.mean(jnp.square(o), axis=-1, keepdims=True) + LN_EPS)
    return o * w.astype(jnp.float32).reshape(H, d)


def _head_group_norm(o, w):
    H, d = o.shape[2], o.shape[3]
    mu = jnp.mean(o, axis=-1, keepdims=True)
    var = jnp.mean(jnp.square(o - mu), axis=-1, keepdims=True)
    return (o - mu) * lax.rsqrt(var + LN_EPS) * w.astype(jnp.float32).reshape(H, d)


def _to_chunks(t):
    B, S, H, d = t.shape
    return t.reshape(B, S // CHUNK, CHUNK, H, d).transpose(0, 3, 1, 2, 4)


def _from_chunks(t):
    B, H, N, C, d = t.shape
    return t.transpose(0, 2, 3, 1, 4).reshape(B, N * C, H, d)


def _chunk_state_scan(decay, kv):
    B, H, N, dk, dv = kv.shape

    def step(state, inp):
        d_n, kv_n = inp
        return d_n[..., None] * state + kv_n, state

    s0 = jnp.zeros((B, H, dk, dv), kv.dtype)
    _, s_before = lax.scan(step, s0, (jnp.moveaxis(decay, 2, 0), jnp.moveaxis(kv, 2, 0)))
    return jnp.moveaxis(s_before, 0, 2)


def _hgrn2_chunkwise(q, log_f, k, v):
    C = q.shape[3]
    causal = jnp.tril(jnp.ones((C, C), dtype=bool))
    b = jnp.cumsum(log_f, axis=3)
    q_in = q * jnp.exp(b)
    k_in = k * jnp.exp(-b)
    a = jnp.where(causal, jnp.einsum('bhntd,bhnsd->bhnts', q_in, k_in), 0.0)
    o_intra = jnp.einsum('bhnts,bhnsv->bhntv', a, v)
    b_last = b[:, :, :, -1:, :]
    kv = jnp.einsum('bhnsd,bhnsv->bhndv', k * jnp.exp(b_last - b), v)
    s_before = _chunk_state_scan(jnp.exp(b_last[:, :, :, 0, :]), kv)
    return o_intra + jnp.einsum('bhntd,bhndv->bhntv', q_in, s_before)


def _retention_chunkwise(q, k, v, log_gamma):
    C = q.shape[3]
    idx = jnp.arange(C, dtype=jnp.float32)
    rel = idx[:, None] - idx[None, :]
    causal = rel >= 0
    dmat = jnp.where(causal, jnp.exp(jnp.where(causal, rel, 0.0)[None] * log_gamma[:, None, None]), 0.0)
    scores = jnp.einsum('bhntd,bhnsd->bhnts', q, k) * dmat[None, :, None]
    o_intra = jnp.einsum('bhnts,bhnsv->bhntv', scores, v)
    k_decay = jnp.exp((C - 1.0 - idx)[None, :] * log_gamma[:, None])
    kv = jnp.einsum('bhnsd,bhnsv->bhndv', k * k_decay[None, :, None, :, None], v)
    B, H, N = q.shape[0], q.shape[1], q.shape[2]
    chunk_decay = jnp.broadcast_to(jnp.exp(C * log_gamma)[None, :, None, None], (B, H, N, 1))
    s_before = _chunk_state_scan(chunk_decay, kv)
    q_decay = jnp.exp((idx + 1.0)[None, :] * log_gamma[:, None])
    o_inter = jnp.einsum('bhntd,bhndv->bhntv', q, s_before) * q_decay[None, :, None, :, None]
    return o_intra + o_inter


def _rotary(t, cos, sin):
    half = t.shape[-1] // 2
    t1, t2 = t[..., :half], t[..., half:]
    return jnp.concatenate([t1 * cos - t2 * sin, t1 * sin + t2 * cos], axis=-1)


def _hybrid_mixer(h, positions, w_in, w_out, lb, hgrn_norm_w, ret_norm_w):
    B, S, _ = h.shape
    proj = jnp.einsum('bsd,de->bse', h, w_in).astype(jnp.float32)
    offs = np.cumsum(PROJ_SIZES)[:-1].tolist()
    hq, hf, hi, hg, rq, rk, rv, rg = jnp.split(proj, offs, axis=-1)

    lbh = lb.astype(jnp.float32).reshape(HGRN_HEADS, HGRN_DK)
    z = hf.reshape(B, S, HGRN_HEADS, HGRN_DK)
    log_f = jnp.log(lbh + (1.0 - lbh) * jax.nn.sigmoid(z))
    k_a = (1.0 - lbh) * jax.nn.sigmoid(-z)
    q_a = jax.nn.silu(hq.reshape(B, S, HGRN_HEADS, HGRN_DK))
    v_a = hi.reshape(B, S, HGRN_HEADS, HGRN_DV)
    o_a = _from_chunks(_hgrn2_chunkwise(_to_chunks(q_a), _to_chunks(log_f), _to_chunks(k_a), _to_chunks(v_a)))
    o_a = _head_rms_norm(o_a, hgrn_norm_w) * jax.nn.silu(hg.reshape(B, S, HGRN_HEADS, HGRN_DV))

    inv_freq = jnp.power(ROPE_BASE, -jnp.arange(0, RET_DK, 2, dtype=jnp.float32) / RET_DK)
    ang = positions.astype(jnp.float32)[..., None] * inv_freq
    cos = jnp.cos(ang)[:, :, None, :]
    sin = jnp.sin(ang)[:, :, None, :]
    q_b = _rotary(rq.reshape(B, S, RET_HEADS, RET_DK), cos, sin) * (RET_DK ** -0.5)
    k_b = _rotary(rk.reshape(B, S, RET_HEADS, RET_DK), cos, sin)
    v_b = rv.reshape(B, S, RET_HEADS, RET_DV)
    log_gamma = jnp.log(1.0 - jnp.exp2(-5.0 - jnp.arange(RET_HEADS, dtype=jnp.float32)))
    o_b = _from_chunks(_retention_chunkwise(_to_chunks(q_b), _to_chunks(k_b), _to_chunks(v_b), log_gamma))
    o_b = _head_group_norm(o_b, ret_norm_w) * jax.nn.silu(rg.reshape(B, S, RET_HEADS, RET_DV))

    o = jnp.concatenate([o_a.reshape(B, S, -1), o_b.reshape(B, S, -1)], axis=-1).astype(h.dtype)
    return jnp.einsum('bse,ed->bsd', o, w_out)


def _grouped_experts(x, expert_idx, gates, w_gate, w_up, w_down):
    N, D = x.shape
    K = expert_idx.shape[1]
    E = w_gate.shape[0]
    flat_e = expert_idx.reshape(-1)
    flat_tok = jnp.repeat(jnp.arange(N, dtype=jnp.int32), K)
    flat_g = gates.reshape(-1)
    order = jnp.argsort(flat_e)
    se, stok, sg = flat_e[order], flat_tok[order], flat_g[order]
    counts = jnp.bincount(flat_e, length=E).astype(jnp.int32)
    starts = jnp.cumsum(counts) - counts
    padded = ((counts + MOE_BLOCK - 1) // MOE_BLOCK) * MOE_BLOCK
    pends = jnp.cumsum(padded)
    pstarts = pends - padded
    dest = pstarts[se] + (jnp.arange(N * K, dtype=jnp.int32) - starts[se])
    n_slots = ((N * K + MOE_BLOCK - 1) // MOE_BLOCK) * MOE_BLOCK + E * MOE_BLOCK
    slot_tok = jnp.full((n_slots,), N, dtype=jnp.int32).at[dest].set(stok)
    slot_g = jnp.zeros((n_slots,), x.dtype).at[dest].set(sg)
    n_blocks = n_slots // MOE_BLOCK
    block_start = jnp.arange(n_blocks, dtype=jnp.int32) * MOE_BLOCK
    block_e = jnp.minimum(jnp.searchsorted(pends, block_start, side='right'), E - 1).astype(jnp.int32)
    x_pad = jnp.concatenate([x, jnp.zeros((1, D), x.dtype)], axis=0)
    xb = x_pad[slot_tok].reshape(n_blocks, MOE_BLOCK, D)

    def expert_block(args):
        xblk, e = args
        a = xblk @ w_gate[e]
        u = xblk @ w_up[e]
        return (jax.nn.silu(a) * u) @ w_down[e]

    yb = lax.map(expert_block, (xb, block_e)).reshape(n_slots, D)
    out = jnp.zeros((N + 1, D), x.dtype).at[slot_tok].add(yb * slot_g[:, None])
    return out[:N]


def _hierarchical_moe(h, w_rg, b_rg, w_re, b_re, w_gate, w_up, w_down):
    B, S, D = h.shape
    x = h.reshape(B * S, D)
    g_logits = (x @ w_rg).astype(jnp.float32) + b_rg.astype(jnp.float32)
    g_prob = jax.nn.softmax(g_logits, axis=-1)
    g_star = jnp.argmax(g_logits, axis=-1).astype(jnp.int32)
    p_star = jnp.take_along_axis(g_prob, g_star[:, None], axis=1)
    e_logits = ((x @ w_re).astype(jnp.float32) + b_re.astype(jnp.float32)).reshape(-1, N_GROUPS, EXPERTS_PER_GROUP)
    e_sel = jnp.take_along_axis(e_logits, g_star[:, None, None], axis=1)[:, 0]
    top_v, top_i = lax.top_k(e_sel, TOP_K)
    gates = (p_star * jax.nn.softmax(top_v, axis=-1)).astype(h.dtype)
    expert_idx = (g_star[:, None] * EXPERTS_PER_GROUP + top_i).astype(jnp.int32)
    y = _grouped_experts(x, expert_idx, gates, w_gate, w_up, w_down)
    return y.reshape(B, S, D)


def setup_inputs(seed: int = 0) -> dict:
    key = jax.random.key(seed)
    ks = jax.random.split(key, 24)
    f32 = jnp.float32

    def nrm(k, shape, s):
        return s * jax.random.normal(k, shape, f32)

    beta = DEEPNORM_BETA
    col_scale = jnp.concatenate([
        jnp.full((PROJ_SIZES[0],), 1.0, f32), jnp.full((PROJ_SIZES[1],), 1.0, f32),
        jnp.full((PROJ_SIZES[2],), beta, f32), jnp.full((PROJ_SIZES[3],), 1.0, f32),
        jnp.full((PROJ_SIZES[4],), 1.0, f32), jnp.full((PROJ_SIZES[5],), 1.0, f32),
        jnp.full((PROJ_SIZES[6],), beta, f32), jnp.full((PROJ_SIZES[7],), 1.0, f32)])
    return {
        "x": jax.random.normal(ks[0], (BATCH, SEQ, D_MODEL), f32),
        "c": jax.random.normal(ks[1], (BATCH, D_MODEL), f32),
        "positions": jnp.tile(jnp.arange(SEQ, dtype=jnp.int32)[None, :], (BATCH, 1)),
        "w_ada": nrm(ks[2], (DEPTH, D_MODEL, 6 * D_MODEL), 0.5 * D_MODEL ** -0.5),
        "b_ada": nrm(ks[3], (DEPTH, 6 * D_MODEL), 0.01),
        "w_in": nrm(ks[4], (DEPTH, D_MODEL, D_PROJ), D_MODEL ** -0.5) * col_scale,
        "w_out": nrm(ks[5], (DEPTH, D_MIX, D_MODEL), D_MIX ** -0.5) * beta,
        "hgrn_lb": nrm(ks[6], (DEPTH + 1, HGRN_HEADS * HGRN_DK), 0.1),
        "hgrn_norm_w": 1.0 + nrm(ks[7], (DEPTH, HGRN_HEADS * HGRN_DV), 0.02),
        "ret_norm_w": 1.0 + nrm(ks[8], (DEPTH, RET_HEADS * RET_DV), 0.02),
        "post_ln1_w": 1.0 + nrm(ks[9], (DEPTH, D_MODEL), 0.02),
        "post_ln1_b": nrm(ks[10], (DEPTH, D_MODEL), 0.02),
        "w_rg": nrm(ks[11], (DEPTH, D_MODEL, N_GROUPS), D_MODEL ** -0.5),
        "b_rg": nrm(ks[12], (DEPTH, N_GROUPS), 0.01),
        "w_re": nrm(ks[13], (DEPTH, D_MODEL, N_EXPERTS), D_MODEL ** -0.5),
        "b_re": nrm(ks[14], (DEPTH, N_EXPERTS), 0.01),
        "w_gate": nrm(ks[15], (DEPTH, N_EXPERTS, D_MODEL, D_EXPERT), D_MODEL ** -0.5),
        "w_up": nrm(ks[16], (DEPTH, N_EXPERTS, D_MODEL, D_EXPERT), D_MODEL ** -0.5) * beta,
        "w_down": nrm(ks[17], (DEPTH, N_EXPERTS, D_EXPERT, D_MODEL), D_EXPERT ** -0.5) * beta,
        "post_ln2_w": 1.0 + nrm(ks[18], (DEPTH, D_MODEL), 0.02),
        "post_ln2_b": nrm(ks[19], (DEPTH, D_MODEL), 0.02),
    }


def reference(x, c, positions, w_ada, b_ada, w_in, w_out, hgrn_lb, hgrn_norm_w, ret_norm_w,
              post_ln1_w, post_ln1_b, w_rg, b_rg, w_re, b_re, w_gate, w_up, w_down,
              post_ln2_w, post_ln2_b):
    lb_all = jnp.cumsum(jax.nn.softmax(hgrn_lb.astype(jnp.float32), axis=0), axis=0)
    c_act = jax.nn.silu(c)
    for l in range(DEPTH):
        mod = c_act @ w_ada[l] + b_ada[l]
        shift1, scale1, gate1, shift2, scale2, gate2 = jnp.split(mod, 6, axis=-1)
        h = _layer_norm(x) * (1.0 + scale1[:, None, :]) + shift1[:, None, :]
        y = _hybrid_mixer(h, positions, w_in[l], w_out[l], lb_all[l], hgrn_norm_w[l], ret_norm_w[l])
        x = _layer_norm(DEEPNORM_ALPHA * x + gate1[:, None, :] * y, post_ln1_w[l], post_ln1_b[l])
        h = _layer_norm(x) * (1.0 + scale2[:, None, :]) + shift2[:, None, :]
        y = _hierarchical_moe(h, w_rg[l], b_rg[l], w_re[l], b_re[l], w_gate[l], w_up[l], w_down[l])
        x = _layer_norm(DEEPNORM_ALPHA * x + gate2[:, None, :] * y, post_ln2_w[l], post_ln2_b[l])
    return x
```

```python
import functools
import math

import numpy as np
import jax
import jax.numpy as jnp
from jax import lax
from jax.experimental import pallas as pl
from jax.experimental.pallas import tpu as pltpu

F32 = jnp.float32
BF16 = jnp.bfloat16

D_MODEL = 1024
N_HEADS = 4
D_HEAD = 128
HGRN_CHUNK = 64
RET_CHUNK = 256
ROPE_BASE = 10000.0
N_EXPERTS = 32
EXPERTS_PER_GROUP = 8
N_GROUPS = 4
D_EXPERT = 512
DEEPNORM_ALPHA = 2.0 ** 0.25
LN_EPS = 1e-5

SEQ_TILE = 512
ROW_BLOCK = 128
SLOT_BLOCK = 256
TOK_TILE = 256
LANES = 128
SUBLANES = 8
VMEM_LIMIT = 56 * 1024 * 1024


def _mm(a, b):
    return jnp.dot(a, b, preferred_element_type=F32)


def _mm_nt(a, b):
    return lax.dot_general(a, b, (((1,), (1,)), ((), ())), preferred_element_type=F32)


def _mm_tn(a, b):
    return lax.dot_general(a, b, (((0,), (0,)), ((), ())), preferred_element_type=F32)


def _sigmoid(x):
    return 1.0 / (1.0 + jnp.exp(-x))


def _silu(x):
    return x * _sigmoid(x)


def _ln(x):
    mu = jnp.mean(x, axis=-1, keepdims=True)
    xc = x - mu
    var = jnp.mean(xc * xc, axis=-1, keepdims=True)
    return xc * lax.rsqrt(var + LN_EPS)


def _const_spec(shape):
    nd = len(shape)
    return pl.BlockSpec(shape, lambda *_: (0,) * nd, pipeline_mode=pl.Buffered(1))


def _ada_kernel(c_ref, w_ref, b_ref, o_ref):
    ca = _silu(c_ref[...])
    o_ref[...] = _mm(ca.astype(BF16), w_ref[...].astype(BF16)) + b_ref[...]


def _ada_call(c, w_ada, b_ada):
    B, D = c.shape
    n_out = w_ada.shape[1]
    return pl.pallas_call(
        _ada_kernel,
        grid=(n_out // D,),
        in_specs=[pl.BlockSpec((B, D), lambda j: (0, 0)),
                  pl.BlockSpec((D, D), lambda j: (0, j)),
                  pl.BlockSpec((1, D), lambda j: (0, j))],
        out_specs=pl.BlockSpec((B, D), lambda j: (0, j)),
        out_shape=jax.ShapeDtypeStruct((B, n_out), F32),
        name="ada_mod",
    )(c, w_ada, b_ada.reshape(1, n_out))


def _mixer_kernel(x_ref, pos_ref, mod_ref, win_ref, wout_ref, lb_ref, hnw_ref, rnw_ref,
                  ln1w_ref, ln1b_ref, invf_ref, dmat_ref, qdec_ref, kdec_ref, wr_ref, br_ref,
                  x1_ref, h2t_ref, lg_ref,
                  h_scr, proj_scr, o_scr, cos_scr, sin_scr, sh_scr, sr_scr, *, ret_chunk_decay):
    T = SEQ_TILE
    D = D_MODEL
    dh = D_HEAD
    hw = N_HEADS * dh

    @pl.when(pl.program_id(1) == 0)
    def _():
        sh_scr[...] = jnp.zeros_like(sh_scr)
        sr_scr[...] = jnp.zeros_like(sr_scr)

    shift1 = mod_ref[0, 0:1, :]
    scale1 = mod_ref[0, 1:2, :]
    gate1 = mod_ref[0, 2:3, :]
    shift2 = mod_ref[0, 3:4, :]
    scale2 = mod_ref[0, 4:5, :]

    for r in range(T // ROW_BLOCK):
        rows = slice(r * ROW_BLOCK, (r + 1) * ROW_BLOCK)
        h = _ln(x_ref[0, rows, :]) * (1.0 + scale1) + shift1
        h_scr[rows, :] = h.astype(BF16)

    pos = pos_ref[0].astype(F32)
    ang_t = invf_ref[...] * pos
    half = lax.broadcasted_iota(jnp.int32, ang_t.shape, 0) < dh // 2
    cos_scr[...] = jnp.cos(ang_t).T
    sin_scr[...] = jnp.where(half, -jnp.sin(ang_t), jnp.sin(ang_t)).T

    proj_scr[...] = _mm(h_scr[...], win_ref[0])

    l0 = lb_ref[0:1, :]
    l1 = lb_ref[1:2, :]
    lmax = jnp.maximum(l0, l1)
    e0 = jnp.exp(l0 - lmax)
    e1 = jnp.exp(l1 - lmax)
    lb = e0 / (e0 + e1)
    hnw = hnw_ref[...]

    C = HGRN_CHUNK
    rr = lax.broadcasted_iota(jnp.int32, (C, C), 0)
    cc = lax.broadcasted_iota(jnp.int32, (C, C), 1)
    causal = rr >= cc
    tri = causal.astype(BF16)

    def hgrn_chunk(c, carry):
        rows = pl.ds(pl.multiple_of(c * C, C), C)
        logfs, ks = [], []
        for hd in range(N_HEADS):
            z = proj_scr[rows, hd * 4 * dh + dh: hd * 4 * dh + 2 * dh]
            e = jnp.exp(-jnp.abs(z))
            r = 1.0 / (1.0 + e)
            er = e * r
            zp = z >= 0
            sig_pos = jnp.where(zp, r, er)
            sig_neg = jnp.where(zp, er, r)
            lbh = lb[:, hd * dh:(hd + 1) * dh]
            logfs.append(jnp.log(lbh + (1.0 - lbh) * sig_pos))
            ks.append((1.0 - lbh) * sig_neg)
        lf = jnp.concatenate(logfs, axis=1)
        hi = lf.astype(BF16)
        r1 = lf - hi.astype(F32)
        mid = r1.astype(BF16)
        lo = (r1 - mid.astype(F32)).astype(BF16)
        bsum = _mm(tri, hi) + _mm(tri, mid) + _mm(tri, lo)
        for hd in range(N_HEADS):
            base = hd * 4 * dh
            b = bsum[:, hd * dh:(hd + 1) * dh]
            b_last = b[C - 1:C, :]
            q = proj_scr[rows, base: base + dh]
            v = proj_scr[rows, base + 2 * dh: base + 3 * dh].astype(BF16)
            g = proj_scr[rows, base + 3 * dh: base + 4 * dh]
            k = ks[hd]
            q_in = (_silu(q) * jnp.exp(b)).astype(BF16)
            k_in = (k * jnp.exp(-b)).astype(BF16)
            k_dec = (k * jnp.exp(b_last - b)).astype(BF16)
            st = sh_scr[hd]
            a = jnp.where(causal, _mm_nt(q_in, k_in), 0.0)
            o = _mm(a.astype(BF16), v) + _mm_nt(q_in, st.astype(BF16))
            sh_scr[hd] = st * jnp.exp(b_last) + _mm_tn(v, k_dec)
            ms = jnp.mean(o * o, axis=-1, keepdims=True)
            on = o * lax.rsqrt(ms + LN_EPS) * hnw[:, hd * dh:(hd + 1) * dh] * _silu(g)
            o_scr[rows, hd * dh:(hd + 1) * dh] = on.astype(BF16)
        return carry

    lax.fori_loop(0, T // C, hgrn_chunk, 0)

    proj_scr[...] = _mm(h_scr[...], win_ref[1])
    rnw = rnw_ref[...]
    RC = RET_CHUNK
    q_scale = dh ** -0.5
    for sub in range(T // RC):
        rows = slice(sub * RC, (sub + 1) * RC)
        cs = cos_scr[rows, :]
        sn = sin_scr[rows, :]
        for hd in range(N_HEADS):
            base = hd * 4 * dh
            rq = proj_scr[rows, base: base + dh]
            rk = proj_scr[rows, base + dh: base + 2 * dh]
            v = proj_scr[rows, base + 2 * dh: base + 3 * dh].astype(BF16)
            g = proj_scr[rows, base + 3 * dh: base + 4 * dh]
            q = (rq * cs + pltpu.roll(rq, dh // 2, 1) * sn) * q_scale
            k = rk * cs + pltpu.roll(rk, dh // 2, 1) * sn
            qb = q.astype(BF16)
            st = sr_scr[hd]
            s = _mm_nt(qb, k.astype(BF16)) * dmat_ref[hd]
            o = _mm(s.astype(BF16), v) + _mm_nt(qb, st.astype(BF16)) * qdec_ref[hd]
            sr_scr[hd] = st * ret_chunk_decay[hd] + _mm_tn(v, (k * kdec_ref[hd]).astype(BF16))
            mu = jnp.mean(o, axis=-1, keepdims=True)
            oc = o - mu
            var = jnp.mean(oc * oc, axis=-1, keepdims=True)
            on = oc * lax.rsqrt(var + LN_EPS) * rnw[:, hd * dh:(hd + 1) * dh] * _silu(g)
            o_scr[rows, hw + hd * dh: hw + (hd + 1) * dh] = on.astype(BF16)

    proj_scr[:, 0:D] = _mm(o_scr[...], wout_ref[...])
    for r in range(T // ROW_BLOCK):
        rows = slice(r * ROW_BLOCK, (r + 1) * ROW_BLOCK)
        u = DEEPNORM_ALPHA * x_ref[0, rows, :] + gate1 * proj_scr[rows, 0:D]
        x1 = _ln(u) * ln1w_ref[...] + ln1b_ref[...]
        x1_ref[0, rows, :] = x1
        h2 = (_ln(x1) * (1.0 + scale2) + shift2).astype(BF16)
        h_scr[rows, :] = h2
        h2f = h2.astype(F32)
        for s in range(SUBLANES):
            h2t_ref[pl.ds(r * ROW_BLOCK * SUBLANES + s, ROW_BLOCK, stride=SUBLANES), :] = (
                h2f[:, s * LANES:(s + 1) * LANES])
    lg_ref[...] = _mm(h_scr[...], wr_ref[...]) + br_ref[...]


def _retention_tables():
    h = np.arange(N_HEADS, dtype=np.float64)
    log_gamma = np.log(1.0 - np.exp2(-5.0 - h))
    idx = np.arange(RET_CHUNK, dtype=np.float64)
    rel = idx[:, None] - idx[None, :]
    dmat = np.where(rel >= 0, np.exp(np.maximum(rel, 0.0)[None] * log_gamma[:, None, None]), 0.0)
    qdec = np.exp((idx + 1.0)[None, :] * log_gamma[:, None])
    kdec = np.exp((RET_CHUNK - 1.0 - idx)[None, :] * log_gamma[:, None])
    cdec = np.exp(RET_CHUNK * log_gamma)
    bc = lambda t: np.broadcast_to(t[:, :, None], (N_HEADS, RET_CHUNK, D_HEAD))
    return (jnp.asarray(dmat, F32), jnp.asarray(bc(qdec), F32), jnp.asarray(bc(kdec), F32),
            tuple(float(np.float32(v)) for v in cdec))


def _mixer_call(x, positions, mod3, win, wout, hgrn_lb, hnw, rnw, ln1w, ln1b, wr, br):
    B, S, D = x.shape
    T = SEQ_TILE
    nj = S // T
    dmat, qdec, kdec, cdec = _retention_tables()
    inv_freq = np.power(ROPE_BASE, -np.arange(0, D_HEAD, 2, dtype=np.float64) / D_HEAD)
    invf = jnp.asarray(np.concatenate([inv_freq, inv_freq])[:, None], F32)
    pos3 = positions.reshape(B, 1, S)
    kern = functools.partial(_mixer_kernel, ret_chunk_decay=cdec)
    return pl.pallas_call(
        kern,
        grid=(B, nj),
        in_specs=[
            pl.BlockSpec((1, T, D), lambda b, j: (b, j, 0)),
            pl.BlockSpec((1, 1, T), lambda b, j: (b, 0, j)),
            pl.BlockSpec((1, 6, D), lambda b, j: (b, 0, 0)),
            _const_spec(win.shape), _const_spec(wout.shape), _const_spec(hgrn_lb.shape),
            _const_spec(hnw.shape), _const_spec(rnw.shape), _const_spec(ln1w.shape),
            _const_spec(ln1b.shape), _const_spec(invf.shape), _const_spec(dmat.shape),
            _const_spec(qdec.shape), _const_spec(kdec.shape), _const_spec(wr.shape),
            _const_spec(br.shape),
        ],
        out_specs=[
            pl.BlockSpec((1, T, D), lambda b, j: (b, j, 0)),
            pl.BlockSpec((T * SUBLANES, LANES), lambda b, j: (b * nj + j, 0)),
            pl.BlockSpec((T, LANES), lambda b, j: (b * nj + j, 0)),
        ],
        out_shape=[
            jax.ShapeDtypeStruct((B, S, D), F32),
            jax.ShapeDtypeStruct((B * S * SUBLANES, LANES), F32),
            jax.ShapeDtypeStruct((B * S, LANES), F32),
        ],
        scratch_shapes=[
            pltpu.VMEM((T, D), BF16),
            pltpu.VMEM((T, 4 * N_HEADS * D_HEAD), F32),
            pltpu.VMEM((T, D), BF16),
            pltpu.VMEM((T, D_HEAD), F32),
            pltpu.VMEM((T, D_HEAD), F32),
            pltpu.VMEM((N_HEADS, D_HEAD, D_HEAD), F32),
            pltpu.VMEM((N_HEADS, D_HEAD, D_HEAD), F32),
        ],
        compiler_params=pltpu.CompilerParams(
            dimension_semantics=("arbitrary", "arbitrary"), vmem_limit_bytes=VMEM_LIMIT),
        name="mixer",
    )(x, pos3, mod3, win, wout, hgrn_lb, hnw, rnw, ln1w, ln1b, invf, dmat, qdec, kdec, wr, br)


def _row_copy(src_ref, dst_ref, src_row, dst_row, sem):
    s0 = pl.multiple_of(src_row * SUBLANES, SUBLANES)
    d0 = pl.multiple_of(dst_row * SUBLANES, SUBLANES)
    return pltpu.make_async_copy(src_ref.at[pl.ds(s0, SUBLANES)], dst_ref.at[pl.ds(d0, SUBLANES)], sem)


def _from_token_tiles(buf_ref, first_row, n_rows):
    return jnp.concatenate(
        [buf_ref[pl.ds(first_row * SUBLANES + s, n_rows, stride=SUBLANES), :] for s in range(SUBLANES)],
        axis=1)


def _expert_kernel(be_ref, tok_ref, nv_ref, h2t_ref, wg_ref, wu_ref, wd_ref, g_ref, y_ref, buf, sem):
    i = pl.program_id(0)
    R = SLOT_BLOCK

    @pl.when(i < nv_ref[0])
    def _():
        def issue(r, carry):
            _row_copy(h2t_ref, buf, tok_ref[i * R + r], r, sem).start()
            return carry
        lax.fori_loop(0, R, issue, 0)
        pltpu.make_async_copy(h2t_ref.at[pl.ds(0, R * SUBLANES)], buf, sem).wait()
        xb = _from_token_tiles(buf, 0, R).astype(BF16)
        a = _mm(xb, wg_ref[0])
        u = _mm(xb, wu_ref[0])
        y = _mm((_silu(a) * u).astype(BF16), wd_ref[0]) * g_ref[...]
        for s in range(SUBLANES):
            y_ref[pl.ds(s, R, stride=SUBLANES), :] = y[:, s * LANES:(s + 1) * LANES]

    @pl.when(i >= nv_ref[0])
    def _():
        y_ref[...] = jnp.zeros_like(y_ref)


def _expert_call(block_e, slot_tok, n_valid, h2t, wg, wu, wd, slot_g):
    n_slots = slot_tok.shape[0]
    R = SLOT_BLOCK
    n_blocks = n_slots // R
    D, F = wg.shape[1], wg.shape[2]
    grid_spec = pltpu.PrefetchScalarGridSpec(
        num_scalar_prefetch=3,
        grid=(n_blocks,),
        in_specs=[
            pl.BlockSpec(memory_space=pl.ANY),
            pl.BlockSpec((1, D, F), lambda i, be, tok, nv: (be[i], 0, 0)),
            pl.BlockSpec((1, D, F), lambda i, be, tok, nv: (be[i], 0, 0)),
            pl.BlockSpec((1, F, D), lambda i, be, tok, nv: (be[i], 0, 0)),
            pl.BlockSpec((R, 1), lambda i, be, tok, nv: (i, 0)),
        ],
        out_specs=pl.BlockSpec((R * SUBLANES, LANES), lambda i, be, tok, nv: (i, 0)),
        scratch_shapes=[pltpu.VMEM((R * SUBLANES, LANES), F32), pltpu.SemaphoreType.DMA],
    )
    return pl.pallas_call(
        _expert_kernel,
        grid_spec=grid_spec,
        out_shape=jax.ShapeDtypeStruct((n_slots * SUBLANES, LANES), F32),
        compiler_params=pltpu.CompilerParams(
            dimension_semantics=("arbitrary",), vmem_limit_bytes=VMEM_LIMIT),
        name="experts",
    )(block_e, slot_tok, n_valid, h2t, wg, wu, wd, slot_g)


def _final_kernel(dest_ref, yt_ref, x1_ref, mod_ref, w_ref, b_ref, o_ref, buf, sem, *, n_tok):
    i = pl.program_id(0)
    TT = TOK_TILE

    def issue(r, carry):
        for k in range(2):
            _row_copy(yt_ref, buf, dest_ref[k * n_tok + i * TT + r], k * TT + r, sem).start()
        return carry
    lax.fori_loop(0, TT, issue, 0)
    pltpu.make_async_copy(yt_ref.at[pl.ds(0, 2 * TT * SUBLANES)], buf, sem).wait()
    y = _from_token_tiles(buf, 0, TT) + _from_token_tiles(buf, TT, TT)
    gate2 = mod_ref[0, 5:6, :]
    u = DEEPNORM_ALPHA * x1_ref[...] + gate2 * y
    o_ref[...] = _ln(u) * w_ref[...] + b_ref[...]


def _final_call(dest, yt, x1, mod3, w, b, seq_len):
    N, D = x1.shape
    TT = TOK_TILE
    per_seq = seq_len // TT
    grid_spec = pltpu.PrefetchScalarGridSpec(
        num_scalar_prefetch=1,
        grid=(N // TT,),
        in_specs=[
            pl.BlockSpec(memory_space=pl.ANY),
            pl.BlockSpec((TT, D), lambda i, d: (i, 0)),
            pl.BlockSpec((1, 6, D), lambda i, d: (i // per_seq, 0, 0)),
            pl.BlockSpec((1, D), lambda i, d: (0, 0)),
            pl.BlockSpec((1, D), lambda i, d: (0, 0)),
        ],
        out_specs=pl.BlockSpec((TT, D), lambda i, d: (i, 0)),
        scratch_shapes=[pltpu.VMEM((2 * TT * SUBLANES, LANES), F32), pltpu.SemaphoreType.DMA],
    )
    return pl.pallas_call(
        functools.partial(_final_kernel, n_tok=N),
        grid_spec=grid_spec,
        out_shape=jax.ShapeDtypeStruct((N, D), F32),
        compiler_params=pltpu.CompilerParams(
            dimension_semantics=("arbitrary",), vmem_limit_bytes=VMEM_LIMIT),
        name="combine_ln",
    )(dest, yt, x1, mod3, w, b)


def _route(lg):
    N = lg.shape[0]
    R = SLOT_BLOCK
    e_logits = lg[:, :N_EXPERTS].reshape(N, N_GROUPS, EXPERTS_PER_GROUP)
    g_logits = lg[:, N_EXPERTS:N_EXPERTS + N_GROUPS]
    g_prob = jax.nn.softmax(g_logits, axis=-1)
    g_star = jnp.argmax(g_logits, axis=-1).astype(jnp.int32)
    p_star = jnp.take_along_axis(g_prob, g_star[:, None], axis=1)
    e_sel = jnp.take_along_axis(e_logits, g_star[:, None, None], axis=1)[:, 0]
    top_v, top_i = lax.top_k(e_sel, 2)
    gates = p_star * jax.nn.softmax(top_v, axis=-1)
    e12 = (g_star[:, None] * EXPERTS_PER_GROUP + top_i).astype(jnp.int32)

    onehot = (e12[:, :, None] == jnp.arange(N_EXPERTS, dtype=jnp.int32)).astype(jnp.int32)
    per_tok = onehot.sum(axis=1)
    before = jnp.cumsum(per_tok, axis=0) - per_tok
    rank = jnp.take_along_axis(before, e12, axis=1)
    counts = per_tok.sum(axis=0)
    nblk = (counts + R - 1) // R
    blk_end = jnp.cumsum(nblk)
    blk_start = blk_end - nblk
    dest = blk_start[e12] * R + rank

    n_blocks = (2 * N) // R + N_EXPERTS
    n_slots = n_blocks * R
    tok = jnp.broadcast_to(jnp.arange(N, dtype=jnp.int32)[:, None], (N, 2))
    slot_tok = (jnp.arange(n_slots, dtype=jnp.int32) % N).at[dest.reshape(-1)].set(tok.reshape(-1))
    slot_g = jnp.zeros((n_slots,), F32).at[dest.reshape(-1)].set(gates.reshape(-1))
    block_e = jnp.minimum(
        jnp.searchsorted(blk_end, jnp.arange(n_blocks, dtype=jnp.int32), side="right"),
        N_EXPERTS - 1).astype(jnp.int32)
    n_valid = blk_end[-1:].astype(jnp.int32)
    dest_km = dest.T.reshape(-1).astype(jnp.int32)
    return block_e, slot_tok, n_valid, slot_g.reshape(n_slots, 1), dest_km


def kernel(x, c, positions, w_ada, b_ada, w_in, w_out, hgrn_lb, hgrn_norm_w, ret_norm_w, post_ln1_w,
           post_ln1_b, w_rg, b_rg, w_re, b_re, w_gate, w_up, w_down, post_ln2_w, post_ln2_b):
    B, S, D = x.shape
    N = B * S
    hw = N_HEADS * D_HEAD

    mod3 = _ada_call(c, w_ada[0], b_ada[0]).reshape(B, 6, D)

    win = (w_in[0].reshape(D, 2, 4, N_HEADS, D_HEAD).transpose(1, 0, 3, 2, 4)
           .reshape(2, D, 4 * hw).astype(BF16))
    wout = w_out[0].astype(BF16)
    pad = LANES - N_EXPERTS - N_GROUPS
    wr = jnp.concatenate([w_re[0], w_rg[0], jnp.zeros((D, pad), F32)], axis=1).astype(BF16)
    br = jnp.concatenate([b_re[0], b_rg[0], jnp.zeros((pad,), F32)]).reshape(1, LANES)

    x1, h2t, lg = _mixer_call(
        x, positions, mod3, win, wout, hgrn_lb, hgrn_norm_w[0].reshape(1, hw),
        ret_norm_w[0].reshape(1, hw), post_ln1_w[0].reshape(1, D), post_ln1_b[0].reshape(1, D), wr, br)

    block_e, slot_tok, n_valid, slot_g, dest = _route(lg)
    yt = _expert_call(block_e, slot_tok, n_valid, h2t, w_gate[0].astype(BF16), w_up[0].astype(BF16),
                      w_down[0].astype(BF16), slot_g)
    out = _final_call(dest, yt, x1.reshape(N, D), mod3, post_ln2_w[0].reshape(1, D),
                      post_ln2_b[0].reshape(1, D), S)
    return out.reshape(B, S, D)
```

```python
import functools
import math

import numpy as np
import jax
import jax.numpy as jnp
from jax import lax
from jax.experimental import pallas as pl
from jax.experimental.pallas import tpu as pltpu

F32 = jnp.float32
BF16 = jnp.bfloat16

D_MODEL = 1024
N_HEADS = 4
D_HEAD = 128
HGRN_CHUNK = 64
RET_CHUNK = 256
ROPE_BASE = 10000.0
N_EXPERTS = 32
EXPERTS_PER_GROUP = 8
N_GROUPS = 4
D_EXPERT = 512
DEEPNORM_ALPHA = 2.0 ** 0.25
LN_EPS = 1e-5

SEQ_TILE = 512
ROW_BLOCK = 128
SLOT_BLOCK = 256
TOK_TILE = 256
ROUTE_TILE = 2048
N_LOGIT_ROWS = 40
INVERT_STEPS = 64
LANES = 128
SUBLANES = 8
VMEM_LIMIT = 56 * 1024 * 1024


def _mm(a, b):
    return jnp.dot(a, b, preferred_element_type=F32)


def _mm_nt(a, b):
    return lax.dot_general(a, b, (((1,), (1,)), ((), ())), preferred_element_type=F32)


def _mm_tn(a, b):
    return lax.dot_general(a, b, (((0,), (0,)), ((), ())), preferred_element_type=F32)


def _sigmoid(x):
    return 1.0 / (1.0 + jnp.exp(-x))


def _silu(x):
    return x * _sigmoid(x)


def _ln(x):
    mu = jnp.mean(x, axis=-1, keepdims=True)
    xc = x - mu
    var = jnp.mean(xc * xc, axis=-1, keepdims=True)
    return xc * lax.rsqrt(var + LN_EPS)


def _const_spec(shape):
    nd = len(shape)
    return pl.BlockSpec(shape, lambda *_: (0,) * nd, pipeline_mode=pl.Buffered(1))


def _ada_kernel(c_ref, w_ref, b_ref, o_ref):
    ca = _silu(c_ref[...])
    o_ref[...] = _mm(ca.astype(BF16), w_ref[...].astype(BF16)) + b_ref[...]


def _ada_call(c, w_ada, b_ada):
    B, D = c.shape
    n_out = w_ada.shape[1]
    return pl.pallas_call(
        _ada_kernel,
        grid=(n_out // D,),
        in_specs=[pl.BlockSpec((B, D), lambda j: (0, 0)),
                  pl.BlockSpec((D, D), lambda j: (0, j)),
                  pl.BlockSpec((1, D), lambda j: (0, j))],
        out_specs=pl.BlockSpec((B, D), lambda j: (0, j)),
        out_shape=jax.ShapeDtypeStruct((B, n_out), F32),
        name="ada_mod",
    )(c, w_ada, b_ada.reshape(1, n_out))


def _mixer_kernel(x_ref, pos_ref, mod_ref, win_ref, wout_ref, lb_ref, hnw_ref, rnw_ref,
                  ln1w_ref, ln1b_ref, invf_ref, dmat_ref, qdec_ref, kdec_ref, wr_ref, br_ref,
                  x1_ref, h2t_ref, lg_ref,
                  h_scr, proj_scr, o_scr, cos_scr, sin_scr, sh_scr, sr_scr, *, ret_chunk_decay):
    T = SEQ_TILE
    D = D_MODEL
    dh = D_HEAD
    hw = N_HEADS * dh

    @pl.when(pl.program_id(1) == 0)
    def _():
        sh_scr[...] = jnp.zeros_like(sh_scr)
        sr_scr[...] = jnp.zeros_like(sr_scr)

    shift1 = mod_ref[0, 0:1, :]
    scale1 = mod_ref[0, 1:2, :]
    gate1 = mod_ref[0, 2:3, :]
    shift2 = mod_ref[0, 3:4, :]
    scale2 = mod_ref[0, 4:5, :]

    for r in range(T // ROW_BLOCK):
        rows = slice(r * ROW_BLOCK, (r + 1) * ROW_BLOCK)
        h = _ln(x_ref[0, rows, :]) * (1.0 + scale1) + shift1
        h_scr[rows, :] = h.astype(BF16)

    pos = pos_ref[0].astype(F32)
    ang_t = invf_ref[...] * pos
    half = lax.broadcasted_iota(jnp.int32, ang_t.shape, 0) < dh // 2
    cos_scr[...] = jnp.cos(ang_t).T
    sin_scr[...] = jnp.where(half, -jnp.sin(ang_t), jnp.sin(ang_t)).T

    proj_scr[...] = _mm(h_scr[...], win_ref[0])

    l0 = lb_ref[0:1, :]
    l1 = lb_ref[1:2, :]
    lmax = jnp.maximum(l0, l1)
    e0 = jnp.exp(l0 - lmax)
    e1 = jnp.exp(l1 - lmax)
    lb = e0 / (e0 + e1)
    hnw = hnw_ref[...]

    C = HGRN_CHUNK
    rr = lax.broadcasted_iota(jnp.int32, (C, C), 0)
    cc = lax.broadcasted_iota(jnp.int32, (C, C), 1)
    causal = rr >= cc
    tri = causal.astype(BF16)

    def hgrn_chunk(c, carry):
        rows = pl.ds(pl.multiple_of(c * C, C), C)
        logfs, ks = [], []
        for hd in range(N_HEADS):
            z = proj_scr[rows, hd * 4 * dh + dh: hd * 4 * dh + 2 * dh]
            e = jnp.exp(-jnp.abs(z))
            r = 1.0 / (1.0 + e)
            er = e * r
            zp = z >= 0
            sig_pos = jnp.where(zp, r, er)
            sig_neg = jnp.where(zp, er, r)
            lbh = lb[:, hd * dh:(hd + 1) * dh]
            logfs.append(jnp.log(lbh + (1.0 - lbh) * sig_pos))
            ks.append((1.0 - lbh) * sig_neg)
        lf = jnp.concatenate(logfs, axis=1)
        hi = lf.astype(BF16)
        r1 = lf - hi.astype(F32)
        mid = r1.astype(BF16)
        lo = (r1 - mid.astype(F32)).astype(BF16)
        bsum = _mm(tri, hi) + _mm(tri, mid) + _mm(tri, lo)
        for hd in range(N_HEADS):
            base = hd * 4 * dh
            b = bsum[:, hd * dh:(hd + 1) * dh]
            b_last = b[C - 1:C, :]
            q = proj_scr[rows, base: base + dh]
            v = proj_scr[rows, base + 2 * dh: base + 3 * dh].astype(BF16)
            g = proj_scr[rows, base + 3 * dh: base + 4 * dh]
            k = ks[hd]
            q_in = (_silu(q) * jnp.exp(b)).astype(BF16)
            k_in = (k * jnp.exp(-b)).astype(BF16)
            k_dec = (k * jnp.exp(b_last - b)).astype(BF16)
            st = sh_scr[hd]
            a = jnp.where(causal, _mm_nt(q_in, k_in), 0.0)
            o = _mm(a.astype(BF16), v) + _mm_nt(q_in, st.astype(BF16))
            sh_scr[hd] = st * jnp.exp(b_last) + _mm_tn(v, k_dec)
            ms = jnp.mean(o * o, axis=-1, keepdims=True)
            on = o * lax.rsqrt(ms + LN_EPS) * hnw[:, hd * dh:(hd + 1) * dh] * _silu(g)
            o_scr[rows, hd * dh:(hd + 1) * dh] = on.astype(BF16)
        return carry

    lax.fori_loop(0, T // C, hgrn_chunk, 0)

    proj_scr[...] = _mm(h_scr[...], win_ref[1])
    rnw = rnw_ref[...]
    RC = RET_CHUNK
    q_scale = dh ** -0.5
    for sub in range(T // RC):
        rows = slice(sub * RC, (sub + 1) * RC)
        cs = cos_scr[rows, :]
        sn = sin_scr[rows, :]
        for hd in range(N_HEADS):
            base = hd * 4 * dh
            rq = proj_scr[rows, base: base + dh]
            rk = proj_scr[rows, base + dh: base + 2 * dh]
            v = proj_scr[rows, base + 2 * dh: base + 3 * dh].astype(BF16)
            g = proj_scr[rows, base + 3 * dh: base + 4 * dh]
            q = (rq * cs + pltpu.roll(rq, dh // 2, 1) * sn) * q_scale
            k = rk * cs + pltpu.roll(rk, dh // 2, 1) * sn
            qb = q.astype(BF16)
            st = sr_scr[hd]
            s = _mm_nt(qb, k.astype(BF16)) * dmat_ref[hd]
            o = _mm(s.astype(BF16), v) + _mm_nt(qb, st.astype(BF16)) * qdec_ref[hd]
            sr_scr[hd] = st * ret_chunk_decay[hd] + _mm_tn(v, (k * kdec_ref[hd]).astype(BF16))
            mu = jnp.mean(o, axis=-1, keepdims=True)
            oc = o - mu
            var = jnp.mean(oc * oc, axis=-1, keepdims=True)
            on = oc * lax.rsqrt(var + LN_EPS) * rnw[:, hd * dh:(hd + 1) * dh] * _silu(g)
            o_scr[rows, hw + hd * dh: hw + (hd + 1) * dh] = on.astype(BF16)

    proj_scr[:, 0:D] = _mm(o_scr[...], wout_ref[...])
    for r in range(T // ROW_BLOCK):
        rows = slice(r * ROW_BLOCK, (r + 1) * ROW_BLOCK)
        u = DEEPNORM_ALPHA * x_ref[0, rows, :] + gate1 * proj_scr[rows, 0:D]
        x1 = _ln(u) * ln1w_ref[...] + ln1b_ref[...]
        x1_ref[0, rows, :] = x1
        h2 = (_ln(x1) * (1.0 + scale2) + shift2).astype(BF16)
        h_scr[rows, :] = h2
        h2f = h2.astype(F32)
        for s in range(SUBLANES):
            h2t_ref[pl.ds(r * ROW_BLOCK * SUBLANES + s, ROW_BLOCK, stride=SUBLANES), :] = (
                h2f[:, s * LANES:(s + 1) * LANES])
    lg_ref[...] = (_mm(h_scr[...], wr_ref[...]) + br_ref[...]).T[0:N_LOGIT_ROWS, :]


def _retention_tables():
    h = np.arange(N_HEADS, dtype=np.float64)
    log_gamma = np.log(1.0 - np.exp2(-5.0 - h))
    idx = np.arange(RET_CHUNK, dtype=np.float64)
    rel = idx[:, None] - idx[None, :]
    dmat = np.where(rel >= 0, np.exp(np.maximum(rel, 0.0)[None] * log_gamma[:, None, None]), 0.0)
    qdec = np.exp((idx + 1.0)[None, :] * log_gamma[:, None])
    kdec = np.exp((RET_CHUNK - 1.0 - idx)[None, :] * log_gamma[:, None])
    cdec = np.exp(RET_CHUNK * log_gamma)
    bc = lambda t: np.broadcast_to(t[:, :, None], (N_HEADS, RET_CHUNK, D_HEAD))
    return (jnp.asarray(dmat, F32), jnp.asarray(bc(qdec), F32), jnp.asarray(bc(kdec), F32),
            tuple(float(np.float32(v)) for v in cdec))


def _mixer_call(x, positions, mod3, win, wout, hgrn_lb, hnw, rnw, ln1w, ln1b, wr, br):
    B, S, D = x.shape
    T = SEQ_TILE
    nj = S // T
    dmat, qdec, kdec, cdec = _retention_tables()
    inv_freq = np.power(ROPE_BASE, -np.arange(0, D_HEAD, 2, dtype=np.float64) / D_HEAD)
    invf = jnp.asarray(np.concatenate([inv_freq, inv_freq])[:, None], F32)
    pos3 = positions.reshape(B, 1, S)
    kern = functools.partial(_mixer_kernel, ret_chunk_decay=cdec)
    return pl.pallas_call(
        kern,
        grid=(B, nj),
        in_specs=[
            pl.BlockSpec((1, T, D), lambda b, j: (b, j, 0)),
            pl.BlockSpec((1, 1, T), lambda b, j: (b, 0, j)),
            pl.BlockSpec((1, 6, D), lambda b, j: (b, 0, 0)),
            _const_spec(win.shape), _const_spec(wout.shape), _const_spec(hgrn_lb.shape),
            _const_spec(hnw.shape), _const_spec(rnw.shape), _const_spec(ln1w.shape),
            _const_spec(ln1b.shape), _const_spec(invf.shape), _const_spec(dmat.shape),
            _const_spec(qdec.shape), _const_spec(kdec.shape), _const_spec(wr.shape),
            _const_spec(br.shape),
        ],
        out_specs=[
            pl.BlockSpec((1, T, D), lambda b, j: (b, j, 0)),
            pl.BlockSpec((T * SUBLANES, LANES), lambda b, j: (b * nj + j, 0)),
            pl.BlockSpec((N_LOGIT_ROWS, T), lambda b, j: (0, b * nj + j)),
        ],
        out_shape=[
            jax.ShapeDtypeStruct((B, S, D), F32),
            jax.ShapeDtypeStruct((B * S * SUBLANES, LANES), F32),
            jax.ShapeDtypeStruct((N_LOGIT_ROWS, B * S), F32),
        ],
        scratch_shapes=[
            pltpu.VMEM((T, D), BF16),
            pltpu.VMEM((T, 4 * N_HEADS * D_HEAD), F32),
            pltpu.VMEM((T, D), BF16),
            pltpu.VMEM((T, D_HEAD), F32),
            pltpu.VMEM((T, D_HEAD), F32),
            pltpu.VMEM((N_HEADS, D_HEAD, D_HEAD), F32),
            pltpu.VMEM((N_HEADS, D_HEAD, D_HEAD), F32),
        ],
        compiler_params=pltpu.CompilerParams(
            dimension_semantics=("arbitrary", "arbitrary"), vmem_limit_bytes=VMEM_LIMIT),
        name="mixer",
    )(x, pos3, mod3, win, wout, hgrn_lb, hnw, rnw, ln1w, ln1b, invf, dmat, qdec, kdec, wr, br)


def _row_copy(src_ref, dst_ref, src_row, dst_row, sem):
    s0 = pl.multiple_of(src_row * SUBLANES, SUBLANES)
    d0 = pl.multiple_of(dst_row * SUBLANES, SUBLANES)
    return pltpu.make_async_copy(src_ref.at[pl.ds(s0, SUBLANES)], dst_ref.at[pl.ds(d0, SUBLANES)], sem)


def _from_token_tiles(buf_ref, first_row, n_rows):
    return jnp.concatenate(
        [buf_ref[pl.ds(first_row * SUBLANES + s, n_rows, stride=SUBLANES), :] for s in range(SUBLANES)],
        axis=1)


def _start_rows(src_ref, dst_ref, idx_ref, idx_base, dst_base, n_rows, sem):
    def issue(r2, carry):
        for u in range(2):
            r = r2 * 2 + u
            _row_copy(src_ref, dst_ref, idx_ref[idx_base + r], dst_base + r, sem).start(priority=u)
        return carry
    lax.fori_loop(0, n_rows // 2, issue, 0)


def _wait_rows(src_ref, dst_ref, n_rows, sem):
    pltpu.make_async_copy(src_ref.at[pl.ds(0, n_rows * SUBLANES)], dst_ref, sem).wait()


def _expert_kernel(be_ref, tok_ref, nv_ref, h2t_ref, wg_ref, wu_ref, wd_ref, y_ref,
                   buf, wg_s, wu_s, wd_s, sems):
    i = pl.program_id(0)
    R = SLOT_BLOCK
    nv = nv_ref[0]

    @pl.when(i == 0)
    def _():
        _start_rows(h2t_ref, buf.at[0], tok_ref, 0, 0, R, sems.at[0])

    for slot in range(2):
        @pl.when((i + 1 < nv) & ((i + 1) % 2 == slot))
        def _():
            _start_rows(h2t_ref, buf.at[slot], tok_ref, (i + 1) * R, 0, R, sems.at[slot])

    prev = be_ref[jnp.maximum(i - 1, 0)]

    @pl.when((i < nv) & ((i == 0) | (be_ref[i] != prev)))
    def _():
        wg_s[...] = wg_ref[0].astype(BF16)
        wu_s[...] = wu_ref[0].astype(BF16)
        wd_s[...] = wd_ref[0].astype(BF16)

    for slot in range(2):
        @pl.when((i < nv) & (i % 2 == slot))
        def _():
            _wait_rows(h2t_ref, buf.at[slot], R, sems.at[slot])
            xb = _from_token_tiles(buf.at[slot], 0, R).astype(BF16)
            a = _mm(xb, wg_s[...])
            u = _mm(xb, wu_s[...])
            y = _mm((_silu(a) * u).astype(BF16), wd_s[...])
            for s in range(SUBLANES):
                y_ref[pl.ds(s, R, stride=SUBLANES), :] = y[:, s * LANES:(s + 1) * LANES]

    @pl.when(i >= nv)
    def _():
        y_ref[...] = jnp.zeros_like(y_ref)


def _expert_call(block_e, slot_tok, n_valid, h2t, wg, wu, wd):
    n_slots = slot_tok.shape[0]
    R = SLOT_BLOCK
    n_blocks = n_slots // R
    D, F = wg.shape[1], wg.shape[2]
    grid_spec = pltpu.PrefetchScalarGridSpec(
        num_scalar_prefetch=3,
        grid=(n_blocks,),
        in_specs=[
            pl.BlockSpec(memory_space=pl.ANY),
            pl.BlockSpec((1, D, F), lambda i, be, tok, nv: (be[i], 0, 0)),
            pl.BlockSpec((1, D, F), lambda i, be, tok, nv: (be[i], 0, 0)),
            pl.BlockSpec((1, F, D), lambda i, be, tok, nv: (be[i], 0, 0)),
        ],
        out_specs=pl.BlockSpec((R * SUBLANES, LANES), lambda i, be, tok, nv: (i, 0)),
        scratch_shapes=[pltpu.VMEM((2, R * SUBLANES, LANES), F32),
                        pltpu.VMEM((D, F), BF16), pltpu.VMEM((D, F), BF16), pltpu.VMEM((F, D), BF16),
                        pltpu.SemaphoreType.DMA((2,))],
    )
    return pl.pallas_call(
        _expert_kernel,
        grid_spec=grid_spec,
        out_shape=jax.ShapeDtypeStruct((n_slots * SUBLANES, LANES), F32),
        compiler_params=pltpu.CompilerParams(
            dimension_semantics=("arbitrary",), vmem_limit_bytes=VMEM_LIMIT),
        name="experts",
    )(block_e, slot_tok, n_valid, h2t, wg, wu, wd)


def _final_kernel(dest_ref, yt_ref, x1_ref, g_ref, mod_ref, w_ref, b_ref, o_ref, buf, sems, *, n_tok):
    i = pl.program_id(0)
    n_steps = pl.num_programs(0)
    TT = TOK_TILE

    def start_tile(tile, slot):
        for k in range(2):
            _start_rows(yt_ref, buf.at[slot], dest_ref, k * n_tok + tile * TT, k * TT, TT, sems.at[slot])

    @pl.when(i == 0)
    def _():
        start_tile(0, 0)

    for slot in range(2):
        @pl.when((i + 1 < n_steps) & ((i + 1) % 2 == slot))
        def _():
            start_tile(i + 1, slot)

    gate2 = mod_ref[0, 5:6, :]
    g = g_ref[...]
    for slot in range(2):
        @pl.when(i % 2 == slot)
        def _():
            _wait_rows(yt_ref, buf.at[slot], 2 * TT, sems.at[slot])
            y = (_from_token_tiles(buf.at[slot], 0, TT) * g[:, 0:1]
                 + _from_token_tiles(buf.at[slot], TT, TT) * g[:, 1:2])
            u = DEEPNORM_ALPHA * x1_ref[...] + gate2 * y
            o_ref[...] = _ln(u) * w_ref[...] + b_ref[...]


def _final_call(dest, yt, x1, gates, mod3, w, b, seq_len):
    N, D = x1.shape
    TT = TOK_TILE
    per_seq = seq_len // TT
    grid_spec = pltpu.PrefetchScalarGridSpec(
        num_scalar_prefetch=1,
        grid=(N // TT,),
        in_specs=[
            pl.BlockSpec(memory_space=pl.ANY),
            pl.BlockSpec((TT, D), lambda i, d: (i, 0)),
            pl.BlockSpec((TT, 2), lambda i, d: (i, 0)),
            pl.BlockSpec((1, 6, D), lambda i, d: (i // per_seq, 0, 0)),
            pl.BlockSpec((1, D), lambda i, d: (0, 0)),
            pl.BlockSpec((1, D), lambda i, d: (0, 0)),
        ],
        out_specs=pl.BlockSpec((TT, D), lambda i, d: (i, 0)),
        scratch_shapes=[pltpu.VMEM((2, 2 * TT * SUBLANES, LANES), F32), pltpu.SemaphoreType.DMA((2,))],
    )
    return pl.pallas_call(
        functools.partial(_final_kernel, n_tok=N),
        grid_spec=grid_spec,
        out_shape=jax.ShapeDtypeStruct((N, D), F32),
        compiler_params=pltpu.CompilerParams(
            dimension_semantics=("arbitrary",), vmem_limit_bytes=VMEM_LIMIT),
        name="combine_ln",
    )(dest, yt, x1, gates, mod3, w, b)


def _route_kernel(lgt_ref, rf_ref, ri_ref, meta_ref, cnt_scr, base_scr, bstart_scr):
    ph = pl.program_id(0)
    t = pl.program_id(1)
    TR = ROUTE_TILE
    E = N_EXPERTS
    R = SLOT_BLOCK
    G = EXPERTS_PER_GROUP
    lg = lgt_ref[...]

    g = [lg[E + k:E + k + 1, :] for k in range(N_GROUPS)]
    gmax = jnp.maximum(jnp.maximum(g[0], g[1]), jnp.maximum(g[2], g[3]))
    gs = jnp.where(g[0] == gmax, 0.0, jnp.where(g[1] == gmax, 1.0, jnp.where(g[2] == gmax, 2.0, 3.0)))
    psum = (jnp.exp(g[0] - gmax) + jnp.exp(g[1] - gmax)) + (jnp.exp(g[2] - gmax) + jnp.exp(g[3] - gmax))
    p_star = 1.0 / psum
    esel = jnp.where(gs == 0.0, lg[0:G], jnp.where(gs == 1.0, lg[G:2 * G],
                                                   jnp.where(gs == 2.0, lg[2 * G:3 * G], lg[3 * G:4 * G])))
    sub = lax.broadcasted_iota(jnp.int32, (G, TR), 0).astype(F32)
    m1 = jnp.max(esel, axis=0, keepdims=True)
    i1 = jnp.min(jnp.where(esel == m1, sub, float(G)), axis=0, keepdims=True)
    es2 = jnp.where(sub == i1, -jnp.inf, esel)
    m2 = jnp.max(es2, axis=0, keepdims=True)
    i2 = jnp.min(jnp.where(es2 == m2, sub, float(G)), axis=0, keepdims=True)
    d = jnp.exp(m2 - m1)
    w1 = 1.0 / (1.0 + d)
    w2 = d * w1
    e1 = gs * G + i1
    e2 = gs * G + i2
    row = lax.broadcasted_iota(jnp.int32, (E, TR), 0).astype(F32)
    oh1 = row == e1
    oh2 = row == e2
    cnt = oh1.astype(F32) + oh2.astype(F32)

    @pl.when(ph == 0)
    def _():
        @pl.when(t == 0)
        def _():
            cnt_scr[...] = jnp.zeros_like(cnt_scr)
        acc = cnt_scr[...]
        for j in range(TR // LANES):
            acc = acc + cnt[:, j * LANES:(j + 1) * LANES]
        cnt_scr[...] = acc

    @pl.when((ph == 1) & (t == 0))
    def _():
        counts = jnp.sum(cnt_scr[...], axis=1, keepdims=True)
        nblk = jnp.floor((counts + (R - 1)) * (1.0 / R))
        nblk_b = jnp.broadcast_to(nblk, (E, LANES))
        r32 = lax.broadcasted_iota(jnp.int32, (E, E), 0)
        c32 = lax.broadcasted_iota(jnp.int32, (E, E), 1)
        bstart = _mm((c32 < r32).astype(BF16), nblk_b.astype(BF16))
        bstart_scr[...] = bstart
        base_scr[...] = jnp.zeros_like(base_scr)
        n_pad = meta_ref.shape[1]
        bend = jnp.broadcast_to((bstart + nblk_b)[:, 0:1], (E, n_pad))
        blk = lax.broadcasted_iota(jnp.int32, (E, n_pad), 1).astype(F32)
        be = jnp.sum((bend <= blk).astype(F32), axis=0, keepdims=True)
        meta_ref[...] = jnp.zeros_like(meta_ref)
        meta_ref[0:1, :] = jnp.minimum(be, E - 1).astype(jnp.int32)
        meta_ref[1:2, :] = bend[E - 1:E, :].astype(jnp.int32)

    @pl.when(ph == 1)
    def _():
        ri = lax.broadcasted_iota(jnp.int32, (LANES, 2 * LANES), 0)
        ci = lax.broadcasted_iota(jnp.int32, (LANES, 2 * LANES), 1)
        w = ((ci >= ri) | (ci >= LANES)).astype(BF16)
        base = base_scr[...]
        slot0 = bstart_scr[...] * R
        d1s, d2s = [], []
        for j in range(TR // LANES):
            sl = slice(j * LANES, (j + 1) * LANES)
            blk = cnt[:, sl]
            res = _mm(blk.astype(BF16), w)
            val = slot0 + base + (res[:, 0:LANES] - blk)
            d1s.append(jnp.sum(jnp.where(oh1[:, sl], val, 0.0), axis=0, keepdims=True))
            d2s.append(jnp.sum(jnp.where(oh2[:, sl], val, 0.0), axis=0, keepdims=True))
            base = base + res[:, LANES:2 * LANES]
        base_scr[...] = base
        ri_ref[...] = jnp.zeros_like(ri_ref)
        ri_ref[0:1, :] = jnp.concatenate(d1s, axis=1).astype(jnp.int32)
        ri_ref[1:2, :] = jnp.concatenate(d2s, axis=1).astype(jnp.int32)
        rf_ref[...] = jnp.zeros_like(rf_ref)
        rf_ref[0:1, :] = p_star * w1
        rf_ref[1:2, :] = p_star * w2


def _route_call(lgt, n_blocks):
    rows, N = lgt.shape
    TR = ROUTE_TILE
    E = N_EXPERTS
    n_pad = -(-n_blocks // LANES) * LANES
    return pl.pallas_call(
        _route_kernel,
        grid=(2, N // TR),
        in_specs=[pl.BlockSpec((rows, TR), lambda ph, t: (0, t))],
        out_specs=[
            pl.BlockSpec((SUBLANES, TR), lambda ph, t: (0, ph * t)),
            pl.BlockSpec((SUBLANES, TR), lambda ph, t: (0, ph * t)),
            pl.BlockSpec((SUBLANES, n_pad), lambda ph, t: (0, 0)),
        ],
        out_shape=[
            jax.ShapeDtypeStruct((SUBLANES, N), F32),
            jax.ShapeDtypeStruct((SUBLANES, N), jnp.int32),
            jax.ShapeDtypeStruct((SUBLANES, n_pad), jnp.int32),
        ],
        scratch_shapes=[pltpu.VMEM((E, LANES), F32), pltpu.VMEM((E, LANES), F32), pltpu.VMEM((E, LANES), F32)],
        compiler_params=pltpu.CompilerParams(dimension_semantics=("arbitrary", "arbitrary")),
        name="route",
    )(lgt)


def _invert_kernel(dest_ref, tok_ref, *, n_tok):
    n_slots = tok_ref.shape[0]
    ph = pl.program_id(0)
    g = pl.program_id(1)
    U = 8
    assert n_tok & (n_tok - 1) == 0, "token count must be a power of two for the padding-slot spread"
    slots_per_step = n_slots // INVERT_STEPS
    toks_per_step = n_tok // INVERT_STEPS

    @pl.when(ph == 0)
    def _():
        def init(i, carry):
            for u in range(U):
                s = g * slots_per_step + i * U + u
                tok_ref[s] = s & (n_tok - 1)
            return carry
        lax.fori_loop(0, slots_per_step // U, init, 0)

    @pl.when(ph == 1)
    def _():
        def fill(i, carry):
            for u in range(U):
                p = g * toks_per_step + i * U + u
                tok_ref[dest_ref[p]] = p
                tok_ref[dest_ref[n_tok + p]] = p
            return carry
        lax.fori_loop(0, toks_per_step // U, fill, 0)


def _invert_call(dest, n_slots):
    n_tok = dest.shape[0] // 2
    assert n_slots % (INVERT_STEPS * 8) == 0 and n_tok % (INVERT_STEPS * 8) == 0
    return pl.pallas_call(
        functools.partial(_invert_kernel, n_tok=n_tok),
        grid=(2, INVERT_STEPS),
        in_specs=[pl.BlockSpec(memory_space=pltpu.SMEM)],
        out_specs=pl.BlockSpec(memory_space=pltpu.SMEM),
        out_shape=jax.ShapeDtypeStruct((n_slots,), jnp.int32),
        compiler_params=pltpu.CompilerParams(dimension_semantics=("arbitrary", "arbitrary")),
        name="invert_slots",
    )(dest)


def kernel(x, c, positions, w_ada, b_ada, w_in, w_out, hgrn_lb, hgrn_norm_w, ret_norm_w, post_ln1_w,
           post_ln1_b, w_rg, b_rg, w_re, b_re, w_gate, w_up, w_down, post_ln2_w, post_ln2_b):
    B, S, D = x.shape
    N = B * S
    hw = N_HEADS * D_HEAD

    mod3 = _ada_call(c, w_ada[0], b_ada[0]).reshape(B, 6, D)

    win = (w_in[0].reshape(D, 2, 4, N_HEADS, D_HEAD).transpose(1, 0, 3, 2, 4)
           .reshape(2, D, 4 * hw).astype(BF16))
    wout = w_out[0].astype(BF16)
    pad = LANES - N_EXPERTS - N_GROUPS
    wr = jnp.concatenate([w_re[0], w_rg[0], jnp.zeros((D, pad), F32)], axis=1).astype(BF16)
    br = jnp.concatenate([b_re[0], b_rg[0], jnp.zeros((pad,), F32)]).reshape(1, LANES)

    x1, h2t, lgt = _mixer_call(
        x, positions, mod3, win, wout, hgrn_lb, hgrn_norm_w[0].reshape(1, hw),
        ret_norm_w[0].reshape(1, hw), post_ln1_w[0].reshape(1, D), post_ln1_b[0].reshape(1, D), wr, br)

    n_blocks = (2 * N) // SLOT_BLOCK + N_EXPERTS
    rf, ri, meta = _route_call(lgt, n_blocks)
    dest = ri[0:2].reshape(2 * N)
    gates = rf[0:2].T
    slot_tok = _invert_call(dest, n_blocks * SLOT_BLOCK)
    yt = _expert_call(meta[0, :n_blocks], slot_tok, meta[1, 0:1], h2t, w_gate[0], w_up[0], w_down[0])
    out = _final_call(dest, yt, x1.reshape(N, D), gates, mod3, post_ln2_w[0].reshape(1, D),
                      post_ln2_b[0].reshape(1, D), S)
    return out.reshape(B, S, D)
```

```python
import functools
import math

import numpy as np
import jax
import jax.numpy as jnp
from jax import lax
from jax.experimental import pallas as pl
from jax.experimental.pallas import tpu as pltpu

F32 = jnp.float32
BF16 = jnp.bfloat16

D_MODEL = 1024
N_HEADS = 4
D_HEAD = 128
HGRN_CHUNK = 64
RET_CHUNK = 256
ROPE_BASE = 10000.0
N_EXPERTS = 32
EXPERTS_PER_GROUP = 8
N_GROUPS = 4
D_EXPERT = 512
DEEPNORM_ALPHA = 2.0 ** 0.25
LN_EPS = 1e-5

SEQ_TILE = 512
ROW_BLOCK = 128
PAIRS_PER_GROUP = EXPERTS_PER_GROUP * (EXPERTS_PER_GROUP - 1) // 2
NB = 128
SLOT_BLOCK = 128
TOK_TILE = 512
ROUTE_TILE = 2048
N_LOGIT_ROWS = 40
INVERT_STEPS = 8
LANES = 128
SUBLANES = 8
VMEM_LIMIT = 56 * 1024 * 1024


def _mm(a, b):
    return jnp.dot(a, b, preferred_element_type=F32)


def _mm_nt(a, b):
    return lax.dot_general(a, b, (((1,), (1,)), ((), ())), preferred_element_type=F32)


def _mm_tn(a, b):
    return lax.dot_general(a, b, (((0,), (0,)), ((), ())), preferred_element_type=F32)


def _sigmoid(x):
    return 1.0 / (1.0 + jnp.exp(-x))


def _silu(x):
    return x * _sigmoid(x)


def _ln(x):
    mu = jnp.mean(x, axis=-1, keepdims=True)
    xc = x - mu
    var = jnp.mean(xc * xc, axis=-1, keepdims=True)
    return xc * lax.rsqrt(var + LN_EPS)


def _const_spec(shape):
    nd = len(shape)
    return pl.BlockSpec(shape, lambda *_: (0,) * nd, pipeline_mode=pl.Buffered(1))


def _ada_kernel(c_ref, w_ref, b_ref, o_ref):
    ca = _silu(c_ref[...])
    o_ref[...] = _mm(ca.astype(BF16), w_ref[...].astype(BF16)) + b_ref[...]


def _ada_call(c, w_ada, b_ada):
    B, D = c.shape
    n_out = w_ada.shape[1]
    return pl.pallas_call(
        _ada_kernel,
        grid=(n_out // D,),
        in_specs=[pl.BlockSpec((B, D), lambda j: (0, 0)),
                  pl.BlockSpec((D, D), lambda j: (0, j)),
                  pl.BlockSpec((1, D), lambda j: (0, j))],
        out_specs=pl.BlockSpec((B, D), lambda j: (0, j)),
        out_shape=jax.ShapeDtypeStruct((B, n_out), F32),
        name="ada_mod",
    )(c, w_ada, b_ada.reshape(1, n_out))


def _mixer_kernel(x_ref, pos_ref, mod_ref, win_ref, wout_ref, lb_ref, hnw_ref, rnw_ref,
                  ln1w_ref, ln1b_ref, invf_ref, dmat_ref, qdec_ref, kdec_ref, wr_ref, br_ref,
                  x1_ref, h2t_ref, lg_ref,
                  h_scr, proj_scr, o_scr, cos_scr, sin_scr, sh_scr, sr_scr, *, ret_chunk_decay):
    T = SEQ_TILE
    D = D_MODEL
    dh = D_HEAD
    hw = N_HEADS * dh

    @pl.when(pl.program_id(1) == 0)
    def _():
        sh_scr[...] = jnp.zeros_like(sh_scr)
        sr_scr[...] = jnp.zeros_like(sr_scr)

    shift1 = mod_ref[0, 0:1, :]
    scale1 = mod_ref[0, 1:2, :]
    gate1 = mod_ref[0, 2:3, :]
    shift2 = mod_ref[0, 3:4, :]
    scale2 = mod_ref[0, 4:5, :]

    for r in range(T // ROW_BLOCK):
        rows = slice(r * ROW_BLOCK, (r + 1) * ROW_BLOCK)
        h = _ln(x_ref[0, rows, :]) * (1.0 + scale1) + shift1
        h_scr[rows, :] = h.astype(BF16)

    pos = pos_ref[0].astype(F32)
    ang_t = invf_ref[...] * pos
    half = lax.broadcasted_iota(jnp.int32, ang_t.shape, 0) < dh // 2
    cos_scr[...] = jnp.cos(ang_t).T
    sin_scr[...] = jnp.where(half, -jnp.sin(ang_t), jnp.sin(ang_t)).T

    proj_scr[...] = _mm(h_scr[...], win_ref[0])

    l0 = lb_ref[0:1, :]
    l1 = lb_ref[1:2, :]
    lmax = jnp.maximum(l0, l1)
    e0 = jnp.exp(l0 - lmax)
    e1 = jnp.exp(l1 - lmax)
    lb = e0 / (e0 + e1)
    hnw = hnw_ref[...]

    C = HGRN_CHUNK
    rr = lax.broadcasted_iota(jnp.int32, (C, C), 0)
    cc = lax.broadcasted_iota(jnp.int32, (C, C), 1)
    causal = rr >= cc
    tri = causal.astype(BF16)

    def hgrn_chunk(c, carry):
        rows = pl.ds(pl.multiple_of(c * C, C), C)
        logfs, ks = [], []
        for hd in range(N_HEADS):
            z = proj_scr[rows, hd * 4 * dh + dh: hd * 4 * dh + 2 * dh]
            e = jnp.exp(-jnp.abs(z))
            r = 1.0 / (1.0 + e)
            er = e * r
            zp = z >= 0
            sig_pos = jnp.where(zp, r, er)
            sig_neg = jnp.where(zp, er, r)
            lbh = lb[:, hd * dh:(hd + 1) * dh]
            logfs.append(jnp.log(lbh + (1.0 - lbh) * sig_pos))
            ks.append((1.0 - lbh) * sig_neg)
        lf = jnp.concatenate(logfs, axis=1)
        hi = lf.astype(BF16)
        r1 = lf - hi.astype(F32)
        mid = r1.astype(BF16)
        lo = (r1 - mid.astype(F32)).astype(BF16)
        bsum = _mm(tri, hi) + _mm(tri, mid) + _mm(tri, lo)
        for hd in range(N_HEADS):
            base = hd * 4 * dh
            b = bsum[:, hd * dh:(hd + 1) * dh]
            b_last = b[C - 1:C, :]
            q = proj_scr[rows, base: base + dh]
            v = proj_scr[rows, base + 2 * dh: base + 3 * dh].astype(BF16)
            g = proj_scr[rows, base + 3 * dh: base + 4 * dh]
            k = ks[hd]
            q_in = (_silu(q) * jnp.exp(b)).astype(BF16)
            k_in = (k * jnp.exp(-b)).astype(BF16)
            k_dec = (k * jnp.exp(b_last - b)).astype(BF16)
            st = sh_scr[hd]
            a = jnp.where(causal, _mm_nt(q_in, k_in), 0.0)
            o = _mm(a.astype(BF16), v) + _mm_nt(q_in, st.astype(BF16))
            sh_scr[hd] = st * jnp.exp(b_last) + _mm_tn(v, k_dec)
            ms = jnp.mean(o * o, axis=-1, keepdims=True)
            on = o * lax.rsqrt(ms + LN_EPS) * hnw[:, hd * dh:(hd + 1) * dh] * _silu(g)
            o_scr[rows, hd * dh:(hd + 1) * dh] = on.astype(BF16)
        return carry

    lax.fori_loop(0, T // C, hgrn_chunk, 0)

    proj_scr[...] = _mm(h_scr[...], win_ref[1])
    rnw = rnw_ref[...]
    RC = RET_CHUNK
    q_scale = dh ** -0.5
    for sub in range(T // RC):
        rows = slice(sub * RC, (sub + 1) * RC)
        cs = cos_scr[rows, :]
        sn = sin_scr[rows, :]
        for hd in range(N_HEADS):
            base = hd * 4 * dh
            rq = proj_scr[rows, base: base + dh]
            rk = proj_scr[rows, base + dh: base + 2 * dh]
            v = proj_scr[rows, base + 2 * dh: base + 3 * dh].astype(BF16)
            g = proj_scr[rows, base + 3 * dh: base + 4 * dh]
            q = (rq * cs + pltpu.roll(rq, dh // 2, 1) * sn) * q_scale
            k = rk * cs + pltpu.roll(rk, dh // 2, 1) * sn
            qb = q.astype(BF16)
            st = sr_scr[hd]
            s = _mm_nt(qb, k.astype(BF16)) * dmat_ref[hd]
            o = _mm(s.astype(BF16), v) + _mm_nt(qb, st.astype(BF16)) * qdec_ref[hd]
            sr_scr[hd] = st * ret_chunk_decay[hd] + _mm_tn(v, (k * kdec_ref[hd]).astype(BF16))
            mu = jnp.mean(o, axis=-1, keepdims=True)
            oc = o - mu
            var = jnp.mean(oc * oc, axis=-1, keepdims=True)
            on = oc * lax.rsqrt(var + LN_EPS) * rnw[:, hd * dh:(hd + 1) * dh] * _silu(g)
            o_scr[rows, hw + hd * dh: hw + (hd + 1) * dh] = on.astype(BF16)

    proj_scr[:, 0:D] = _mm(o_scr[...], wout_ref[...])
    for r in range(T // ROW_BLOCK):
        rows = slice(r * ROW_BLOCK, (r + 1) * ROW_BLOCK)
        u = DEEPNORM_ALPHA * x_ref[0, rows, :] + gate1 * proj_scr[rows, 0:D]
        x1 = _ln(u) * ln1w_ref[...] + ln1b_ref[...]
        x1_ref[0, rows, :] = x1
        h2 = (_ln(x1) * (1.0 + scale2) + shift2).astype(BF16)
        h_scr[rows, :] = h2
        h2f = h2.astype(F32)
        for s in range(SUBLANES):
            h2t_ref[pl.ds(r * ROW_BLOCK * SUBLANES + s, ROW_BLOCK, stride=SUBLANES), :] = (
                h2f[:, s * LANES:(s + 1) * LANES])
    lg_ref[...] = (_mm(h_scr[...], wr_ref[...]) + br_ref[...]).T[0:N_LOGIT_ROWS, :]


def _retention_tables():
    h = np.arange(N_HEADS, dtype=np.float64)
    log_gamma = np.log(1.0 - np.exp2(-5.0 - h))
    idx = np.arange(RET_CHUNK, dtype=np.float64)
    rel = idx[:, None] - idx[None, :]
    dmat = np.where(rel >= 0, np.exp(np.maximum(rel, 0.0)[None] * log_gamma[:, None, None]), 0.0)
    qdec = np.exp((idx + 1.0)[None, :] * log_gamma[:, None])
    kdec = np.exp((RET_CHUNK - 1.0 - idx)[None, :] * log_gamma[:, None])
    cdec = np.exp(RET_CHUNK * log_gamma)
    bc = lambda t: np.broadcast_to(t[:, :, None], (N_HEADS, RET_CHUNK, D_HEAD))
    return (jnp.asarray(dmat, F32), jnp.asarray(bc(qdec), F32), jnp.asarray(bc(kdec), F32),
            tuple(float(np.float32(v)) for v in cdec))


def _mixer_call(x, positions, mod3, win, wout, hgrn_lb, hnw, rnw, ln1w, ln1b, wr, br):
    B, S, D = x.shape
    T = SEQ_TILE
    nj = S // T
    dmat, qdec, kdec, cdec = _retention_tables()
    inv_freq = np.power(ROPE_BASE, -np.arange(0, D_HEAD, 2, dtype=np.float64) / D_HEAD)
    invf = jnp.asarray(np.concatenate([inv_freq, inv_freq])[:, None], F32)
    pos3 = positions.reshape(B, 1, S)
    kern = functools.partial(_mixer_kernel, ret_chunk_decay=cdec)
    return pl.pallas_call(
        kern,
        grid=(B, nj),
        in_specs=[
            pl.BlockSpec((1, T, D), lambda b, j: (b, j, 0)),
            pl.BlockSpec((1, 1, T), lambda b, j: (b, 0, j)),
            pl.BlockSpec((1, 6, D), lambda b, j: (b, 0, 0)),
            _const_spec(win.shape), _const_spec(wout.shape), _const_spec(hgrn_lb.shape),
            _const_spec(hnw.shape), _const_spec(rnw.shape), _const_spec(ln1w.shape),
            _const_spec(ln1b.shape), _const_spec(invf.shape), _const_spec(dmat.shape),
            _const_spec(qdec.shape), _const_spec(kdec.shape), _const_spec(wr.shape),
            _const_spec(br.shape),
        ],
        out_specs=[
            pl.BlockSpec((1, T, D), lambda b, j: (b, j, 0)),
            pl.BlockSpec((T * SUBLANES, LANES), lambda b, j: (b * nj + j, 0)),
            pl.BlockSpec((N_LOGIT_ROWS, T), lambda b, j: (0, b * nj + j)),
        ],
        out_shape=[
            jax.ShapeDtypeStruct((B, S, D), F32),
            jax.ShapeDtypeStruct((B * S * SUBLANES, LANES), F32),
            jax.ShapeDtypeStruct((N_LOGIT_ROWS, B * S), F32),
        ],
        scratch_shapes=[
            pltpu.VMEM((T, D), BF16),
            pltpu.VMEM((T, 4 * N_HEADS * D_HEAD), F32),
            pltpu.VMEM((T, D), BF16),
            pltpu.VMEM((T, D_HEAD), F32),
            pltpu.VMEM((T, D_HEAD), F32),
            pltpu.VMEM((N_HEADS, D_HEAD, D_HEAD), F32),
            pltpu.VMEM((N_HEADS, D_HEAD, D_HEAD), F32),
        ],
        compiler_params=pltpu.CompilerParams(
            dimension_semantics=("arbitrary", "arbitrary"), vmem_limit_bytes=VMEM_LIMIT),
        name="mixer",
    )(x, pos3, mod3, win, wout, hgrn_lb, hnw, rnw, ln1w, ln1b, invf, dmat, qdec, kdec, wr, br)


def _row_copy(src_ref, dst_ref, src_row, dst_row, sem):
    s0 = pl.multiple_of(src_row * SUBLANES, SUBLANES)
    d0 = pl.multiple_of(dst_row * SUBLANES, SUBLANES)
    return pltpu.make_async_copy(src_ref.at[pl.ds(s0, SUBLANES)], dst_ref.at[pl.ds(d0, SUBLANES)], sem)


def _from_token_tiles(buf_ref, first_row, n_rows):
    return jnp.concatenate(
        [buf_ref[pl.ds(first_row * SUBLANES + s, n_rows, stride=SUBLANES), :] for s in range(SUBLANES)],
        axis=1)


def _start_rows(src_ref, dst_ref, idx_ref, idx_base, dst_base, n_rows, sem):
    def issue(r2, carry):
        for u in range(2):
            r = r2 * 2 + u
            _row_copy(src_ref, dst_ref, idx_ref[idx_base + r], dst_base + r, sem).start(priority=u)
        return carry
    lax.fori_loop(0, n_rows // 2, issue, 0)


def _wait_rows(src_ref, dst_ref, n_rows, sem):
    pltpu.make_async_copy(src_ref.at[pl.ds(0, n_rows * SUBLANES)], dst_ref, sem).wait()


def _bf16_bits(x):
    return lax.bitcast_convert_type(x.astype(BF16).astype(F32), jnp.uint32)


def _expert_kernel(ea_ref, eb_ref, tok_ref, nv_ref, h2t_ref,
                   wga_ref, wua_ref, wda_ref, wgb_ref, wub_ref, wdb_ref, y_ref, buf, sems):
    i = pl.program_id(0)
    R = SLOT_BLOCK
    nv = nv_ref[0]
    del ea_ref, eb_ref

    @pl.when(i == 0)
    def _():
        _start_rows(h2t_ref, buf.at[0], tok_ref, 0, 0, R, sems.at[0])

    for slot in range(2):
        @pl.when((i + 1 < nv) & ((i + 1) % 2 == slot))
        def _():
            _start_rows(h2t_ref, buf.at[slot], tok_ref, (i + 1) * R, 0, R, sems.at[slot])

    def ffn(xb, wg_ref, wu_ref, wd_ref):
        a = _mm(xb, wg_ref[0])
        u = _mm(xb, wu_ref[0])
        return _mm((_silu(a) * u).astype(BF16), wd_ref[0])

    for slot in range(2):
        @pl.when((i < nv) & (i % 2 == slot))
        def _():
            _wait_rows(h2t_ref, buf.at[slot], R, sems.at[slot])
            xb = _from_token_tiles(buf.at[slot], 0, R).astype(BF16)
            ya = ffn(xb, wga_ref, wua_ref, wda_ref)
            yb = ffn(xb, wgb_ref, wub_ref, wdb_ref)
            packed = _bf16_bits(ya) | (_bf16_bits(yb) >> 16)
            for s in range(SUBLANES):
                y_ref[pl.ds(s, R, stride=SUBLANES), :] = packed[:, s * LANES:(s + 1) * LANES]

    @pl.when(i >= nv)
    def _():
        y_ref[...] = jnp.zeros_like(y_ref)


def _expert_call(block_ea, block_eb, slot_tok, n_valid, h2t, wg, wu, wd):
    n_slots = slot_tok.shape[0]
    R = SLOT_BLOCK
    n_blocks = n_slots // R
    D, F = wg.shape[1], wg.shape[2]
    wa = lambda i, ea, eb, tok, nv: (ea[i], 0, 0)
    wb = lambda i, ea, eb, tok, nv: (eb[i], 0, 0)
    grid_spec = pltpu.PrefetchScalarGridSpec(
        num_scalar_prefetch=4,
        grid=(n_blocks,),
        in_specs=[
            pl.BlockSpec(memory_space=pl.ANY),
            pl.BlockSpec((1, D, F), wa), pl.BlockSpec((1, D, F), wa), pl.BlockSpec((1, F, D), wa),
            pl.BlockSpec((1, D, F), wb), pl.BlockSpec((1, D, F), wb), pl.BlockSpec((1, F, D), wb),
        ],
        out_specs=pl.BlockSpec((R * SUBLANES, LANES), lambda i, ea, eb, tok, nv: (i, 0)),
        scratch_shapes=[pltpu.VMEM((2, R * SUBLANES, LANES), F32), pltpu.SemaphoreType.DMA((2,))],
    )
    return pl.pallas_call(
        _expert_kernel,
        grid_spec=grid_spec,
        out_shape=jax.ShapeDtypeStruct((n_slots * SUBLANES, LANES), jnp.uint32),
        compiler_params=pltpu.CompilerParams(
            dimension_semantics=("arbitrary",), vmem_limit_bytes=VMEM_LIMIT),
        name="experts",
    )(block_ea, block_eb, slot_tok, n_valid, h2t, wg, wu, wd, wg, wu, wd)


def _final_kernel(dest_ref, yt_ref, x1_ref, g_ref, mod_ref, w_ref, b_ref, o_ref, buf, sems):
    i = pl.program_id(0)
    n_steps = pl.num_programs(0)
    TT = TOK_TILE

    def start_tile(tile, slot):
        _start_rows(yt_ref, buf.at[slot], dest_ref, tile * TT, 0, TT, sems.at[slot])

    @pl.when(i == 0)
    def _():
        start_tile(0, 0)

    for slot in range(2):
        @pl.when((i + 1 < n_steps) & ((i + 1) % 2 == slot))
        def _():
            start_tile(i + 1, slot)

    gate2 = mod_ref[0, 5:6, :]
    g = g_ref[...]
    for slot in range(2):
        @pl.when(i % 2 == slot)
        def _():
            _wait_rows(yt_ref, buf.at[slot], TT, sems.at[slot])
            for r in range(TT // ROW_BLOCK):
                rows = slice(r * ROW_BLOCK, (r + 1) * ROW_BLOCK)
                packed = _from_token_tiles(buf.at[slot], r * ROW_BLOCK, ROW_BLOCK)
                ya = lax.bitcast_convert_type(packed & jnp.uint32(0xFFFF0000), F32)
                yb = lax.bitcast_convert_type(packed << 16, F32)
                y = ya * g[rows, 0:1] + yb * g[rows, 1:2]
                u = DEEPNORM_ALPHA * x1_ref[rows, :] + gate2 * y
                o_ref[rows, :] = _ln(u) * w_ref[...] + b_ref[...]


def _final_call(dest, yt, x1, gates, mod3, w, b, seq_len):
    N, D = x1.shape
    TT = TOK_TILE
    per_seq = seq_len // TT
    grid_spec = pltpu.PrefetchScalarGridSpec(
        num_scalar_prefetch=1,
        grid=(N // TT,),
        in_specs=[
            pl.BlockSpec(memory_space=pl.ANY),
            pl.BlockSpec((TT, D), lambda i, d: (i, 0)),
            pl.BlockSpec((TT, 2), lambda i, d: (i, 0)),
            pl.BlockSpec((1, 6, D), lambda i, d: (i // per_seq, 0, 0)),
            pl.BlockSpec((1, D), lambda i, d: (0, 0)),
            pl.BlockSpec((1, D), lambda i, d: (0, 0)),
        ],
        out_specs=pl.BlockSpec((TT, D), lambda i, d: (i, 0)),
        scratch_shapes=[pltpu.VMEM((2, TT * SUBLANES, LANES), jnp.uint32), pltpu.SemaphoreType.DMA((2,))],
    )
    return pl.pallas_call(
        _final_kernel,
        grid_spec=grid_spec,
        out_shape=jax.ShapeDtypeStruct((N, D), F32),
        compiler_params=pltpu.CompilerParams(
            dimension_semantics=("arbitrary",), vmem_limit_bytes=VMEM_LIMIT),
        name="combine_ln",
    )(dest, yt, x1, gates, mod3, w, b)


def _route_kernel(lgt_ref, rf_ref, ri_ref, meta_ref, cnt_scr, base_scr, bstart_scr):
    ph = pl.program_id(0)
    t = pl.program_id(1)
    TR = ROUTE_TILE
    E = N_EXPERTS
    R = SLOT_BLOCK
    G = EXPERTS_PER_GROUP
    lg = lgt_ref[...]

    g = [lg[E + k:E + k + 1, :] for k in range(N_GROUPS)]
    gmax = jnp.maximum(jnp.maximum(g[0], g[1]), jnp.maximum(g[2], g[3]))
    gs = jnp.where(g[0] == gmax, 0.0, jnp.where(g[1] == gmax, 1.0, jnp.where(g[2] == gmax, 2.0, 3.0)))
    psum = (jnp.exp(g[0] - gmax) + jnp.exp(g[1] - gmax)) + (jnp.exp(g[2] - gmax) + jnp.exp(g[3] - gmax))
    p_star = 1.0 / psum
    esel = jnp.where(gs == 0.0, lg[0:G], jnp.where(gs == 1.0, lg[G:2 * G],
                                                   jnp.where(gs == 2.0, lg[2 * G:3 * G], lg[3 * G:4 * G])))
    sub = lax.broadcasted_iota(jnp.int32, (G, TR), 0).astype(F32)
    m1 = jnp.max(esel, axis=0, keepdims=True)
    i1 = jnp.min(jnp.where(esel == m1, sub, float(G)), axis=0, keepdims=True)
    es2 = jnp.where(sub == i1, -jnp.inf, esel)
    m2 = jnp.max(es2, axis=0, keepdims=True)
    i2 = jnp.min(jnp.where(es2 == m2, sub, float(G)), axis=0, keepdims=True)
    d = jnp.exp(m2 - m1)
    w1 = 1.0 / (1.0 + d)
    w2 = d * w1
    lo = jnp.minimum(i1, i2)
    hi = jnp.maximum(i1, i2)
    bucket = gs * PAIRS_PER_GROUP + (lo * (2 * G - 1 - lo) * 0.5 + (hi - lo - 1.0))
    first_is_lo = i1 < i2
    gate_a = p_star * jnp.where(first_is_lo, w1, w2)
    gate_b = p_star * jnp.where(first_is_lo, w2, w1)
    row = lax.broadcasted_iota(jnp.int32, (NB, TR), 0).astype(F32)
    oh = row == bucket
    cnt = oh.astype(F32)

    @pl.when(ph == 0)
    def _():
        @pl.when(t == 0)
        def _():
            cnt_scr[...] = jnp.zeros_like(cnt_scr)
        acc = cnt_scr[...]
        for j in range(TR // LANES):
            acc = acc + cnt[:, j * LANES:(j + 1) * LANES]
        cnt_scr[...] = acc

    @pl.when((ph == 1) & (t == 0))
    def _():
        counts = jnp.sum(cnt_scr[...], axis=1, keepdims=True)
        nblk = jnp.floor((counts + (R - 1)) * (1.0 / R))
        nblk_b = jnp.broadcast_to(nblk, (NB, LANES))
        rb = lax.broadcasted_iota(jnp.int32, (NB, NB), 0)
        cb = lax.broadcasted_iota(jnp.int32, (NB, NB), 1)
        bstart = _mm((cb < rb).astype(BF16), nblk_b.astype(BF16))
        bstart_scr[...] = bstart
        base_scr[...] = jnp.zeros_like(base_scr)
        k = lax.broadcasted_iota(jnp.int32, (NB, 1), 0).astype(F32)
        grp = sum((k >= float(m * PAIRS_PER_GROUP)).astype(F32) for m in range(1, N_GROUPS))
        p = k - grp * PAIRS_PER_GROUP
        pair_start = [i * (2 * G - 1 - i) // 2 for i in range(G - 1)]
        ia = sum((p >= float(s)).astype(F32) for s in pair_start[1:])
        ib = p - ia * (2 * G - 1 - ia) * 0.5 + ia + 1.0
        n_pad = meta_ref.shape[1]
        blk = lax.broadcasted_iota(jnp.int32, (NB, n_pad), 1).astype(F32)
        first = jnp.broadcast_to(bstart[:, 0:1], (NB, n_pad))
        last = jnp.broadcast_to((bstart + nblk_b)[:, 0:1], (NB, n_pad))
        member = (first <= blk) & (blk < last)
        ea = jnp.sum(jnp.where(member, grp * G + ia, 0.0), axis=0, keepdims=True)
        eb = jnp.sum(jnp.where(member, grp * G + ib, 0.0), axis=0, keepdims=True)
        meta_ref[...] = jnp.zeros_like(meta_ref)
        meta_ref[0:1, :] = ea.astype(jnp.int32)
        meta_ref[1:2, :] = eb.astype(jnp.int32)
        meta_ref[2:3, :] = jnp.max(last, axis=0, keepdims=True).astype(jnp.int32)

    @pl.when(ph == 1)
    def _():
        ri = lax.broadcasted_iota(jnp.int32, (LANES, 2 * LANES), 0)
        ci = lax.broadcasted_iota(jnp.int32, (LANES, 2 * LANES), 1)
        w = ((ci >= ri) | (ci >= LANES)).astype(BF16)
        base = base_scr[...]
        slot0 = bstart_scr[...] * R
        ds = []
        for j in range(TR // LANES):
            sl = slice(j * LANES, (j + 1) * LANES)
            blk = cnt[:, sl]
            res = _mm(blk.astype(BF16), w)
            val = slot0 + base + (res[:, 0:LANES] - blk)
            ds.append(jnp.sum(jnp.where(oh[:, sl], val, 0.0), axis=0, keepdims=True))
            base = base + res[:, LANES:2 * LANES]
        base_scr[...] = base
        ri_ref[...] = jnp.zeros_like(ri_ref)
        ri_ref[0:1, :] = jnp.concatenate(ds, axis=1).astype(jnp.int32)
        rf_ref[...] = jnp.zeros_like(rf_ref)
        rf_ref[0:1, :] = gate_a
        rf_ref[1:2, :] = gate_b


def _route_call(lgt, n_blocks):
    rows, N = lgt.shape
    TR = ROUTE_TILE
    E = NB
    n_pad = -(-n_blocks // LANES) * LANES
    return pl.pallas_call(
        _route_kernel,
        grid=(2, N // TR),
        in_specs=[pl.BlockSpec((rows, TR), lambda ph, t: (0, t))],
        out_specs=[
            pl.BlockSpec((SUBLANES, TR), lambda ph, t: (0, ph * t)),
            pl.BlockSpec((SUBLANES, TR), lambda ph, t: (0, ph * t)),
            pl.BlockSpec((SUBLANES, n_pad), lambda ph, t: (0, 0)),
        ],
        out_shape=[
            jax.ShapeDtypeStruct((SUBLANES, N), F32),
            jax.ShapeDtypeStruct((SUBLANES, N), jnp.int32),
            jax.ShapeDtypeStruct((SUBLANES, n_pad), jnp.int32),
        ],
        scratch_shapes=[pltpu.VMEM((E, LANES), F32), pltpu.VMEM((E, LANES), F32), pltpu.VMEM((E, LANES), F32)],
        compiler_params=pltpu.CompilerParams(dimension_semantics=("arbitrary", "arbitrary")),
        name="route",
    )(lgt)


def _invert_kernel(dest_ref, fill_ref, tok_ref, sem):
    n_tok = dest_ref.shape[0]
    g = pl.program_id(0)
    U = 8
    toks_per_step = n_tok // INVERT_STEPS

    @pl.when(g == 0)
    def _():
        init = pltpu.make_async_copy(fill_ref, tok_ref, sem)
        init.start()
        init.wait()

    def fill(i, carry):
        for u in range(U):
            p = g * toks_per_step + i * U + u
            tok_ref[dest_ref[p]] = p
        return carry
    lax.fori_loop(0, toks_per_step // U, fill, 0)


def _invert_call(dest, n_slots):
    n_tok = dest.shape[0]
    assert n_tok % (INVERT_STEPS * 8) == 0
    fill = jnp.arange(n_slots, dtype=jnp.int32) % n_tok
    return pl.pallas_call(
        _invert_kernel,
        grid=(INVERT_STEPS,),
        in_specs=[pl.BlockSpec(memory_space=pltpu.SMEM), pl.BlockSpec(memory_space=pl.ANY)],
        out_specs=pl.BlockSpec(memory_space=pltpu.SMEM),
        out_shape=jax.ShapeDtypeStruct((n_slots,), jnp.int32),
        scratch_shapes=[pltpu.SemaphoreType.DMA],
        compiler_params=pltpu.CompilerParams(dimension_semantics=("arbitrary",)),
        name="invert_slots",
    )(dest, fill)


def kernel(x, c, positions, w_ada, b_ada, w_in, w_out, hgrn_lb, hgrn_norm_w, ret_norm_w, post_ln1_w,
           post_ln1_b, w_rg, b_rg, w_re, b_re, w_gate, w_up, w_down, post_ln2_w, post_ln2_b):
    B, S, D = x.shape
    N = B * S
    hw = N_HEADS * D_HEAD

    mod3 = _ada_call(c, w_ada[0], b_ada[0]).reshape(B, 6, D)

    win = (w_in[0].reshape(D, 2, 4, N_HEADS, D_HEAD).transpose(1, 0, 3, 2, 4)
           .reshape(2, D, 4 * hw).astype(BF16))
    wout = w_out[0].astype(BF16)
    pad = LANES - N_EXPERTS - N_GROUPS
    wr = jnp.concatenate([w_re[0], w_rg[0], jnp.zeros((D, pad), F32)], axis=1).astype(BF16)
    br = jnp.concatenate([b_re[0], b_rg[0], jnp.zeros((pad,), F32)]).reshape(1, LANES)

    x1, h2t, lgt = _mixer_call(
        x, positions, mod3, win, wout, hgrn_lb, hgrn_norm_w[0].reshape(1, hw),
        ret_norm_w[0].reshape(1, hw), post_ln1_w[0].reshape(1, D), post_ln1_b[0].reshape(1, D), wr, br)

    n_blocks = N // SLOT_BLOCK + N_GROUPS * PAIRS_PER_GROUP
    rf, ri, meta = _route_call(lgt, n_blocks)
    dest = ri[0]
    gates = rf[0:2].T
    slot_tok = _invert_call(dest, n_blocks * SLOT_BLOCK)
    yt = _expert_call(meta[0, :n_blocks], meta[1, :n_blocks], slot_tok, meta[2, 0:1], h2t,
                      w_gate[0].astype(BF16), w_up[0].astype(BF16), w_down[0].astype(BF16))
    out = _final_call(dest, yt, x1.reshape(N, D), gates, mod3, post_ln2_w[0].reshape(1, D),
                      post_ln2_b[0].reshape(1, D), S)
    return out.reshape(B, S, D)
```

```python
import functools
import math

import numpy as np
import jax
import jax.numpy as jnp
from jax import lax
from jax.experimental import pallas as pl
from jax.experimental.pallas import tpu as pltpu

F32 = jnp.float32
BF16 = jnp.bfloat16

D_MODEL = 1024
N_HEADS = 4
D_HEAD = 128
HGRN_CHUNK = 64
RET_CHUNK = 256
ROPE_BASE = 10000.0
N_EXPERTS = 32
EXPERTS_PER_GROUP = 8
N_GROUPS = 4
D_EXPERT = 512
DEEPNORM_ALPHA = 2.0 ** 0.25
LN_EPS = 1e-5

SEQ_TILE = 512
ROW_BLOCK = 128
PAIRS_PER_GROUP = EXPERTS_PER_GROUP * (EXPERTS_PER_GROUP - 1) // 2
NB = 128
SLOT_BLOCK = 128
TOK_TILE = 512
ROUTE_TILE = 2048
N_LOGIT_ROWS = 40
DISPATCH_TILE = 512
LANES = 128
SUBLANES = 8
VMEM_LIMIT = 56 * 1024 * 1024


def _mm(a, b):
    return jnp.dot(a, b, preferred_element_type=F32)


def _mm_nt(a, b):
    return lax.dot_general(a, b, (((1,), (1,)), ((), ())), preferred_element_type=F32)


def _mm_tn(a, b):
    return lax.dot_general(a, b, (((0,), (0,)), ((), ())), preferred_element_type=F32)


def _sigmoid(x):
    return 1.0 / (1.0 + jnp.exp(-x))


def _silu(x):
    return x * _sigmoid(x)


def _ln(x):
    mu = jnp.mean(x, axis=-1, keepdims=True)
    xc = x - mu
    var = jnp.mean(xc * xc, axis=-1, keepdims=True)
    return xc * lax.rsqrt(var + LN_EPS)


def _const_spec(shape):
    nd = len(shape)
    return pl.BlockSpec(shape, lambda *_: (0,) * nd, pipeline_mode=pl.Buffered(1))


def _ada_kernel(c_ref, w_ref, b_ref, o_ref):
    ca = _silu(c_ref[...])
    o_ref[...] = _mm(ca.astype(BF16), w_ref[...].astype(BF16)) + b_ref[...]


def _ada_call(c, w_ada, b_ada):
    B, D = c.shape
    n_out = w_ada.shape[1]
    return pl.pallas_call(
        _ada_kernel,
        grid=(n_out // D,),
        in_specs=[pl.BlockSpec((B, D), lambda j: (0, 0)),
                  pl.BlockSpec((D, D), lambda j: (0, j)),
                  pl.BlockSpec((1, D), lambda j: (0, j))],
        out_specs=pl.BlockSpec((B, D), lambda j: (0, j)),
        out_shape=jax.ShapeDtypeStruct((B, n_out), F32),
        name="ada_mod",
    )(c, w_ada, b_ada.reshape(1, n_out))


def _mixer_kernel(x_ref, pos_ref, mod_ref, win_ref, wout_ref, lb_ref, hnw_ref, rnw_ref,
                  ln1w_ref, ln1b_ref, invf_ref, dmat_ref, qdec_ref, kdec_ref, wr_ref, br_ref,
                  x1_ref, h2t_ref, lg_ref,
                  h_scr, proj_scr, o_scr, cos_scr, sin_scr, sh_scr, sr_scr, *, ret_chunk_decay):
    T = SEQ_TILE
    D = D_MODEL
    dh = D_HEAD
    hw = N_HEADS * dh

    @pl.when(pl.program_id(1) == 0)
    def _():
        sh_scr[...] = jnp.zeros_like(sh_scr)
        sr_scr[...] = jnp.zeros_like(sr_scr)

    shift1 = mod_ref[0, 0:1, :]
    scale1 = mod_ref[0, 1:2, :]
    gate1 = mod_ref[0, 2:3, :]
    shift2 = mod_ref[0, 3:4, :]
    scale2 = mod_ref[0, 4:5, :]

    for r in range(T // ROW_BLOCK):
        rows = slice(r * ROW_BLOCK, (r + 1) * ROW_BLOCK)
        h = _ln(x_ref[0, rows, :]) * (1.0 + scale1) + shift1
        h_scr[rows, :] = h.astype(BF16)

    pos = pos_ref[0].astype(F32)
    ang_t = invf_ref[...] * pos
    half = lax.broadcasted_iota(jnp.int32, ang_t.shape, 0) < dh // 2
    cos_scr[...] = jnp.cos(ang_t).T
    sin_scr[...] = jnp.where(half, -jnp.sin(ang_t), jnp.sin(ang_t)).T

    proj_scr[...] = _mm(h_scr[...], win_ref[0])

    l0 = lb_ref[0:1, :]
    l1 = lb_ref[1:2, :]
    lmax = jnp.maximum(l0, l1)
    e0 = jnp.exp(l0 - lmax)
    e1 = jnp.exp(l1 - lmax)
    lb = e0 / (e0 + e1)
    hnw = hnw_ref[...]

    C = HGRN_CHUNK
    rr = lax.broadcasted_iota(jnp.int32, (C, C), 0)
    cc = lax.broadcasted_iota(jnp.int32, (C, C), 1)
    causal = rr >= cc
    tri = causal.astype(BF16)

    def hgrn_chunk(c, carry):
        rows = pl.ds(pl.multiple_of(c * C, C), C)
        logfs, ks = [], []
        for hd in range(N_HEADS):
            z = proj_scr[rows, hd * 4 * dh + dh: hd * 4 * dh + 2 * dh]
            e = jnp.exp(-jnp.abs(z))
            r = 1.0 / (1.0 + e)
            er = e * r
            zp = z >= 0
            sig_pos = jnp.where(zp, r, er)
            sig_neg = jnp.where(zp, er, r)
            lbh = lb[:, hd * dh:(hd + 1) * dh]
            logfs.append(jnp.log(lbh + (1.0 - lbh) * sig_pos))
            ks.append((1.0 - lbh) * sig_neg)
        lf = jnp.concatenate(logfs, axis=1)
        hi = lf.astype(BF16)
        r1 = lf - hi.astype(F32)
        mid = r1.astype(BF16)
        lo = (r1 - mid.astype(F32)).astype(BF16)
        bsum = _mm(tri, hi) + _mm(tri, mid) + _mm(tri, lo)
        for hd in range(N_HEADS):
            base = hd * 4 * dh
            b = bsum[:, hd * dh:(hd + 1) * dh]
            b_last = b[C - 1:C, :]
            q = proj_scr[rows, base: base + dh]
            v = proj_scr[rows, base + 2 * dh: base + 3 * dh].astype(BF16)
            g = proj_scr[rows, base + 3 * dh: base + 4 * dh]
            k = ks[hd]
            q_in = (_silu(q) * jnp.exp(b)).astype(BF16)
            k_in = (k * jnp.exp(-b)).astype(BF16)
            k_dec = (k * jnp.exp(b_last - b)).astype(BF16)
            st = sh_scr[hd]
            a = jnp.where(causal, _mm_nt(q_in, k_in), 0.0)
            o = _mm(a.astype(BF16), v) + _mm_nt(q_in, st.astype(BF16))
            sh_scr[hd] = st * jnp.exp(b_last) + _mm_tn(v, k_dec)
            ms = jnp.mean(o * o, axis=-1, keepdims=True)
            on = o * lax.rsqrt(ms + LN_EPS) * hnw[:, hd * dh:(hd + 1) * dh] * _silu(g)
            o_scr[rows, hd * dh:(hd + 1) * dh] = on.astype(BF16)
        return carry

    lax.fori_loop(0, T // C, hgrn_chunk, 0)

    proj_scr[...] = _mm(h_scr[...], win_ref[1])
    rnw = rnw_ref[...]
    RC = RET_CHUNK
    q_scale = dh ** -0.5
    for sub in range(T // RC):
        rows = slice(sub * RC, (sub + 1) * RC)
        cs = cos_scr[rows, :]
        sn = sin_scr[rows, :]
        for hd in range(N_HEADS):
            base = hd * 4 * dh
            rq = proj_scr[rows, base: base + dh]
            rk = proj_scr[rows, base + dh: base + 2 * dh]
            v = proj_scr[rows, base + 2 * dh: base + 3 * dh].astype(BF16)
            g = proj_scr[rows, base + 3 * dh: base + 4 * dh]
            q = (rq * cs + pltpu.roll(rq, dh // 2, 1) * sn) * q_scale
            k = rk * cs + pltpu.roll(rk, dh // 2, 1) * sn
            qb = q.astype(BF16)
            st = sr_scr[hd]
            s = _mm_nt(qb, k.astype(BF16)) * dmat_ref[hd]
            o = _mm(s.astype(BF16), v) + _mm_nt(qb, st.astype(BF16)) * qdec_ref[hd]
            sr_scr[hd] = st * ret_chunk_decay[hd] + _mm_tn(v, (k * kdec_ref[hd]).astype(BF16))
            mu = jnp.mean(o, axis=-1, keepdims=True)
            oc = o - mu
            var = jnp.mean(oc * oc, axis=-1, keepdims=True)
            on = oc * lax.rsqrt(var + LN_EPS) * rnw[:, hd * dh:(hd + 1) * dh] * _silu(g)
            o_scr[rows, hw + hd * dh: hw + (hd + 1) * dh] = on.astype(BF16)

    proj_scr[:, 0:D] = _mm(o_scr[...], wout_ref[...])
    for r in range(T // ROW_BLOCK):
        rows = slice(r * ROW_BLOCK, (r + 1) * ROW_BLOCK)
        u = DEEPNORM_ALPHA * x_ref[0, rows, :] + gate1 * proj_scr[rows, 0:D]
        x1 = _ln(u) * ln1w_ref[...] + ln1b_ref[...]
        x1_ref[0, rows, :] = x1
        h2 = (_ln(x1) * (1.0 + scale2) + shift2).astype(BF16)
        h_scr[rows, :] = h2
        h2f = h2.astype(F32)
        for s in range(SUBLANES):
            h2t_ref[pl.ds(r * ROW_BLOCK * SUBLANES + s, ROW_BLOCK, stride=SUBLANES), :] = (
                h2f[:, s * LANES:(s + 1) * LANES])
    lg_ref[...] = (_mm(h_scr[...], wr_ref[...]) + br_ref[...]).T[0:N_LOGIT_ROWS, :]


def _retention_tables():
    h = np.arange(N_HEADS, dtype=np.float64)
    log_gamma = np.log(1.0 - np.exp2(-5.0 - h))
    idx = np.arange(RET_CHUNK, dtype=np.float64)
    rel = idx[:, None] - idx[None, :]
    dmat = np.where(rel >= 0, np.exp(np.maximum(rel, 0.0)[None] * log_gamma[:, None, None]), 0.0)
    qdec = np.exp((idx + 1.0)[None, :] * log_gamma[:, None])
    kdec = np.exp((RET_CHUNK - 1.0 - idx)[None, :] * log_gamma[:, None])
    cdec = np.exp(RET_CHUNK * log_gamma)
    bc = lambda t: np.broadcast_to(t[:, :, None], (N_HEADS, RET_CHUNK, D_HEAD))
    return (jnp.asarray(dmat, F32), jnp.asarray(bc(qdec), F32), jnp.asarray(bc(kdec), F32),
            tuple(float(np.float32(v)) for v in cdec))


def _mixer_call(x, positions, mod3, win, wout, hgrn_lb, hnw, rnw, ln1w, ln1b, wr, br):
    B, S, D = x.shape
    T = SEQ_TILE
    nj = S // T
    dmat, qdec, kdec, cdec = _retention_tables()
    inv_freq = np.power(ROPE_BASE, -np.arange(0, D_HEAD, 2, dtype=np.float64) / D_HEAD)
    invf = jnp.asarray(np.concatenate([inv_freq, inv_freq])[:, None], F32)
    pos3 = positions.reshape(B, 1, S)
    kern = functools.partial(_mixer_kernel, ret_chunk_decay=cdec)
    return pl.pallas_call(
        kern,
        grid=(B, nj),
        in_specs=[
            pl.BlockSpec((1, T, D), lambda b, j: (b, j, 0)),
            pl.BlockSpec((1, 1, T), lambda b, j: (b, 0, j)),
            pl.BlockSpec((1, 6, D), lambda b, j: (b, 0, 0)),
            _const_spec(win.shape), _const_spec(wout.shape), _const_spec(hgrn_lb.shape),
            _const_spec(hnw.shape), _const_spec(rnw.shape), _const_spec(ln1w.shape),
            _const_spec(ln1b.shape), _const_spec(invf.shape), _const_spec(dmat.shape),
            _const_spec(qdec.shape), _const_spec(kdec.shape), _const_spec(wr.shape),
            _const_spec(br.shape),
        ],
        out_specs=[
            pl.BlockSpec((1, T, D), lambda b, j: (b, j, 0)),
            pl.BlockSpec((T * SUBLANES, LANES), lambda b, j: (b * nj + j, 0)),
            pl.BlockSpec((N_LOGIT_ROWS, T), lambda b, j: (0, b * nj + j)),
        ],
        out_shape=[
            jax.ShapeDtypeStruct((B, S, D), F32),
            jax.ShapeDtypeStruct((B * S * SUBLANES, LANES), F32),
            jax.ShapeDtypeStruct((N_LOGIT_ROWS, B * S), F32),
        ],
        scratch_shapes=[
            pltpu.VMEM((T, D), BF16),
            pltpu.VMEM((T, 4 * N_HEADS * D_HEAD), F32),
            pltpu.VMEM((T, D), BF16),
            pltpu.VMEM((T, D_HEAD), F32),
            pltpu.VMEM((T, D_HEAD), F32),
            pltpu.VMEM((N_HEADS, D_HEAD, D_HEAD), F32),
            pltpu.VMEM((N_HEADS, D_HEAD, D_HEAD), F32),
        ],
        compiler_params=pltpu.CompilerParams(
            dimension_semantics=("arbitrary", "arbitrary"), vmem_limit_bytes=VMEM_LIMIT),
        name="mixer",
    )(x, pos3, mod3, win, wout, hgrn_lb, hnw, rnw, ln1w, ln1b, invf, dmat, qdec, kdec, wr, br)


def _row_copy(src_ref, dst_ref, src_row, dst_row, sem):
    s0 = pl.multiple_of(src_row * SUBLANES, SUBLANES)
    d0 = pl.multiple_of(dst_row * SUBLANES, SUBLANES)
    return pltpu.make_async_copy(src_ref.at[pl.ds(s0, SUBLANES)], dst_ref.at[pl.ds(d0, SUBLANES)], sem)


def _from_token_tiles(buf_ref, first_row, n_rows):
    return jnp.concatenate(
        [buf_ref[pl.ds(first_row * SUBLANES + s, n_rows, stride=SUBLANES), :] for s in range(SUBLANES)],
        axis=1)


def _start_rows(src_ref, dst_ref, idx_ref, idx_base, dst_base, n_rows, sem):
    def issue(r2, carry):
        for u in range(2):
            r = r2 * 2 + u
            _row_copy(src_ref, dst_ref, idx_ref[idx_base + r], dst_base + r, sem).start(priority=u)
        return carry
    lax.fori_loop(0, n_rows // 2, issue, 0)


def _wait_rows(src_ref, dst_ref, n_rows, sem):
    pltpu.make_async_copy(src_ref.at[pl.ds(0, n_rows * SUBLANES)], dst_ref, sem).wait()


def _bf16_bits(x):
    return lax.bitcast_convert_type(x.astype(BF16).astype(F32), jnp.uint32)


def _dispatch_kernel(dest_ref, h2t_ref, zeros_ref, xs_ref, sem):
    del zeros_ref
    i = pl.program_id(0)
    TT = DISPATCH_TILE

    def issue(r2, carry):
        for u in range(2):
            t = i * TT + r2 * 2 + u
            _row_copy(h2t_ref, xs_ref, t, dest_ref[t], sem).start(priority=u)
        return carry
    lax.fori_loop(0, TT // 2, issue, 0)

    @pl.when(i > 0)
    def _():
        _wait_rows(h2t_ref, xs_ref.at[pl.ds(0, TT * SUBLANES)], TT, sem)

    @pl.when(i == pl.num_programs(0) - 1)
    def _():
        _wait_rows(h2t_ref, xs_ref.at[pl.ds(0, TT * SUBLANES)], TT, sem)


def _dispatch_call(dest, h2t, n_slots):
    N = dest.shape[0]
    grid_spec = pltpu.PrefetchScalarGridSpec(
        num_scalar_prefetch=1,
        grid=(N // DISPATCH_TILE,),
        in_specs=[pl.BlockSpec(memory_space=pl.ANY), pl.BlockSpec(memory_space=pl.ANY)],
        out_specs=pl.BlockSpec(memory_space=pl.ANY),
        scratch_shapes=[pltpu.SemaphoreType.DMA],
    )
    return pl.pallas_call(
        _dispatch_kernel,
        grid_spec=grid_spec,
        out_shape=jax.ShapeDtypeStruct((n_slots * SUBLANES, LANES), F32),
        input_output_aliases={2: 0},
        compiler_params=pltpu.CompilerParams(dimension_semantics=("arbitrary",)),
        name="dispatch",
    )(dest, h2t, jnp.zeros((n_slots * SUBLANES, LANES), F32))


def _expert_kernel(ea_ref, eb_ref, nv_ref, xs_ref,
                   wga_ref, wua_ref, wda_ref, wgb_ref, wub_ref, wdb_ref, y_ref):
    i = pl.program_id(0)
    R = SLOT_BLOCK
    del ea_ref, eb_ref

    def ffn(xb, wg_ref, wu_ref, wd_ref):
        a = _mm(xb, wg_ref[0])
        u = _mm(xb, wu_ref[0])
        return _mm((_silu(a) * u).astype(BF16), wd_ref[0])

    @pl.when(i < nv_ref[0])
    def _():
        xb = _from_token_tiles(xs_ref, 0, R).astype(BF16)
        ya = ffn(xb, wga_ref, wua_ref, wda_ref)
        yb = ffn(xb, wgb_ref, wub_ref, wdb_ref)
        packed = _bf16_bits(ya) | (_bf16_bits(yb) >> 16)
        for s in range(SUBLANES):
            y_ref[pl.ds(s, R, stride=SUBLANES), :] = packed[:, s * LANES:(s + 1) * LANES]

    @pl.when(i >= nv_ref[0])
    def _():
        y_ref[...] = jnp.zeros_like(y_ref)


def _expert_call(block_ea, block_eb, n_valid, xs, wg, wu, wd):
    R = SLOT_BLOCK
    n_blocks = xs.shape[0] // (R * SUBLANES)
    D, F = wg.shape[1], wg.shape[2]
    wa = lambda i, ea, eb, nv: (ea[i], 0, 0)
    wb = lambda i, ea, eb, nv: (eb[i], 0, 0)
    grid_spec = pltpu.PrefetchScalarGridSpec(
        num_scalar_prefetch=3,
        grid=(n_blocks,),
        in_specs=[
            pl.BlockSpec((R * SUBLANES, LANES), lambda i, ea, eb, nv: (i, 0)),
            pl.BlockSpec((1, D, F), wa), pl.BlockSpec((1, D, F), wa), pl.BlockSpec((1, F, D), wa),
            pl.BlockSpec((1, D, F), wb), pl.BlockSpec((1, D, F), wb), pl.BlockSpec((1, F, D), wb),
        ],
        out_specs=pl.BlockSpec((R * SUBLANES, LANES), lambda i, ea, eb, nv: (i, 0)),
    )
    return pl.pallas_call(
        _expert_kernel,
        grid_spec=grid_spec,
        out_shape=jax.ShapeDtypeStruct(xs.shape, jnp.uint32),
        compiler_params=pltpu.CompilerParams(
            dimension_semantics=("arbitrary",), vmem_limit_bytes=VMEM_LIMIT),
        name="experts",
    )(block_ea, block_eb, n_valid, xs, wg, wu, wd, wg, wu, wd)


def _final_kernel(dest_ref, yt_ref, x1_ref, g_ref, mod_ref, w_ref, b_ref, o_ref, buf, sems):
    i = pl.program_id(0)
    n_steps = pl.num_programs(0)
    TT = TOK_TILE

    def start_tile(tile, slot):
        _start_rows(yt_ref, buf.at[slot], dest_ref, tile * TT, 0, TT, sems.at[slot])

    @pl.when(i == 0)
    def _():
        start_tile(0, 0)

    for slot in range(2):
        @pl.when((i + 1 < n_steps) & ((i + 1) % 2 == slot))
        def _():
            start_tile(i + 1, slot)

    gate2 = mod_ref[0, 5:6, :]
    g = g_ref[...]
    for slot in range(2):
        @pl.when(i % 2 == slot)
        def _():
            _wait_rows(yt_ref, buf.at[slot], TT, sems.at[slot])
            for r in range(TT // ROW_BLOCK):
                rows = slice(r * ROW_BLOCK, (r + 1) * ROW_BLOCK)
                packed = _from_token_tiles(buf.at[slot], r * ROW_BLOCK, ROW_BLOCK)
                ya = lax.bitcast_convert_type(packed & jnp.uint32(0xFFFF0000), F32)
                yb = lax.bitcast_convert_type(packed << 16, F32)
                y = ya * g[rows, 0:1] + yb * g[rows, 1:2]
                u = DEEPNORM_ALPHA * x1_ref[rows, :] + gate2 * y
                o_ref[rows, :] = _ln(u) * w_ref[...] + b_ref[...]


def _final_call(dest, yt, x1, gates, mod3, w, b, seq_len):
    N, D = x1.shape
    TT = TOK_TILE
    per_seq = seq_len // TT
    grid_spec = pltpu.PrefetchScalarGridSpec(
        num_scalar_prefetch=1,
        grid=(N // TT,),
        in_specs=[
            pl.BlockSpec(memory_space=pl.ANY),
            pl.BlockSpec((TT, D), lambda i, d: (i, 0)),
            pl.BlockSpec((TT, 2), lambda i, d: (i, 0)),
            pl.BlockSpec((1, 6, D), lambda i, d: (i // per_seq, 0, 0)),
            pl.BlockSpec((1, D), lambda i, d: (0, 0)),
            pl.BlockSpec((1, D), lambda i, d: (0, 0)),
        ],
        out_specs=pl.BlockSpec((TT, D), lambda i, d: (i, 0)),
        scratch_shapes=[pltpu.VMEM((2, TT * SUBLANES, LANES), jnp.uint32), pltpu.SemaphoreType.DMA((2,))],
    )
    return pl.pallas_call(
        _final_kernel,
        grid_spec=grid_spec,
        out_shape=jax.ShapeDtypeStruct((N, D), F32),
        compiler_params=pltpu.CompilerParams(
            dimension_semantics=("arbitrary",), vmem_limit_bytes=VMEM_LIMIT),
        name="combine_ln",
    )(dest, yt, x1, gates, mod3, w, b)


def _route_kernel(lgt_ref, rf_ref, ri_ref, meta_ref, cnt_scr, base_scr, bstart_scr):
    ph = pl.program_id(0)
    t = pl.program_id(1)
    TR = ROUTE_TILE
    E = N_EXPERTS
    R = SLOT_BLOCK
    G = EXPERTS_PER_GROUP
    lg = lgt_ref[...]

    g = [lg[E + k:E + k + 1, :] for k in range(N_GROUPS)]
    gmax = jnp.maximum(jnp.maximum(g[0], g[1]), jnp.maximum(g[2], g[3]))
    gs = jnp.where(g[0] == gmax, 0.0, jnp.where(g[1] == gmax, 1.0, jnp.where(g[2] == gmax, 2.0, 3.0)))
    psum = (jnp.exp(g[0] - gmax) + jnp.exp(g[1] - gmax)) + (jnp.exp(g[2] - gmax) + jnp.exp(g[3] - gmax))
    p_star = 1.0 / psum
    esel = jnp.where(gs == 0.0, lg[0:G], jnp.where(gs == 1.0, lg[G:2 * G],
                                                   jnp.where(gs == 2.0, lg[2 * G:3 * G], lg[3 * G:4 * G])))
    sub = lax.broadcasted_iota(jnp.int32, (G, TR), 0).astype(F32)
    m1 = jnp.max(esel, axis=0, keepdims=True)
    i1 = jnp.min(jnp.where(esel == m1, sub, float(G)), axis=0, keepdims=True)
    es2 = jnp.where(sub == i1, -jnp.inf, esel)
    m2 = jnp.max(es2, axis=0, keepdims=True)
    i2 = jnp.min(jnp.where(es2 == m2, sub, float(G)), axis=0, keepdims=True)
    d = jnp.exp(m2 - m1)
    w1 = 1.0 / (1.0 + d)
    w2 = d * w1
    lo = jnp.minimum(i1, i2)
    hi = jnp.maximum(i1, i2)
    bucket = gs * PAIRS_PER_GROUP + (lo * (2 * G - 1 - lo) * 0.5 + (hi - lo - 1.0))
    first_is_lo = i1 < i2
    gate_a = p_star * jnp.where(first_is_lo, w1, w2)
    gate_b = p_star * jnp.where(first_is_lo, w2, w1)
    row = lax.broadcasted_iota(jnp.int32, (NB, TR), 0).astype(F32)
    oh = row == bucket
    cnt = oh.astype(F32)

    @pl.when(ph == 0)
    def _():
        @pl.when(t == 0)
        def _():
            cnt_scr[...] = jnp.zeros_like(cnt_scr)
        acc = cnt_scr[...]
        for j in range(TR // LANES):
            acc = acc + cnt[:, j * LANES:(j + 1) * LANES]
        cnt_scr[...] = acc

    @pl.when((ph == 1) & (t == 0))
    def _():
        counts = jnp.sum(cnt_scr[...], axis=1, keepdims=True)
        nblk = jnp.floor((counts + (R - 1)) * (1.0 / R))
        nblk_b = jnp.broadcast_to(nblk, (NB, LANES))
        rb = lax.broadcasted_iota(jnp.int32, (NB, NB), 0)
        cb = lax.broadcasted_iota(jnp.int32, (NB, NB), 1)
        bstart = _mm((cb < rb).astype(BF16), nblk_b.astype(BF16))
        bstart_scr[...] = bstart
        base_scr[...] = jnp.zeros_like(base_scr)
        k = lax.broadcasted_iota(jnp.int32, (NB, 1), 0).astype(F32)
        grp = sum((k >= float(m * PAIRS_PER_GROUP)).astype(F32) for m in range(1, N_GROUPS))
        p = k - grp * PAIRS_PER_GROUP
        pair_start = [i * (2 * G - 1 - i) // 2 for i in range(G - 1)]
        ia = sum((p >= float(s)).astype(F32) for s in pair_start[1:])
        ib = p - ia * (2 * G - 1 - ia) * 0.5 + ia + 1.0
        n_pad = meta_ref.shape[1]
        blk = lax.broadcasted_iota(jnp.int32, (NB, n_pad), 1).astype(F32)
        first = jnp.broadcast_to(bstart[:, 0:1], (NB, n_pad))
        last = jnp.broadcast_to((bstart + nblk_b)[:, 0:1], (NB, n_pad))
        member = (first <= blk) & (blk < last)
        ea = jnp.sum(jnp.where(member, grp * G + ia, 0.0), axis=0, keepdims=True)
        eb = jnp.sum(jnp.where(member, grp * G + ib, 0.0), axis=0, keepdims=True)
        meta_ref[...] = jnp.zeros_like(meta_ref)
        meta_ref[0:1, :] = ea.astype(jnp.int32)
        meta_ref[1:2, :] = eb.astype(jnp.int32)
        meta_ref[2:3, :] = jnp.max(last, axis=0, keepdims=True).astype(jnp.int32)

    @pl.when(ph == 1)
    def _():
        ri = lax.broadcasted_iota(jnp.int32, (LANES, 2 * LANES), 0)
        ci = lax.broadcasted_iota(jnp.int32, (LANES, 2 * LANES), 1)
        w = ((ci >= ri) | (ci >= LANES)).astype(BF16)
        base = base_scr[...]
        slot0 = bstart_scr[...] * R
        ds = []
        for j in range(TR // LANES):
            sl = slice(j * LANES, (j + 1) * LANES)
            blk = cnt[:, sl]
            res = _mm(blk.astype(BF16), w)
            val = slot0 + base + (res[:, 0:LANES] - blk)
            ds.append(jnp.sum(jnp.where(oh[:, sl], val, 0.0), axis=0, keepdims=True))
            base = base + res[:, LANES:2 * LANES]
        base_scr[...] = base
        ri_ref[...] = jnp.zeros_like(ri_ref)
        ri_ref[0:1, :] = jnp.concatenate(ds, axis=1).astype(jnp.int32)
        rf_ref[...] = jnp.zeros_like(rf_ref)
        rf_ref[0:1, :] = gate_a
        rf_ref[1:2, :] = gate_b


def _route_call(lgt, n_blocks):
    rows, N = lgt.shape
    TR = ROUTE_TILE
    E = NB
    n_pad = -(-n_blocks // LANES) * LANES
    return pl.pallas_call(
        _route_kernel,
        grid=(2, N // TR),
        in_specs=[pl.BlockSpec((rows, TR), lambda ph, t: (0, t))],
        out_specs=[
            pl.BlockSpec((SUBLANES, TR), lambda ph, t: (0, ph * t)),
            pl.BlockSpec((SUBLANES, TR), lambda ph, t: (0, ph * t)),
            pl.BlockSpec((SUBLANES, n_pad), lambda ph, t: (0, 0)),
        ],
        out_shape=[
            jax.ShapeDtypeStruct((SUBLANES, N), F32),
            jax.ShapeDtypeStruct((SUBLANES, N), jnp.int32),
            jax.ShapeDtypeStruct((SUBLANES, n_pad), jnp.int32),
        ],
        scratch_shapes=[pltpu.VMEM((E, LANES), F32), pltpu.VMEM((E, LANES), F32), pltpu.VMEM((E, LANES), F32)],
        compiler_params=pltpu.CompilerParams(dimension_semantics=("arbitrary", "arbitrary")),
        name="route",
    )(lgt)


def kernel(x, c, positions, w_ada, b_ada, w_in, w_out, hgrn_lb, hgrn_norm_w, ret_norm_w, post_ln1_w,
           post_ln1_b, w_rg, b_rg, w_re, b_re, w_gate, w_up, w_down, post_ln2_w, post_ln2_b):
    B, S, D = x.shape
    N = B * S
    hw = N_HEADS * D_HEAD

    mod3 = _ada_call(c, w_ada[0], b_ada[0]).reshape(B, 6, D)

    win = (w_in[0].reshape(D, 2, 4, N_HEADS, D_HEAD).transpose(1, 0, 3, 2, 4)
           .reshape(2, D, 4 * hw).astype(BF16))
    wout = w_out[0].astype(BF16)
    pad = LANES - N_EXPERTS - N_GROUPS
    wr = jnp.concatenate([w_re[0], w_rg[0], jnp.zeros((D, pad), F32)], axis=1).astype(BF16)
    br = jnp.concatenate([b_re[0], b_rg[0], jnp.zeros((pad,), F32)]).reshape(1, LANES)

    x1, h2t, lgt = _mixer_call(
        x, positions, mod3, win, wout, hgrn_lb, hgrn_norm_w[0].reshape(1, hw),
        ret_norm_w[0].reshape(1, hw), post_ln1_w[0].reshape(1, D), post_ln1_b[0].reshape(1, D), wr, br)

    n_blocks = N // SLOT_BLOCK + N_GROUPS * PAIRS_PER_GROUP
    rf, ri, meta = _route_call(lgt, n_blocks)
    dest = ri[0]
    gates = rf[0:2].T
    xs = _dispatch_call(dest, h2t, n_blocks * SLOT_BLOCK)
    yt = _expert_call(meta[0, :n_blocks], meta[1, :n_blocks], meta[2, 0:1], xs,
                      w_gate[0].astype(BF16), w_up[0].astype(BF16), w_down[0].astype(BF16))
    out = _final_call(dest, yt, x1.reshape(N, D), gates, mod3, post_ln2_w[0].reshape(1, D),
                      post_ln2_b[0].reshape(1, D), S)
    return out.reshape(B, S, D)
```

```python
import functools
import math

import numpy as np
import jax
import jax.numpy as jnp
from jax import lax
from jax.experimental import pallas as pl
from jax.experimental.pallas import tpu as pltpu

F32 = jnp.float32
BF16 = jnp.bfloat16

D_MODEL = 1024
N_HEADS = 4
D_HEAD = 128
HGRN_CHUNK = 64
RET_CHUNK = 256
ROPE_BASE = 10000.0
N_EXPERTS = 32
EXPERTS_PER_GROUP = 8
N_GROUPS = 4
D_EXPERT = 512
DEEPNORM_ALPHA = 2.0 ** 0.25
LN_EPS = 1e-5

SEQ_TILE = 512
ROW_BLOCK = 128
PAIRS_PER_GROUP = EXPERTS_PER_GROUP * (EXPERTS_PER_GROUP - 1) // 2
NB = 128
SLOT_BLOCK = 256
TOK_TILE = 512
ROUTE_TILE = 2048
N_LOGIT_ROWS = 40
INVERT_STEPS = 8
LANES = 128
SUBLANES = 8
VMEM_LIMIT = 56 * 1024 * 1024


def _mm(a, b):
    return jnp.dot(a, b, preferred_element_type=F32)


def _mm_nt(a, b):
    return lax.dot_general(a, b, (((1,), (1,)), ((), ())), preferred_element_type=F32)


def _mm_tn(a, b):
    return lax.dot_general(a, b, (((0,), (0,)), ((), ())), preferred_element_type=F32)


def _sigmoid(x):
    return 1.0 / (1.0 + jnp.exp(-x))


def _silu(x):
    return x * _sigmoid(x)


def _ln(x):
    mu = jnp.mean(x, axis=-1, keepdims=True)
    xc = x - mu
    var = jnp.mean(xc * xc, axis=-1, keepdims=True)
    return xc * lax.rsqrt(var + LN_EPS)


def _const_spec(shape):
    nd = len(shape)
    return pl.BlockSpec(shape, lambda *_: (0,) * nd, pipeline_mode=pl.Buffered(1))


def _ada_kernel(c_ref, w_ref, b_ref, o_ref):
    ca = _silu(c_ref[...])
    o_ref[...] = _mm(ca.astype(BF16), w_ref[...].astype(BF16)) + b_ref[...]


def _ada_call(c, w_ada, b_ada):
    B, D = c.shape
    n_out = w_ada.shape[1]
    return pl.pallas_call(
        _ada_kernel,
        grid=(n_out // D,),
        in_specs=[pl.BlockSpec((B, D), lambda j: (0, 0)),
                  pl.BlockSpec((D, D), lambda j: (0, j)),
                  pl.BlockSpec((1, D), lambda j: (0, j))],
        out_specs=pl.BlockSpec((B, D), lambda j: (0, j)),
        out_shape=jax.ShapeDtypeStruct((B, n_out), F32),
        name="ada_mod",
    )(c, w_ada, b_ada.reshape(1, n_out))


def _mixer_kernel(x_ref, pos_ref, mod_ref, win_ref, wout_ref, lb_ref, hnw_ref, rnw_ref,
                  ln1w_ref, ln1b_ref, invf_ref, dmat_ref, qdec_ref, kdec_ref, wr_ref, br_ref,
                  x1_ref, h2t_ref, lg_ref,
                  h_scr, proj_scr, o_scr, cos_scr, sin_scr, sh_scr, sr_scr, *, ret_chunk_decay):
    T = SEQ_TILE
    D = D_MODEL
    dh = D_HEAD
    hw = N_HEADS * dh

    @pl.when(pl.program_id(1) == 0)
    def _():
        sh_scr[...] = jnp.zeros_like(sh_scr)
        sr_scr[...] = jnp.zeros_like(sr_scr)

    shift1 = mod_ref[0, 0:1, :]
    scale1 = mod_ref[0, 1:2, :]
    gate1 = mod_ref[0, 2:3, :]
    shift2 = mod_ref[0, 3:4, :]
    scale2 = mod_ref[0, 4:5, :]

    for r in range(T // ROW_BLOCK):
        rows = slice(r * ROW_BLOCK, (r + 1) * ROW_BLOCK)
        h = _ln(x_ref[0, rows, :]) * (1.0 + scale1) + shift1
        h_scr[rows, :] = h.astype(BF16)

    pos = pos_ref[0].astype(F32)
    ang_t = invf_ref[...] * pos
    half = lax.broadcasted_iota(jnp.int32, ang_t.shape, 0) < dh // 2
    cos_scr[...] = jnp.cos(ang_t).T
    sin_scr[...] = jnp.where(half, -jnp.sin(ang_t), jnp.sin(ang_t)).T

    proj_scr[...] = _mm(h_scr[...], win_ref[0])

    l0 = lb_ref[0:1, :]
    l1 = lb_ref[1:2, :]
    lmax = jnp.maximum(l0, l1)
    e0 = jnp.exp(l0 - lmax)
    e1 = jnp.exp(l1 - lmax)
    lb = e0 / (e0 + e1)
    hnw = hnw_ref[...]

    C = HGRN_CHUNK
    rr = lax.broadcasted_iota(jnp.int32, (C, C), 0)
    cc = lax.broadcasted_iota(jnp.int32, (C, C), 1)
    causal = rr >= cc
    tri = causal.astype(BF16)

    def hgrn_chunk(c, carry):
        rows = pl.ds(pl.multiple_of(c * C, C), C)
        logfs, ks = [], []
        for hd in range(N_HEADS):
            z = proj_scr[rows, hd * 4 * dh + dh: hd * 4 * dh + 2 * dh]
            e = jnp.exp(-jnp.abs(z))
            r = 1.0 / (1.0 + e)
            er = e * r
            zp = z >= 0
            sig_pos = jnp.where(zp, r, er)
            sig_neg = jnp.where(zp, er, r)
            lbh = lb[:, hd * dh:(hd + 1) * dh]
            logfs.append(jnp.log(lbh + (1.0 - lbh) * sig_pos))
            ks.append((1.0 - lbh) * sig_neg)
        lf = jnp.concatenate(logfs, axis=1)
        hi = lf.astype(BF16)
        r1 = lf - hi.astype(F32)
        mid = r1.astype(BF16)
        lo = (r1 - mid.astype(F32)).astype(BF16)
        bsum = _mm(tri, hi) + _mm(tri, mid) + _mm(tri, lo)
        for hd in range(N_HEADS):
            base = hd * 4 * dh
            b = bsum[:, hd * dh:(hd + 1) * dh]
            b_last = b[C - 1:C, :]
            q = proj_scr[rows, base: base + dh]
            v = proj_scr[rows, base + 2 * dh: base + 3 * dh].astype(BF16)
            g = proj_scr[rows, base + 3 * dh: base + 4 * dh]
            k = ks[hd]
            q_in = (_silu(q) * jnp.exp(b)).astype(BF16)
            k_in = (k * jnp.exp(-b)).astype(BF16)
            k_dec = (k * jnp.exp(b_last - b)).astype(BF16)
            st = sh_scr[hd]
            a = jnp.where(causal, _mm_nt(q_in, k_in), 0.0)
            o = _mm(a.astype(BF16), v) + _mm_nt(q_in, st.astype(BF16))
            sh_scr[hd] = st * jnp.exp(b_last) + _mm_tn(v, k_dec)
            ms = jnp.mean(o * o, axis=-1, keepdims=True)
            on = o * lax.rsqrt(ms + LN_EPS) * hnw[:, hd * dh:(hd + 1) * dh] * _silu(g)
            o_scr[rows, hd * dh:(hd + 1) * dh] = on.astype(BF16)
        return carry

    lax.fori_loop(0, T // C, hgrn_chunk, 0)

    proj_scr[...] = _mm(h_scr[...], win_ref[1])
    rnw = rnw_ref[...]
    RC = RET_CHUNK
    q_scale = dh ** -0.5
    for sub in range(T // RC):
        rows = slice(sub * RC, (sub + 1) * RC)
        cs = cos_scr[rows, :]
        sn = sin_scr[rows, :]
        for hd in range(N_HEADS):
            base = hd * 4 * dh
            rq = proj_scr[rows, base: base + dh]
            rk = proj_scr[rows, base + dh: base + 2 * dh]
            v = proj_scr[rows, base + 2 * dh: base + 3 * dh].astype(BF16)
            g = proj_scr[rows, base + 3 * dh: base + 4 * dh]
            q = (rq * cs + pltpu.roll(rq, dh // 2, 1) * sn) * q_scale
            k = rk * cs + pltpu.roll(rk, dh // 2, 1) * sn
            qb = q.astype(BF16)
            st = sr_scr[hd]
            s = _mm_nt(qb, k.astype(BF16)) * dmat_ref[hd]
            o = _mm(s.astype(BF16), v) + _mm_nt(qb, st.astype(BF16)) * qdec_ref[hd]
            sr_scr[hd] = st * ret_chunk_decay[hd] + _mm_tn(v, (k * kdec_ref[hd]).astype(BF16))
            mu = jnp.mean(o, axis=-1, keepdims=True)
            oc = o - mu
            var = jnp.mean(oc * oc, axis=-1, keepdims=True)
            on = oc * lax.rsqrt(var + LN_EPS) * rnw[:, hd * dh:(hd + 1) * dh] * _silu(g)
            o_scr[rows, hw + hd * dh: hw + (hd + 1) * dh] = on.astype(BF16)

    proj_scr[:, 0:D] = _mm(o_scr[...], wout_ref[...])
    for r in range(T // ROW_BLOCK):
        rows = slice(r * ROW_BLOCK, (r + 1) * ROW_BLOCK)
        u = DEEPNORM_ALPHA * x_ref[0, rows, :] + gate1 * proj_scr[rows, 0:D]
        x1 = _ln(u) * ln1w_ref[...] + ln1b_ref[...]
        x1_ref[0, rows, :] = x1
        h2 = (_ln(x1) * (1.0 + scale2) + shift2).astype(BF16)
        h_scr[rows, :] = h2
        h2f = h2.astype(F32)
        for s in range(SUBLANES):
            h2t_ref[pl.ds(r * ROW_BLOCK * SUBLANES + s, ROW_BLOCK, stride=SUBLANES), :] = (
                h2f[:, s * LANES:(s + 1) * LANES])
    lg_ref[...] = (_mm(h_scr[...], wr_ref[...]) + br_ref[...]).T[0:N_LOGIT_ROWS, :]


def _retention_tables():
    h = np.arange(N_HEADS, dtype=np.float64)
    log_gamma = np.log(1.0 - np.exp2(-5.0 - h))
    idx = np.arange(RET_CHUNK, dtype=np.float64)
    rel = idx[:, None] - idx[None, :]
    dmat = np.where(rel >= 0, np.exp(np.maximum(rel, 0.0)[None] * log_gamma[:, None, None]), 0.0)
    qdec = np.exp((idx + 1.0)[None, :] * log_gamma[:, None])
    kdec = np.exp((RET_CHUNK - 1.0 - idx)[None, :] * log_gamma[:, None])
    cdec = np.exp(RET_CHUNK * log_gamma)
    bc = lambda t: np.broadcast_to(t[:, :, None], (N_HEADS, RET_CHUNK, D_HEAD))
    return (jnp.asarray(dmat, F32), jnp.asarray(bc(qdec), F32), jnp.asarray(bc(kdec), F32),
            tuple(float(np.float32(v)) for v in cdec))


def _mixer_call(x, positions, mod3, win, wout, hgrn_lb, hnw, rnw, ln1w, ln1b, wr, br):
    B, S, D = x.shape
    T = SEQ_TILE
    nj = S // T
    dmat, qdec, kdec, cdec = _retention_tables()
    inv_freq = np.power(ROPE_BASE, -np.arange(0, D_HEAD, 2, dtype=np.float64) / D_HEAD)
    invf = jnp.asarray(np.concatenate([inv_freq, inv_freq])[:, None], F32)
    pos3 = positions.reshape(B, 1, S)
    kern = functools.partial(_mixer_kernel, ret_chunk_decay=cdec)
    return pl.pallas_call(
        kern,
        grid=(B, nj),
        in_specs=[
            pl.BlockSpec((1, T, D), lambda b, j: (b, j, 0)),
            pl.BlockSpec((1, 1, T), lambda b, j: (b, 0, j)),
            pl.BlockSpec((1, 6, D), lambda b, j: (b, 0, 0)),
            _const_spec(win.shape), _const_spec(wout.shape), _const_spec(hgrn_lb.shape),
            _const_spec(hnw.shape), _const_spec(rnw.shape), _const_spec(ln1w.shape),
            _const_spec(ln1b.shape), _const_spec(invf.shape), _const_spec(dmat.shape),
            _const_spec(qdec.shape), _const_spec(kdec.shape), _const_spec(wr.shape),
            _const_spec(br.shape),
        ],
        out_specs=[
            pl.BlockSpec((1, T, D), lambda b, j: (b, j, 0)),
            pl.BlockSpec((T * SUBLANES, LANES), lambda b, j: (b * nj + j, 0)),
            pl.BlockSpec((N_LOGIT_ROWS, T), lambda b, j: (0, b * nj + j)),
        ],
        out_shape=[
            jax.ShapeDtypeStruct((B, S, D), F32),
            jax.ShapeDtypeStruct((B * S * SUBLANES, LANES), F32),
            jax.ShapeDtypeStruct((N_LOGIT_ROWS, B * S), F32),
        ],
        scratch_shapes=[
            pltpu.VMEM((T, D), BF16),
            pltpu.VMEM((T, 4 * N_HEADS * D_HEAD), F32),
            pltpu.VMEM((T, D), BF16),
            pltpu.VMEM((T, D_HEAD), F32),
            pltpu.VMEM((T, D_HEAD), F32),
            pltpu.VMEM((N_HEADS, D_HEAD, D_HEAD), F32),
            pltpu.VMEM((N_HEADS, D_HEAD, D_HEAD), F32),
        ],
        compiler_params=pltpu.CompilerParams(
            dimension_semantics=("arbitrary", "arbitrary"), vmem_limit_bytes=VMEM_LIMIT),
        name="mixer",
    )(x, pos3, mod3, win, wout, hgrn_lb, hnw, rnw, ln1w, ln1b, invf, dmat, qdec, kdec, wr, br)


def _row_copy(src_ref, dst_ref, src_row, dst_row, sem):
    s0 = pl.multiple_of(src_row * SUBLANES, SUBLANES)
    d0 = pl.multiple_of(dst_row * SUBLANES, SUBLANES)
    return pltpu.make_async_copy(src_ref.at[pl.ds(s0, SUBLANES)], dst_ref.at[pl.ds(d0, SUBLANES)], sem)


def _from_token_tiles(buf_ref, first_row, n_rows):
    return jnp.concatenate(
        [buf_ref[pl.ds(first_row * SUBLANES + s, n_rows, stride=SUBLANES), :] for s in range(SUBLANES)],
        axis=1)


def _start_rows(src_ref, dst_ref, idx_ref, idx_base, dst_base, n_rows, sem):
    def issue(r2, carry):
        for u in range(2):
            r = r2 * 2 + u
            _row_copy(src_ref, dst_ref, idx_ref[idx_base + r], dst_base + r, sem).start(priority=u)
        return carry
    lax.fori_loop(0, n_rows // 2, issue, 0)


def _wait_rows(src_ref, dst_ref, n_rows, sem):
    pltpu.make_async_copy(src_ref.at[pl.ds(0, n_rows * SUBLANES)], dst_ref, sem).wait()


def _bf16_bits(x):
    return lax.bitcast_convert_type(x.astype(BF16).astype(F32), jnp.uint32)


def _expert_kernel(ea_ref, eb_ref, tok_ref, nv_ref, h2t_ref,
                   wga_ref, wua_ref, wda_ref, wgb_ref, wub_ref, wdb_ref, y_ref, buf, sems):
    i = pl.program_id(0)
    R = SLOT_BLOCK
    nv = nv_ref[0]
    del ea_ref, eb_ref

    @pl.when(i == 0)
    def _():
        _start_rows(h2t_ref, buf.at[0], tok_ref, 0, 0, R, sems.at[0])

    for slot in range(2):
        @pl.when((i + 1 < nv) & ((i + 1) % 2 == slot))
        def _():
            _start_rows(h2t_ref, buf.at[slot], tok_ref, (i + 1) * R, 0, R, sems.at[slot])

    def ffn(xb, wg_ref, wu_ref, wd_ref):
        a = _mm(xb, wg_ref[0])
        u = _mm(xb, wu_ref[0])
        return _mm((_silu(a) * u).astype(BF16), wd_ref[0])

    for slot in range(2):
        @pl.when((i < nv) & (i % 2 == slot))
        def _():
            _wait_rows(h2t_ref, buf.at[slot], R, sems.at[slot])
            xb = _from_token_tiles(buf.at[slot], 0, R).astype(BF16)
            ya = ffn(xb, wga_ref, wua_ref, wda_ref)
            yb = ffn(xb, wgb_ref, wub_ref, wdb_ref)
            packed = _bf16_bits(ya) | (_bf16_bits(yb) >> 16)
            for s in range(SUBLANES):
                y_ref[pl.ds(s, R, stride=SUBLANES), :] = packed[:, s * LANES:(s + 1) * LANES]

    @pl.when(i >= nv)
    def _():
        y_ref[...] = jnp.zeros_like(y_ref)


def _expert_call(block_ea, block_eb, slot_tok, n_valid, h2t, wg, wu, wd):
    n_slots = slot_tok.shape[0]
    R = SLOT_BLOCK
    n_blocks = n_slots // R
    D, F = wg.shape[1], wg.shape[2]
    wa = lambda i, ea, eb, tok, nv: (ea[i], 0, 0)
    wb = lambda i, ea, eb, tok, nv: (eb[i], 0, 0)
    grid_spec = pltpu.PrefetchScalarGridSpec(
        num_scalar_prefetch=4,
        grid=(n_blocks,),
        in_specs=[
            pl.BlockSpec(memory_space=pl.ANY),
            pl.BlockSpec((1, D, F), wa), pl.BlockSpec((1, D, F), wa), pl.BlockSpec((1, F, D), wa),
            pl.BlockSpec((1, D, F), wb), pl.BlockSpec((1, D, F), wb), pl.BlockSpec((1, F, D), wb),
        ],
        out_specs=pl.BlockSpec((R * SUBLANES, LANES), lambda i, ea, eb, tok, nv: (i, 0)),
        scratch_shapes=[pltpu.VMEM((2, R * SUBLANES, LANES), F32), pltpu.SemaphoreType.DMA((2,))],
    )
    return pl.pallas_call(
        _expert_kernel,
        grid_spec=grid_spec,
        out_shape=jax.ShapeDtypeStruct((n_slots * SUBLANES, LANES), jnp.uint32),
        compiler_params=pltpu.CompilerParams(
            dimension_semantics=("arbitrary",), vmem_limit_bytes=VMEM_LIMIT),
        name="experts",
    )(block_ea, block_eb, slot_tok, n_valid, h2t, wg, wu, wd, wg, wu, wd)


def _final_kernel(dest_ref, yt_ref, x1_ref, g_ref, mod_ref, w_ref, b_ref, o_ref, buf, sems):
    i = pl.program_id(0)
    n_steps = pl.num_programs(0)
    TT = TOK_TILE

    def start_tile(tile, slot):
        _start_rows(yt_ref, buf.at[slot], dest_ref, tile * TT, 0, TT, sems.at[slot])

    @pl.when(i == 0)
    def _():
        start_tile(0, 0)

    for slot in range(2):
        @pl.when((i + 1 < n_steps) & ((i + 1) % 2 == slot))
        def _():
            start_tile(i + 1, slot)

    gate2 = mod_ref[0, 5:6, :]
    g = g_ref[...]
    for slot in range(2):
        @pl.when(i % 2 == slot)
        def _():
            _wait_rows(yt_ref, buf.at[slot], TT, sems.at[slot])
            for r in range(TT // ROW_BLOCK):
                rows = slice(r * ROW_BLOCK, (r + 1) * ROW_BLOCK)
                packed = _from_token_tiles(buf.at[slot], r * ROW_BLOCK, ROW_BLOCK)
                ya = lax.bitcast_convert_type(packed & jnp.uint32(0xFFFF0000), F32)
                yb = lax.bitcast_convert_type(packed << 16, F32)
                y = ya * g[rows, 0:1] + yb * g[rows, 1:2]
                u = DEEPNORM_ALPHA * x1_ref[rows, :] + gate2 * y
                o_ref[rows, :] = _ln(u) * w_ref[...] + b_ref[...]


def _final_call(dest, yt, x1, gates, mod3, w, b, seq_len):
    N, D = x1.shape
    TT = TOK_TILE
    per_seq = seq_len // TT
    grid_spec = pltpu.PrefetchScalarGridSpec(
        num_scalar_prefetch=1,
        grid=(N // TT,),
        in_specs=[
            pl.BlockSpec(memory_space=pl.ANY),
            pl.BlockSpec((TT, D), lambda i, d: (i, 0)),
            pl.BlockSpec((TT, 2), lambda i, d: (i, 0)),
            pl.BlockSpec((1, 6, D), lambda i, d: (i // per_seq, 0, 0)),
            pl.BlockSpec((1, D), lambda i, d: (0, 0)),
            pl.BlockSpec((1, D), lambda i, d: (0, 0)),
        ],
        out_specs=pl.BlockSpec((TT, D), lambda i, d: (i, 0)),
        scratch_shapes=[pltpu.VMEM((2, TT * SUBLANES, LANES), jnp.uint32), pltpu.SemaphoreType.DMA((2,))],
    )
    return pl.pallas_call(
        _final_kernel,
        grid_spec=grid_spec,
        out_shape=jax.ShapeDtypeStruct((N, D), F32),
        compiler_params=pltpu.CompilerParams(
            dimension_semantics=("arbitrary",), vmem_limit_bytes=VMEM_LIMIT),
        name="combine_ln",
    )(dest, yt, x1, gates, mod3, w, b)


def _route_kernel(lgt_ref, rf_ref, ri_ref, meta_ref, cnt_scr, base_scr, bstart_scr):
    ph = pl.program_id(0)
    t = pl.program_id(1)
    TR = ROUTE_TILE
    E = N_EXPERTS
    R = SLOT_BLOCK
    G = EXPERTS_PER_GROUP
    lg = lgt_ref[...]

    g = [lg[E + k:E + k + 1, :] for k in range(N_GROUPS)]
    gmax = jnp.maximum(jnp.maximum(g[0], g[1]), jnp.maximum(g[2], g[3]))
    gs = jnp.where(g[0] == gmax, 0.0, jnp.where(g[1] == gmax, 1.0, jnp.where(g[2] == gmax, 2.0, 3.0)))
    psum = (jnp.exp(g[0] - gmax) + jnp.exp(g[1] - gmax)) + (jnp.exp(g[2] - gmax) + jnp.exp(g[3] - gmax))
    p_star = 1.0 / psum
    esel = jnp.where(gs == 0.0, lg[0:G], jnp.where(gs == 1.0, lg[G:2 * G],
                                                   jnp.where(gs == 2.0, lg[2 * G:3 * G], lg[3 * G:4 * G])))
    sub = lax.broadcasted_iota(jnp.int32, (G, TR), 0).astype(F32)
    m1 = jnp.max(esel, axis=0, keepdims=True)
    i1 = jnp.min(jnp.where(esel == m1, sub, float(G)), axis=0, keepdims=True)
    es2 = jnp.where(sub == i1, -jnp.inf, esel)
    m2 = jnp.max(es2, axis=0, keepdims=True)
    i2 = jnp.min(jnp.where(es2 == m2, sub, float(G)), axis=0, keepdims=True)
    d = jnp.exp(m2 - m1)
    w1 = 1.0 / (1.0 + d)
    w2 = d * w1
    lo = jnp.minimum(i1, i2)
    hi = jnp.maximum(i1, i2)
    bucket = gs * PAIRS_PER_GROUP + (lo * (2 * G - 1 - lo) * 0.5 + (hi - lo - 1.0))
    first_is_lo = i1 < i2
    gate_a = p_star * jnp.where(first_is_lo, w1, w2)
    gate_b = p_star * jnp.where(first_is_lo, w2, w1)
    row = lax.broadcasted_iota(jnp.int32, (NB, TR), 0).astype(F32)
    oh = row == bucket
    cnt = oh.astype(F32)

    @pl.when(ph == 0)
    def _():
        @pl.when(t == 0)
        def _():
            cnt_scr[...] = jnp.zeros_like(cnt_scr)
        acc = cnt_scr[...]
        for j in range(TR // LANES):
            acc = acc + cnt[:, j * LANES:(j + 1) * LANES]
        cnt_scr[...] = acc

    @pl.when((ph == 1) & (t == 0))
    def _():
        counts = jnp.sum(cnt_scr[...], axis=1, keepdims=True)
        nblk = jnp.floor((counts + (R - 1)) * (1.0 / R))
        nblk_b = jnp.broadcast_to(nblk, (NB, LANES))
        rb = lax.broadcasted_iota(jnp.int32, (NB, NB), 0)
        cb = lax.broadcasted_iota(jnp.int32, (NB, NB), 1)
        bstart = _mm((cb < rb).astype(BF16), nblk_b.astype(BF16))
        bstart_scr[...] = bstart
        base_scr[...] = jnp.zeros_like(base_scr)
        k = lax.broadcasted_iota(jnp.int32, (NB, 1), 0).astype(F32)
        grp = sum((k >= float(m * PAIRS_PER_GROUP)).astype(F32) for m in range(1, N_GROUPS))
        p = k - grp * PAIRS_PER_GROUP
        pair_start = [i * (2 * G - 1 - i) // 2 for i in range(G - 1)]
        ia = sum((p >= float(s)).astype(F32) for s in pair_start[1:])
        ib = p - ia * (2 * G - 1 - ia) * 0.5 + ia + 1.0
        n_pad = meta_ref.shape[1]
        blk = lax.broadcasted_iota(jnp.int32, (NB, n_pad), 1).astype(F32)
        first = jnp.broadcast_to(bstart[:, 0:1], (NB, n_pad))
        last = jnp.broadcast_to((bstart + nblk_b)[:, 0:1], (NB, n_pad))
        member = (first <= blk) & (blk < last)
        ea = jnp.sum(jnp.where(member, grp * G + ia, 0.0), axis=0, keepdims=True)
        eb = jnp.sum(jnp.where(member, grp * G + ib, 0.0), axis=0, keepdims=True)
        meta_ref[...] = jnp.zeros_like(meta_ref)
        meta_ref[0:1, :] = ea.astype(jnp.int32)
        meta_ref[1:2, :] = eb.astype(jnp.int32)
        meta_ref[2:3, :] = jnp.max(last, axis=0, keepdims=True).astype(jnp.int32)

    @pl.when(ph == 1)
    def _():
        ri = lax.broadcasted_iota(jnp.int32, (LANES, 2 * LANES), 0)
        ci = lax.broadcasted_iota(jnp.int32, (LANES, 2 * LANES), 1)
        w = ((ci >= ri) | (ci >= LANES)).astype(BF16)
        base = base_scr[...]
        slot0 = bstart_scr[...] * R
        ds = []
        for j in range(TR // LANES):
            sl = slice(j * LANES, (j + 1) * LANES)
            blk = cnt[:, sl]
            res = _mm(blk.astype(BF16), w)
            val = slot0 + base + (res[:, 0:LANES] - blk)
            ds.append(jnp.sum(jnp.where(oh[:, sl], val, 0.0), axis=0, keepdims=True))
            base = base + res[:, LANES:2 * LANES]
        base_scr[...] = base
        ri_ref[...] = jnp.zeros_like(ri_ref)
        ri_ref[0:1, :] = jnp.concatenate(ds, axis=1).astype(jnp.int32)
        rf_ref[...] = jnp.zeros_like(rf_ref)
        rf_ref[0:1, :] = gate_a
        rf_ref[1:2, :] = gate_b


def _route_call(lgt, n_blocks):
    rows, N = lgt.shape
    TR = ROUTE_TILE
    E = NB
    n_pad = -(-n_blocks // LANES) * LANES
    return pl.pallas_call(
        _route_kernel,
        grid=(2, N // TR),
        in_specs=[pl.BlockSpec((rows, TR), lambda ph, t: (0, t))],
        out_specs=[
            pl.BlockSpec((SUBLANES, TR), lambda ph, t: (0, ph * t)),
            pl.BlockSpec((SUBLANES, TR), lambda ph, t: (0, ph * t)),
            pl.BlockSpec((SUBLANES, n_pad), lambda ph, t: (0, 0)),
        ],
        out_shape=[
            jax.ShapeDtypeStruct((SUBLANES, N), F32),
            jax.ShapeDtypeStruct((SUBLANES, N), jnp.int32),
            jax.ShapeDtypeStruct((SUBLANES, n_pad), jnp.int32),
        ],
        scratch_shapes=[pltpu.VMEM((E, LANES), F32), pltpu.VMEM((E, LANES), F32), pltpu.VMEM((E, LANES), F32)],
        compiler_params=pltpu.CompilerParams(dimension_semantics=("arbitrary", "arbitrary")),
        name="route",
    )(lgt)


def _invert_kernel(dest_ref, fill_ref, tok_ref, sem):
    n_tok = dest_ref.shape[0]
    g = pl.program_id(0)
    U = 8
    toks_per_step = n_tok // INVERT_STEPS

    @pl.when(g == 0)
    def _():
        init = pltpu.make_async_copy(fill_ref, tok_ref, sem)
        init.start()
        init.wait()

    def fill(i, carry):
        for u in range(U):
            p = g * toks_per_step + i * U + u
            tok_ref[dest_ref[p]] = p
        return carry
    lax.fori_loop(0, toks_per_step // U, fill, 0)


def _invert_call(dest, n_slots):
    n_tok = dest.shape[0]
    assert n_tok % (INVERT_STEPS * 8) == 0
    fill = jnp.arange(n_slots, dtype=jnp.int32) % n_tok
    return pl.pallas_call(
        _invert_kernel,
        grid=(INVERT_STEPS,),
        in_specs=[pl.BlockSpec(memory_space=pltpu.SMEM), pl.BlockSpec(memory_space=pl.ANY)],
        out_specs=pl.BlockSpec(memory_space=pltpu.SMEM),
        out_shape=jax.ShapeDtypeStruct((n_slots,), jnp.int32),
        scratch_shapes=[pltpu.SemaphoreType.DMA],
        compiler_params=pltpu.CompilerParams(dimension_semantics=("arbitrary",)),
        name="invert_slots",
    )(dest, fill)


def kernel(x, c, positions, w_ada, b_ada, w_in, w_out, hgrn_lb, hgrn_norm_w, ret_norm_w, post_ln1_w,
           post_ln1_b, w_rg, b_rg, w_re, b_re, w_gate, w_up, w_down, post_ln2_w, post_ln2_b):
    B, S, D = x.shape
    N = B * S
    hw = N_HEADS * D_HEAD

    mod3 = _ada_call(c, w_ada[0], b_ada[0]).reshape(B, 6, D)

    win = (w_in[0].reshape(D, 2, 4, N_HEADS, D_HEAD).transpose(1, 0, 3, 2, 4)
           .reshape(2, D, 4 * hw).astype(BF16))
    wout = w_out[0].astype(BF16)
    pad = LANES - N_EXPERTS - N_GROUPS
    wr = jnp.concatenate([w_re[0], w_rg[0], jnp.zeros((D, pad), F32)], axis=1).astype(BF16)
    br = jnp.concatenate([b_re[0], b_rg[0], jnp.zeros((pad,), F32)]).reshape(1, LANES)

    x1, h2t, lgt = _mixer_call(
        x, positions, mod3, win, wout, hgrn_lb, hgrn_norm_w[0].reshape(1, hw),
        ret_norm_w[0].reshape(1, hw), post_ln1_w[0].reshape(1, D), post_ln1_b[0].reshape(1, D), wr, br)

    n_blocks = N // SLOT_BLOCK + N_GROUPS * PAIRS_PER_GROUP
    rf, ri, meta = _route_call(lgt, n_blocks)
    dest = ri[0]
    gates = rf[0:2].T
    slot_tok = _invert_call(dest, n_blocks * SLOT_BLOCK)
    yt = _expert_call(meta[0, :n_blocks], meta[1, :n_blocks], slot_tok, meta[2, 0:1], h2t,
                      w_gate[0].astype(BF16), w_up[0].astype(BF16), w_down[0].astype(BF16))
    out = _final_call(dest, yt, x1.reshape(N, D), gates, mod3, post_ln2_w[0].reshape(1, D),
                      post_ln2_b[0].reshape(1, D), S)
    return out.reshape(B, S, D)
```

```python
import functools
import math

import numpy as np
import jax
import jax.numpy as jnp
from jax import lax
from jax.experimental import pallas as pl
from jax.experimental.pallas import tpu as pltpu

F32 = jnp.float32
BF16 = jnp.bfloat16

D_MODEL = 1024
N_HEADS = 4
D_HEAD = 128
HGRN_CHUNK = 64
RET_CHUNK = 256
ROPE_BASE = 10000.0
N_EXPERTS = 32
EXPERTS_PER_GROUP = 8
N_GROUPS = 4
D_EXPERT = 512
DEEPNORM_ALPHA = 2.0 ** 0.25
LN_EPS = 1e-5

SEQ_TILE = 512
ROW_BLOCK = 128
PAIRS_PER_GROUP = EXPERTS_PER_GROUP * (EXPERTS_PER_GROUP - 1) // 2
NB = 128
SLOT_BLOCK = 256
TOK_TILE = 512
ROUTE_TILE = 2048
N_LOGIT_ROWS = 40
INVERT_STEPS = 8
LANES = 128
SUBLANES = 8
VMEM_LIMIT = 56 * 1024 * 1024


def _mm(a, b):
    return jnp.dot(a, b, preferred_element_type=F32)


def _mm_nt(a, b):
    return lax.dot_general(a, b, (((1,), (1,)), ((), ())), preferred_element_type=F32)


def _mm_tn(a, b):
    return lax.dot_general(a, b, (((0,), (0,)), ((), ())), preferred_element_type=F32)


def _sigmoid(x):
    return 1.0 / (1.0 + jnp.exp(-x))


def _silu(x):
    return x * _sigmoid(x)


def _ln(x):
    mu = jnp.mean(x, axis=-1, keepdims=True)
    xc = x - mu
    var = jnp.mean(xc * xc, axis=-1, keepdims=True)
    return xc * lax.rsqrt(var + LN_EPS)


def _const_spec(shape):
    nd = len(shape)
    return pl.BlockSpec(shape, lambda *_: (0,) * nd, pipeline_mode=pl.Buffered(1))


def _ada_kernel(c_ref, w_ref, b_ref, o_ref):
    ca = _silu(c_ref[...])
    o_ref[...] = _mm(ca.astype(BF16), w_ref[...].astype(BF16)) + b_ref[...]


def _ada_call(c, w_ada, b_ada):
    B, D = c.shape
    n_out = w_ada.shape[1]
    return pl.pallas_call(
        _ada_kernel,
        grid=(n_out // D,),
        in_specs=[pl.BlockSpec((B, D), lambda j: (0, 0)),
                  pl.BlockSpec((D, D), lambda j: (0, j)),
                  pl.BlockSpec((1, D), lambda j: (0, j))],
        out_specs=pl.BlockSpec((B, D), lambda j: (0, j)),
        out_shape=jax.ShapeDtypeStruct((B, n_out), F32),
        name="ada_mod",
    )(c, w_ada, b_ada.reshape(1, n_out))


def _mixer_kernel(x_ref, pos_ref, mod_ref, win_ref, wout_ref, lb_ref, hnw_ref, rnw_ref,
                  ln1w_ref, ln1b_ref, invf_ref, dmat_ref, qdec_ref, kdec_ref, wr_ref, br_ref,
                  x1_ref, h2t_ref, lg_ref,
                  h_scr, proj_scr, proj2_scr, o_scr, cos_scr, sin_scr, sh_scr, sr_scr, *, ret_chunk_decay):
    T = SEQ_TILE
    D = D_MODEL
    dh = D_HEAD
    hw = N_HEADS * dh

    @pl.when(pl.program_id(1) == 0)
    def _():
        sh_scr[...] = jnp.zeros_like(sh_scr)
        sr_scr[...] = jnp.zeros_like(sr_scr)

    shift1 = mod_ref[0, 0:1, :]
    scale1 = mod_ref[0, 1:2, :]
    gate1 = mod_ref[0, 2:3, :]
    shift2 = mod_ref[0, 3:4, :]
    scale2 = mod_ref[0, 4:5, :]

    for r in range(T // ROW_BLOCK):
        rows = slice(r * ROW_BLOCK, (r + 1) * ROW_BLOCK)
        h = _ln(x_ref[0, rows, :]) * (1.0 + scale1) + shift1
        h_scr[rows, :] = h.astype(BF16)

    pos = pos_ref[0].astype(F32)
    ang_t = invf_ref[...] * pos
    cos_t = jnp.cos(ang_t)
    sin_t = jnp.sin(ang_t)
    cos_scr[...] = jnp.concatenate([cos_t, cos_t], axis=0).T
    sin_scr[...] = jnp.concatenate([-sin_t, sin_t], axis=0).T

    proj_scr[...] = _mm(h_scr[...], win_ref[0])
    proj2_scr[...] = _mm(h_scr[...], win_ref[1])

    l0 = lb_ref[0:1, :]
    l1 = lb_ref[1:2, :]
    lmax = jnp.maximum(l0, l1)
    e0 = jnp.exp(l0 - lmax)
    e1 = jnp.exp(l1 - lmax)
    lb = e0 / (e0 + e1)
    hnw = hnw_ref[...]

    C = HGRN_CHUNK
    rr = lax.broadcasted_iota(jnp.int32, (C, C), 0)
    cc = lax.broadcasted_iota(jnp.int32, (C, C), 1)
    causal = rr >= cc
    tri = causal.astype(BF16)

    def hgrn_chunk(c):
        rows = slice(c * C, (c + 1) * C)
        logfs, ks = [], []
        for hd in range(N_HEADS):
            z = proj_scr[rows, hd * 4 * dh + dh: hd * 4 * dh + 2 * dh]
            e = jnp.exp(-jnp.abs(z))
            r = 1.0 / (1.0 + e)
            er = e * r
            zp = z >= 0
            sig_pos = jnp.where(zp, r, er)
            sig_neg = jnp.where(zp, er, r)
            lbh = lb[:, hd * dh:(hd + 1) * dh]
            logfs.append(jnp.log(lbh + (1.0 - lbh) * sig_pos))
            ks.append((1.0 - lbh) * sig_neg)
        lf = jnp.concatenate(logfs, axis=1)
        hi = lf.astype(BF16)
        r1 = lf - hi.astype(F32)
        mid = r1.astype(BF16)
        lo = (r1 - mid.astype(F32)).astype(BF16)
        bsum = _mm(tri, hi) + _mm(tri, mid) + _mm(tri, lo)
        for hd in range(N_HEADS):
            base = hd * 4 * dh
            b = bsum[:, hd * dh:(hd + 1) * dh]
            b_last = b[C - 1:C, :]
            q = proj_scr[rows, base: base + dh]
            v = proj_scr[rows, base + 2 * dh: base + 3 * dh].astype(BF16)
            g = proj_scr[rows, base + 3 * dh: base + 4 * dh]
            k = ks[hd]
            q_in = (_silu(q) * jnp.exp(b)).astype(BF16)
            k_in = (k * jnp.exp(-b)).astype(BF16)
            k_dec = (k * jnp.exp(b_last - b)).astype(BF16)
            st = sh_scr[hd]
            a = jnp.where(causal, _mm_nt(q_in, k_in), 0.0)
            o = _mm(a.astype(BF16), v) + _mm_nt(q_in, st.astype(BF16))
            sh_scr[hd] = st * jnp.exp(b_last) + _mm_tn(v, k_dec)
            ms = jnp.mean(o * o, axis=-1, keepdims=True)
            on = o * lax.rsqrt(ms + LN_EPS) * hnw[:, hd * dh:(hd + 1) * dh] * _silu(g)
            o_scr[rows, hd * dh:(hd + 1) * dh] = on.astype(BF16)

    rnw = rnw_ref[...]
    RC = RET_CHUNK
    q_scale = dh ** -0.5

    def retention_unit(sub, hd):
        rows = slice(sub * RC, (sub + 1) * RC)
        cs = cos_scr[rows, :]
        sn = sin_scr[rows, :]
        base = hd * 4 * dh
        rq = proj2_scr[rows, base: base + dh]
        rk = proj2_scr[rows, base + dh: base + 2 * dh]
        v = proj2_scr[rows, base + 2 * dh: base + 3 * dh].astype(BF16)
        g = proj2_scr[rows, base + 3 * dh: base + 4 * dh]
        q = (rq * cs + pltpu.roll(rq, dh // 2, 1) * sn) * q_scale
        k = rk * cs + pltpu.roll(rk, dh // 2, 1) * sn
        qb = q.astype(BF16)
        st = sr_scr[hd]
        s = _mm_nt(qb, k.astype(BF16)) * dmat_ref[hd]
        o = _mm(s.astype(BF16), v) + _mm_nt(qb, st.astype(BF16)) * qdec_ref[hd]
        sr_scr[hd] = st * ret_chunk_decay[hd] + _mm_tn(v, (k * kdec_ref[hd]).astype(BF16))
        mu = jnp.mean(o, axis=-1, keepdims=True)
        oc = o - mu
        var = jnp.mean(oc * oc, axis=-1, keepdims=True)
        on = oc * lax.rsqrt(var + LN_EPS) * rnw[:, hd * dh:(hd + 1) * dh] * _silu(g)
        o_scr[rows, hw + hd * dh: hw + (hd + 1) * dh] = on.astype(BF16)

    n_chunks = T // C
    units = [(sub, hd) for sub in range(T // RC) for hd in range(N_HEADS)]
    for c in range(n_chunks):
        hgrn_chunk(c)
        for sub, hd in units[c * len(units) // n_chunks:(c + 1) * len(units) // n_chunks]:
            retention_unit(sub, hd)

    proj_scr[:, 0:D] = _mm(o_scr[...], wout_ref[...])
    for r in range(T // ROW_BLOCK):
        rows = slice(r * ROW_BLOCK, (r + 1) * ROW_BLOCK)
        u = DEEPNORM_ALPHA * x_ref[0, rows, :] + gate1 * proj_scr[rows, 0:D]
        x1 = _ln(u) * ln1w_ref[...] + ln1b_ref[...]
        x1_ref[0, rows, :] = x1
        h2 = (_ln(x1) * (1.0 + scale2) + shift2).astype(BF16)
        h_scr[rows, :] = h2
        h2f = h2.astype(F32)
        for s in range(SUBLANES):
            h2t_ref[pl.ds(r * ROW_BLOCK * SUBLANES + s, ROW_BLOCK, stride=SUBLANES), :] = (
                h2f[:, s * LANES:(s + 1) * LANES])
    lg_ref[...] = (_mm(h_scr[...], wr_ref[...]) + br_ref[...]).T[0:N_LOGIT_ROWS, :]


def _retention_tables():
    h = np.arange(N_HEADS, dtype=np.float64)
    log_gamma = np.log(1.0 - np.exp2(-5.0 - h))
    idx = np.arange(RET_CHUNK, dtype=np.float64)
    rel = idx[:, None] - idx[None, :]
    dmat = np.where(rel >= 0, np.exp(np.maximum(rel, 0.0)[None] * log_gamma[:, None, None]), 0.0)
    qdec = np.exp((idx + 1.0)[None, :] * log_gamma[:, None])
    kdec = np.exp((RET_CHUNK - 1.0 - idx)[None, :] * log_gamma[:, None])
    cdec = np.exp(RET_CHUNK * log_gamma)
    bc = lambda t: np.broadcast_to(t[:, :, None], (N_HEADS, RET_CHUNK, D_HEAD))
    return (jnp.asarray(dmat, F32), jnp.asarray(bc(qdec), F32), jnp.asarray(bc(kdec), F32),
            tuple(float(np.float32(v)) for v in cdec))


def _mixer_call(x, positions, mod3, win, wout, hgrn_lb, hnw, rnw, ln1w, ln1b, wr, br):
    B, S, D = x.shape
    T = SEQ_TILE
    nj = S // T
    dmat, qdec, kdec, cdec = _retention_tables()
    inv_freq = np.power(ROPE_BASE, -np.arange(0, D_HEAD, 2, dtype=np.float64) / D_HEAD)
    invf = jnp.asarray(inv_freq[:, None], F32)
    pos3 = positions.reshape(B, 1, S)
    kern = functools.partial(_mixer_kernel, ret_chunk_decay=cdec)
    return pl.pallas_call(
        kern,
        grid=(B, nj),
        in_specs=[
            pl.BlockSpec((1, T, D), lambda b, j: (b, j, 0)),
            pl.BlockSpec((1, 1, T), lambda b, j: (b, 0, j)),
            pl.BlockSpec((1, 6, D), lambda b, j: (b, 0, 0)),
            _const_spec(win.shape), _const_spec(wout.shape), _const_spec(hgrn_lb.shape),
            _const_spec(hnw.shape), _const_spec(rnw.shape), _const_spec(ln1w.shape),
            _const_spec(ln1b.shape), _const_spec(invf.shape), _const_spec(dmat.shape),
            _const_spec(qdec.shape), _const_spec(kdec.shape), _const_spec(wr.shape),
            _const_spec(br.shape),
        ],
        out_specs=[
            pl.BlockSpec((1, T, D), lambda b, j: (b, j, 0)),
            pl.BlockSpec((T * SUBLANES, LANES), lambda b, j: (b * nj + j, 0)),
            pl.BlockSpec((N_LOGIT_ROWS, T), lambda b, j: (0, b * nj + j)),
        ],
        out_shape=[
            jax.ShapeDtypeStruct((B, S, D), F32),
            jax.ShapeDtypeStruct((B * S * SUBLANES, LANES), F32),
            jax.ShapeDtypeStruct((N_LOGIT_ROWS, B * S), F32),
        ],
        scratch_shapes=[
            pltpu.VMEM((T, D), BF16),
            pltpu.VMEM((T, 4 * N_HEADS * D_HEAD), F32),
            pltpu.VMEM((T, 4 * N_HEADS * D_HEAD), F32),
            pltpu.VMEM((T, D), BF16),
            pltpu.VMEM((T, D_HEAD), F32),
            pltpu.VMEM((T, D_HEAD), F32),
            pltpu.VMEM((N_HEADS, D_HEAD, D_HEAD), F32),
            pltpu.VMEM((N_HEADS, D_HEAD, D_HEAD), F32),
        ],
        compiler_params=pltpu.CompilerParams(
            dimension_semantics=("arbitrary", "arbitrary"), vmem_limit_bytes=VMEM_LIMIT),
        name="mixer",
    )(x, pos3, mod3, win, wout, hgrn_lb, hnw, rnw, ln1w, ln1b, invf, dmat, qdec, kdec, wr, br)


def _row_copy(src_ref, dst_ref, src_row, dst_row, sem):
    s0 = pl.multiple_of(src_row * SUBLANES, SUBLANES)
    d0 = pl.multiple_of(dst_row * SUBLANES, SUBLANES)
    return pltpu.make_async_copy(src_ref.at[pl.ds(s0, SUBLANES)], dst_ref.at[pl.ds(d0, SUBLANES)], sem)


def _from_token_tiles(buf_ref, first_row, n_rows):
    return jnp.concatenate(
        [buf_ref[pl.ds(first_row * SUBLANES + s, n_rows, stride=SUBLANES), :] for s in range(SUBLANES)],
        axis=1)


def _start_rows(src_ref, dst_ref, idx_ref, idx_base, dst_base, n_rows, sem):
    def issue(r2, carry):
        for u in range(2):
            r = r2 * 2 + u
            _row_copy(src_ref, dst_ref, idx_ref[idx_base + r], dst_base + r, sem).start(priority=u)
        return carry
    lax.fori_loop(0, n_rows // 2, issue, 0)


def _wait_rows(src_ref, dst_ref, n_rows, sem):
    pltpu.make_async_copy(src_ref.at[pl.ds(0, n_rows * SUBLANES)], dst_ref, sem).wait()


def _bf16_bits(x):
    return lax.bitcast_convert_type(x.astype(BF16).astype(F32), jnp.uint32)


def _expert_kernel(ea_ref, eb_ref, tok_ref, nv_ref, h2t_ref,
                   wga_ref, wua_ref, wda_ref, wgb_ref, wub_ref, wdb_ref, y_ref, buf, sems):
    i = pl.program_id(0)
    R = SLOT_BLOCK
    nv = nv_ref[0]
    del ea_ref, eb_ref

    @pl.when(i == 0)
    def _():
        _start_rows(h2t_ref, buf.at[0], tok_ref, 0, 0, R, sems.at[0])

    for slot in range(2):
        @pl.when((i + 1 < nv) & ((i + 1) % 2 == slot))
        def _():
            _start_rows(h2t_ref, buf.at[slot], tok_ref, (i + 1) * R, 0, R, sems.at[slot])

    def ffn(xb, wg_ref, wu_ref, wd_ref):
        a = _mm(xb, wg_ref[0])
        u = _mm(xb, wu_ref[0])
        return _mm((_silu(a) * u).astype(BF16), wd_ref[0])

    for slot in range(2):
        @pl.when((i < nv) & (i % 2 == slot))
        def _():
            _wait_rows(h2t_ref, buf.at[slot], R, sems.at[slot])
            xb = _from_token_tiles(buf.at[slot], 0, R).astype(BF16)
            ya = ffn(xb, wga_ref, wua_ref, wda_ref)
            yb = ffn(xb, wgb_ref, wub_ref, wdb_ref)
            packed = _bf16_bits(ya) | (_bf16_bits(yb) >> 16)
            for s in range(SUBLANES):
                y_ref[pl.ds(s, R, stride=SUBLANES), :] = packed[:, s * LANES:(s + 1) * LANES]

    @pl.when(i >= nv)
    def _():
        y_ref[...] = jnp.zeros_like(y_ref)


def _expert_call(block_ea, block_eb, slot_tok, n_valid, h2t, wg, wu, wd):
    n_slots = slot_tok.shape[0]
    R = SLOT_BLOCK
    n_blocks = n_slots // R
    D, F = wg.shape[1], wg.shape[2]
    wa = lambda i, ea, eb, tok, nv: (ea[i], 0, 0)
    wb = lambda i, ea, eb, tok, nv: (eb[i], 0, 0)
    grid_spec = pltpu.PrefetchScalarGridSpec(
        num_scalar_prefetch=4,
        grid=(n_blocks,),
        in_specs=[
            pl.BlockSpec(memory_space=pl.ANY),
            pl.BlockSpec((1, D, F), wa), pl.BlockSpec((1, D, F), wa), pl.BlockSpec((1, F, D), wa),
            pl.BlockSpec((1, D, F), wb), pl.BlockSpec((1, D, F), wb), pl.BlockSpec((1, F, D), wb),
        ],
        out_specs=pl.BlockSpec((R * SUBLANES, LANES), lambda i, ea, eb, tok, nv: (i, 0)),
        scratch_shapes=[pltpu.VMEM((2, R * SUBLANES, LANES), F32), pltpu.SemaphoreType.DMA((2,))],
    )
    return pl.pallas_call(
        _expert_kernel,
        grid_spec=grid_spec,
        out_shape=jax.ShapeDtypeStruct((n_slots * SUBLANES, LANES), jnp.uint32),
        compiler_params=pltpu.CompilerParams(
            dimension_semantics=("arbitrary",), vmem_limit_bytes=VMEM_LIMIT),
        name="experts",
    )(block_ea, block_eb, slot_tok, n_valid, h2t, wg, wu, wd, wg, wu, wd)


def _final_kernel(dest_ref, yt_ref, x1_ref, g_ref, mod_ref, w_ref, b_ref, o_ref, buf, sems):
    i = pl.program_id(0)
    n_steps = pl.num_programs(0)
    TT = TOK_TILE

    def start_tile(tile, slot):
        _start_rows(yt_ref, buf.at[slot], dest_ref, tile * TT, 0, TT, sems.at[slot])

    @pl.when(i == 0)
    def _():
        start_tile(0, 0)

    for slot in range(2):
        @pl.when((i + 1 < n_steps) & ((i + 1) % 2 == slot))
        def _():
            start_tile(i + 1, slot)

    gate2 = mod_ref[0, 5:6, :]
    g = g_ref[...]
    for slot in range(2):
        @pl.when(i % 2 == slot)
        def _():
            _wait_rows(yt_ref, buf.at[slot], TT, sems.at[slot])
            for r in range(TT // ROW_BLOCK):
                rows = slice(r * ROW_BLOCK, (r + 1) * ROW_BLOCK)
                packed = _from_token_tiles(buf.at[slot], r * ROW_BLOCK, ROW_BLOCK)
                ya = lax.bitcast_convert_type(packed & jnp.uint32(0xFFFF0000), F32)
                yb = lax.bitcast_convert_type(packed << 16, F32)
                y = ya * g[rows, 0:1] + yb * g[rows, 1:2]
                u = DEEPNORM_ALPHA * x1_ref[rows, :] + gate2 * y
                o_ref[rows, :] = _ln(u) * w_ref[...] + b_ref[...]


def _final_call(dest, yt, x1, gates, mod3, w, b, seq_len):
    N, D = x1.shape
    TT = TOK_TILE
    per_seq = seq_len // TT
    grid_spec = pltpu.PrefetchScalarGridSpec(
        num_scalar_prefetch=1,
        grid=(N // TT,),
        in_specs=[
            pl.BlockSpec(memory_space=pl.ANY),
            pl.BlockSpec((TT, D), lambda i, d: (i, 0)),
            pl.BlockSpec((TT, 2), lambda i, d: (i, 0)),
            pl.BlockSpec((1, 6, D), lambda i, d: (i // per_seq, 0, 0)),
            pl.BlockSpec((1, D), lambda i, d: (0, 0)),
            pl.BlockSpec((1, D), lambda i, d: (0, 0)),
        ],
        out_specs=pl.BlockSpec((TT, D), lambda i, d: (i, 0)),
        scratch_shapes=[pltpu.VMEM((2, TT * SUBLANES, LANES), jnp.uint32), pltpu.SemaphoreType.DMA((2,))],
    )
    return pl.pallas_call(
        _final_kernel,
        grid_spec=grid_spec,
        out_shape=jax.ShapeDtypeStruct((N, D), F32),
        compiler_params=pltpu.CompilerParams(
            dimension_semantics=("arbitrary",), vmem_limit_bytes=VMEM_LIMIT),
        name="combine_ln",
    )(dest, yt, x1, gates, mod3, w, b)


def _route_kernel(lgt_ref, rf_ref, ri_ref, meta_ref, cnt_scr, base_scr, bstart_scr):
    ph = pl.program_id(0)
    t = pl.program_id(1)
    TR = ROUTE_TILE
    E = N_EXPERTS
    R = SLOT_BLOCK
    G = EXPERTS_PER_GROUP
    lg = lgt_ref[...]

    g = [lg[E + k:E + k + 1, :] for k in range(N_GROUPS)]
    gmax = jnp.maximum(jnp.maximum(g[0], g[1]), jnp.maximum(g[2], g[3]))
    gs = jnp.where(g[0] == gmax, 0.0, jnp.where(g[1] == gmax, 1.0, jnp.where(g[2] == gmax, 2.0, 3.0)))
    psum = (jnp.exp(g[0] - gmax) + jnp.exp(g[1] - gmax)) + (jnp.exp(g[2] - gmax) + jnp.exp(g[3] - gmax))
    p_star = 1.0 / psum
    esel = jnp.where(gs == 0.0, lg[0:G], jnp.where(gs == 1.0, lg[G:2 * G],
                                                   jnp.where(gs == 2.0, lg[2 * G:3 * G], lg[3 * G:4 * G])))
    sub = lax.broadcasted_iota(jnp.int32, (G, TR), 0).astype(F32)
    m1 = jnp.max(esel, axis=0, keepdims=True)
    i1 = jnp.min(jnp.where(esel == m1, sub, float(G)), axis=0, keepdims=True)
    es2 = jnp.where(sub == i1, -jnp.inf, esel)
    m2 = jnp.max(es2, axis=0, keepdims=True)
    i2 = jnp.min(jnp.where(es2 == m2, sub, float(G)), axis=0, keepdims=True)
    d = jnp.exp(m2 - m1)
    w1 = 1.0 / (1.0 + d)
    w2 = d * w1
    lo = jnp.minimum(i1, i2)
    hi = jnp.maximum(i1, i2)
    bucket = gs * PAIRS_PER_GROUP + (lo * (2 * G - 1 - lo) * 0.5 + (hi - lo - 1.0))
    first_is_lo = i1 < i2
    gate_a = p_star * jnp.where(first_is_lo, w1, w2)
    gate_b = p_star * jnp.where(first_is_lo, w2, w1)
    row = lax.broadcasted_iota(jnp.int32, (NB, TR), 0).astype(F32)
    oh = row == bucket
    cnt = oh.astype(F32)

    @pl.when(ph == 0)
    def _():
        @pl.when(t == 0)
        def _():
            cnt_scr[...] = jnp.zeros_like(cnt_scr)
        acc = cnt_scr[...]
        for j in range(TR // LANES):
            acc = acc + cnt[:, j * LANES:(j + 1) * LANES]
        cnt_scr[...] = acc

    @pl.when((ph == 1) & (t == 0))
    def _():
        counts = jnp.sum(cnt_scr[...], axis=1, keepdims=True)
        nblk = jnp.floor((counts + (R - 1)) * (1.0 / R))
        nblk_b = jnp.broadcast_to(nblk, (NB, LANES))
        rb = lax.broadcasted_iota(jnp.int32, (NB, NB), 0)
        cb = lax.broadcasted_iota(jnp.int32, (NB, NB), 1)
        bstart = _mm((cb < rb).astype(BF16), nblk_b.astype(BF16))
        bstart_scr[...] = bstart
        base_scr[...] = jnp.zeros_like(base_scr)
        k = lax.broadcasted_iota(jnp.int32, (NB, 1), 0).astype(F32)
        grp = sum((k >= float(m * PAIRS_PER_GROUP)).astype(F32) for m in range(1, N_GROUPS))
        p = k - grp * PAIRS_PER_GROUP
        pair_start = [i * (2 * G - 1 - i) // 2 for i in range(G - 1)]
        ia = sum((p >= float(s)).astype(F32) for s in pair_start[1:])
        ib = p - ia * (2 * G - 1 - ia) * 0.5 + ia + 1.0
        n_pad = meta_ref.shape[1]
        blk = lax.broadcasted_iota(jnp.int32, (NB, n_pad), 1).astype(F32)
        first = jnp.broadcast_to(bstart[:, 0:1], (NB, n_pad))
        last = jnp.broadcast_to((bstart + nblk_b)[:, 0:1], (NB, n_pad))
        member = (first <= blk) & (blk < last)
        ea = jnp.sum(jnp.where(member, grp * G + ia, 0.0), axis=0, keepdims=True)
        eb = jnp.sum(jnp.where(member, grp * G + ib, 0.0), axis=0, keepdims=True)
        meta_ref[...] = jnp.zeros_like(meta_ref)
        meta_ref[0:1, :] = ea.astype(jnp.int32)
        meta_ref[1:2, :] = eb.astype(jnp.int32)
        meta_ref[2:3, :] = jnp.max(last, axis=0, keepdims=True).astype(jnp.int32)

    @pl.when(ph == 1)
    def _():
        ri = lax.broadcasted_iota(jnp.int32, (LANES, 2 * LANES), 0)
        ci = lax.broadcasted_iota(jnp.int32, (LANES, 2 * LANES), 1)
        w = ((ci >= ri) | (ci >= LANES)).astype(BF16)
        base = base_scr[...]
        slot0 = bstart_scr[...] * R
        ds = []
        for j in range(TR // LANES):
            sl = slice(j * LANES, (j + 1) * LANES)
            blk = cnt[:, sl]
            res = _mm(blk.astype(BF16), w)
            val = slot0 + base + (res[:, 0:LANES] - blk)
            ds.append(jnp.sum(jnp.where(oh[:, sl], val, 0.0), axis=0, keepdims=True))
            base = base + res[:, LANES:2 * LANES]
        base_scr[...] = base
        ri_ref[...] = jnp.zeros_like(ri_ref)
        ri_ref[0:1, :] = jnp.concatenate(ds, axis=1).astype(jnp.int32)
        rf_ref[...] = jnp.zeros_like(rf_ref)
        rf_ref[0:1, :] = gate_a
        rf_ref[1:2, :] = gate_b


def _route_call(lgt, n_blocks):
    rows, N = lgt.shape
    TR = ROUTE_TILE
    E = NB
    n_pad = -(-n_blocks // LANES) * LANES
    return pl.pallas_call(
        _route_kernel,
        grid=(2, N // TR),
        in_specs=[pl.BlockSpec((rows, TR), lambda ph, t: (0, t))],
        out_specs=[
            pl.BlockSpec((SUBLANES, TR), lambda ph, t: (0, ph * t)),
            pl.BlockSpec((SUBLANES, TR), lambda ph, t: (0, ph * t)),
            pl.BlockSpec((SUBLANES, n_pad), lambda ph, t: (0, 0)),
        ],
        out_shape=[
            jax.ShapeDtypeStruct((SUBLANES, N), F32),
            jax.ShapeDtypeStruct((SUBLANES, N), jnp.int32),
            jax.ShapeDtypeStruct((SUBLANES, n_pad), jnp.int32),
        ],
        scratch_shapes=[pltpu.VMEM((E, LANES), F32), pltpu.VMEM((E, LANES), F32), pltpu.VMEM((E, LANES), F32)],
        compiler_params=pltpu.CompilerParams(dimension_semantics=("arbitrary", "arbitrary")),
        name="route",
    )(lgt)


def _invert_kernel(dest_ref, fill_ref, tok_ref, sem):
    n_tok = dest_ref.shape[0]
    g = pl.program_id(0)
    U = 8
    toks_per_step = n_tok // INVERT_STEPS

    @pl.when(g == 0)
    def _():
        init = pltpu.make_async_copy(fill_ref, tok_ref, sem)
        init.start()
        init.wait()

    def fill(i, carry):
        for u in range(U):
            p = g * toks_per_step + i * U + u
            tok_ref[dest_ref[p]] = p
        return carry
    lax.fori_loop(0, toks_per_step // U, fill, 0)


def _invert_call(dest, n_slots):
    n_tok = dest.shape[0]
    assert n_tok % (INVERT_STEPS * 8) == 0
    fill = jnp.arange(n_slots, dtype=jnp.int32) % n_tok
    return pl.pallas_call(
        _invert_kernel,
        grid=(INVERT_STEPS,),
        in_specs=[pl.BlockSpec(memory_space=pltpu.SMEM), pl.BlockSpec(memory_space=pl.ANY)],
        out_specs=pl.BlockSpec(memory_space=pltpu.SMEM),
        out_shape=jax.ShapeDtypeStruct((n_slots,), jnp.int32),
        scratch_shapes=[pltpu.SemaphoreType.DMA],
        compiler_params=pltpu.CompilerParams(dimension_semantics=("arbitrary",)),
        name="invert_slots",
    )(dest, fill)


def kernel(x, c, positions, w_ada, b_ada, w_in, w_out, hgrn_lb, hgrn_norm_w, ret_norm_w, post_ln1_w,
           post_ln1_b, w_rg, b_rg, w_re, b_re, w_gate, w_up, w_down, post_ln2_w, post_ln2_b):
    B, S, D = x.shape
    N = B * S
    hw = N_HEADS * D_HEAD

    mod3 = _ada_call(c, w_ada[0], b_ada[0]).reshape(B, 6, D)

    win = (w_in[0].reshape(D, 2, 4, N_HEADS, D_HEAD).transpose(1, 0, 3, 2, 4)
           .reshape(2, D, 4 * hw).astype(BF16))
    wout = w_out[0].astype(BF16)
    pad = LANES - N_EXPERTS - N_GROUPS
    wr = jnp.concatenate([w_re[0], w_rg[0], jnp.zeros((D, pad), F32)], axis=1).astype(BF16)
    br = jnp.concatenate([b_re[0], b_rg[0], jnp.zeros((pad,), F32)]).reshape(1, LANES)

    x1, h2t, lgt = _mixer_call(
        x, positions, mod3, win, wout, hgrn_lb, hgrn_norm_w[0].reshape(1, hw),
        ret_norm_w[0].reshape(1, hw), post_ln1_w[0].reshape(1, D), post_ln1_b[0].reshape(1, D), wr, br)

    n_blocks = N // SLOT_BLOCK + N_GROUPS * PAIRS_PER_GROUP
    rf, ri, meta = _route_call(lgt, n_blocks)
    dest = ri[0]
    gates = rf[0:2].T
    slot_tok = _invert_call(dest, n_blocks * SLOT_BLOCK)
    yt = _expert_call(meta[0, :n_blocks], meta[1, :n_blocks], slot_tok, meta[2, 0:1], h2t,
                      w_gate[0].astype(BF16), w_up[0].astype(BF16), w_down[0].astype(BF16))
    out = _final_call(dest, yt, x1.reshape(N, D), gates, mod3, post_ln2_w[0].reshape(1, D),
                      post_ln2_b[0].reshape(1, D), S)
    return out.reshape(B, S, D)
```

```python
import functools
import math

import numpy as np
import jax
import jax.numpy as jnp
from jax import lax
from jax.experimental import pallas as pl
from jax.experimental.pallas import tpu as pltpu

F32 = jnp.float32
BF16 = jnp.bfloat16

D_MODEL = 1024
N_HEADS = 4
D_HEAD = 128
HGRN_CHUNK = 64
RET_CHUNK = 256
ROPE_BASE = 10000.0
N_EXPERTS = 32
EXPERTS_PER_GROUP = 8
N_GROUPS = 4
D_EXPERT = 512
DEEPNORM_ALPHA = 2.0 ** 0.25
LN_EPS = 1e-5

SEQ_TILE = 512
ROW_BLOCK = 128
PAIRS_PER_GROUP = EXPERTS_PER_GROUP * (EXPERTS_PER_GROUP - 1) // 2
NB = 128
SLOT_BLOCK = 256
GATHER_RING = 3
TOK_TILE = 1024
ROUTE_TILE = 2048
N_LOGIT_ROWS = 40
INVERT_STEPS = 8
LANES = 128
SUBLANES = 8
VMEM_LIMIT = 56 * 1024 * 1024


def _mm(a, b):
    return jnp.dot(a, b, preferred_element_type=F32)


def _mm_nt(a, b):
    return lax.dot_general(a, b, (((1,), (1,)), ((), ())), preferred_element_type=F32)


def _mm_tn(a, b):
    return lax.dot_general(a, b, (((0,), (0,)), ((), ())), preferred_element_type=F32)


def _sigmoid(x):
    return 1.0 / (1.0 + jnp.exp(-x))


def _silu(x):
    return x * _sigmoid(x)


def _ln(x):
    mu = jnp.mean(x, axis=-1, keepdims=True)
    xc = x - mu
    var = jnp.mean(xc * xc, axis=-1, keepdims=True)
    return xc * lax.rsqrt(var + LN_EPS)


def _const_spec(shape):
    nd = len(shape)
    return pl.BlockSpec(shape, lambda *_: (0,) * nd, pipeline_mode=pl.Buffered(1))


def _ada_kernel(c_ref, w_ref, b_ref, o_ref):
    ca = _silu(c_ref[...])
    o_ref[...] = _mm(ca.astype(BF16), w_ref[...].astype(BF16)) + b_ref[...]


def _ada_call(c, w_ada, b_ada):
    B, D = c.shape
    n_out = w_ada.shape[1]
    return pl.pallas_call(
        _ada_kernel,
        grid=(n_out // D,),
        in_specs=[pl.BlockSpec((B, D), lambda j: (0, 0)),
                  pl.BlockSpec((D, D), lambda j: (0, j)),
                  pl.BlockSpec((1, D), lambda j: (0, j))],
        out_specs=pl.BlockSpec((B, D), lambda j: (0, j)),
        out_shape=jax.ShapeDtypeStruct((B, n_out), F32),
        name="ada_mod",
    )(c, w_ada, b_ada.reshape(1, n_out))


def _mixer_kernel(x_ref, pos_ref, mod_ref, win_ref, wout_ref, lb_ref, hnw_ref, rnw_ref,
                  ln1w_ref, ln1b_ref, invf_ref, dmat_ref, qdec_ref, kdec_ref, wr_ref, br_ref,
                  x1_ref, h2t_ref, lg_ref,
                  h_scr, proj_scr, proj2_scr, o_scr, cos_scr, sin_scr, sh_scr, sr_scr, *, ret_chunk_decay):
    T = SEQ_TILE
    D = D_MODEL
    dh = D_HEAD
    hw = N_HEADS * dh

    @pl.when(pl.program_id(1) == 0)
    def _():
        sh_scr[...] = jnp.zeros_like(sh_scr)
        sr_scr[...] = jnp.zeros_like(sr_scr)

    shift1 = mod_ref[0, 0:1, :]
    scale1 = mod_ref[0, 1:2, :]
    gate1 = mod_ref[0, 2:3, :]
    shift2 = mod_ref[0, 3:4, :]
    scale2 = mod_ref[0, 4:5, :]

    for r in range(T // ROW_BLOCK):
        rows = slice(r * ROW_BLOCK, (r + 1) * ROW_BLOCK)
        h = _ln(x_ref[0, rows, :]) * (1.0 + scale1) + shift1
        h_scr[rows, :] = h.astype(BF16)

    pos = pos_ref[0].astype(F32)
    ang_t = invf_ref[...] * pos
    cos_t = jnp.cos(ang_t)
    sin_t = jnp.sin(ang_t)
    cos_scr[...] = jnp.concatenate([cos_t, cos_t], axis=0).T
    sin_scr[...] = jnp.concatenate([-sin_t, sin_t], axis=0).T

    proj_scr[...] = _mm(h_scr[...], win_ref[0])
    proj2_scr[...] = _mm(h_scr[...], win_ref[1])

    l0 = lb_ref[0:1, :]
    l1 = lb_ref[1:2, :]
    lmax = jnp.maximum(l0, l1)
    e0 = jnp.exp(l0 - lmax)
    e1 = jnp.exp(l1 - lmax)
    lb = e0 / (e0 + e1)
    hnw = hnw_ref[...]

    C = HGRN_CHUNK
    rr = lax.broadcasted_iota(jnp.int32, (C, C), 0)
    cc = lax.broadcasted_iota(jnp.int32, (C, C), 1)
    causal = rr >= cc
    tri = causal.astype(BF16)

    def hgrn_chunk(c):
        rows = slice(c * C, (c + 1) * C)
        logfs, ks = [], []
        for hd in range(N_HEADS):
            z = proj_scr[rows, hd * 4 * dh + dh: hd * 4 * dh + 2 * dh]
            e = jnp.exp(-jnp.abs(z))
            r = 1.0 / (1.0 + e)
            er = e * r
            zp = z >= 0
            sig_pos = jnp.where(zp, r, er)
            sig_neg = jnp.where(zp, er, r)
            lbh = lb[:, hd * dh:(hd + 1) * dh]
            logfs.append(jnp.log(lbh + (1.0 - lbh) * sig_pos))
            ks.append((1.0 - lbh) * sig_neg)
        lf = jnp.concatenate(logfs, axis=1)
        hi = lf.astype(BF16)
        r1 = lf - hi.astype(F32)
        mid = r1.astype(BF16)
        lo = (r1 - mid.astype(F32)).astype(BF16)
        bsum = _mm(tri, hi) + _mm(tri, mid) + _mm(tri, lo)
        for hd in range(N_HEADS):
            base = hd * 4 * dh
            b = bsum[:, hd * dh:(hd + 1) * dh]
            b_last = b[C - 1:C, :]
            q = proj_scr[rows, base: base + dh]
            v = proj_scr[rows, base + 2 * dh: base + 3 * dh].astype(BF16)
            g = proj_scr[rows, base + 3 * dh: base + 4 * dh]
            k = ks[hd]
            q_in = (_silu(q) * jnp.exp(b)).astype(BF16)
            k_in = (k * jnp.exp(-b)).astype(BF16)
            k_dec = (k * jnp.exp(b_last - b)).astype(BF16)
            st = sh_scr[hd]
            a = jnp.where(causal, _mm_nt(q_in, k_in), 0.0)
            o = _mm(a.astype(BF16), v) + _mm_nt(q_in, st.astype(BF16))
            sh_scr[hd] = st * jnp.exp(b_last) + _mm_tn(v, k_dec)
            ms = jnp.mean(o * o, axis=-1, keepdims=True)
            on = o * lax.rsqrt(ms + LN_EPS) * hnw[:, hd * dh:(hd + 1) * dh] * _silu(g)
            o_scr[rows, hd * dh:(hd + 1) * dh] = on.astype(BF16)

    rnw = rnw_ref[...]
    RC = RET_CHUNK
    q_scale = dh ** -0.5

    def retention_unit(sub, hd):
        rows = slice(sub * RC, (sub + 1) * RC)
        cs = cos_scr[rows, :]
        sn = sin_scr[rows, :]
        base = hd * 4 * dh
        rq = proj2_scr[rows, base: base + dh]
        rk = proj2_scr[rows, base + dh: base + 2 * dh]
        v = proj2_scr[rows, base + 2 * dh: base + 3 * dh].astype(BF16)
        g = proj2_scr[rows, base + 3 * dh: base + 4 * dh]
        q = (rq * cs + pltpu.roll(rq, dh // 2, 1) * sn) * q_scale
        k = rk * cs + pltpu.roll(rk, dh // 2, 1) * sn
        qb = q.astype(BF16)
        st = sr_scr[hd]
        s = _mm_nt(qb, k.astype(BF16)) * dmat_ref[hd]
        o = _mm(s.astype(BF16), v) + _mm_nt(qb, st.astype(BF16)) * qdec_ref[hd]
        sr_scr[hd] = st * ret_chunk_decay[hd] + _mm_tn(v, (k * kdec_ref[hd]).astype(BF16))
        mu = jnp.mean(o, axis=-1, keepdims=True)
        oc = o - mu
        var = jnp.mean(oc * oc, axis=-1, keepdims=True)
        on = oc * lax.rsqrt(var + LN_EPS) * rnw[:, hd * dh:(hd + 1) * dh] * _silu(g)
        o_scr[rows, hw + hd * dh: hw + (hd + 1) * dh] = on.astype(BF16)

    n_chunks = T // C
    units = [(sub, hd) for sub in range(T // RC) for hd in range(N_HEADS)]
    for c in range(n_chunks):
        hgrn_chunk(c)
        for sub, hd in units[c * len(units) // n_chunks:(c + 1) * len(units) // n_chunks]:
            retention_unit(sub, hd)

    proj_scr[:, 0:D] = _mm(o_scr[...], wout_ref[...])
    for r in range(T // ROW_BLOCK):
        rows = slice(r * ROW_BLOCK, (r + 1) * ROW_BLOCK)
        u = DEEPNORM_ALPHA * x_ref[0, rows, :] + gate1 * proj_scr[rows, 0:D]
        x1 = _ln(u) * ln1w_ref[...] + ln1b_ref[...]
        x1_ref[0, rows, :] = x1
        h2 = (_ln(x1) * (1.0 + scale2) + shift2).astype(BF16)
        h_scr[rows, :] = h2
        h2f = h2.astype(F32)
        for s in range(SUBLANES):
            h2t_ref[pl.ds(r * ROW_BLOCK * SUBLANES + s, ROW_BLOCK, stride=SUBLANES), :] = (
                h2f[:, s * LANES:(s + 1) * LANES])
    lg_ref[...] = (_mm(h_scr[...], wr_ref[...]) + br_ref[...]).T[0:N_LOGIT_ROWS, :]


def _retention_tables():
    h = np.arange(N_HEADS, dtype=np.float64)
    log_gamma = np.log(1.0 - np.exp2(-5.0 - h))
    idx = np.arange(RET_CHUNK, dtype=np.float64)
    rel = idx[:, None] - idx[None, :]
    dmat = np.where(rel >= 0, np.exp(np.maximum(rel, 0.0)[None] * log_gamma[:, None, None]), 0.0)
    qdec = np.exp((idx + 1.0)[None, :] * log_gamma[:, None])
    kdec = np.exp((RET_CHUNK - 1.0 - idx)[None, :] * log_gamma[:, None])
    cdec = np.exp(RET_CHUNK * log_gamma)
    bc = lambda t: np.broadcast_to(t[:, :, None], (N_HEADS, RET_CHUNK, D_HEAD))
    return (jnp.asarray(dmat, F32), jnp.asarray(bc(qdec), F32), jnp.asarray(bc(kdec), F32),
            tuple(float(np.float32(v)) for v in cdec))


def _mixer_call(x, positions, mod3, win, wout, hgrn_lb, hnw, rnw, ln1w, ln1b, wr, br):
    B, S, D = x.shape
    T = SEQ_TILE
    nj = S // T
    dmat, qdec, kdec, cdec = _retention_tables()
    inv_freq = np.power(ROPE_BASE, -np.arange(0, D_HEAD, 2, dtype=np.float64) / D_HEAD)
    invf = jnp.asarray(inv_freq[:, None], F32)
    pos3 = positions.reshape(B, 1, S)
    kern = functools.partial(_mixer_kernel, ret_chunk_decay=cdec)
    return pl.pallas_call(
        kern,
        grid=(B, nj),
        in_specs=[
            pl.BlockSpec((1, T, D), lambda b, j: (b, j, 0)),
            pl.BlockSpec((1, 1, T), lambda b, j: (b, 0, j)),
            pl.BlockSpec((1, 6, D), lambda b, j: (b, 0, 0)),
            _const_spec(win.shape), _const_spec(wout.shape), _const_spec(hgrn_lb.shape),
            _const_spec(hnw.shape), _const_spec(rnw.shape), _const_spec(ln1w.shape),
            _const_spec(ln1b.shape), _const_spec(invf.shape), _const_spec(dmat.shape),
            _const_spec(qdec.shape), _const_spec(kdec.shape), _const_spec(wr.shape),
            _const_spec(br.shape),
        ],
        out_specs=[
            pl.BlockSpec((1, T, D), lambda b, j: (b, j, 0)),
            pl.BlockSpec((T * SUBLANES, LANES), lambda b, j: (b * nj + j, 0)),
            pl.BlockSpec((N_LOGIT_ROWS, T), lambda b, j: (0, b * nj + j)),
        ],
        out_shape=[
            jax.ShapeDtypeStruct((B, S, D), F32),
            jax.ShapeDtypeStruct((B * S * SUBLANES, LANES), F32),
            jax.ShapeDtypeStruct((N_LOGIT_ROWS, B * S), F32),
        ],
        scratch_shapes=[
            pltpu.VMEM((T, D), BF16),
            pltpu.VMEM((T, 4 * N_HEADS * D_HEAD), F32),
            pltpu.VMEM((T, 4 * N_HEADS * D_HEAD), F32),
            pltpu.VMEM((T, D), BF16),
            pltpu.VMEM((T, D_HEAD), F32),
            pltpu.VMEM((T, D_HEAD), F32),
            pltpu.VMEM((N_HEADS, D_HEAD, D_HEAD), F32),
            pltpu.VMEM((N_HEADS, D_HEAD, D_HEAD), F32),
        ],
        compiler_params=pltpu.CompilerParams(
            dimension_semantics=("arbitrary", "arbitrary"), vmem_limit_bytes=VMEM_LIMIT),
        name="mixer",
    )(x, pos3, mod3, win, wout, hgrn_lb, hnw, rnw, ln1w, ln1b, invf, dmat, qdec, kdec, wr, br)


def _row_copy(src_ref, dst_ref, src_row, dst_row, sem):
    s0 = pl.multiple_of(src_row * SUBLANES, SUBLANES)
    d0 = pl.multiple_of(dst_row * SUBLANES, SUBLANES)
    return pltpu.make_async_copy(src_ref.at[pl.ds(s0, SUBLANES)], dst_ref.at[pl.ds(d0, SUBLANES)], sem)


def _from_token_tiles(buf_ref, first_row, n_rows):
    return jnp.concatenate(
        [buf_ref[pl.ds(first_row * SUBLANES + s, n_rows, stride=SUBLANES), :] for s in range(SUBLANES)],
        axis=1)


def _start_rows(src_ref, dst_ref, idx_ref, idx_base, dst_base, n_rows, sem):
    def issue(r2, carry):
        for u in range(2):
            r = r2 * 2 + u
            _row_copy(src_ref, dst_ref, idx_ref[idx_base + r], dst_base + r, sem).start(priority=u)
        return carry
    lax.fori_loop(0, n_rows // 2, issue, 0)


def _wait_rows(src_ref, dst_ref, n_rows, sem):
    pltpu.make_async_copy(src_ref.at[pl.ds(0, n_rows * SUBLANES)], dst_ref, sem).wait()


def _bf16_bits(x):
    return lax.bitcast_convert_type(x.astype(BF16).astype(F32), jnp.uint32)


def _expert_kernel(ea_ref, eb_ref, tok_ref, nv_ref, h2t_ref,
                   wga_ref, wua_ref, wda_ref, wgb_ref, wub_ref, wdb_ref, y_ref, buf, sems):
    i = pl.program_id(0)
    R = SLOT_BLOCK
    nv = nv_ref[0]
    del ea_ref, eb_ref

    ahead = GATHER_RING - 1

    @pl.when(i == 0)
    def _():
        for blk in range(ahead):
            @pl.when(blk < nv)
            def _():
                _start_rows(h2t_ref, buf.at[blk], tok_ref, blk * R, 0, R, sems.at[blk])

    for slot in range(GATHER_RING):
        @pl.when((i + ahead < nv) & ((i + ahead) % GATHER_RING == slot))
        def _():
            _start_rows(h2t_ref, buf.at[slot], tok_ref, (i + ahead) * R, 0, R, sems.at[slot])

    def ffn(xb, wg_ref, wu_ref, wd_ref):
        a = _mm(xb, wg_ref[0])
        u = _mm(xb, wu_ref[0])
        return _mm((_silu(a) * u).astype(BF16), wd_ref[0])

    for slot in range(GATHER_RING):
        @pl.when((i < nv) & (i % GATHER_RING == slot))
        def _():
            _wait_rows(h2t_ref, buf.at[slot], R, sems.at[slot])
            xb = _from_token_tiles(buf.at[slot], 0, R).astype(BF16)
            ya = ffn(xb, wga_ref, wua_ref, wda_ref)
            yb = ffn(xb, wgb_ref, wub_ref, wdb_ref)
            packed = _bf16_bits(ya) | (_bf16_bits(yb) >> 16)
            for s in range(SUBLANES):
                y_ref[pl.ds(s, R, stride=SUBLANES), :] = packed[:, s * LANES:(s + 1) * LANES]

    @pl.when(i >= nv)
    def _():
        y_ref[...] = jnp.zeros_like(y_ref)


def _expert_call(block_ea, block_eb, slot_tok, n_valid, h2t, wg, wu, wd):
    n_slots = slot_tok.shape[0]
    R = SLOT_BLOCK
    n_blocks = n_slots // R
    D, F = wg.shape[1], wg.shape[2]
    wa = lambda i, ea, eb, tok, nv: (ea[i], 0, 0)
    wb = lambda i, ea, eb, tok, nv: (eb[i], 0, 0)
    grid_spec = pltpu.PrefetchScalarGridSpec(
        num_scalar_prefetch=4,
        grid=(n_blocks,),
        in_specs=[
            pl.BlockSpec(memory_space=pl.ANY),
            pl.BlockSpec((1, D, F), wa), pl.BlockSpec((1, D, F), wa), pl.BlockSpec((1, F, D), wa),
            pl.BlockSpec((1, D, F), wb), pl.BlockSpec((1, D, F), wb), pl.BlockSpec((1, F, D), wb),
        ],
        out_specs=pl.BlockSpec((R * SUBLANES, LANES), lambda i, ea, eb, tok, nv: (i, 0)),
        scratch_shapes=[pltpu.VMEM((GATHER_RING, R * SUBLANES, LANES), F32),
                        pltpu.SemaphoreType.DMA((GATHER_RING,))],
    )
    return pl.pallas_call(
        _expert_kernel,
        grid_spec=grid_spec,
        out_shape=jax.ShapeDtypeStruct((n_slots * SUBLANES, LANES), jnp.uint32),
        compiler_params=pltpu.CompilerParams(
            dimension_semantics=("arbitrary",), vmem_limit_bytes=VMEM_LIMIT),
        name="experts",
    )(block_ea, block_eb, slot_tok, n_valid, h2t, wg, wu, wd, wg, wu, wd)


def _final_kernel(dest_ref, yt_ref, x1_ref, g_ref, mod_ref, w_ref, b_ref, o_ref, buf, sems):
    i = pl.program_id(0)
    n_steps = pl.num_programs(0)
    TT = TOK_TILE

    def start_tile(tile, slot):
        _start_rows(yt_ref, buf.at[slot], dest_ref, tile * TT, 0, TT, sems.at[slot])

    @pl.when(i == 0)
    def _():
        start_tile(0, 0)

    for slot in range(2):
        @pl.when((i + 1 < n_steps) & ((i + 1) % 2 == slot))
        def _():
            start_tile(i + 1, slot)

    gate2 = mod_ref[0, 5:6, :]
    g = g_ref[...]
    for slot in range(2):
        @pl.when(i % 2 == slot)
        def _():
            _wait_rows(yt_ref, buf.at[slot], TT, sems.at[slot])
            for r in range(TT // ROW_BLOCK):
                rows = slice(r * ROW_BLOCK, (r + 1) * ROW_BLOCK)
                packed = _from_token_tiles(buf.at[slot], r * ROW_BLOCK, ROW_BLOCK)
                ya = lax.bitcast_convert_type(packed & jnp.uint32(0xFFFF0000), F32)
                yb = lax.bitcast_convert_type(packed << 16, F32)
                y = ya * g[rows, 0:1] + yb * g[rows, 1:2]
                u = DEEPNORM_ALPHA * x1_ref[rows, :] + gate2 * y
                o_ref[rows, :] = _ln(u) * w_ref[...] + b_ref[...]


def _final_call(dest, yt, x1, gates, mod3, w, b, seq_len):
    N, D = x1.shape
    TT = TOK_TILE
    per_seq = seq_len // TT
    grid_spec = pltpu.PrefetchScalarGridSpec(
        num_scalar_prefetch=1,
        grid=(N // TT,),
        in_specs=[
            pl.BlockSpec(memory_space=pl.ANY),
            pl.BlockSpec((TT, D), lambda i, d: (i, 0)),
            pl.BlockSpec((TT, 2), lambda i, d: (i, 0)),
            pl.BlockSpec((1, 6, D), lambda i, d: (i // per_seq, 0, 0)),
            pl.BlockSpec((1, D), lambda i, d: (0, 0)),
            pl.BlockSpec((1, D), lambda i, d: (0, 0)),
        ],
        out_specs=pl.BlockSpec((TT, D), lambda i, d: (i, 0)),
        scratch_shapes=[pltpu.VMEM((2, TT * SUBLANES, LANES), jnp.uint32), pltpu.SemaphoreType.DMA((2,))],
    )
    return pl.pallas_call(
        _final_kernel,
        grid_spec=grid_spec,
        out_shape=jax.ShapeDtypeStruct((N, D), F32),
        compiler_params=pltpu.CompilerParams(
            dimension_semantics=("arbitrary",), vmem_limit_bytes=VMEM_LIMIT),
        name="combine_ln",
    )(dest, yt, x1, gates, mod3, w, b)


def _route_kernel(lgt_ref, rf_ref, ri_ref, meta_ref, cnt_scr, base_scr, bstart_scr):
    ph = pl.program_id(0)
    t = pl.program_id(1)
    TR = ROUTE_TILE
    E = N_EXPERTS
    R = SLOT_BLOCK
    G = EXPERTS_PER_GROUP
    lg = lgt_ref[...]

    g = [lg[E + k:E + k + 1, :] for k in range(N_GROUPS)]
    gmax = jnp.maximum(jnp.maximum(g[0], g[1]), jnp.maximum(g[2], g[3]))
    gs = jnp.where(g[0] == gmax, 0.0, jnp.where(g[1] == gmax, 1.0, jnp.where(g[2] == gmax, 2.0, 3.0)))
    psum = (jnp.exp(g[0] - gmax) + jnp.exp(g[1] - gmax)) + (jnp.exp(g[2] - gmax) + jnp.exp(g[3] - gmax))
    p_star = 1.0 / psum
    esel = jnp.where(gs == 0.0, lg[0:G], jnp.where(gs == 1.0, lg[G:2 * G],
                                                   jnp.where(gs == 2.0, lg[2 * G:3 * G], lg[3 * G:4 * G])))
    sub = lax.broadcasted_iota(jnp.int32, (G, TR), 0).astype(F32)
    m1 = jnp.max(esel, axis=0, keepdims=True)
    i1 = jnp.min(jnp.where(esel == m1, sub, float(G)), axis=0, keepdims=True)
    es2 = jnp.where(sub == i1, -jnp.inf, esel)
    m2 = jnp.max(es2, axis=0, keepdims=True)
    i2 = jnp.min(jnp.where(es2 == m2, sub, float(G)), axis=0, keepdims=True)
    d = jnp.exp(m2 - m1)
    w1 = 1.0 / (1.0 + d)
    w2 = d * w1
    lo = jnp.minimum(i1, i2)
    hi = jnp.maximum(i1, i2)
    bucket = gs * PAIRS_PER_GROUP + (lo * (2 * G - 1 - lo) * 0.5 + (hi - lo - 1.0))
    first_is_lo = i1 < i2
    gate_a = p_star * jnp.where(first_is_lo, w1, w2)
    gate_b = p_star * jnp.where(first_is_lo, w2, w1)
    row = lax.broadcasted_iota(jnp.int32, (NB, TR), 0).astype(F32)
    oh = row == bucket
    cnt = oh.astype(F32)

    @pl.when(ph == 0)
    def _():
        @pl.when(t == 0)
        def _():
            cnt_scr[...] = jnp.zeros_like(cnt_scr)
        acc = cnt_scr[...]
        for j in range(TR // LANES):
            acc = acc + cnt[:, j * LANES:(j + 1) * LANES]
        cnt_scr[...] = acc

    @pl.when((ph == 1) & (t == 0))
    def _():
        counts = jnp.sum(cnt_scr[...], axis=1, keepdims=True)
        nblk = jnp.floor((counts + (R - 1)) * (1.0 / R))
        nblk_b = jnp.broadcast_to(nblk, (NB, LANES))
        rb = lax.broadcasted_iota(jnp.int32, (NB, NB), 0)
        cb = lax.broadcasted_iota(jnp.int32, (NB, NB), 1)
        bstart = _mm((cb < rb).astype(BF16), nblk_b.astype(BF16))
        bstart_scr[...] = bstart
        base_scr[...] = jnp.zeros_like(base_scr)
        k = lax.broadcasted_iota(jnp.int32, (NB, 1), 0).astype(F32)
        grp = sum((k >= float(m * PAIRS_PER_GROUP)).astype(F32) for m in range(1, N_GROUPS))
        p = k - grp * PAIRS_PER_GROUP
        pair_start = [i * (2 * G - 1 - i) // 2 for i in range(G - 1)]
        ia = sum((p >= float(s)).astype(F32) for s in pair_start[1:])
        ib = p - ia * (2 * G - 1 - ia) * 0.5 + ia + 1.0
        n_pad = meta_ref.shape[1]
        blk = lax.broadcasted_iota(jnp.int32, (NB, n_pad), 1).astype(F32)
        first = jnp.broadcast_to(bstart[:, 0:1], (NB, n_pad))
        last = jnp.broadcast_to((bstart + nblk_b)[:, 0:1], (NB, n_pad))
        member = (first <= blk) & (blk < last)
        ea = jnp.sum(jnp.where(member, grp * G + ia, 0.0), axis=0, keepdims=True)
        eb = jnp.sum(jnp.where(member, grp * G + ib, 0.0), axis=0, keepdims=True)
        meta_ref[...] = jnp.zeros_like(meta_ref)
        meta_ref[0:1, :] = ea.astype(jnp.int32)
        meta_ref[1:2, :] = eb.astype(jnp.int32)
        meta_ref[2:3, :] = jnp.max(last, axis=0, keepdims=True).astype(jnp.int32)

    @pl.when(ph == 1)
    def _():
        ri = lax.broadcasted_iota(jnp.int32, (LANES, 2 * LANES), 0)
        ci = lax.broadcasted_iota(jnp.int32, (LANES, 2 * LANES), 1)
        w = ((ci >= ri) | (ci >= LANES)).astype(BF16)
        base = base_scr[...]
        slot0 = bstart_scr[...] * R
        ds = []
        for j in range(TR // LANES):
            sl = slice(j * LANES, (j + 1) * LANES)
            blk = cnt[:, sl]
            res = _mm(blk.astype(BF16), w)
            val = slot0 + base + (res[:, 0:LANES] - blk)
            ds.append(jnp.sum(jnp.where(oh[:, sl], val, 0.0), axis=0, keepdims=True))
            base = base + res[:, LANES:2 * LANES]
        base_scr[...] = base
        ri_ref[...] = jnp.zeros_like(ri_ref)
        ri_ref[0:1, :] = jnp.concatenate(ds, axis=1).astype(jnp.int32)
        rf_ref[...] = jnp.zeros_like(rf_ref)
        rf_ref[0:1, :] = gate_a
        rf_ref[1:2, :] = gate_b


def _route_call(lgt, n_blocks):
    rows, N = lgt.shape
    TR = ROUTE_TILE
    E = NB
    n_pad = -(-n_blocks // LANES) * LANES
    return pl.pallas_call(
        _route_kernel,
        grid=(2, N // TR),
        in_specs=[pl.BlockSpec((rows, TR), lambda ph, t: (0, t))],
        out_specs=[
            pl.BlockSpec((SUBLANES, TR), lambda ph, t: (0, ph * t)),
            pl.BlockSpec((SUBLANES, TR), lambda ph, t: (0, ph * t)),
            pl.BlockSpec((SUBLANES, n_pad), lambda ph, t: (0, 0)),
        ],
        out_shape=[
            jax.ShapeDtypeStruct((SUBLANES, N), F32),
            jax.ShapeDtypeStruct((SUBLANES, N), jnp.int32),
            jax.ShapeDtypeStruct((SUBLANES, n_pad), jnp.int32),
        ],
        scratch_shapes=[pltpu.VMEM((E, LANES), F32), pltpu.VMEM((E, LANES), F32), pltpu.VMEM((E, LANES), F32)],
        compiler_params=pltpu.CompilerParams(dimension_semantics=("arbitrary", "arbitrary")),
        name="route",
    )(lgt)


def _invert_kernel(dest_ref, fill_ref, tok_ref, sem):
    n_tok = dest_ref.shape[0]
    g = pl.program_id(0)
    U = 8
    toks_per_step = n_tok // INVERT_STEPS

    @pl.when(g == 0)
    def _():
        init = pltpu.make_async_copy(fill_ref, tok_ref, sem)
        init.start()
        init.wait()

    def fill(i, carry):
        for u in range(U):
            p = g * toks_per_step + i * U + u
            tok_ref[dest_ref[p]] = p
        return carry
    lax.fori_loop(0, toks_per_step // U, fill, 0)


def _invert_call(dest, n_slots):
    n_tok = dest.shape[0]
    assert n_tok % (INVERT_STEPS * 8) == 0
    fill = jnp.arange(n_slots, dtype=jnp.int32) % n_tok
    return pl.pallas_call(
        _invert_kernel,
        grid=(INVERT_STEPS,),
        in_specs=[pl.BlockSpec(memory_space=pltpu.SMEM), pl.BlockSpec(memory_space=pl.ANY)],
        out_specs=pl.BlockSpec(memory_space=pltpu.SMEM),
        out_shape=jax.ShapeDtypeStruct((n_slots,), jnp.int32),
        scratch_shapes=[pltpu.SemaphoreType.DMA],
        compiler_params=pltpu.CompilerParams(dimension_semantics=("arbitrary",)),
        name="invert_slots",
    )(dest, fill)


def kernel(x, c, positions, w_ada, b_ada, w_in, w_out, hgrn_lb, hgrn_norm_w, ret_norm_w, post_ln1_w,
           post_ln1_b, w_rg, b_rg, w_re, b_re, w_gate, w_up, w_down, post_ln2_w, post_ln2_b):
    B, S, D = x.shape
    N = B * S
    hw = N_HEADS * D_HEAD

    mod3 = _ada_call(c, w_ada[0], b_ada[0]).reshape(B, 6, D)

    win = (w_in[0].reshape(D, 2, 4, N_HEADS, D_HEAD).transpose(1, 0, 3, 2, 4)
           .reshape(2, D, 4 * hw).astype(BF16))
    wout = w_out[0].astype(BF16)
    pad = LANES - N_EXPERTS - N_GROUPS
    wr = jnp.concatenate([w_re[0], w_rg[0], jnp.zeros((D, pad), F32)], axis=1).astype(BF16)
    br = jnp.concatenate([b_re[0], b_rg[0], jnp.zeros((pad,), F32)]).reshape(1, LANES)

    x1, h2t, lgt = _mixer_call(
        x, positions, mod3, win, wout, hgrn_lb, hgrn_norm_w[0].reshape(1, hw),
        ret_norm_w[0].reshape(1, hw), post_ln1_w[0].reshape(1, D), post_ln1_b[0].reshape(1, D), wr, br)

    n_blocks = N // SLOT_BLOCK + N_GROUPS * PAIRS_PER_GROUP
    rf, ri, meta = _route_call(lgt, n_blocks)
    dest = ri[0]
    gates = rf[0:2].T
    slot_tok = _invert_call(dest, n_blocks * SLOT_BLOCK)
    yt = _expert_call(meta[0, :n_blocks], meta[1, :n_blocks], slot_tok, meta[2, 0:1], h2t,
                      w_gate[0].astype(BF16), w_up[0].astype(BF16), w_down[0].astype(BF16))
    out = _final_call(dest, yt, x1.reshape(N, D), gates, mod3, post_ln2_w[0].reshape(1, D),
                      post_ln2_b[0].reshape(1, D), S)
    return out.reshape(B, S, D)
```

```python
import functools
import math

import numpy as np
import jax
import jax.numpy as jnp
from jax import lax
from jax.experimental import pallas as pl
from jax.experimental.pallas import tpu as pltpu

F32 = jnp.float32
BF16 = jnp.bfloat16

D_MODEL = 1024
N_HEADS = 4
D_HEAD = 128
HGRN_CHUNK = 64
RET_CHUNK = 256
ROPE_BASE = 10000.0
N_EXPERTS = 32
EXPERTS_PER_GROUP = 8
N_GROUPS = 4
D_EXPERT = 512
DEEPNORM_ALPHA = 2.0 ** 0.25
LN_EPS = 1e-5

SEQ_TILE = 512
ROW_BLOCK = 128
PAIRS_PER_GROUP = EXPERTS_PER_GROUP * (EXPERTS_PER_GROUP - 1) // 2
NB = 128
SLOT_BLOCK = 256
GATHER_RING = 2
TOK_TILE = 512
ROUTE_TILE = 2048
N_LOGIT_ROWS = 40
INVERT_STEPS = 8
LANES = 128
SUBLANES = 8
VMEM_LIMIT = 56 * 1024 * 1024


def _mm(a, b):
    return jnp.dot(a, b, preferred_element_type=F32)


def _mm_nt(a, b):
    return lax.dot_general(a, b, (((1,), (1,)), ((), ())), preferred_element_type=F32)


def _mm_tn(a, b):
    return lax.dot_general(a, b, (((0,), (0,)), ((), ())), preferred_element_type=F32)


def _sigmoid(x):
    return 1.0 / (1.0 + jnp.exp(-x))


def _silu(x):
    return x * _sigmoid(x)


def _ln(x):
    mu = jnp.mean(x, axis=-1, keepdims=True)
    xc = x - mu
    var = jnp.mean(xc * xc, axis=-1, keepdims=True)
    return xc * lax.rsqrt(var + LN_EPS)


def _const_spec(shape):
    nd = len(shape)
    return pl.BlockSpec(shape, lambda *_: (0,) * nd, pipeline_mode=pl.Buffered(1))


def _ada_kernel(c_ref, w_ref, b_ref, o_ref):
    ca = _silu(c_ref[...])
    o_ref[...] = _mm(ca.astype(BF16), w_ref[...].astype(BF16)) + b_ref[...]


def _ada_call(c, w_ada, b_ada):
    B, D = c.shape
    n_out = w_ada.shape[1]
    return pl.pallas_call(
        _ada_kernel,
        grid=(n_out // D,),
        in_specs=[pl.BlockSpec((B, D), lambda j: (0, 0)),
                  pl.BlockSpec((D, D), lambda j: (0, j)),
                  pl.BlockSpec((1, D), lambda j: (0, j))],
        out_specs=pl.BlockSpec((B, D), lambda j: (0, j)),
        out_shape=jax.ShapeDtypeStruct((B, n_out), F32),
        name="ada_mod",
    )(c, w_ada, b_ada.reshape(1, n_out))


def _mixer_kernel(x_ref, pos_ref, mod_ref, win_ref, wout_ref, lb_ref, hnw_ref, rnw_ref,
                  ln1w_ref, ln1b_ref, invf_ref, dmat_ref, qdec_ref, kdec_ref, wr_ref, br_ref,
                  xp_ref, modp_ref,
                  x1_ref, h2t_ref, lg_ref,
                  h_scr, proj_scr, proj2_scr, o_scr, cos_scr, sin_scr, sh_scr, sr_scr, y_scr, h2_scr,
                  *, ret_chunk_decay, tiles_per_seq):
    T = SEQ_TILE
    D = D_MODEL
    dh = D_HEAD
    hw = N_HEADS * dh
    step = pl.program_id(0)
    n_tiles = pl.num_programs(0) - 1

    @pl.when(step == 0)
    def _():
        y_scr[...] = jnp.zeros_like(y_scr)

    @pl.when((step < n_tiles) & (step % tiles_per_seq == 0))
    def _():
        sh_scr[...] = jnp.zeros_like(sh_scr)
        sr_scr[...] = jnp.zeros_like(sr_scr)

    gate1p = modp_ref[0, 2:3, :]
    shift2p = modp_ref[0, 3:4, :]
    scale2p = modp_ref[0, 4:5, :]

    def post_block(r):
        rows = slice(r * ROW_BLOCK, (r + 1) * ROW_BLOCK)
        u = DEEPNORM_ALPHA * xp_ref[0, rows, :] + gate1p * y_scr[rows, :]
        x1 = _ln(u) * ln1w_ref[...] + ln1b_ref[...]
        x1_ref[0, rows, :] = x1
        h2 = (_ln(x1) * (1.0 + scale2p) + shift2p).astype(BF16)
        h2_scr[rows, :] = h2
        h2f = h2.astype(F32)
        for s in range(SUBLANES):
            h2t_ref[pl.ds(r * ROW_BLOCK * SUBLANES + s, ROW_BLOCK, stride=SUBLANES), :] = (
                h2f[:, s * LANES:(s + 1) * LANES])

    shift1 = mod_ref[0, 0:1, :]
    scale1 = mod_ref[0, 1:2, :]

    for r in range(T // ROW_BLOCK):
        rows = slice(r * ROW_BLOCK, (r + 1) * ROW_BLOCK)
        h = _ln(x_ref[0, rows, :]) * (1.0 + scale1) + shift1
        h_scr[rows, :] = h.astype(BF16)

    n_post = T // ROW_BLOCK
    proj_scr[...] = _mm(h_scr[...], win_ref[0])
    for r in range(n_post // 2):
        post_block(r)
    proj2_scr[...] = _mm(h_scr[...], win_ref[1])
    for r in range(n_post // 2, n_post):
        post_block(r)
    lg_ref[...] = (_mm(h2_scr[...], wr_ref[...]) + br_ref[...]).T[0:N_LOGIT_ROWS, :]

    pos = pos_ref[0].astype(F32)
    ang_t = invf_ref[...] * pos
    cos_t = jnp.cos(ang_t)
    sin_t = jnp.sin(ang_t)
    cos_scr[...] = jnp.concatenate([cos_t, cos_t], axis=0).T
    sin_scr[...] = jnp.concatenate([-sin_t, sin_t], axis=0).T

    l0 = lb_ref[0:1, :]
    l1 = lb_ref[1:2, :]
    lmax = jnp.maximum(l0, l1)
    e0 = jnp.exp(l0 - lmax)
    e1 = jnp.exp(l1 - lmax)
    lb = e0 / (e0 + e1)
    hnw = hnw_ref[...]

    C = HGRN_CHUNK
    rr = lax.broadcasted_iota(jnp.int32, (C, C), 0)
    cc = lax.broadcasted_iota(jnp.int32, (C, C), 1)
    causal = rr >= cc
    tri = causal.astype(BF16)

    def hgrn_chunk(c):
        rows = slice(c * C, (c + 1) * C)
        logfs, ks = [], []
        for hd in range(N_HEADS):
            z = proj_scr[rows, hd * 4 * dh + dh: hd * 4 * dh + 2 * dh]
            e = jnp.exp(-jnp.abs(z))
            r = 1.0 / (1.0 + e)
            er = e * r
            zp = z >= 0
            sig_pos = jnp.where(zp, r, er)
            sig_neg = jnp.where(zp, er, r)
            lbh = lb[:, hd * dh:(hd + 1) * dh]
            logfs.append(jnp.log(lbh + (1.0 - lbh) * sig_pos))
            ks.append((1.0 - lbh) * sig_neg)
        lf = jnp.concatenate(logfs, axis=1)
        hi = lf.astype(BF16)
        r1 = lf - hi.astype(F32)
        mid = r1.astype(BF16)
        lo = (r1 - mid.astype(F32)).astype(BF16)
        bsum = _mm(tri, hi) + _mm(tri, mid) + _mm(tri, lo)
        for hd in range(N_HEADS):
            base = hd * 4 * dh
            b = bsum[:, hd * dh:(hd + 1) * dh]
            b_last = b[C - 1:C, :]
            q = proj_scr[rows, base: base + dh]
            v = proj_scr[rows, base + 2 * dh: base + 3 * dh].astype(BF16)
            g = proj_scr[rows, base + 3 * dh: base + 4 * dh]
            k = ks[hd]
            q_in = (_silu(q) * jnp.exp(b)).astype(BF16)
            k_in = (k * jnp.exp(-b)).astype(BF16)
            k_dec = (k * jnp.exp(b_last - b)).astype(BF16)
            st = sh_scr[hd]
            a = jnp.where(causal, _mm_nt(q_in, k_in), 0.0)
            o = _mm(a.astype(BF16), v) + _mm_nt(q_in, st.astype(BF16))
            sh_scr[hd] = st * jnp.exp(b_last) + _mm_tn(v, k_dec)
            ms = jnp.mean(o * o, axis=-1, keepdims=True)
            on = o * lax.rsqrt(ms + LN_EPS) * hnw[:, hd * dh:(hd + 1) * dh] * _silu(g)
            o_scr[rows, hd * dh:(hd + 1) * dh] = on.astype(BF16)

    rnw = rnw_ref[...]
    RC = RET_CHUNK
    q_scale = dh ** -0.5

    def retention_unit(sub, hd):
        rows = slice(sub * RC, (sub + 1) * RC)
        cs = cos_scr[rows, :]
        sn = sin_scr[rows, :]
        base = hd * 4 * dh
        rq = proj2_scr[rows, base: base + dh]
        rk = proj2_scr[rows, base + dh: base + 2 * dh]
        v = proj2_scr[rows, base + 2 * dh: base + 3 * dh].astype(BF16)
        g = proj2_scr[rows, base + 3 * dh: base + 4 * dh]
        q = (rq * cs + pltpu.roll(rq, dh // 2, 1) * sn) * q_scale
        k = rk * cs + pltpu.roll(rk, dh // 2, 1) * sn
        qb = q.astype(BF16)
        st = sr_scr[hd]
        s = _mm_nt(qb, k.astype(BF16)) * dmat_ref[hd]
        o = _mm(s.astype(BF16), v) + _mm_nt(qb, st.astype(BF16)) * qdec_ref[hd]
        sr_scr[hd] = st * ret_chunk_decay[hd] + _mm_tn(v, (k * kdec_ref[hd]).astype(BF16))
        mu = jnp.mean(o, axis=-1, keepdims=True)
        oc = o - mu
        var = jnp.mean(oc * oc, axis=-1, keepdims=True)
        on = oc * lax.rsqrt(var + LN_EPS) * rnw[:, hd * dh:(hd + 1) * dh] * _silu(g)
        o_scr[rows, hw + hd * dh: hw + (hd + 1) * dh] = on.astype(BF16)

    n_chunks = T // C
    units = [(sub, hd) for sub in range(T // RC) for hd in range(N_HEADS)]
    for c in range(n_chunks):
        hgrn_chunk(c)
        for sub, hd in units[c * len(units) // n_chunks:(c + 1) * len(units) // n_chunks]:
            retention_unit(sub, hd)

    y_scr[...] = _mm(o_scr[...], wout_ref[...])


def _retention_tables():
    h = np.arange(N_HEADS, dtype=np.float64)
    log_gamma = np.log(1.0 - np.exp2(-5.0 - h))
    idx = np.arange(RET_CHUNK, dtype=np.float64)
    rel = idx[:, None] - idx[None, :]
    dmat = np.where(rel >= 0, np.exp(np.maximum(rel, 0.0)[None] * log_gamma[:, None, None]), 0.0)
    qdec = np.exp((idx + 1.0)[None, :] * log_gamma[:, None])
    kdec = np.exp((RET_CHUNK - 1.0 - idx)[None, :] * log_gamma[:, None])
    cdec = np.exp(RET_CHUNK * log_gamma)
    bc = lambda t: np.broadcast_to(t[:, :, None], (N_HEADS, RET_CHUNK, D_HEAD))
    return (jnp.asarray(dmat, F32), jnp.asarray(bc(qdec), F32), jnp.asarray(bc(kdec), F32),
            tuple(float(np.float32(v)) for v in cdec))


def _mixer_call(x, positions, mod3, win, wout, hgrn_lb, hnw, rnw, ln1w, ln1b, wr, br):
    B, S, D = x.shape
    T = SEQ_TILE
    nj = S // T
    dmat, qdec, kdec, cdec = _retention_tables()
    inv_freq = np.power(ROPE_BASE, -np.arange(0, D_HEAD, 2, dtype=np.float64) / D_HEAD)
    invf = jnp.asarray(inv_freq[:, None], F32)
    pos3 = positions.reshape(B, 1, S)
    kern = functools.partial(_mixer_kernel, ret_chunk_decay=cdec, tiles_per_seq=nj)
    n_tiles = B * nj
    cur = lambda s: jnp.minimum(s, n_tiles - 1)
    prev = lambda s: jnp.maximum(s - 1, 0)
    return pl.pallas_call(
        kern,
        grid=(n_tiles + 1,),
        in_specs=[
            pl.BlockSpec((1, T, D), lambda s: (cur(s) // nj, cur(s) % nj, 0)),
            pl.BlockSpec((1, 1, T), lambda s: (cur(s) // nj, 0, cur(s) % nj)),
            pl.BlockSpec((1, 6, D), lambda s: (cur(s) // nj, 0, 0)),
            _const_spec(win.shape), _const_spec(wout.shape), _const_spec(hgrn_lb.shape),
            _const_spec(hnw.shape), _const_spec(rnw.shape), _const_spec(ln1w.shape),
            _const_spec(ln1b.shape), _const_spec(invf.shape), _const_spec(dmat.shape),
            _const_spec(qdec.shape), _const_spec(kdec.shape), _const_spec(wr.shape),
            _const_spec(br.shape),
            pl.BlockSpec((1, T, D), lambda s: (prev(s) // nj, prev(s) % nj, 0)),
            pl.BlockSpec((1, 6, D), lambda s: (prev(s) // nj, 0, 0)),
        ],
        out_specs=[
            pl.BlockSpec((1, T, D), lambda s: (prev(s) // nj, prev(s) % nj, 0)),
            pl.BlockSpec((T * SUBLANES, LANES), lambda s: (prev(s), 0)),
            pl.BlockSpec((N_LOGIT_ROWS, T), lambda s: (0, prev(s))),
        ],
        out_shape=[
            jax.ShapeDtypeStruct((B, S, D), F32),
            jax.ShapeDtypeStruct((B * S * SUBLANES, LANES), F32),
            jax.ShapeDtypeStruct((N_LOGIT_ROWS, B * S), F32),
        ],
        scratch_shapes=[
            pltpu.VMEM((T, D), BF16),
            pltpu.VMEM((T, 4 * N_HEADS * D_HEAD), F32),
            pltpu.VMEM((T, 4 * N_HEADS * D_HEAD), F32),
            pltpu.VMEM((T, D), BF16),
            pltpu.VMEM((T, D_HEAD), F32),
            pltpu.VMEM((T, D_HEAD), F32),
            pltpu.VMEM((N_HEADS, D_HEAD, D_HEAD), F32),
            pltpu.VMEM((N_HEADS, D_HEAD, D_HEAD), F32),
            pltpu.VMEM((T, D), F32),
            pltpu.VMEM((T, D), BF16),
        ],
        compiler_params=pltpu.CompilerParams(
            dimension_semantics=("arbitrary",), vmem_limit_bytes=VMEM_LIMIT),
        name="mixer",
    )(x, pos3, mod3, win, wout, hgrn_lb, hnw, rnw, ln1w, ln1b, invf, dmat, qdec, kdec, wr, br, x, mod3)


def _row_copy(src_ref, dst_ref, src_row, dst_row, sem):
    s0 = pl.multiple_of(src_row * SUBLANES, SUBLANES)
    d0 = pl.multiple_of(dst_row * SUBLANES, SUBLANES)
    return pltpu.make_async_copy(src_ref.at[pl.ds(s0, SUBLANES)], dst_ref.at[pl.ds(d0, SUBLANES)], sem)


def _from_token_tiles(buf_ref, first_row, n_rows):
    return jnp.concatenate(
        [buf_ref[pl.ds(first_row * SUBLANES + s, n_rows, stride=SUBLANES), :] for s in range(SUBLANES)],
        axis=1)


def _start_rows(src_ref, dst_ref, idx_ref, idx_base, dst_base, n_rows, sem):
    def issue(r2, carry):
        for u in range(2):
            r = r2 * 2 + u
            _row_copy(src_ref, dst_ref, idx_ref[idx_base + r], dst_base + r, sem).start(priority=u)
        return carry
    lax.fori_loop(0, n_rows // 2, issue, 0)


def _wait_rows(src_ref, dst_ref, n_rows, sem):
    pltpu.make_async_copy(src_ref.at[pl.ds(0, n_rows * SUBLANES)], dst_ref, sem).wait()


def _bf16_bits(x):
    return lax.bitcast_convert_type(x.astype(BF16).astype(F32), jnp.uint32)


def _expert_kernel(ea_ref, eb_ref, tok_ref, nv_ref, h2t_ref,
                   wga_ref, wua_ref, wda_ref, wgb_ref, wub_ref, wdb_ref, y_ref, buf, sems):
    i = pl.program_id(0)
    R = SLOT_BLOCK
    nv = nv_ref[0]
    del ea_ref, eb_ref

    ahead = GATHER_RING - 1

    @pl.when(i == 0)
    def _():
        for blk in range(ahead):
            @pl.when(blk < nv)
            def _():
                _start_rows(h2t_ref, buf.at[blk], tok_ref, blk * R, 0, R, sems.at[blk])

    for slot in range(GATHER_RING):
        @pl.when((i + ahead < nv) & ((i + ahead) % GATHER_RING == slot))
        def _():
            _start_rows(h2t_ref, buf.at[slot], tok_ref, (i + ahead) * R, 0, R, sems.at[slot])

    def ffn(xb, wg_ref, wu_ref, wd_ref):
        a = _mm(xb, wg_ref[0])
        u = _mm(xb, wu_ref[0])
        return _mm((_silu(a) * u).astype(BF16), wd_ref[0])

    for slot in range(GATHER_RING):
        @pl.when((i < nv) & (i % GATHER_RING == slot))
        def _():
            _wait_rows(h2t_ref, buf.at[slot], R, sems.at[slot])
            xb = _from_token_tiles(buf.at[slot], 0, R).astype(BF16)
            ya = ffn(xb, wga_ref, wua_ref, wda_ref)
            yb = ffn(xb, wgb_ref, wub_ref, wdb_ref)
            packed = _bf16_bits(ya) | (_bf16_bits(yb) >> 16)
            for s in range(SUBLANES):
                y_ref[pl.ds(s, R, stride=SUBLANES), :] = packed[:, s * LANES:(s + 1) * LANES]

    @pl.when(i >= nv)
    def _():
        y_ref[...] = jnp.zeros_like(y_ref)


def _expert_call(block_ea, block_eb, slot_tok, n_valid, h2t, wg, wu, wd):
    n_slots = slot_tok.shape[0]
    R = SLOT_BLOCK
    n_blocks = n_slots // R
    D, F = wg.shape[1], wg.shape[2]
    wa = lambda i, ea, eb, tok, nv: (ea[i], 0, 0)
    wb = lambda i, ea, eb, tok, nv: (eb[i], 0, 0)
    grid_spec = pltpu.PrefetchScalarGridSpec(
        num_scalar_prefetch=4,
        grid=(n_blocks,),
        in_specs=[
            pl.BlockSpec(memory_space=pl.ANY),
            pl.BlockSpec((1, D, F), wa), pl.BlockSpec((1, D, F), wa), pl.BlockSpec((1, F, D), wa),
            pl.BlockSpec((1, D, F), wb), pl.BlockSpec((1, D, F), wb), pl.BlockSpec((1, F, D), wb),
        ],
        out_specs=pl.BlockSpec((R * SUBLANES, LANES), lambda i, ea, eb, tok, nv: (i, 0)),
        scratch_shapes=[pltpu.VMEM((GATHER_RING, R * SUBLANES, LANES), F32),
                        pltpu.SemaphoreType.DMA((GATHER_RING,))],
    )
    return pl.pallas_call(
        _expert_kernel,
        grid_spec=grid_spec,
        out_shape=jax.ShapeDtypeStruct((n_slots * SUBLANES, LANES), jnp.uint32),
        compiler_params=pltpu.CompilerParams(
            dimension_semantics=("arbitrary",), vmem_limit_bytes=VMEM_LIMIT),
        name="experts",
    )(block_ea, block_eb, slot_tok, n_valid, h2t, wg, wu, wd, wg, wu, wd)


def _final_kernel(dest_ref, yt_ref, x1_ref, g_ref, mod_ref, w_ref, b_ref, o_ref, buf, sems):
    i = pl.program_id(0)
    n_steps = pl.num_programs(0)
    TT = TOK_TILE

    def start_tile(tile, slot):
        _start_rows(yt_ref, buf.at[slot], dest_ref, tile * TT, 0, TT, sems.at[slot])

    @pl.when(i == 0)
    def _():
        start_tile(0, 0)

    for slot in range(2):
        @pl.when((i + 1 < n_steps) & ((i + 1) % 2 == slot))
        def _():
            start_tile(i + 1, slot)

    gate2 = mod_ref[0, 5:6, :]
    g = g_ref[...]
    for slot in range(2):
        @pl.when(i % 2 == slot)
        def _():
            _wait_rows(yt_ref, buf.at[slot], TT, sems.at[slot])
            for r in range(TT // ROW_BLOCK):
                rows = slice(r * ROW_BLOCK, (r + 1) * ROW_BLOCK)
                packed = _from_token_tiles(buf.at[slot], r * ROW_BLOCK, ROW_BLOCK)
                ya = lax.bitcast_convert_type(packed & jnp.uint32(0xFFFF0000), F32)
                yb = lax.bitcast_convert_type(packed << 16, F32)
                y = ya * g[rows, 0:1] + yb * g[rows, 1:2]
                u = DEEPNORM_ALPHA * x1_ref[rows, :] + gate2 * y
                o_ref[rows, :] = _ln(u) * w_ref[...] + b_ref[...]


def _final_call(dest, yt, x1, gates, mod3, w, b, seq_len):
    N, D = x1.shape
    TT = TOK_TILE
    per_seq = seq_len // TT
    grid_spec = pltpu.PrefetchScalarGridSpec(
        num_scalar_prefetch=1,
        grid=(N // TT,),
        in_specs=[
            pl.BlockSpec(memory_space=pl.ANY),
            pl.BlockSpec((TT, D), lambda i, d: (i, 0)),
            pl.BlockSpec((TT, 2), lambda i, d: (i, 0)),
            pl.BlockSpec((1, 6, D), lambda i, d: (i // per_seq, 0, 0)),
            pl.BlockSpec((1, D), lambda i, d: (0, 0)),
            pl.BlockSpec((1, D), lambda i, d: (0, 0)),
        ],
        out_specs=pl.BlockSpec((TT, D), lambda i, d: (i, 0)),
        scratch_shapes=[pltpu.VMEM((2, TT * SUBLANES, LANES), jnp.uint32), pltpu.SemaphoreType.DMA((2,))],
    )
    return pl.pallas_call(
        _final_kernel,
        grid_spec=grid_spec,
        out_shape=jax.ShapeDtypeStruct((N, D), F32),
        compiler_params=pltpu.CompilerParams(
            dimension_semantics=("arbitrary",), vmem_limit_bytes=VMEM_LIMIT),
        name="combine_ln",
    )(dest, yt, x1, gates, mod3, w, b)


def _route_kernel(lgt_ref, rf_ref, ri_ref, meta_ref, cnt_scr, base_scr, bstart_scr):
    ph = pl.program_id(0)
    t = pl.program_id(1)
    TR = ROUTE_TILE
    E = N_EXPERTS
    R = SLOT_BLOCK
    G = EXPERTS_PER_GROUP
    lg = lgt_ref[...]

    g = [lg[E + k:E + k + 1, :] for k in range(N_GROUPS)]
    gmax = jnp.maximum(jnp.maximum(g[0], g[1]), jnp.maximum(g[2], g[3]))
    gs = jnp.where(g[0] == gmax, 0.0, jnp.where(g[1] == gmax, 1.0, jnp.where(g[2] == gmax, 2.0, 3.0)))
    psum = (jnp.exp(g[0] - gmax) + jnp.exp(g[1] - gmax)) + (jnp.exp(g[2] - gmax) + jnp.exp(g[3] - gmax))
    p_star = 1.0 / psum
    esel = jnp.where(gs == 0.0, lg[0:G], jnp.where(gs == 1.0, lg[G:2 * G],
                                                   jnp.where(gs == 2.0, lg[2 * G:3 * G], lg[3 * G:4 * G])))
    sub = lax.broadcasted_iota(jnp.int32, (G, TR), 0).astype(F32)
    m1 = jnp.max(esel, axis=0, keepdims=True)
    i1 = jnp.min(jnp.where(esel == m1, sub, float(G)), axis=0, keepdims=True)
    es2 = jnp.where(sub == i1, -jnp.inf, esel)
    m2 = jnp.max(es2, axis=0, keepdims=True)
    i2 = jnp.min(jnp.where(es2 == m2, sub, float(G)), axis=0, keepdims=True)
    d = jnp.exp(m2 - m1)
    w1 = 1.0 / (1.0 + d)
    w2 = d * w1
    lo = jnp.minimum(i1, i2)
    hi = jnp.maximum(i1, i2)
    bucket = gs * PAIRS_PER_GROUP + (lo * (2 * G - 1 - lo) * 0.5 + (hi - lo - 1.0))
    first_is_lo = i1 < i2
    gate_a = p_star * jnp.where(first_is_lo, w1, w2)
    gate_b = p_star * jnp.where(first_is_lo, w2, w1)
    row = lax.broadcasted_iota(jnp.int32, (NB, TR), 0).astype(F32)
    oh = row == bucket
    cnt = oh.astype(F32)

    @pl.when(ph == 0)
    def _():
        @pl.when(t == 0)
        def _():
            cnt_scr[...] = jnp.zeros_like(cnt_scr)
        acc = cnt_scr[...]
        for j in range(TR // LANES):
            acc = acc + cnt[:, j * LANES:(j + 1) * LANES]
        cnt_scr[...] = acc

    @pl.when((ph == 1) & (t == 0))
    def _():
        counts = jnp.sum(cnt_scr[...], axis=1, keepdims=True)
        nblk = jnp.floor((counts + (R - 1)) * (1.0 / R))
        nblk_b = jnp.broadcast_to(nblk, (NB, LANES))
        rb = lax.broadcasted_iota(jnp.int32, (NB, NB), 0)
        cb = lax.broadcasted_iota(jnp.int32, (NB, NB), 1)
        bstart = _mm((cb < rb).astype(BF16), nblk_b.astype(BF16))
        bstart_scr[...] = bstart
        base_scr[...] = jnp.zeros_like(base_scr)
        k = lax.broadcasted_iota(jnp.int32, (NB, 1), 0).astype(F32)
        grp = sum((k >= float(m * PAIRS_PER_GROUP)).astype(F32) for m in range(1, N_GROUPS))
        p = k - grp * PAIRS_PER_GROUP
        pair_start = [i * (2 * G - 1 - i) // 2 for i in range(G - 1)]
        ia = sum((p >= float(s)).astype(F32) for s in pair_start[1:])
        ib = p - ia * (2 * G - 1 - ia) * 0.5 + ia + 1.0
        n_pad = meta_ref.shape[1]
        blk = lax.broadcasted_iota(jnp.int32, (NB, n_pad), 1).astype(F32)
        first = jnp.broadcast_to(bstart[:, 0:1], (NB, n_pad))
        last = jnp.broadcast_to((bstart + nblk_b)[:, 0:1], (NB, n_pad))
        member = (first <= blk) & (blk < last)
        ea = jnp.sum(jnp.where(member, grp * G + ia, 0.0), axis=0, keepdims=True)
        eb = jnp.sum(jnp.where(member, grp * G + ib, 0.0), axis=0, keepdims=True)
        meta_ref[...] = jnp.zeros_like(meta_ref)
        meta_ref[0:1, :] = ea.astype(jnp.int32)
        meta_ref[1:2, :] = eb.astype(jnp.int32)
        meta_ref[2:3, :] = jnp.max(last, axis=0, keepdims=True).astype(jnp.int32)

    @pl.when(ph == 1)
    def _():
        ri = lax.broadcasted_iota(jnp.int32, (LANES, 2 * LANES), 0)
        ci = lax.broadcasted_iota(jnp.int32, (LANES, 2 * LANES), 1)
        w = ((ci >= ri) | (ci >= LANES)).astype(BF16)
        base = base_scr[...]
        slot0 = bstart_scr[...] * R
        ds = []
        for j in range(TR // LANES):
            sl = slice(j * LANES, (j + 1) * LANES)
            blk = cnt[:, sl]
            res = _mm(blk.astype(BF16), w)
            val = slot0 + base + (res[:, 0:LANES] - blk)
            ds.append(jnp.sum(jnp.where(oh[:, sl], val, 0.0), axis=0, keepdims=True))
            base = base + res[:, LANES:2 * LANES]
        base_scr[...] = base
        ri_ref[...] = jnp.zeros_like(ri_ref)
        ri_ref[0:1, :] = jnp.concatenate(ds, axis=1).astype(jnp.int32)
        rf_ref[...] = jnp.zeros_like(rf_ref)
        rf_ref[0:1, :] = gate_a
        rf_ref[1:2, :] = gate_b


def _route_call(lgt, n_blocks):
    rows, N = lgt.shape
    TR = ROUTE_TILE
    E = NB
    n_pad = -(-n_blocks // LANES) * LANES
    return pl.pallas_call(
        _route_kernel,
        grid=(2, N // TR),
        in_specs=[pl.BlockSpec((rows, TR), lambda ph, t: (0, t))],
        out_specs=[
            pl.BlockSpec((SUBLANES, TR), lambda ph, t: (0, ph * t)),
            pl.BlockSpec((SUBLANES, TR), lambda ph, t: (0, ph * t)),
            pl.BlockSpec((SUBLANES, n_pad), lambda ph, t: (0, 0)),
        ],
        out_shape=[
            jax.ShapeDtypeStruct((SUBLANES, N), F32),
            jax.ShapeDtypeStruct((SUBLANES, N), jnp.int32),
            jax.ShapeDtypeStruct((SUBLANES, n_pad), jnp.int32),
        ],
        scratch_shapes=[pltpu.VMEM((E, LANES), F32), pltpu.VMEM((E, LANES), F32), pltpu.VMEM((E, LANES), F32)],
        compiler_params=pltpu.CompilerParams(dimension_semantics=("arbitrary", "arbitrary")),
        name="route",
    )(lgt)


def _invert_kernel(dest_ref, fill_ref, tok_ref, sem):
    n_tok = dest_ref.shape[0]
    g = pl.program_id(0)
    U = 8
    toks_per_step = n_tok // INVERT_STEPS

    @pl.when(g == 0)
    def _():
        init = pltpu.make_async_copy(fill_ref, tok_ref, sem)
        init.start()
        init.wait()

    def fill(i, carry):
        for u in range(U):
            p = g * toks_per_step + i * U + u
            tok_ref[dest_ref[p]] = p
        return carry
    lax.fori_loop(0, toks_per_step // U, fill, 0)


def _invert_call(dest, n_slots):
    n_tok = dest.shape[0]
    assert n_tok % (INVERT_STEPS * 8) == 0
    fill = jnp.arange(n_slots, dtype=jnp.int32) % n_tok
    return pl.pallas_call(
        _invert_kernel,
        grid=(INVERT_STEPS,),
        in_specs=[pl.BlockSpec(memory_space=pltpu.SMEM), pl.BlockSpec(memory_space=pl.ANY)],
        out_specs=pl.BlockSpec(memory_space=pltpu.SMEM),
        out_shape=jax.ShapeDtypeStruct((n_slots,), jnp.int32),
        scratch_shapes=[pltpu.SemaphoreType.DMA],
        compiler_params=pltpu.CompilerParams(dimension_semantics=("arbitrary",)),
        name="invert_slots",
    )(dest, fill)


def kernel(x, c, positions, w_ada, b_ada, w_in, w_out, hgrn_lb, hgrn_norm_w, ret_norm_w, post_ln1_w,
           post_ln1_b, w_rg, b_rg, w_re, b_re, w_gate, w_up, w_down, post_ln2_w, post_ln2_b):
    B, S, D = x.shape
    N = B * S
    hw = N_HEADS * D_HEAD

    mod3 = _ada_call(c, w_ada[0], b_ada[0]).reshape(B, 6, D)

    win = (w_in[0].reshape(D, 2, 4, N_HEADS, D_HEAD).transpose(1, 0, 3, 2, 4)
           .reshape(2, D, 4 * hw).astype(BF16))
    wout = w_out[0].astype(BF16)
    pad = LANES - N_EXPERTS - N_GROUPS
    wr = jnp.concatenate([w_re[0], w_rg[0], jnp.zeros((D, pad), F32)], axis=1).astype(BF16)
    br = jnp.concatenate([b_re[0], b_rg[0], jnp.zeros((pad,), F32)]).reshape(1, LANES)

    x1, h2t, lgt = _mixer_call(
        x, positions, mod3, win, wout, hgrn_lb, hgrn_norm_w[0].reshape(1, hw),
        ret_norm_w[0].reshape(1, hw), post_ln1_w[0].reshape(1, D), post_ln1_b[0].reshape(1, D), wr, br)

    n_blocks = N // SLOT_BLOCK + N_GROUPS * PAIRS_PER_GROUP
    rf, ri, meta = _route_call(lgt, n_blocks)
    dest = ri[0]
    gates = rf[0:2].T
    slot_tok = _invert_call(dest, n_blocks * SLOT_BLOCK)
    yt = _expert_call(meta[0, :n_blocks], meta[1, :n_blocks], slot_tok, meta[2, 0:1], h2t,
                      w_gate[0].astype(BF16), w_up[0].astype(BF16), w_down[0].astype(BF16))
    out = _final_call(dest, yt, x1.reshape(N, D), gates, mod3, post_ln2_w[0].reshape(1, D),
                      post_ln2_b[0].reshape(1, D), S)
    return out.reshape(B, S, D)
```

```python
import functools
import math

import numpy as np
import jax
import jax.numpy as jnp
from jax import lax
from jax.experimental import pallas as pl
from jax.experimental.pallas import tpu as pltpu

F32 = jnp.float32
BF16 = jnp.bfloat16

D_MODEL = 1024
N_HEADS = 4
D_HEAD = 128
HGRN_CHUNK = 64
RET_CHUNK = 256
ROPE_BASE = 10000.0
N_EXPERTS = 32
EXPERTS_PER_GROUP = 8
N_GROUPS = 4
D_EXPERT = 512
DEEPNORM_ALPHA = 2.0 ** 0.25
LN_EPS = 1e-5

SEQ_TILE = 512
ROW_BLOCK = 128
PAIRS_PER_GROUP = EXPERTS_PER_GROUP * (EXPERTS_PER_GROUP - 1) // 2
NB = 128
SLOT_BLOCK = 256
GATHER_RING = 2
TOK_TILE = 512
ROUTE_TILE = 2048
N_LOGIT_ROWS = 40
INVERT_STEPS = 8
LANES = 128
SUBLANES = 8
VMEM_LIMIT = 56 * 1024 * 1024


def _mm(a, b):
    return jnp.dot(a, b, preferred_element_type=F32)


def _mm_nt(a, b):
    return lax.dot_general(a, b, (((1,), (1,)), ((), ())), preferred_element_type=F32)


def _mm_tn(a, b):
    return lax.dot_general(a, b, (((0,), (0,)), ((), ())), preferred_element_type=F32)


def _sigmoid(x):
    return 1.0 / (1.0 + jnp.exp(-x))


def _silu(x):
    return x * _sigmoid(x)


def _ln(x):
    mu = jnp.mean(x, axis=-1, keepdims=True)
    xc = x - mu
    var = jnp.mean(xc * xc, axis=-1, keepdims=True)
    return xc * lax.rsqrt(var + LN_EPS)


def _const_spec(shape):
    nd = len(shape)
    return pl.BlockSpec(shape, lambda *_: (0,) * nd, pipeline_mode=pl.Buffered(1))


def _ada_kernel(c_ref, w_ref, b_ref, o_ref):
    ca = _silu(c_ref[...])
    o_ref[...] = _mm(ca.astype(BF16), w_ref[...].astype(BF16)) + b_ref[...]


def _ada_call(c, w_ada, b_ada):
    B, D = c.shape
    n_out = w_ada.shape[1]
    return pl.pallas_call(
        _ada_kernel,
        grid=(n_out // D,),
        in_specs=[pl.BlockSpec((B, D), lambda j: (0, 0)),
                  pl.BlockSpec((D, D), lambda j: (0, j)),
                  pl.BlockSpec((1, D), lambda j: (0, j))],
        out_specs=pl.BlockSpec((B, D), lambda j: (0, j)),
        out_shape=jax.ShapeDtypeStruct((B, n_out), F32),
        name="ada_mod",
    )(c, w_ada, b_ada.reshape(1, n_out))


def _mixer_kernel(x_ref, pos_ref, mod_ref, win_ref, wout_ref, lb_ref, hnw_ref, rnw_ref,
                  ln1w_ref, ln1b_ref, invf_ref, dmat_ref, qdec_ref, kdec_ref, wr_ref, br_ref,
                  xp_ref, modp_ref, wg32_ref, wu32_ref, wd32_ref,
                  x1_ref, h2t_ref, lg_ref, wg16_ref, wu16_ref, wd16_ref,
                  h_scr, proj_scr, proj2_scr, o_scr, cos_scr, sin_scr, sh_scr, sr_scr, y_scr, h2_scr,
                  *, ret_chunk_decay, tiles_per_seq):
    T = SEQ_TILE
    D = D_MODEL
    dh = D_HEAD
    hw = N_HEADS * dh
    hw4 = 4 * hw
    step = pl.program_id(0)
    n_tiles = pl.num_programs(0) - 1

    wg16_ref[...] = wg32_ref[...].astype(BF16)
    wu16_ref[...] = wu32_ref[...].astype(BF16)
    wd16_ref[...] = wd32_ref[...].astype(BF16)

    @pl.when(step == 0)
    def _():
        y_scr[...] = jnp.zeros_like(y_scr)

    @pl.when((step < n_tiles) & (step % tiles_per_seq == 0))
    def _():
        sh_scr[...] = jnp.zeros_like(sh_scr)
        sr_scr[...] = jnp.zeros_like(sr_scr)

    gate1p = modp_ref[0, 2:3, :]
    shift2p = modp_ref[0, 3:4, :]
    scale2p = modp_ref[0, 4:5, :]

    def post_block(r):
        rows = slice(r * ROW_BLOCK, (r + 1) * ROW_BLOCK)
        u = DEEPNORM_ALPHA * xp_ref[0, rows, :] + gate1p * y_scr[rows, :]
        x1 = _ln(u) * ln1w_ref[...] + ln1b_ref[...]
        x1_ref[0, rows, :] = x1
        h2 = (_ln(x1) * (1.0 + scale2p) + shift2p).astype(BF16)
        h2_scr[rows, :] = h2
        h2f = h2.astype(F32)
        for s in range(SUBLANES):
            h2t_ref[pl.ds(r * ROW_BLOCK * SUBLANES + s, ROW_BLOCK, stride=SUBLANES), :] = (
                h2f[:, s * LANES:(s + 1) * LANES])

    shift1 = mod_ref[0, 0:1, :]
    scale1 = mod_ref[0, 1:2, :]

    for r in range(T // ROW_BLOCK):
        rows = slice(r * ROW_BLOCK, (r + 1) * ROW_BLOCK)
        h = _ln(x_ref[0, rows, :]) * (1.0 + scale1) + shift1
        h_scr[rows, :] = h.astype(BF16)

    n_post = T // ROW_BLOCK
    proj_scr[...] = _mm(h_scr[...], win_ref[:, 0:hw4])
    for r in range(n_post // 2):
        post_block(r)
    proj2_scr[...] = _mm(h_scr[...], win_ref[:, hw4:2 * hw4])
    for r in range(n_post // 2, n_post):
        post_block(r)
    lg_ref[...] = (_mm(h2_scr[...], wr_ref[...]) + br_ref[...]).T[0:N_LOGIT_ROWS, :]

    pos = pos_ref[0].astype(F32)
    ang_t = invf_ref[...] * pos
    cos_t = jnp.cos(ang_t)
    sin_t = jnp.sin(ang_t)
    cos_scr[...] = jnp.concatenate([cos_t, cos_t], axis=0).T
    sin_scr[...] = jnp.concatenate([-sin_t, sin_t], axis=0).T

    l0 = lb_ref[0:1, :]
    l1 = lb_ref[1:2, :]
    lmax = jnp.maximum(l0, l1)
    e0 = jnp.exp(l0 - lmax)
    e1 = jnp.exp(l1 - lmax)
    lb = e0 / (e0 + e1)
    hnw = hnw_ref[...]

    C = HGRN_CHUNK
    rr = lax.broadcasted_iota(jnp.int32, (C, C), 0)
    cc = lax.broadcasted_iota(jnp.int32, (C, C), 1)
    causal = rr >= cc
    tri = causal.astype(BF16)

    def hgrn_chunk(c):
        rows = slice(c * C, (c + 1) * C)
        logfs, ks = [], []
        for hd in range(N_HEADS):
            z = proj_scr[rows, hw + hd * dh: hw + (hd + 1) * dh]
            e = jnp.exp(-jnp.abs(z))
            r = 1.0 / (1.0 + e)
            er = e * r
            zp = z >= 0
            sig_pos = jnp.where(zp, r, er)
            sig_neg = jnp.where(zp, er, r)
            lbh = lb[:, hd * dh:(hd + 1) * dh]
            logfs.append(jnp.log(lbh + (1.0 - lbh) * sig_pos))
            ks.append((1.0 - lbh) * sig_neg)
        lf = jnp.concatenate(logfs, axis=1)
        hi = lf.astype(BF16)
        r1 = lf - hi.astype(F32)
        mid = r1.astype(BF16)
        lo = (r1 - mid.astype(F32)).astype(BF16)
        bsum = _mm(tri, hi) + _mm(tri, mid) + _mm(tri, lo)
        for hd in range(N_HEADS):
            cols = slice(hd * dh, (hd + 1) * dh)
            b = bsum[:, cols]
            b_last = b[C - 1:C, :]
            q = proj_scr[rows, cols]
            v = proj_scr[rows, 2 * hw + hd * dh: 2 * hw + (hd + 1) * dh].astype(BF16)
            g = proj_scr[rows, 3 * hw + hd * dh: 3 * hw + (hd + 1) * dh]
            k = ks[hd]
            q_in = (_silu(q) * jnp.exp(b)).astype(BF16)
            k_in = (k * jnp.exp(-b)).astype(BF16)
            k_dec = (k * jnp.exp(b_last - b)).astype(BF16)
            st = sh_scr[hd]
            a = jnp.where(causal, _mm_nt(q_in, k_in), 0.0)
            o = _mm(a.astype(BF16), v) + _mm_nt(q_in, st.astype(BF16))
            sh_scr[hd] = st * jnp.exp(b_last) + _mm_tn(v, k_dec)
            ms = jnp.mean(o * o, axis=-1, keepdims=True)
            on = o * lax.rsqrt(ms + LN_EPS) * hnw[:, hd * dh:(hd + 1) * dh] * _silu(g)
            o_scr[rows, hd * dh:(hd + 1) * dh] = on.astype(BF16)

    rnw = rnw_ref[...]
    RC = RET_CHUNK
    q_scale = dh ** -0.5

    def retention_unit(sub, hd):
        rows = slice(sub * RC, (sub + 1) * RC)
        cs = cos_scr[rows, :]
        sn = sin_scr[rows, :]
        rq = proj2_scr[rows, hd * dh: (hd + 1) * dh]
        rk = proj2_scr[rows, hw + hd * dh: hw + (hd + 1) * dh]
        v = proj2_scr[rows, 2 * hw + hd * dh: 2 * hw + (hd + 1) * dh].astype(BF16)
        g = proj2_scr[rows, 3 * hw + hd * dh: 3 * hw + (hd + 1) * dh]
        q = (rq * cs + pltpu.roll(rq, dh // 2, 1) * sn) * q_scale
        k = rk * cs + pltpu.roll(rk, dh // 2, 1) * sn
        qb = q.astype(BF16)
        st = sr_scr[hd]
        s = _mm_nt(qb, k.astype(BF16)) * dmat_ref[hd]
        o = _mm(s.astype(BF16), v) + _mm_nt(qb, st.astype(BF16)) * qdec_ref[hd]
        sr_scr[hd] = st * ret_chunk_decay[hd] + _mm_tn(v, (k * kdec_ref[hd]).astype(BF16))
        mu = jnp.mean(o, axis=-1, keepdims=True)
        oc = o - mu
        var = jnp.mean(oc * oc, axis=-1, keepdims=True)
        on = oc * lax.rsqrt(var + LN_EPS) * rnw[:, hd * dh:(hd + 1) * dh] * _silu(g)
        o_scr[rows, hw + hd * dh: hw + (hd + 1) * dh] = on.astype(BF16)

    n_chunks = T // C
    units = [(sub, hd) for sub in range(T // RC) for hd in range(N_HEADS)]
    for c in range(n_chunks):
        hgrn_chunk(c)
        for sub, hd in units[c * len(units) // n_chunks:(c + 1) * len(units) // n_chunks]:
            retention_unit(sub, hd)

    y_scr[...] = _mm(o_scr[...], wout_ref[...])


def _retention_tables():
    h = np.arange(N_HEADS, dtype=np.float64)
    log_gamma = np.log(1.0 - np.exp2(-5.0 - h))
    idx = np.arange(RET_CHUNK, dtype=np.float64)
    rel = idx[:, None] - idx[None, :]
    dmat = np.where(rel >= 0, np.exp(np.maximum(rel, 0.0)[None] * log_gamma[:, None, None]), 0.0)
    qdec = np.exp((idx + 1.0)[None, :] * log_gamma[:, None])
    kdec = np.exp((RET_CHUNK - 1.0 - idx)[None, :] * log_gamma[:, None])
    cdec = np.exp(RET_CHUNK * log_gamma)
    bc = lambda t: np.broadcast_to(t[:, :, None], (N_HEADS, RET_CHUNK, D_HEAD))
    return (jnp.asarray(dmat, F32), jnp.asarray(bc(qdec), F32), jnp.asarray(bc(kdec), F32),
            tuple(float(np.float32(v)) for v in cdec))


def _mixer_call(x, positions, mod3, win, wout, hgrn_lb, hnw, rnw, ln1w, ln1b, wr, br, wg, wu, wd):
    B, S, D = x.shape
    T = SEQ_TILE
    nj = S // T
    dmat, qdec, kdec, cdec = _retention_tables()
    inv_freq = np.power(ROPE_BASE, -np.arange(0, D_HEAD, 2, dtype=np.float64) / D_HEAD)
    invf = jnp.asarray(inv_freq[:, None], F32)
    pos3 = positions.reshape(B, 1, S)
    kern = functools.partial(_mixer_kernel, ret_chunk_decay=cdec, tiles_per_seq=nj)
    n_tiles = B * nj
    cur = lambda s: jnp.minimum(s, n_tiles - 1)
    prev = lambda s: jnp.maximum(s - 1, 0)
    E, _, F = wg.shape
    wg2, wu2, wd2 = wg.reshape(E * D, F), wu.reshape(E * D, F), wd.reshape(E * F, D)
    rg, rd = (E * D) // n_tiles, (E * F) // n_tiles
    assert rg * n_tiles == E * D and rd * n_tiles == E * F and rd % 16 == 0
    wspec_g = pl.BlockSpec((rg, F), lambda s: (cur(s), 0))
    wspec_d = pl.BlockSpec((rd, D), lambda s: (cur(s), 0))
    outs = pl.pallas_call(
        kern,
        grid=(n_tiles + 1,),
        in_specs=[
            pl.BlockSpec((1, T, D), lambda s: (cur(s) // nj, cur(s) % nj, 0)),
            pl.BlockSpec((1, 1, T), lambda s: (cur(s) // nj, 0, cur(s) % nj)),
            pl.BlockSpec((1, 6, D), lambda s: (cur(s) // nj, 0, 0)),
            _const_spec(win.shape), _const_spec(wout.shape), _const_spec(hgrn_lb.shape),
            _const_spec(hnw.shape), _const_spec(rnw.shape), _const_spec(ln1w.shape),
            _const_spec(ln1b.shape), _const_spec(invf.shape), _const_spec(dmat.shape),
            _const_spec(qdec.shape), _const_spec(kdec.shape), _const_spec(wr.shape),
            _const_spec(br.shape),
            pl.BlockSpec((1, T, D), lambda s: (prev(s) // nj, prev(s) % nj, 0)),
            pl.BlockSpec((1, 6, D), lambda s: (prev(s) // nj, 0, 0)),
            wspec_g, wspec_g, wspec_d,
        ],
        out_specs=[
            pl.BlockSpec((1, T, D), lambda s: (prev(s) // nj, prev(s) % nj, 0)),
            pl.BlockSpec((T * SUBLANES, LANES), lambda s: (prev(s), 0)),
            pl.BlockSpec((N_LOGIT_ROWS, T), lambda s: (0, prev(s))),
            wspec_g, wspec_g, wspec_d,
        ],
        out_shape=[
            jax.ShapeDtypeStruct((B, S, D), F32),
            jax.ShapeDtypeStruct((B * S * SUBLANES, LANES), F32),
            jax.ShapeDtypeStruct((N_LOGIT_ROWS, B * S), F32),
            jax.ShapeDtypeStruct(wg2.shape, BF16),
            jax.ShapeDtypeStruct(wu2.shape, BF16),
            jax.ShapeDtypeStruct(wd2.shape, BF16),
        ],
        scratch_shapes=[
            pltpu.VMEM((T, D), BF16),
            pltpu.VMEM((T, 4 * N_HEADS * D_HEAD), F32),
            pltpu.VMEM((T, 4 * N_HEADS * D_HEAD), F32),
            pltpu.VMEM((T, D), BF16),
            pltpu.VMEM((T, D_HEAD), F32),
            pltpu.VMEM((T, D_HEAD), F32),
            pltpu.VMEM((N_HEADS, D_HEAD, D_HEAD), F32),
            pltpu.VMEM((N_HEADS, D_HEAD, D_HEAD), F32),
            pltpu.VMEM((T, D), F32),
            pltpu.VMEM((T, D), BF16),
        ],
        compiler_params=pltpu.CompilerParams(
            dimension_semantics=("arbitrary",), vmem_limit_bytes=VMEM_LIMIT),
        name="mixer",
    )(x, pos3, mod3, win, wout, hgrn_lb, hnw, rnw, ln1w, ln1b, invf, dmat, qdec, kdec, wr, br, x, mod3,
      wg2, wu2, wd2)
    x1, h2t, lgt, wg16, wu16, wd16 = outs
    return x1, h2t, lgt, wg16.reshape(E, D, F), wu16.reshape(E, D, F), wd16.reshape(E, F, D)


def _row_copy(src_ref, dst_ref, src_row, dst_row, sem):
    s0 = pl.multiple_of(src_row * SUBLANES, SUBLANES)
    d0 = pl.multiple_of(dst_row * SUBLANES, SUBLANES)
    return pltpu.make_async_copy(src_ref.at[pl.ds(s0, SUBLANES)], dst_ref.at[pl.ds(d0, SUBLANES)], sem)


def _from_token_tiles(buf_ref, first_row, n_rows):
    return jnp.concatenate(
        [buf_ref[pl.ds(first_row * SUBLANES + s, n_rows, stride=SUBLANES), :] for s in range(SUBLANES)],
        axis=1)


def _start_rows(src_ref, dst_ref, idx_ref, idx_base, dst_base, n_rows, sem):
    def issue(r2, carry):
        for u in range(2):
            r = r2 * 2 + u
            _row_copy(src_ref, dst_ref, idx_ref[idx_base + r], dst_base + r, sem).start(priority=u)
        return carry
    lax.fori_loop(0, n_rows // 2, issue, 0)


def _wait_rows(src_ref, dst_ref, n_rows, sem):
    pltpu.make_async_copy(src_ref.at[pl.ds(0, n_rows * SUBLANES)], dst_ref, sem).wait()


def _bf16_bits(x):
    return lax.bitcast_convert_type(x.astype(BF16).astype(F32), jnp.uint32)


def _expert_kernel(ea_ref, eb_ref, tok_ref, nv_ref, h2t_ref,
                   wga_ref, wua_ref, wda_ref, wgb_ref, wub_ref, wdb_ref, y_ref, buf, sems):
    i = pl.program_id(0)
    R = SLOT_BLOCK
    nv = nv_ref[0]
    del ea_ref, eb_ref

    ahead = GATHER_RING - 1

    @pl.when(i == 0)
    def _():
        for blk in range(ahead):
            @pl.when(blk < nv)
            def _():
                _start_rows(h2t_ref, buf.at[blk], tok_ref, blk * R, 0, R, sems.at[blk])

    for slot in range(GATHER_RING):
        @pl.when((i + ahead < nv) & ((i + ahead) % GATHER_RING == slot))
        def _():
            _start_rows(h2t_ref, buf.at[slot], tok_ref, (i + ahead) * R, 0, R, sems.at[slot])

    def ffn(xb, wg_ref, wu_ref, wd_ref):
        a = _mm(xb, wg_ref[0])
        u = _mm(xb, wu_ref[0])
        return _mm((_silu(a) * u).astype(BF16), wd_ref[0])

    for slot in range(GATHER_RING):
        @pl.when((i < nv) & (i % GATHER_RING == slot))
        def _():
            _wait_rows(h2t_ref, buf.at[slot], R, sems.at[slot])
            xb = _from_token_tiles(buf.at[slot], 0, R).astype(BF16)
            ya = ffn(xb, wga_ref, wua_ref, wda_ref)
            yb = ffn(xb, wgb_ref, wub_ref, wdb_ref)
            packed = _bf16_bits(ya) | (_bf16_bits(yb) >> 16)
            for s in range(SUBLANES):
                y_ref[pl.ds(s, R, stride=SUBLANES), :] = packed[:, s * LANES:(s + 1) * LANES]

    @pl.when(i >= nv)
    def _():
        y_ref[...] = jnp.zeros_like(y_ref)


def _expert_call(block_ea, block_eb, slot_tok, n_valid, h2t, wg, wu, wd):
    n_slots = slot_tok.shape[0]
    R = SLOT_BLOCK
    n_blocks = n_slots // R
    D, F = wg.shape[1], wg.shape[2]
    wa = lambda i, ea, eb, tok, nv: (ea[i], 0, 0)
    wb = lambda i, ea, eb, tok, nv: (eb[i], 0, 0)
    grid_spec = pltpu.PrefetchScalarGridSpec(
        num_scalar_prefetch=4,
        grid=(n_blocks,),
        in_specs=[
            pl.BlockSpec(memory_space=pl.ANY),
            pl.BlockSpec((1, D, F), wa), pl.BlockSpec((1, D, F), wa), pl.BlockSpec((1, F, D), wa),
            pl.BlockSpec((1, D, F), wb), pl.BlockSpec((1, D, F), wb), pl.BlockSpec((1, F, D), wb),
        ],
        out_specs=pl.BlockSpec((R * SUBLANES, LANES), lambda i, ea, eb, tok, nv: (i, 0)),
        scratch_shapes=[pltpu.VMEM((GATHER_RING, R * SUBLANES, LANES), F32),
                        pltpu.SemaphoreType.DMA((GATHER_RING,))],
    )
    return pl.pallas_call(
        _expert_kernel,
        grid_spec=grid_spec,
        out_shape=jax.ShapeDtypeStruct((n_slots * SUBLANES, LANES), jnp.uint32),
        compiler_params=pltpu.CompilerParams(
            dimension_semantics=("arbitrary",), vmem_limit_bytes=VMEM_LIMIT),
        name="experts",
    )(block_ea, block_eb, slot_tok, n_valid, h2t, wg, wu, wd, wg, wu, wd)


def _final_kernel(dest_ref, yt_ref, x1_ref, g_ref, mod_ref, w_ref, b_ref, o_ref, buf, sems):
    i = pl.program_id(0)
    n_steps = pl.num_programs(0)
    TT = TOK_TILE

    def start_tile(tile, slot):
        _start_rows(yt_ref, buf.at[slot], dest_ref, tile * TT, 0, TT, sems.at[slot])

    @pl.when(i == 0)
    def _():
        start_tile(0, 0)

    for slot in range(2):
        @pl.when((i + 1 < n_steps) & ((i + 1) % 2 == slot))
        def _():
            start_tile(i + 1, slot)

    gate2 = mod_ref[0, 5:6, :]
    g = g_ref[...]
    for slot in range(2):
        @pl.when(i % 2 == slot)
        def _():
            _wait_rows(yt_ref, buf.at[slot], TT, sems.at[slot])
            for r in range(TT // ROW_BLOCK):
                rows = slice(r * ROW_BLOCK, (r + 1) * ROW_BLOCK)
                packed = _from_token_tiles(buf.at[slot], r * ROW_BLOCK, ROW_BLOCK)
                ya = lax.bitcast_convert_type(packed & jnp.uint32(0xFFFF0000), F32)
                yb = lax.bitcast_convert_type(packed << 16, F32)
                y = ya * g[rows, 0:1] + yb * g[rows, 1:2]
                u = DEEPNORM_ALPHA * x1_ref[rows, :] + gate2 * y
                o_ref[rows, :] = _ln(u) * w_ref[...] + b_ref[...]


def _final_call(dest, yt, x1, gates, mod3, w, b, seq_len):
    N, D = x1.shape
    TT = TOK_TILE
    per_seq = seq_len // TT
    grid_spec = pltpu.PrefetchScalarGridSpec(
        num_scalar_prefetch=1,
        grid=(N // TT,),
        in_specs=[
            pl.BlockSpec(memory_space=pl.ANY),
            pl.BlockSpec((TT, D), lambda i, d: (i, 0)),
            pl.BlockSpec((TT, 2), lambda i, d: (i, 0)),
            pl.BlockSpec((1, 6, D), lambda i, d: (i // per_seq, 0, 0)),
            pl.BlockSpec((1, D), lambda i, d: (0, 0)),
            pl.BlockSpec((1, D), lambda i, d: (0, 0)),
        ],
        out_specs=pl.BlockSpec((TT, D), lambda i, d: (i, 0)),
        scratch_shapes=[pltpu.VMEM((2, TT * SUBLANES, LANES), jnp.uint32), pltpu.SemaphoreType.DMA((2,))],
    )
    return pl.pallas_call(
        _final_kernel,
        grid_spec=grid_spec,
        out_shape=jax.ShapeDtypeStruct((N, D), F32),
        compiler_params=pltpu.CompilerParams(
            dimension_semantics=("arbitrary",), vmem_limit_bytes=VMEM_LIMIT),
        name="combine_ln",
    )(dest, yt, x1, gates, mod3, w, b)


def _route_kernel(lgt_ref, rf_ref, ri_ref, meta_ref, cnt_scr, base_scr, bstart_scr):
    ph = pl.program_id(0)
    t = pl.program_id(1)
    TR = ROUTE_TILE
    E = N_EXPERTS
    R = SLOT_BLOCK
    G = EXPERTS_PER_GROUP
    lg = lgt_ref[...]

    g = [lg[E + k:E + k + 1, :] for k in range(N_GROUPS)]
    gmax = jnp.maximum(jnp.maximum(g[0], g[1]), jnp.maximum(g[2], g[3]))
    gs = jnp.where(g[0] == gmax, 0.0, jnp.where(g[1] == gmax, 1.0, jnp.where(g[2] == gmax, 2.0, 3.0)))
    psum = (jnp.exp(g[0] - gmax) + jnp.exp(g[1] - gmax)) + (jnp.exp(g[2] - gmax) + jnp.exp(g[3] - gmax))
    p_star = 1.0 / psum
    esel = jnp.where(gs == 0.0, lg[0:G], jnp.where(gs == 1.0, lg[G:2 * G],
                                                   jnp.where(gs == 2.0, lg[2 * G:3 * G], lg[3 * G:4 * G])))
    sub = lax.broadcasted_iota(jnp.int32, (G, TR), 0).astype(F32)
    m1 = jnp.max(esel, axis=0, keepdims=True)
    i1 = jnp.min(jnp.where(esel == m1, sub, float(G)), axis=0, keepdims=True)
    es2 = jnp.where(sub == i1, -jnp.inf, esel)
    m2 = jnp.max(es2, axis=0, keepdims=True)
    i2 = jnp.min(jnp.where(es2 == m2, sub, float(G)), axis=0, keepdims=True)
    d = jnp.exp(m2 - m1)
    w1 = 1.0 / (1.0 + d)
    w2 = d * w1
    lo = jnp.minimum(i1, i2)
    hi = jnp.maximum(i1, i2)
    bucket = gs * PAIRS_PER_GROUP + (lo * (2 * G - 1 - lo) * 0.5 + (hi - lo - 1.0))
    first_is_lo = i1 < i2
    gate_a = p_star * jnp.where(first_is_lo, w1, w2)
    gate_b = p_star * jnp.where(first_is_lo, w2, w1)
    row = lax.broadcasted_iota(jnp.int32, (NB, TR), 0).astype(F32)
    oh = row == bucket
    cnt = oh.astype(F32)

    @pl.when(ph == 0)
    def _():
        @pl.when(t == 0)
        def _():
            cnt_scr[...] = jnp.zeros_like(cnt_scr)
        acc = cnt_scr[...]
        for j in range(TR // LANES):
            acc = acc + cnt[:, j * LANES:(j + 1) * LANES]
        cnt_scr[...] = acc

    @pl.when((ph == 1) & (t == 0))
    def _():
        counts = jnp.sum(cnt_scr[...], axis=1, keepdims=True)
        nblk = jnp.floor((counts + (R - 1)) * (1.0 / R))
        nblk_b = jnp.broadcast_to(nblk, (NB, LANES))
        rb = lax.broadcasted_iota(jnp.int32, (NB, NB), 0)
        cb = lax.broadcasted_iota(jnp.int32, (NB, NB), 1)
        bstart = _mm((cb < rb).astype(BF16), nblk_b.astype(BF16))
        bstart_scr[...] = bstart
        base_scr[...] = jnp.zeros_like(base_scr)
        k = lax.broadcasted_iota(jnp.int32, (NB, 1), 0).astype(F32)
        grp = sum((k >= float(m * PAIRS_PER_GROUP)).astype(F32) for m in range(1, N_GROUPS))
        p = k - grp * PAIRS_PER_GROUP
        pair_start = [i * (2 * G - 1 - i) // 2 for i in range(G - 1)]
        ia = sum((p >= float(s)).astype(F32) for s in pair_start[1:])
        ib = p - ia * (2 * G - 1 - ia) * 0.5 + ia + 1.0
        n_pad = meta_ref.shape[1]
        blk = lax.broadcasted_iota(jnp.int32, (NB, n_pad), 1).astype(F32)
        first = jnp.broadcast_to(bstart[:, 0:1], (NB, n_pad))
        last = jnp.broadcast_to((bstart + nblk_b)[:, 0:1], (NB, n_pad))
        member = (first <= blk) & (blk < last)
        ea = jnp.sum(jnp.where(member, grp * G + ia, 0.0), axis=0, keepdims=True)
        eb = jnp.sum(jnp.where(member, grp * G + ib, 0.0), axis=0, keepdims=True)
        meta_ref[...] = jnp.zeros_like(meta_ref)
        meta_ref[0:1, :] = ea.astype(jnp.int32)
        meta_ref[1:2, :] = eb.astype(jnp.int32)
        meta_ref[2:3, :] = jnp.max(last, axis=0, keepdims=True).astype(jnp.int32)

    @pl.when(ph == 1)
    def _():
        ri = lax.broadcasted_iota(jnp.int32, (LANES, 2 * LANES), 0)
        ci = lax.broadcasted_iota(jnp.int32, (LANES, 2 * LANES), 1)
        w = ((ci >= ri) | (ci >= LANES)).astype(BF16)
        base = base_scr[...]
        slot0 = bstart_scr[...] * R
        ds = []
        for j in range(TR // LANES):
            sl = slice(j * LANES, (j + 1) * LANES)
            blk = cnt[:, sl]
            res = _mm(blk.astype(BF16), w)
            val = slot0 + base + (res[:, 0:LANES] - blk)
            ds.append(jnp.sum(jnp.where(oh[:, sl], val, 0.0), axis=0, keepdims=True))
            base = base + res[:, LANES:2 * LANES]
        base_scr[...] = base
        ri_ref[...] = jnp.zeros_like(ri_ref)
        ri_ref[0:1, :] = jnp.concatenate(ds, axis=1).astype(jnp.int32)
        rf_ref[...] = jnp.zeros_like(rf_ref)
        rf_ref[0:1, :] = gate_a
        rf_ref[1:2, :] = gate_b


def _route_call(lgt, n_blocks):
    rows, N = lgt.shape
    TR = ROUTE_TILE
    E = NB
    n_pad = -(-n_blocks // LANES) * LANES
    return pl.pallas_call(
        _route_kernel,
        grid=(2, N // TR),
        in_specs=[pl.BlockSpec((rows, TR), lambda ph, t: (0, t))],
        out_specs=[
            pl.BlockSpec((SUBLANES, TR), lambda ph, t: (0, ph * t)),
            pl.BlockSpec((SUBLANES, TR), lambda ph, t: (0, ph * t)),
            pl.BlockSpec((SUBLANES, n_pad), lambda ph, t: (0, 0)),
        ],
        out_shape=[
            jax.ShapeDtypeStruct((SUBLANES, N), F32),
            jax.ShapeDtypeStruct((SUBLANES, N), jnp.int32),
            jax.ShapeDtypeStruct((SUBLANES, n_pad), jnp.int32),
        ],
        scratch_shapes=[pltpu.VMEM((E, LANES), F32), pltpu.VMEM((E, LANES), F32), pltpu.VMEM((E, LANES), F32)],
        compiler_params=pltpu.CompilerParams(dimension_semantics=("arbitrary", "arbitrary")),
        name="route",
    )(lgt)


def _invert_kernel(dest_ref, fill_ref, tok_ref, sem):
    n_tok = dest_ref.shape[0]
    g = pl.program_id(0)
    U = 8
    toks_per_step = n_tok // INVERT_STEPS

    @pl.when(g == 0)
    def _():
        init = pltpu.make_async_copy(fill_ref, tok_ref, sem)
        init.start()
        init.wait()

    def fill(i, carry):
        p0 = g * toks_per_step + i * U
        slots = [dest_ref[p0 + u] for u in range(U)]
        for u in range(U):
            tok_ref[slots[u]] = p0 + u
        return carry
    lax.fori_loop(0, toks_per_step // U, fill, 0)


def _invert_call(dest, n_slots):
    n_tok = dest.shape[0]
    assert n_tok % (INVERT_STEPS * 8) == 0
    fill = jnp.arange(n_slots, dtype=jnp.int32) % n_tok
    return pl.pallas_call(
        _invert_kernel,
        grid=(INVERT_STEPS,),
        in_specs=[pl.BlockSpec(memory_space=pltpu.SMEM), pl.BlockSpec(memory_space=pl.ANY)],
        out_specs=pl.BlockSpec(memory_space=pltpu.SMEM),
        out_shape=jax.ShapeDtypeStruct((n_slots,), jnp.int32),
        scratch_shapes=[pltpu.SemaphoreType.DMA],
        compiler_params=pltpu.CompilerParams(dimension_semantics=("arbitrary",)),
        name="invert_slots",
    )(dest, fill)


def kernel(x, c, positions, w_ada, b_ada, w_in, w_out, hgrn_lb, hgrn_norm_w, ret_norm_w, post_ln1_w,
           post_ln1_b, w_rg, b_rg, w_re, b_re, w_gate, w_up, w_down, post_ln2_w, post_ln2_b):
    B, S, D = x.shape
    N = B * S
    hw = N_HEADS * D_HEAD

    mod3 = _ada_call(c, w_ada[0], b_ada[0]).reshape(B, 6, D)

    win = w_in[0].astype(BF16)
    wout = w_out[0].astype(BF16)
    pad = LANES - N_EXPERTS - N_GROUPS
    wr = jnp.concatenate([w_re[0], w_rg[0], jnp.zeros((D, pad), F32)], axis=1).astype(BF16)
    br = jnp.concatenate([b_re[0], b_rg[0], jnp.zeros((pad,), F32)]).reshape(1, LANES)

    x1, h2t, lgt, wg16, wu16, wd16 = _mixer_call(
        x, positions, mod3, win, wout, hgrn_lb, hgrn_norm_w[0].reshape(1, hw),
        ret_norm_w[0].reshape(1, hw), post_ln1_w[0].reshape(1, D), post_ln1_b[0].reshape(1, D), wr, br,
        w_gate[0], w_up[0], w_down[0])

    n_blocks = N // SLOT_BLOCK + N_GROUPS * PAIRS_PER_GROUP
    rf, ri, meta = _route_call(lgt, n_blocks)
    dest = ri[0]
    gates = rf[0:2].T
    slot_tok = _invert_call(dest, n_blocks * SLOT_BLOCK)
    yt = _expert_call(meta[0, :n_blocks], meta[1, :n_blocks], slot_tok, meta[2, 0:1], h2t,
                      wg16, wu16, wd16)
    out = _final_call(dest, yt, x1.reshape(N, D), gates, mod3, post_ln2_w[0].reshape(1, D),
                      post_ln2_b[0].reshape(1, D), S)
    return out.reshape(B, S, D)
```

```python
import functools
import math

import numpy as np
import jax
import jax.numpy as jnp
from jax import lax
from jax.experimental import pallas as pl
from jax.experimental.pallas import tpu as pltpu

F32 = jnp.float32
BF16 = jnp.bfloat16

D_MODEL = 1024
N_HEADS = 4
D_HEAD = 128
HGRN_CHUNK = 64
RET_CHUNK = 256
ROPE_BASE = 10000.0
N_EXPERTS = 32
EXPERTS_PER_GROUP = 8
N_GROUPS = 4
D_EXPERT = 512
DEEPNORM_ALPHA = 2.0 ** 0.25
LN_EPS = 1e-5

SEQ_TILE = 512
ROW_BLOCK = 128
PAIRS_PER_GROUP = EXPERTS_PER_GROUP * (EXPERTS_PER_GROUP - 1) // 2
NB = 128
SLOT_BLOCK = 256
DISPATCH_TILE = 512
DISPATCH_RING = 3
TOK_TILE = 512
ROUTE_TILE = 2048
N_LOGIT_ROWS = 40
LANES = 128
SUBLANES = 8
VMEM_LIMIT = 56 * 1024 * 1024


def _mm(a, b):
    return jnp.dot(a, b, preferred_element_type=F32)


def _mm_nt(a, b):
    return lax.dot_general(a, b, (((1,), (1,)), ((), ())), preferred_element_type=F32)


def _mm_tn(a, b):
    return lax.dot_general(a, b, (((0,), (0,)), ((), ())), preferred_element_type=F32)


def _sigmoid(x):
    return 1.0 / (1.0 + jnp.exp(-x))


def _silu(x):
    return x * _sigmoid(x)


def _ln(x):
    mu = jnp.mean(x, axis=-1, keepdims=True)
    xc = x - mu
    var = jnp.mean(xc * xc, axis=-1, keepdims=True)
    return xc * lax.rsqrt(var + LN_EPS)


def _const_spec(shape):
    nd = len(shape)
    return pl.BlockSpec(shape, lambda *_: (0,) * nd, pipeline_mode=pl.Buffered(1))


def _ada_kernel(c_ref, w_ref, b_ref, o_ref):
    ca = _silu(c_ref[...])
    o_ref[...] = _mm(ca.astype(BF16), w_ref[...].astype(BF16)) + b_ref[...]


def _ada_call(c, w_ada, b_ada):
    B, D = c.shape
    n_out = w_ada.shape[1]
    return pl.pallas_call(
        _ada_kernel,
        grid=(n_out // D,),
        in_specs=[pl.BlockSpec((B, D), lambda j: (0, 0)),
                  pl.BlockSpec((D, D), lambda j: (0, j)),
                  pl.BlockSpec((1, D), lambda j: (0, j))],
        out_specs=pl.BlockSpec((B, D), lambda j: (0, j)),
        out_shape=jax.ShapeDtypeStruct((B, n_out), F32),
        name="ada_mod",
    )(c, w_ada, b_ada.reshape(1, n_out))


def _mixer_kernel(x_ref, pos_ref, mod_ref, win_ref, wout_ref, lb_ref, hnw_ref, rnw_ref,
                  ln1w_ref, ln1b_ref, invf_ref, dmat_ref, qdec_ref, kdec_ref, wr_ref, br_ref,
                  xp_ref, modp_ref, wg32_ref, wu32_ref, wd32_ref,
                  x1_ref, h2t_ref, lg_ref, wg16_ref, wu16_ref, wd16_ref,
                  h_scr, proj_scr, proj2_scr, o_scr, cos_scr, sin_scr, sh_scr, sr_scr, y_scr, h2_scr,
                  *, ret_chunk_decay, tiles_per_seq):
    T = SEQ_TILE
    D = D_MODEL
    dh = D_HEAD
    hw = N_HEADS * dh
    hw4 = 4 * hw
    step = pl.program_id(0)
    n_tiles = pl.num_programs(0) - 1

    wg16_ref[...] = wg32_ref[...].astype(BF16)
    wu16_ref[...] = wu32_ref[...].astype(BF16)
    wd16_ref[...] = wd32_ref[...].astype(BF16)

    @pl.when(step == 0)
    def _():
        y_scr[...] = jnp.zeros_like(y_scr)

    @pl.when((step < n_tiles) & (step % tiles_per_seq == 0))
    def _():
        sh_scr[...] = jnp.zeros_like(sh_scr)
        sr_scr[...] = jnp.zeros_like(sr_scr)

    gate1p = modp_ref[0, 2:3, :]
    shift2p = modp_ref[0, 3:4, :]
    scale2p = modp_ref[0, 4:5, :]

    def post_block(r):
        rows = slice(r * ROW_BLOCK, (r + 1) * ROW_BLOCK)
        u = DEEPNORM_ALPHA * xp_ref[0, rows, :] + gate1p * y_scr[rows, :]
        x1 = _ln(u) * ln1w_ref[...] + ln1b_ref[...]
        x1_ref[0, rows, :] = x1
        h2 = (_ln(x1) * (1.0 + scale2p) + shift2p).astype(BF16)
        h2_scr[rows, :] = h2
        h2f = h2.astype(F32)
        for s in range(SUBLANES):
            h2t_ref[pl.ds(r * ROW_BLOCK * SUBLANES + s, ROW_BLOCK, stride=SUBLANES), :] = (
                h2f[:, s * LANES:(s + 1) * LANES])

    shift1 = mod_ref[0, 0:1, :]
    scale1 = mod_ref[0, 1:2, :]

    for r in range(T // ROW_BLOCK):
        rows = slice(r * ROW_BLOCK, (r + 1) * ROW_BLOCK)
        h = _ln(x_ref[0, rows, :]) * (1.0 + scale1) + shift1
        h_scr[rows, :] = h.astype(BF16)

    n_post = T // ROW_BLOCK
    proj_scr[...] = _mm(h_scr[...], win_ref[:, 0:hw4])
    for r in range(n_post // 2):
        post_block(r)
    proj2_scr[...] = _mm(h_scr[...], win_ref[:, hw4:2 * hw4])
    for r in range(n_post // 2, n_post):
        post_block(r)
    lg_ref[...] = (_mm(h2_scr[...], wr_ref[...]) + br_ref[...]).T[0:N_LOGIT_ROWS, :]

    pos = pos_ref[0].astype(F32)
    ang_t = invf_ref[...] * pos
    cos_t = jnp.cos(ang_t)
    sin_t = jnp.sin(ang_t)
    cos_scr[...] = jnp.concatenate([cos_t, cos_t], axis=0).T
    sin_scr[...] = jnp.concatenate([-sin_t, sin_t], axis=0).T

    l0 = lb_ref[0:1, :]
    l1 = lb_ref[1:2, :]
    lmax = jnp.maximum(l0, l1)
    e0 = jnp.exp(l0 - lmax)
    e1 = jnp.exp(l1 - lmax)
    lb = e0 / (e0 + e1)
    hnw = hnw_ref[...]

    C = HGRN_CHUNK
    rr = lax.broadcasted_iota(jnp.int32, (C, C), 0)
    cc = lax.broadcasted_iota(jnp.int32, (C, C), 1)
    causal = rr >= cc
    tri = causal.astype(BF16)

    def hgrn_chunk(c):
        rows = slice(c * C, (c + 1) * C)
        logfs, ks = [], []
        for hd in range(N_HEADS):
            z = proj_scr[rows, hw + hd * dh: hw + (hd + 1) * dh]
            e = jnp.exp(-jnp.abs(z))
            r = 1.0 / (1.0 + e)
            er = e * r
            zp = z >= 0
            sig_pos = jnp.where(zp, r, er)
            sig_neg = jnp.where(zp, er, r)
            lbh = lb[:, hd * dh:(hd + 1) * dh]
            logfs.append(jnp.log(lbh + (1.0 - lbh) * sig_pos))
            ks.append((1.0 - lbh) * sig_neg)
        lf = jnp.concatenate(logfs, axis=1)
        hi = lf.astype(BF16)
        r1 = lf - hi.astype(F32)
        mid = r1.astype(BF16)
        lo = (r1 - mid.astype(F32)).astype(BF16)
        bsum = _mm(tri, hi) + _mm(tri, mid) + _mm(tri, lo)
        for hd in range(N_HEADS):
            cols = slice(hd * dh, (hd + 1) * dh)
            b = bsum[:, cols]
            b_last = b[C - 1:C, :]
            q = proj_scr[rows, cols]
            v = proj_scr[rows, 2 * hw + hd * dh: 2 * hw + (hd + 1) * dh].astype(BF16)
            g = proj_scr[rows, 3 * hw + hd * dh: 3 * hw + (hd + 1) * dh]
            k = ks[hd]
            q_in = (_silu(q) * jnp.exp(b)).astype(BF16)
            k_in = (k * jnp.exp(-b)).astype(BF16)
            k_dec = (k * jnp.exp(b_last - b)).astype(BF16)
            st = sh_scr[hd]
            a = jnp.where(causal, _mm_nt(q_in, k_in), 0.0)
            o = _mm(a.astype(BF16), v) + _mm_nt(q_in, st.astype(BF16))
            sh_scr[hd] = st * jnp.exp(b_last) + _mm_tn(v, k_dec)
            ms = jnp.mean(o * o, axis=-1, keepdims=True)
            on = o * lax.rsqrt(ms + LN_EPS) * hnw[:, hd * dh:(hd + 1) * dh] * _silu(g)
            o_scr[rows, hd * dh:(hd + 1) * dh] = on.astype(BF16)

    rnw = rnw_ref[...]
    RC = RET_CHUNK
    q_scale = dh ** -0.5

    def retention_unit(sub, hd):
        rows = slice(sub * RC, (sub + 1) * RC)
        cs = cos_scr[rows, :]
        sn = sin_scr[rows, :]
        rq = proj2_scr[rows, hd * dh: (hd + 1) * dh]
        rk = proj2_scr[rows, hw + hd * dh: hw + (hd + 1) * dh]
        v = proj2_scr[rows, 2 * hw + hd * dh: 2 * hw + (hd + 1) * dh].astype(BF16)
        g = proj2_scr[rows, 3 * hw + hd * dh: 3 * hw + (hd + 1) * dh]
        q = (rq * cs + pltpu.roll(rq, dh // 2, 1) * sn) * q_scale
        k = rk * cs + pltpu.roll(rk, dh // 2, 1) * sn
        qb = q.astype(BF16)
        st = sr_scr[hd]
        s = _mm_nt(qb, k.astype(BF16)) * dmat_ref[hd]
        o = _mm(s.astype(BF16), v) + _mm_nt(qb, st.astype(BF16)) * qdec_ref[hd]
        sr_scr[hd] = st * ret_chunk_decay[hd] + _mm_tn(v, (k * kdec_ref[hd]).astype(BF16))
        mu = jnp.mean(o, axis=-1, keepdims=True)
        oc = o - mu
        var = jnp.mean(oc * oc, axis=-1, keepdims=True)
        on = oc * lax.rsqrt(var + LN_EPS) * rnw[:, hd * dh:(hd + 1) * dh] * _silu(g)
        o_scr[rows, hw + hd * dh: hw + (hd + 1) * dh] = on.astype(BF16)

    n_chunks = T // C
    units = [(sub, hd) for sub in range(T // RC) for hd in range(N_HEADS)]
    for c in range(n_chunks):
        hgrn_chunk(c)
        for sub, hd in units[c * len(units) // n_chunks:(c + 1) * len(units) // n_chunks]:
            retention_unit(sub, hd)

    y_scr[...] = _mm(o_scr[...], wout_ref[...])


def _retention_tables():
    h = np.arange(N_HEADS, dtype=np.float64)
    log_gamma = np.log(1.0 - np.exp2(-5.0 - h))
    idx = np.arange(RET_CHUNK, dtype=np.float64)
    rel = idx[:, None] - idx[None, :]
    dmat = np.where(rel >= 0, np.exp(np.maximum(rel, 0.0)[None] * log_gamma[:, None, None]), 0.0)
    qdec = np.exp((idx + 1.0)[None, :] * log_gamma[:, None])
    kdec = np.exp((RET_CHUNK - 1.0 - idx)[None, :] * log_gamma[:, None])
    cdec = np.exp(RET_CHUNK * log_gamma)
    bc = lambda t: np.broadcast_to(t[:, :, None], (N_HEADS, RET_CHUNK, D_HEAD))
    return (jnp.asarray(dmat, F32), jnp.asarray(bc(qdec), F32), jnp.asarray(bc(kdec), F32),
            tuple(float(np.float32(v)) for v in cdec))


def _mixer_call(x, positions, mod3, win, wout, hgrn_lb, hnw, rnw, ln1w, ln1b, wr, br, wg, wu, wd):
    B, S, D = x.shape
    T = SEQ_TILE
    nj = S // T
    dmat, qdec, kdec, cdec = _retention_tables()
    inv_freq = np.power(ROPE_BASE, -np.arange(0, D_HEAD, 2, dtype=np.float64) / D_HEAD)
    invf = jnp.asarray(inv_freq[:, None], F32)
    pos3 = positions.reshape(B, 1, S)
    kern = functools.partial(_mixer_kernel, ret_chunk_decay=cdec, tiles_per_seq=nj)
    n_tiles = B * nj
    cur = lambda s: jnp.minimum(s, n_tiles - 1)
    prev = lambda s: jnp.maximum(s - 1, 0)
    E, _, F = wg.shape
    wg2, wu2, wd2 = wg.reshape(E * D, F), wu.reshape(E * D, F), wd.reshape(E * F, D)
    rg, rd = (E * D) // n_tiles, (E * F) // n_tiles
    assert rg * n_tiles == E * D and rd * n_tiles == E * F and rd % 16 == 0
    wspec_g = pl.BlockSpec((rg, F), lambda s: (cur(s), 0))
    wspec_d = pl.BlockSpec((rd, D), lambda s: (cur(s), 0))
    outs = pl.pallas_call(
        kern,
        grid=(n_tiles + 1,),
        in_specs=[
            pl.BlockSpec((1, T, D), lambda s: (cur(s) // nj, cur(s) % nj, 0)),
            pl.BlockSpec((1, 1, T), lambda s: (cur(s) // nj, 0, cur(s) % nj)),
            pl.BlockSpec((1, 6, D), lambda s: (cur(s) // nj, 0, 0)),
            _const_spec(win.shape), _const_spec(wout.shape), _const_spec(hgrn_lb.shape),
            _const_spec(hnw.shape), _const_spec(rnw.shape), _const_spec(ln1w.shape),
            _const_spec(ln1b.shape), _const_spec(invf.shape), _const_spec(dmat.shape),
            _const_spec(qdec.shape), _const_spec(kdec.shape), _const_spec(wr.shape),
            _const_spec(br.shape),
            pl.BlockSpec((1, T, D), lambda s: (prev(s) // nj, prev(s) % nj, 0)),
            pl.BlockSpec((1, 6, D), lambda s: (prev(s) // nj, 0, 0)),
            wspec_g, wspec_g, wspec_d,
        ],
        out_specs=[
            pl.BlockSpec((1, T, D), lambda s: (prev(s) // nj, prev(s) % nj, 0)),
            pl.BlockSpec((T * SUBLANES, LANES), lambda s: (prev(s), 0)),
            pl.BlockSpec((N_LOGIT_ROWS, T), lambda s: (0, prev(s))),
            wspec_g, wspec_g, wspec_d,
        ],
        out_shape=[
            jax.ShapeDtypeStruct((B, S, D), F32),
            jax.ShapeDtypeStruct((B * S * SUBLANES, LANES), F32),
            jax.ShapeDtypeStruct((N_LOGIT_ROWS, B * S), F32),
            jax.ShapeDtypeStruct(wg2.shape, BF16),
            jax.ShapeDtypeStruct(wu2.shape, BF16),
            jax.ShapeDtypeStruct(wd2.shape, BF16),
        ],
        scratch_shapes=[
            pltpu.VMEM((T, D), BF16),
            pltpu.VMEM((T, 4 * N_HEADS * D_HEAD), F32),
            pltpu.VMEM((T, 4 * N_HEADS * D_HEAD), F32),
            pltpu.VMEM((T, D), BF16),
            pltpu.VMEM((T, D_HEAD), F32),
            pltpu.VMEM((T, D_HEAD), F32),
            pltpu.VMEM((N_HEADS, D_HEAD, D_HEAD), F32),
            pltpu.VMEM((N_HEADS, D_HEAD, D_HEAD), F32),
            pltpu.VMEM((T, D), F32),
            pltpu.VMEM((T, D), BF16),
        ],
        compiler_params=pltpu.CompilerParams(
            dimension_semantics=("arbitrary",), vmem_limit_bytes=VMEM_LIMIT),
        name="mixer",
    )(x, pos3, mod3, win, wout, hgrn_lb, hnw, rnw, ln1w, ln1b, invf, dmat, qdec, kdec, wr, br, x, mod3,
      wg2, wu2, wd2)
    x1, h2t, lgt, wg16, wu16, wd16 = outs
    return x1, h2t, lgt, wg16.reshape(E, D, F), wu16.reshape(E, D, F), wd16.reshape(E, F, D)


def _row_copy(src_ref, dst_ref, src_row, dst_row, sem):
    s0 = pl.multiple_of(src_row * SUBLANES, SUBLANES)
    d0 = pl.multiple_of(dst_row * SUBLANES, SUBLANES)
    return pltpu.make_async_copy(src_ref.at[pl.ds(s0, SUBLANES)], dst_ref.at[pl.ds(d0, SUBLANES)], sem)


def _from_token_tiles(buf_ref, first_row, n_rows):
    return jnp.concatenate(
        [buf_ref[pl.ds(first_row * SUBLANES + s, n_rows, stride=SUBLANES), :] for s in range(SUBLANES)],
        axis=1)


def _start_rows(src_ref, dst_ref, idx_ref, idx_base, dst_base, n_rows, sem):
    def issue(r2, carry):
        for u in range(2):
            r = r2 * 2 + u
            _row_copy(src_ref, dst_ref, idx_ref[idx_base + r], dst_base + r, sem).start(priority=u)
        return carry
    lax.fori_loop(0, n_rows // 2, issue, 0)


def _wait_rows(src_ref, dst_ref, n_rows, sem):
    pltpu.make_async_copy(src_ref.at[pl.ds(0, n_rows * SUBLANES)], dst_ref, sem).wait()


def _bf16_bits(x):
    return lax.bitcast_convert_type(x.astype(BF16).astype(F32), jnp.uint32)


def _dispatch_kernel(dest_ref, h2t_ref, zeros_ref, xs_ref, buf, load_sems, scat_sems):
    del zeros_ref
    i = pl.program_id(0)
    n = pl.num_programs(0)
    TT = DISPATCH_TILE
    rows = TT * SUBLANES

    def load(tile, slot):
        src = h2t_ref.at[pl.ds(pl.multiple_of(tile * rows, rows), rows)]
        return pltpu.make_async_copy(src, buf.at[slot], load_sems.at[slot])

    def wait_scatter(slot):
        _wait_rows(buf.at[slot], xs_ref.at[pl.ds(0, rows)], TT, scat_sems.at[slot])

    @pl.when(i == 0)
    def _():
        load(0, 0).start()

    for slot in range(DISPATCH_RING):
        @pl.when((i + 1 < n) & ((i + 1) % DISPATCH_RING == slot))
        def _():
            @pl.when(i + 1 >= DISPATCH_RING)
            def _():
                wait_scatter(slot)
            load(i + 1, slot).start()

    for slot in range(DISPATCH_RING):
        @pl.when(i % DISPATCH_RING == slot)
        def _():
            load(i, slot).wait()

            def issue(r2, carry):
                for u in range(2):
                    r = r2 * 2 + u
                    _row_copy(buf.at[slot], xs_ref, r, dest_ref[i * TT + r],
                              scat_sems.at[slot]).start(priority=u)
                return carry
            lax.fori_loop(0, TT // 2, issue, 0)

    @pl.when(i == n - 1)
    def _():
        for slot in range(DISPATCH_RING):
            wait_scatter(slot)


def _dispatch_call(dest, h2t, n_slots):
    N = dest.shape[0]
    TT = DISPATCH_TILE
    assert N // TT >= DISPATCH_RING
    grid_spec = pltpu.PrefetchScalarGridSpec(
        num_scalar_prefetch=1,
        grid=(N // TT,),
        in_specs=[pl.BlockSpec(memory_space=pl.ANY), pl.BlockSpec(memory_space=pl.ANY)],
        out_specs=pl.BlockSpec(memory_space=pl.ANY),
        scratch_shapes=[pltpu.VMEM((DISPATCH_RING, TT * SUBLANES, LANES), F32),
                        pltpu.SemaphoreType.DMA((DISPATCH_RING,)),
                        pltpu.SemaphoreType.DMA((DISPATCH_RING,))],
    )
    return pl.pallas_call(
        _dispatch_kernel,
        grid_spec=grid_spec,
        out_shape=jax.ShapeDtypeStruct((n_slots * SUBLANES, LANES), F32),
        input_output_aliases={2: 0},
        compiler_params=pltpu.CompilerParams(dimension_semantics=("arbitrary",)),
        name="dispatch",
    )(dest, h2t, jnp.zeros((n_slots * SUBLANES, LANES), F32))


def _expert_kernel(ea_ref, eb_ref, nv_ref, xs_ref,
                   wga_ref, wua_ref, wda_ref, wgb_ref, wub_ref, wdb_ref, y_ref):
    i = pl.program_id(0)
    R = SLOT_BLOCK
    del ea_ref, eb_ref

    def ffn(xb, wg_ref, wu_ref, wd_ref):
        a = _mm(xb, wg_ref[0])
        u = _mm(xb, wu_ref[0])
        return _mm((_silu(a) * u).astype(BF16), wd_ref[0])

    @pl.when(i < nv_ref[0])
    def _():
        xb = _from_token_tiles(xs_ref, 0, R).astype(BF16)
        ya = ffn(xb, wga_ref, wua_ref, wda_ref)
        yb = ffn(xb, wgb_ref, wub_ref, wdb_ref)
        packed = _bf16_bits(ya) | (_bf16_bits(yb) >> 16)
        for s in range(SUBLANES):
            y_ref[pl.ds(s, R, stride=SUBLANES), :] = packed[:, s * LANES:(s + 1) * LANES]

    @pl.when(i >= nv_ref[0])
    def _():
        y_ref[...] = jnp.zeros_like(y_ref)


def _expert_call(block_ea, block_eb, n_valid, xs, wg, wu, wd):
    R = SLOT_BLOCK
    n_blocks = xs.shape[0] // (R * SUBLANES)
    D, F = wg.shape[1], wg.shape[2]
    wa = lambda i, ea, eb, nv: (ea[i], 0, 0)
    wb = lambda i, ea, eb, nv: (eb[i], 0, 0)
    blk = lambda i, ea, eb, nv: (jnp.minimum(i, nv[0] - 1), 0)
    grid_spec = pltpu.PrefetchScalarGridSpec(
        num_scalar_prefetch=3,
        grid=(n_blocks,),
        in_specs=[
            pl.BlockSpec((R * SUBLANES, LANES), blk),
            pl.BlockSpec((1, D, F), wa), pl.BlockSpec((1, D, F), wa), pl.BlockSpec((1, F, D), wa),
            pl.BlockSpec((1, D, F), wb), pl.BlockSpec((1, D, F), wb), pl.BlockSpec((1, F, D), wb),
        ],
        out_specs=pl.BlockSpec((R * SUBLANES, LANES), lambda i, ea, eb, nv: (i, 0)),
    )
    return pl.pallas_call(
        _expert_kernel,
        grid_spec=grid_spec,
        out_shape=jax.ShapeDtypeStruct(xs.shape, jnp.uint32),
        compiler_params=pltpu.CompilerParams(
            dimension_semantics=("arbitrary",), vmem_limit_bytes=VMEM_LIMIT),
        name="experts",
    )(block_ea, block_eb, n_valid, xs, wg, wu, wd, wg, wu, wd)


def _final_kernel(dest_ref, yt_ref, x1_ref, g_ref, mod_ref, w_ref, b_ref, o_ref, buf, sems):
    i = pl.program_id(0)
    n_steps = pl.num_programs(0)
    TT = TOK_TILE

    def start_tile(tile, slot):
        _start_rows(yt_ref, buf.at[slot], dest_ref, tile * TT, 0, TT, sems.at[slot])

    @pl.when(i == 0)
    def _():
        start_tile(0, 0)

    for slot in range(2):
        @pl.when((i + 1 < n_steps) & ((i + 1) % 2 == slot))
        def _():
            start_tile(i + 1, slot)

    gate2 = mod_ref[0, 5:6, :]
    g = g_ref[...]
    for slot in range(2):
        @pl.when(i % 2 == slot)
        def _():
            _wait_rows(yt_ref, buf.at[slot], TT, sems.at[slot])
            for r in range(TT // ROW_BLOCK):
                rows = slice(r * ROW_BLOCK, (r + 1) * ROW_BLOCK)
                packed = _from_token_tiles(buf.at[slot], r * ROW_BLOCK, ROW_BLOCK)
                ya = lax.bitcast_convert_type(packed & jnp.uint32(0xFFFF0000), F32)
                yb = lax.bitcast_convert_type(packed << 16, F32)
                y = ya * g[rows, 0:1] + yb * g[rows, 1:2]
                u = DEEPNORM_ALPHA * x1_ref[rows, :] + gate2 * y
                o_ref[rows, :] = _ln(u) * w_ref[...] + b_ref[...]


def _final_call(dest, yt, x1, gates, mod3, w, b, seq_len):
    N, D = x1.shape
    TT = TOK_TILE
    per_seq = seq_len // TT
    grid_spec = pltpu.PrefetchScalarGridSpec(
        num_scalar_prefetch=1,
        grid=(N // TT,),
        in_specs=[
            pl.BlockSpec(memory_space=pl.ANY),
            pl.BlockSpec((TT, D), lambda i, d: (i, 0)),
            pl.BlockSpec((TT, 2), lambda i, d: (i, 0)),
            pl.BlockSpec((1, 6, D), lambda i, d: (i // per_seq, 0, 0)),
            pl.BlockSpec((1, D), lambda i, d: (0, 0)),
            pl.BlockSpec((1, D), lambda i, d: (0, 0)),
        ],
        out_specs=pl.BlockSpec((TT, D), lambda i, d: (i, 0)),
        scratch_shapes=[pltpu.VMEM((2, TT * SUBLANES, LANES), jnp.uint32), pltpu.SemaphoreType.DMA((2,))],
    )
    return pl.pallas_call(
        _final_kernel,
        grid_spec=grid_spec,
        out_shape=jax.ShapeDtypeStruct((N, D), F32),
        compiler_params=pltpu.CompilerParams(
            dimension_semantics=("arbitrary",), vmem_limit_bytes=VMEM_LIMIT),
        name="combine_ln",
    )(dest, yt, x1, gates, mod3, w, b)


def _route_kernel(lgt_ref, rf_ref, ri_ref, meta_ref, cnt_scr, base_scr, bstart_scr):
    ph = pl.program_id(0)
    t = pl.program_id(1)
    TR = ROUTE_TILE
    E = N_EXPERTS
    R = SLOT_BLOCK
    G = EXPERTS_PER_GROUP
    lg = lgt_ref[...]

    g = [lg[E + k:E + k + 1, :] for k in range(N_GROUPS)]
    gmax = jnp.maximum(jnp.maximum(g[0], g[1]), jnp.maximum(g[2], g[3]))
    gs = jnp.where(g[0] == gmax, 0.0, jnp.where(g[1] == gmax, 1.0, jnp.where(g[2] == gmax, 2.0, 3.0)))
    psum = (jnp.exp(g[0] - gmax) + jnp.exp(g[1] - gmax)) + (jnp.exp(g[2] - gmax) + jnp.exp(g[3] - gmax))
    p_star = 1.0 / psum
    esel = jnp.where(gs == 0.0, lg[0:G], jnp.where(gs == 1.0, lg[G:2 * G],
                                                   jnp.where(gs == 2.0, lg[2 * G:3 * G], lg[3 * G:4 * G])))
    sub = lax.broadcasted_iota(jnp.int32, (G, TR), 0).astype(F32)
    m1 = jnp.max(esel, axis=0, keepdims=True)
    i1 = jnp.min(jnp.where(esel == m1, sub, float(G)), axis=0, keepdims=True)
    es2 = jnp.where(sub == i1, -jnp.inf, esel)
    m2 = jnp.max(es2, axis=0, keepdims=True)
    i2 = jnp.min(jnp.where(es2 == m2, sub, float(G)), axis=0, keepdims=True)
    d = jnp.exp(m2 - m1)
    w1 = 1.0 / (1.0 + d)
    w2 = d * w1
    lo = jnp.minimum(i1, i2)
    hi = jnp.maximum(i1, i2)
    bucket = gs * PAIRS_PER_GROUP + (lo * (2 * G - 1 - lo) * 0.5 + (hi - lo - 1.0))
    first_is_lo = i1 < i2
    gate_a = p_star * jnp.where(first_is_lo, w1, w2)
    gate_b = p_star * jnp.where(first_is_lo, w2, w1)
    row = lax.broadcasted_iota(jnp.int32, (NB, TR), 0).astype(F32)
    oh = row == bucket
    cnt = oh.astype(F32)

    @pl.when(ph == 0)
    def _():
        @pl.when(t == 0)
        def _():
            cnt_scr[...] = jnp.zeros_like(cnt_scr)
        acc = cnt_scr[...]
        for j in range(TR // LANES):
            acc = acc + cnt[:, j * LANES:(j + 1) * LANES]
        cnt_scr[...] = acc

    @pl.when((ph == 1) & (t == 0))
    def _():
        counts = jnp.sum(cnt_scr[...], axis=1, keepdims=True)
        nblk = jnp.floor((counts + (R - 1)) * (1.0 / R))
        nblk_b = jnp.broadcast_to(nblk, (NB, LANES))
        rb = lax.broadcasted_iota(jnp.int32, (NB, NB), 0)
        cb = lax.broadcasted_iota(jnp.int32, (NB, NB), 1)
        bstart = _mm((cb < rb).astype(BF16), nblk_b.astype(BF16))
        bstart_scr[...] = bstart
        base_scr[...] = jnp.zeros_like(base_scr)
        k = lax.broadcasted_iota(jnp.int32, (NB, 1), 0).astype(F32)
        grp = sum((k >= float(m * PAIRS_PER_GROUP)).astype(F32) for m in range(1, N_GROUPS))
        p = k - grp * PAIRS_PER_GROUP
        pair_start = [i * (2 * G - 1 - i) // 2 for i in range(G - 1)]
        ia = sum((p >= float(s)).astype(F32) for s in pair_start[1:])
        ib = p - ia * (2 * G - 1 - ia) * 0.5 + ia + 1.0
        n_pad = meta_ref.shape[1]
        blk = lax.broadcasted_iota(jnp.int32, (NB, n_pad), 1).astype(F32)
        first = jnp.broadcast_to(bstart[:, 0:1], (NB, n_pad))
        last = jnp.broadcast_to((bstart + nblk_b)[:, 0:1], (NB, n_pad))
        member = (first <= blk) & (blk < last)
        ea = jnp.sum(jnp.where(member, grp * G + ia, 0.0), axis=0, keepdims=True)
        eb = jnp.sum(jnp.where(member, grp * G + ib, 0.0), axis=0, keepdims=True)
        meta_ref[...] = jnp.zeros_like(meta_ref)
        meta_ref[0:1, :] = ea.astype(jnp.int32)
        meta_ref[1:2, :] = eb.astype(jnp.int32)
        meta_ref[2:3, :] = jnp.max(last, axis=0, keepdims=True).astype(jnp.int32)

    @pl.when(ph == 1)
    def _():
        ri = lax.broadcasted_iota(jnp.int32, (LANES, 2 * LANES), 0)
        ci = lax.broadcasted_iota(jnp.int32, (LANES, 2 * LANES), 1)
        w = ((ci >= ri) | (ci >= LANES)).astype(BF16)
        base = base_scr[...]
        slot0 = bstart_scr[...] * R
        ds = []
        for j in range(TR // LANES):
            sl = slice(j * LANES, (j + 1) * LANES)
            blk = cnt[:, sl]
            res = _mm(blk.astype(BF16), w)
            val = slot0 + base + (res[:, 0:LANES] - blk)
            ds.append(jnp.sum(jnp.where(oh[:, sl], val, 0.0), axis=0, keepdims=True))
            base = base + res[:, LANES:2 * LANES]
        base_scr[...] = base
        ri_ref[...] = jnp.zeros_like(ri_ref)
        ri_ref[0:1, :] = jnp.concatenate(ds, axis=1).astype(jnp.int32)
        rf_ref[...] = jnp.zeros_like(rf_ref)
        rf_ref[0:1, :] = gate_a
        rf_ref[1:2, :] = gate_b


def _route_call(lgt, n_blocks):
    rows, N = lgt.shape
    TR = ROUTE_TILE
    E = NB
    n_pad = -(-n_blocks // LANES) * LANES
    return pl.pallas_call(
        _route_kernel,
        grid=(2, N // TR),
        in_specs=[pl.BlockSpec((rows, TR), lambda ph, t: (0, t))],
        out_specs=[
            pl.BlockSpec((SUBLANES, TR), lambda ph, t: (0, ph * t)),
            pl.BlockSpec((SUBLANES, TR), lambda ph, t: (0, ph * t)),
            pl.BlockSpec((SUBLANES, n_pad), lambda ph, t: (0, 0)),
        ],
        out_shape=[
            jax.ShapeDtypeStruct((SUBLANES, N), F32),
            jax.ShapeDtypeStruct((SUBLANES, N), jnp.int32),
            jax.ShapeDtypeStruct((SUBLANES, n_pad), jnp.int32),
        ],
        scratch_shapes=[pltpu.VMEM((E, LANES), F32), pltpu.VMEM((E, LANES), F32), pltpu.VMEM((E, LANES), F32)],
        compiler_params=pltpu.CompilerParams(dimension_semantics=("arbitrary", "arbitrary")),
        name="route",
    )(lgt)


def kernel(x, c, positions, w_ada, b_ada, w_in, w_out, hgrn_lb, hgrn_norm_w, ret_norm_w, post_ln1_w,
           post_ln1_b, w_rg, b_rg, w_re, b_re, w_gate, w_up, w_down, post_ln2_w, post_ln2_b):
    B, S, D = x.shape
    N = B * S
    hw = N_HEADS * D_HEAD

    mod3 = _ada_call(c, w_ada[0], b_ada[0]).reshape(B, 6, D)

    win = w_in[0].astype(BF16)
    wout = w_out[0].astype(BF16)
    pad = LANES - N_EXPERTS - N_GROUPS
    wr = jnp.concatenate([w_re[0], w_rg[0], jnp.zeros((D, pad), F32)], axis=1).astype(BF16)
    br = jnp.concatenate([b_re[0], b_rg[0], jnp.zeros((pad,), F32)]).reshape(1, LANES)

    x1, h2t, lgt, wg16, wu16, wd16 = _mixer_call(
        x, positions, mod3, win, wout, hgrn_lb, hgrn_norm_w[0].reshape(1, hw),
        ret_norm_w[0].reshape(1, hw), post_ln1_w[0].reshape(1, D), post_ln1_b[0].reshape(1, D), wr, br,
        w_gate[0], w_up[0], w_down[0])

    n_blocks = N // SLOT_BLOCK + N_GROUPS * PAIRS_PER_GROUP
    rf, ri, meta = _route_call(lgt, n_blocks)
    dest = ri[0]
    gates = rf[0:2].T
    xs = _dispatch_call(dest, h2t, n_blocks * SLOT_BLOCK)
    yt = _expert_call(meta[0, :n_blocks], meta[1, :n_blocks], meta[2, 0:1], xs, wg16, wu16, wd16)
    out = _final_call(dest, yt, x1.reshape(N, D), gates, mod3, post_ln2_w[0].reshape(1, D),
                      post_ln2_b[0].reshape(1, D), S)
    return out.reshape(B, S, D)
```

```python
import functools
import math

import numpy as np
import jax
import jax.numpy as jnp
from jax import lax
from jax.experimental import pallas as pl
from jax.experimental.pallas import tpu as pltpu

F32 = jnp.float32
BF16 = jnp.bfloat16

D_MODEL = 1024
N_HEADS = 4
D_HEAD = 128
HGRN_CHUNK = 64
RET_CHUNK = 256
ROPE_BASE = 10000.0
N_EXPERTS = 32
EXPERTS_PER_GROUP = 8
N_GROUPS = 4
D_EXPERT = 512
DEEPNORM_ALPHA = 2.0 ** 0.25
LN_EPS = 1e-5

SEQ_TILE = 512
ROW_BLOCK = 128
PAIRS_PER_GROUP = EXPERTS_PER_GROUP * (EXPERTS_PER_GROUP - 1) // 2
NB = 128
SLOT_BLOCK = 256
DISPATCH_TILE = 512
DISPATCH_RING = 3
PAD_COPY_SIZES = (128, 64, 32, 16, 8, 4, 2, 1)
TOK_TILE = 512
ROUTE_TILE = 2048
N_LOGIT_ROWS = 40
LANES = 128
SUBLANES = 8
VMEM_LIMIT = 56 * 1024 * 1024


def _mm(a, b):
    return jnp.dot(a, b, preferred_element_type=F32)


def _mm_nt(a, b):
    return lax.dot_general(a, b, (((1,), (1,)), ((), ())), preferred_element_type=F32)


def _mm_tn(a, b):
    return lax.dot_general(a, b, (((0,), (0,)), ((), ())), preferred_element_type=F32)


def _sigmoid(x):
    return 1.0 / (1.0 + jnp.exp(-x))


def _silu(x):
    return x * _sigmoid(x)


def _ln(x):
    mu = jnp.mean(x, axis=-1, keepdims=True)
    xc = x - mu
    var = jnp.mean(xc * xc, axis=-1, keepdims=True)
    return xc * lax.rsqrt(var + LN_EPS)


def _const_spec(shape):
    nd = len(shape)
    return pl.BlockSpec(shape, lambda *_: (0,) * nd, pipeline_mode=pl.Buffered(1))


def _ada_kernel(c_ref, w_ref, b_ref, o_ref):
    ca = _silu(c_ref[...])
    o_ref[...] = _mm(ca.astype(BF16), w_ref[...].astype(BF16)) + b_ref[...]


def _ada_call(c, w_ada, b_ada):
    B, D = c.shape
    n_out = w_ada.shape[1]
    return pl.pallas_call(
        _ada_kernel,
        grid=(n_out // D,),
        in_specs=[pl.BlockSpec((B, D), lambda j: (0, 0)),
                  pl.BlockSpec((D, D), lambda j: (0, j)),
                  pl.BlockSpec((1, D), lambda j: (0, j))],
        out_specs=pl.BlockSpec((B, D), lambda j: (0, j)),
        out_shape=jax.ShapeDtypeStruct((B, n_out), F32),
        name="ada_mod",
    )(c, w_ada, b_ada.reshape(1, n_out))


def _mixer_kernel(x_ref, pos_ref, mod_ref, win_ref, wout_ref, lb_ref, hnw_ref, rnw_ref,
                  ln1w_ref, ln1b_ref, invf_ref, dmat_ref, qdec_ref, kdec_ref, wr_ref, br_ref,
                  xp_ref, modp_ref, wg32_ref, wu32_ref, wd32_ref,
                  x1_ref, h2t_ref, lg_ref, wg16_ref, wu16_ref, wd16_ref,
                  h_scr, proj_scr, proj2_scr, o_scr, cos_scr, sin_scr, sh_scr, sr_scr, y_scr, h2_scr,
                  *, ret_chunk_decay, tiles_per_seq):
    T = SEQ_TILE
    D = D_MODEL
    dh = D_HEAD
    hw = N_HEADS * dh
    hw4 = 4 * hw
    step = pl.program_id(0)
    n_tiles = pl.num_programs(0) - 1

    wg16_ref[...] = wg32_ref[...].astype(BF16)
    wu16_ref[...] = wu32_ref[...].astype(BF16)
    wd16_ref[...] = wd32_ref[...].astype(BF16)

    @pl.when(step == 0)
    def _():
        y_scr[...] = jnp.zeros_like(y_scr)

    @pl.when((step < n_tiles) & (step % tiles_per_seq == 0))
    def _():
        sh_scr[...] = jnp.zeros_like(sh_scr)
        sr_scr[...] = jnp.zeros_like(sr_scr)

    gate1p = modp_ref[0, 2:3, :]
    shift2p = modp_ref[0, 3:4, :]
    scale2p = modp_ref[0, 4:5, :]

    def post_block(r):
        rows = slice(r * ROW_BLOCK, (r + 1) * ROW_BLOCK)
        u = DEEPNORM_ALPHA * xp_ref[0, rows, :] + gate1p * y_scr[rows, :]
        x1 = _ln(u) * ln1w_ref[...] + ln1b_ref[...]
        x1_ref[0, rows, :] = x1
        h2 = (_ln(x1) * (1.0 + scale2p) + shift2p).astype(BF16)
        h2_scr[rows, :] = h2
        h2f = h2.astype(F32)
        for s in range(SUBLANES):
            h2t_ref[pl.ds(r * ROW_BLOCK * SUBLANES + s, ROW_BLOCK, stride=SUBLANES), :] = (
                h2f[:, s * LANES:(s + 1) * LANES])

    shift1 = mod_ref[0, 0:1, :]
    scale1 = mod_ref[0, 1:2, :]

    for r in range(T // ROW_BLOCK):
        rows = slice(r * ROW_BLOCK, (r + 1) * ROW_BLOCK)
        h = _ln(x_ref[0, rows, :]) * (1.0 + scale1) + shift1
        h_scr[rows, :] = h.astype(BF16)

    n_post = T // ROW_BLOCK
    proj_scr[...] = _mm(h_scr[...], win_ref[:, 0:hw4])
    for r in range(n_post // 2):
        post_block(r)
    proj2_scr[...] = _mm(h_scr[...], win_ref[:, hw4:2 * hw4])
    for r in range(n_post // 2, n_post):
        post_block(r)
    lg_ref[...] = (_mm(h2_scr[...], wr_ref[...]) + br_ref[...]).T[0:N_LOGIT_ROWS, :]

    pos = pos_ref[0].astype(F32)
    ang_t = invf_ref[...] * pos
    cos_t = jnp.cos(ang_t)
    sin_t = jnp.sin(ang_t)
    cos_scr[...] = jnp.concatenate([cos_t, cos_t], axis=0).T
    sin_scr[...] = jnp.concatenate([-sin_t, sin_t], axis=0).T

    l0 = lb_ref[0:1, :]
    l1 = lb_ref[1:2, :]
    lmax = jnp.maximum(l0, l1)
    e0 = jnp.exp(l0 - lmax)
    e1 = jnp.exp(l1 - lmax)
    lb = e0 / (e0 + e1)
    hnw = hnw_ref[...]

    C = HGRN_CHUNK
    rr = lax.broadcasted_iota(jnp.int32, (C, C), 0)
    cc = lax.broadcasted_iota(jnp.int32, (C, C), 1)
    causal = rr >= cc
    tri = causal.astype(BF16)

    def hgrn_chunk(c):
        rows = slice(c * C, (c + 1) * C)
        logfs, ks = [], []
        for hd in range(N_HEADS):
            z = proj_scr[rows, hw + hd * dh: hw + (hd + 1) * dh]
            e = jnp.exp(-jnp.abs(z))
            r = 1.0 / (1.0 + e)
            er = e * r
            zp = z >= 0
            sig_pos = jnp.where(zp, r, er)
            sig_neg = jnp.where(zp, er, r)
            lbh = lb[:, hd * dh:(hd + 1) * dh]
            logfs.append(jnp.log(lbh + (1.0 - lbh) * sig_pos))
            ks.append((1.0 - lbh) * sig_neg)
        lf = jnp.concatenate(logfs, axis=1)
        hi = lf.astype(BF16)
        r1 = lf - hi.astype(F32)
        mid = r1.astype(BF16)
        lo = (r1 - mid.astype(F32)).astype(BF16)
        bsum = _mm(tri, hi) + _mm(tri, mid) + _mm(tri, lo)
        for hd in range(N_HEADS):
            cols = slice(hd * dh, (hd + 1) * dh)
            b = bsum[:, cols]
            b_last = b[C - 1:C, :]
            q = proj_scr[rows, cols]
            v = proj_scr[rows, 2 * hw + hd * dh: 2 * hw + (hd + 1) * dh].astype(BF16)
            g = proj_scr[rows, 3 * hw + hd * dh: 3 * hw + (hd + 1) * dh]
            k = ks[hd]
            q_in = (_silu(q) * jnp.exp(b)).astype(BF16)
            k_in = (k * jnp.exp(-b)).astype(BF16)
            k_dec = (k * jnp.exp(b_last - b)).astype(BF16)
            st = sh_scr[hd]
            a = jnp.where(causal, _mm_nt(q_in, k_in), 0.0)
            o = _mm(a.astype(BF16), v) + _mm_nt(q_in, st.astype(BF16))
            sh_scr[hd] = st * jnp.exp(b_last) + _mm_tn(v, k_dec)
            ms = jnp.mean(o * o, axis=-1, keepdims=True)
            on = o * lax.rsqrt(ms + LN_EPS) * hnw[:, hd * dh:(hd + 1) * dh] * _silu(g)
            o_scr[rows, hd * dh:(hd + 1) * dh] = on.astype(BF16)

    rnw = rnw_ref[...]
    RC = RET_CHUNK
    q_scale = dh ** -0.5

    def retention_unit(sub, hd):
        rows = slice(sub * RC, (sub + 1) * RC)
        cs = cos_scr[rows, :]
        sn = sin_scr[rows, :]
        rq = proj2_scr[rows, hd * dh: (hd + 1) * dh]
        rk = proj2_scr[rows, hw + hd * dh: hw + (hd + 1) * dh]
        v = proj2_scr[rows, 2 * hw + hd * dh: 2 * hw + (hd + 1) * dh].astype(BF16)
        g = proj2_scr[rows, 3 * hw + hd * dh: 3 * hw + (hd + 1) * dh]
        q = (rq * cs + pltpu.roll(rq, dh // 2, 1) * sn) * q_scale
        k = rk * cs + pltpu.roll(rk, dh // 2, 1) * sn
        qb = q.astype(BF16)
        st = sr_scr[hd]
        s = _mm_nt(qb, k.astype(BF16)) * dmat_ref[hd]
        o = _mm(s.astype(BF16), v) + _mm_nt(qb, st.astype(BF16)) * qdec_ref[hd]
        sr_scr[hd] = st * ret_chunk_decay[hd] + _mm_tn(v, (k * kdec_ref[hd]).astype(BF16))
        mu = jnp.mean(o, axis=-1, keepdims=True)
        oc = o - mu
        var = jnp.mean(oc * oc, axis=-1, keepdims=True)
        on = oc * lax.rsqrt(var + LN_EPS) * rnw[:, hd * dh:(hd + 1) * dh] * _silu(g)
        o_scr[rows, hw + hd * dh: hw + (hd + 1) * dh] = on.astype(BF16)

    n_chunks = T // C
    units = [(sub, hd) for sub in range(T // RC) for hd in range(N_HEADS)]
    for c in range(n_chunks):
        hgrn_chunk(c)
        for sub, hd in units[c * len(units) // n_chunks:(c + 1) * len(units) // n_chunks]:
            retention_unit(sub, hd)

    y_scr[...] = _mm(o_scr[...], wout_ref[...])


def _retention_tables():
    h = np.arange(N_HEADS, dtype=np.float64)
    log_gamma = np.log(1.0 - np.exp2(-5.0 - h))
    idx = np.arange(RET_CHUNK, dtype=np.float64)
    rel = idx[:, None] - idx[None, :]
    dmat = np.where(rel >= 0, np.exp(np.maximum(rel, 0.0)[None] * log_gamma[:, None, None]), 0.0)
    qdec = np.exp((idx + 1.0)[None, :] * log_gamma[:, None])
    kdec = np.exp((RET_CHUNK - 1.0 - idx)[None, :] * log_gamma[:, None])
    cdec = np.exp(RET_CHUNK * log_gamma)
    bc = lambda t: np.broadcast_to(t[:, :, None], (N_HEADS, RET_CHUNK, D_HEAD))
    return (jnp.asarray(dmat, F32), jnp.asarray(bc(qdec), F32), jnp.asarray(bc(kdec), F32),
            tuple(float(np.float32(v)) for v in cdec))


def _mixer_call(x, positions, mod3, win, wout, hgrn_lb, hnw, rnw, ln1w, ln1b, wr, br, wg, wu, wd):
    B, S, D = x.shape
    T = SEQ_TILE
    nj = S // T
    dmat, qdec, kdec, cdec = _retention_tables()
    inv_freq = np.power(ROPE_BASE, -np.arange(0, D_HEAD, 2, dtype=np.float64) / D_HEAD)
    invf = jnp.asarray(inv_freq[:, None], F32)
    pos3 = positions.reshape(B, 1, S)
    kern = functools.partial(_mixer_kernel, ret_chunk_decay=cdec, tiles_per_seq=nj)
    n_tiles = B * nj
    cur = lambda s: jnp.minimum(s, n_tiles - 1)
    prev = lambda s: jnp.maximum(s - 1, 0)
    E, _, F = wg.shape
    wg2, wu2, wd2 = wg.reshape(E * D, F), wu.reshape(E * D, F), wd.reshape(E * F, D)
    rg, rd = (E * D) // n_tiles, (E * F) // n_tiles
    assert rg * n_tiles == E * D and rd * n_tiles == E * F and rd % 16 == 0
    wspec_g = pl.BlockSpec((rg, F), lambda s: (cur(s), 0))
    wspec_d = pl.BlockSpec((rd, D), lambda s: (cur(s), 0))
    outs = pl.pallas_call(
        kern,
        grid=(n_tiles + 1,),
        in_specs=[
            pl.BlockSpec((1, T, D), lambda s: (cur(s) // nj, cur(s) % nj, 0)),
            pl.BlockSpec((1, 1, T), lambda s: (cur(s) // nj, 0, cur(s) % nj)),
            pl.BlockSpec((1, 6, D), lambda s: (cur(s) // nj, 0, 0)),
            _const_spec(win.shape), _const_spec(wout.shape), _const_spec(hgrn_lb.shape),
            _const_spec(hnw.shape), _const_spec(rnw.shape), _const_spec(ln1w.shape),
            _const_spec(ln1b.shape), _const_spec(invf.shape), _const_spec(dmat.shape),
            _const_spec(qdec.shape), _const_spec(kdec.shape), _const_spec(wr.shape),
            _const_spec(br.shape),
            pl.BlockSpec((1, T, D), lambda s: (prev(s) // nj, prev(s) % nj, 0)),
            pl.BlockSpec((1, 6, D), lambda s: (prev(s) // nj, 0, 0)),
            wspec_g, wspec_g, wspec_d,
        ],
        out_specs=[
            pl.BlockSpec((1, T, D), lambda s: (prev(s) // nj, prev(s) % nj, 0)),
            pl.BlockSpec((T * SUBLANES, LANES), lambda s: (prev(s), 0)),
            pl.BlockSpec((N_LOGIT_ROWS, T), lambda s: (0, prev(s))),
            wspec_g, wspec_g, wspec_d,
        ],
        out_shape=[
            jax.ShapeDtypeStruct((B, S, D), F32),
            jax.ShapeDtypeStruct((B * S * SUBLANES, LANES), F32),
            jax.ShapeDtypeStruct((N_LOGIT_ROWS, B * S), F32),
            jax.ShapeDtypeStruct(wg2.shape, BF16),
            jax.ShapeDtypeStruct(wu2.shape, BF16),
            jax.ShapeDtypeStruct(wd2.shape, BF16),
        ],
        scratch_shapes=[
            pltpu.VMEM((T, D), BF16),
            pltpu.VMEM((T, 4 * N_HEADS * D_HEAD), F32),
            pltpu.VMEM((T, 4 * N_HEADS * D_HEAD), F32),
            pltpu.VMEM((T, D), BF16),
            pltpu.VMEM((T, D_HEAD), F32),
            pltpu.VMEM((T, D_HEAD), F32),
            pltpu.VMEM((N_HEADS, D_HEAD, D_HEAD), F32),
            pltpu.VMEM((N_HEADS, D_HEAD, D_HEAD), F32),
            pltpu.VMEM((T, D), F32),
            pltpu.VMEM((T, D), BF16),
        ],
        compiler_params=pltpu.CompilerParams(
            dimension_semantics=("arbitrary",), vmem_limit_bytes=VMEM_LIMIT),
        name="mixer",
    )(x, pos3, mod3, win, wout, hgrn_lb, hnw, rnw, ln1w, ln1b, invf, dmat, qdec, kdec, wr, br, x, mod3,
      wg2, wu2, wd2)
    x1, h2t, lgt, wg16, wu16, wd16 = outs
    return x1, h2t, lgt, wg16.reshape(E, D, F), wu16.reshape(E, D, F), wd16.reshape(E, F, D)


def _row_copy(src_ref, dst_ref, src_row, dst_row, sem):
    s0 = pl.multiple_of(src_row * SUBLANES, SUBLANES)
    d0 = pl.multiple_of(dst_row * SUBLANES, SUBLANES)
    return pltpu.make_async_copy(src_ref.at[pl.ds(s0, SUBLANES)], dst_ref.at[pl.ds(d0, SUBLANES)], sem)


def _from_token_tiles(buf_ref, first_row, n_rows):
    return jnp.concatenate(
        [buf_ref[pl.ds(first_row * SUBLANES + s, n_rows, stride=SUBLANES), :] for s in range(SUBLANES)],
        axis=1)


def _start_rows(src_ref, dst_ref, idx_ref, idx_base, dst_base, n_rows, sem):
    def issue(r2, carry):
        for u in range(2):
            r = r2 * 2 + u
            _row_copy(src_ref, dst_ref, idx_ref[idx_base + r], dst_base + r, sem).start(priority=u)
        return carry
    lax.fori_loop(0, n_rows // 2, issue, 0)


def _wait_rows(src_ref, dst_ref, n_rows, sem):
    pltpu.make_async_copy(src_ref.at[pl.ds(0, n_rows * SUBLANES)], dst_ref, sem).wait()


def _bf16_bits(x):
    return lax.bitcast_convert_type(x.astype(BF16).astype(F32), jnp.uint32)


def _dispatch_kernel(dest_ref, pad0_ref, padn_ref, nv_ref, h2t_ref, xs_ref,
                     buf, zbuf, load_sems, scat_sems, zsem):
    i = pl.program_id(0)
    n = pl.num_programs(0)
    TT = DISPATCH_TILE
    rows = TT * SUBLANES

    def pad_copies(wait):
        def bucket(b, carry):
            pos = pad0_ref[b]
            left = padn_ref[b]
            for size in PAD_COPY_SIZES:
                hit = (left & size) != 0
                first = 0 if wait else pl.multiple_of(pos * SUBLANES, SUBLANES)
                copy = pltpu.make_async_copy(
                    zbuf.at[pl.ds(0, size * SUBLANES)], xs_ref.at[pl.ds(first, size * SUBLANES)], zsem)

                @pl.when(hit)
                def _():
                    copy.wait() if wait else copy.start()
                pos = pos + jnp.where(hit, size, 0)
            return carry
        lax.fori_loop(0, pad0_ref.shape[0], bucket, 0)

        zrows = PAD_COPY_SIZES[0] * SUBLANES

        def unused_block(blk, carry):
            for part in range(SLOT_BLOCK // PAD_COPY_SIZES[0]):
                first = 0 if wait else pl.multiple_of(blk * SLOT_BLOCK * SUBLANES + part * zrows, zrows)
                copy = pltpu.make_async_copy(zbuf, xs_ref.at[pl.ds(first, zrows)], zsem)
                copy.wait() if wait else copy.start()
            return carry
        lax.fori_loop(nv_ref[0], xs_ref.shape[0] // (SLOT_BLOCK * SUBLANES), unused_block, 0)

    @pl.when(i == 0)
    def _():
        zbuf[...] = jnp.zeros_like(zbuf)
        pad_copies(wait=False)

    def load(tile, slot):
        src = h2t_ref.at[pl.ds(pl.multiple_of(tile * rows, rows), rows)]
        return pltpu.make_async_copy(src, buf.at[slot], load_sems.at[slot])

    def wait_scatter(slot):
        _wait_rows(buf.at[slot], xs_ref.at[pl.ds(0, rows)], TT, scat_sems.at[slot])

    @pl.when(i == 0)
    def _():
        load(0, 0).start()

    for slot in range(DISPATCH_RING):
        @pl.when((i + 1 < n) & ((i + 1) % DISPATCH_RING == slot))
        def _():
            @pl.when(i + 1 >= DISPATCH_RING)
            def _():
                wait_scatter(slot)
            load(i + 1, slot).start()

    for slot in range(DISPATCH_RING):
        @pl.when(i % DISPATCH_RING == slot)
        def _():
            load(i, slot).wait()

            def issue(r2, carry):
                for u in range(2):
                    r = r2 * 2 + u
                    _row_copy(buf.at[slot], xs_ref, r, dest_ref[i * TT + r],
                              scat_sems.at[slot]).start(priority=u)
                return carry
            lax.fori_loop(0, TT // 2, issue, 0)

    @pl.when(i == n - 1)
    def _():
        for slot in range(DISPATCH_RING):
            wait_scatter(slot)
        pad_copies(wait=True)


def _dispatch_call(dest, pad_start, pad_len, n_valid, h2t, n_slots):
    N = dest.shape[0]
    TT = DISPATCH_TILE
    assert N // TT >= DISPATCH_RING and SLOT_BLOCK <= 2 * PAD_COPY_SIZES[0]
    assert SLOT_BLOCK % PAD_COPY_SIZES[0] == 0
    grid_spec = pltpu.PrefetchScalarGridSpec(
        num_scalar_prefetch=4,
        grid=(N // TT,),
        in_specs=[pl.BlockSpec(memory_space=pl.ANY)],
        out_specs=pl.BlockSpec(memory_space=pl.ANY),
        scratch_shapes=[pltpu.VMEM((DISPATCH_RING, TT * SUBLANES, LANES), F32),
                        pltpu.VMEM((PAD_COPY_SIZES[0] * SUBLANES, LANES), F32),
                        pltpu.SemaphoreType.DMA((DISPATCH_RING,)),
                        pltpu.SemaphoreType.DMA((DISPATCH_RING,)),
                        pltpu.SemaphoreType.DMA],
    )
    return pl.pallas_call(
        _dispatch_kernel,
        grid_spec=grid_spec,
        out_shape=jax.ShapeDtypeStruct((n_slots * SUBLANES, LANES), F32),
        compiler_params=pltpu.CompilerParams(dimension_semantics=("arbitrary",)),
        name="dispatch",
    )(dest, pad_start, pad_len, n_valid, h2t)


def _expert_kernel(ea_ref, eb_ref, nv_ref, xs_ref, wg_ref, wu_ref, wd_ref, y_ref):
    i = pl.program_id(0)
    R = SLOT_BLOCK
    G = EXPERTS_PER_GROUP

    def ffn(xb, e):
        a = _mm(xb, wg_ref[e])
        u = _mm(xb, wu_ref[e])
        return _mm((_silu(a) * u).astype(BF16), wd_ref[e])

    @pl.when(i < nv_ref[0])
    def _():
        xb = _from_token_tiles(xs_ref, 0, R).astype(BF16)
        ya = ffn(xb, ea_ref[i] & (G - 1))
        yb = ffn(xb, eb_ref[i] & (G - 1))
        packed = _bf16_bits(ya) | (_bf16_bits(yb) >> 16)
        for s in range(SUBLANES):
            y_ref[pl.ds(s, R, stride=SUBLANES), :] = packed[:, s * LANES:(s + 1) * LANES]

    @pl.when(i >= nv_ref[0])
    def _():
        y_ref[...] = jnp.zeros_like(y_ref)


def _expert_call(block_ea, block_eb, n_valid, xs, wg, wu, wd):
    R = SLOT_BLOCK
    n_blocks = xs.shape[0] // (R * SUBLANES)
    D, F = wg.shape[1], wg.shape[2]
    G = EXPERTS_PER_GROUP
    last = lambda i, nv: jnp.minimum(i, nv[0] - 1)
    grp = lambda i, ea, eb, nv: (ea[last(i, nv)] // G, 0, 0)
    blk = lambda i, ea, eb, nv: (last(i, nv), 0)
    grid_spec = pltpu.PrefetchScalarGridSpec(
        num_scalar_prefetch=3,
        grid=(n_blocks,),
        in_specs=[
            pl.BlockSpec((R * SUBLANES, LANES), blk),
            pl.BlockSpec((G, D, F), grp, pipeline_mode=pl.Buffered(1)),
            pl.BlockSpec((G, D, F), grp, pipeline_mode=pl.Buffered(1)),
            pl.BlockSpec((G, F, D), grp, pipeline_mode=pl.Buffered(1)),
        ],
        out_specs=pl.BlockSpec((R * SUBLANES, LANES), lambda i, ea, eb, nv: (i, 0)),
    )
    return pl.pallas_call(
        _expert_kernel,
        grid_spec=grid_spec,
        out_shape=jax.ShapeDtypeStruct(xs.shape, jnp.uint32),
        compiler_params=pltpu.CompilerParams(
            dimension_semantics=("arbitrary",), vmem_limit_bytes=VMEM_LIMIT),
        name="experts",
    )(block_ea, block_eb, n_valid, xs, wg, wu, wd)


def _final_kernel(dest_ref, yt_ref, x1_ref, g_ref, mod_ref, w_ref, b_ref, o_ref, buf, sems):
    i = pl.program_id(0)
    n_steps = pl.num_programs(0)
    TT = TOK_TILE

    def start_tile(tile, slot):
        _start_rows(yt_ref, buf.at[slot], dest_ref, tile * TT, 0, TT, sems.at[slot])

    @pl.when(i == 0)
    def _():
        start_tile(0, 0)

    for slot in range(2):
        @pl.when((i + 1 < n_steps) & ((i + 1) % 2 == slot))
        def _():
            start_tile(i + 1, slot)

    gate2 = mod_ref[0, 5:6, :]
    g = g_ref[...]
    for slot in range(2):
        @pl.when(i % 2 == slot)
        def _():
            _wait_rows(yt_ref, buf.at[slot], TT, sems.at[slot])
            for r in range(TT // ROW_BLOCK):
                rows = slice(r * ROW_BLOCK, (r + 1) * ROW_BLOCK)
                packed = _from_token_tiles(buf.at[slot], r * ROW_BLOCK, ROW_BLOCK)
                ya = lax.bitcast_convert_type(packed & jnp.uint32(0xFFFF0000), F32)
                yb = lax.bitcast_convert_type(packed << 16, F32)
                y = ya * g[rows, 0:1] + yb * g[rows, 1:2]
                u = DEEPNORM_ALPHA * x1_ref[rows, :] + gate2 * y
                o_ref[rows, :] = _ln(u) * w_ref[...] + b_ref[...]


def _final_call(dest, yt, x1, gates, mod3, w, b, seq_len):
    N, D = x1.shape
    TT = TOK_TILE
    per_seq = seq_len // TT
    grid_spec = pltpu.PrefetchScalarGridSpec(
        num_scalar_prefetch=1,
        grid=(N // TT,),
        in_specs=[
            pl.BlockSpec(memory_space=pl.ANY),
            pl.BlockSpec((TT, D), lambda i, d: (i, 0)),
            pl.BlockSpec((TT, 2), lambda i, d: (i, 0)),
            pl.BlockSpec((1, 6, D), lambda i, d: (i // per_seq, 0, 0)),
            pl.BlockSpec((1, D), lambda i, d: (0, 0)),
            pl.BlockSpec((1, D), lambda i, d: (0, 0)),
        ],
        out_specs=pl.BlockSpec((TT, D), lambda i, d: (i, 0)),
        scratch_shapes=[pltpu.VMEM((2, TT * SUBLANES, LANES), jnp.uint32), pltpu.SemaphoreType.DMA((2,))],
    )
    return pl.pallas_call(
        _final_kernel,
        grid_spec=grid_spec,
        out_shape=jax.ShapeDtypeStruct((N, D), F32),
        compiler_params=pltpu.CompilerParams(
            dimension_semantics=("arbitrary",), vmem_limit_bytes=VMEM_LIMIT),
        name="combine_ln",
    )(dest, yt, x1, gates, mod3, w, b)


def _route_kernel(lgt_ref, rf_ref, ri_ref, meta_ref, cnt_scr, base_scr, bstart_scr):
    ph = pl.program_id(0)
    t = pl.program_id(1)
    TR = ROUTE_TILE
    E = N_EXPERTS
    R = SLOT_BLOCK
    G = EXPERTS_PER_GROUP
    lg = lgt_ref[...]

    g = [lg[E + k:E + k + 1, :] for k in range(N_GROUPS)]
    gmax = jnp.maximum(jnp.maximum(g[0], g[1]), jnp.maximum(g[2], g[3]))
    gs = jnp.where(g[0] == gmax, 0.0, jnp.where(g[1] == gmax, 1.0, jnp.where(g[2] == gmax, 2.0, 3.0)))
    psum = (jnp.exp(g[0] - gmax) + jnp.exp(g[1] - gmax)) + (jnp.exp(g[2] - gmax) + jnp.exp(g[3] - gmax))
    p_star = 1.0 / psum
    esel = jnp.where(gs == 0.0, lg[0:G], jnp.where(gs == 1.0, lg[G:2 * G],
                                                   jnp.where(gs == 2.0, lg[2 * G:3 * G], lg[3 * G:4 * G])))
    sub = lax.broadcasted_iota(jnp.int32, (G, TR), 0).astype(F32)
    m1 = jnp.max(esel, axis=0, keepdims=True)
    i1 = jnp.min(jnp.where(esel == m1, sub, float(G)), axis=0, keepdims=True)
    es2 = jnp.where(sub == i1, -jnp.inf, esel)
    m2 = jnp.max(es2, axis=0, keepdims=True)
    i2 = jnp.min(jnp.where(es2 == m2, sub, float(G)), axis=0, keepdims=True)
    d = jnp.exp(m2 - m1)
    w1 = 1.0 / (1.0 + d)
    w2 = d * w1
    lo = jnp.minimum(i1, i2)
    hi = jnp.maximum(i1, i2)
    bucket = gs * PAIRS_PER_GROUP + (lo * (2 * G - 1 - lo) * 0.5 + (hi - lo - 1.0))
    first_is_lo = i1 < i2
    gate_a = p_star * jnp.where(first_is_lo, w1, w2)
    gate_b = p_star * jnp.where(first_is_lo, w2, w1)
    row = lax.broadcasted_iota(jnp.int32, (NB, TR), 0).astype(F32)
    oh = row == bucket
    cnt = oh.astype(F32)

    @pl.when(ph == 0)
    def _():
        @pl.when(t == 0)
        def _():
            cnt_scr[...] = jnp.zeros_like(cnt_scr)
        acc = cnt_scr[...]
        for j in range(TR // LANES):
            acc = acc + cnt[:, j * LANES:(j + 1) * LANES]
        cnt_scr[...] = acc

    @pl.when((ph == 1) & (t == 0))
    def _():
        counts = jnp.sum(cnt_scr[...], axis=1, keepdims=True)
        nblk = jnp.floor((counts + (R - 1)) * (1.0 / R))
        nblk_b = jnp.broadcast_to(nblk, (NB, LANES))
        rb = lax.broadcasted_iota(jnp.int32, (NB, NB), 0)
        cb = lax.broadcasted_iota(jnp.int32, (NB, NB), 1)
        bstart = _mm((cb < rb).astype(BF16), nblk_b.astype(BF16))
        bstart_scr[...] = bstart
        base_scr[...] = jnp.zeros_like(base_scr)
        k = lax.broadcasted_iota(jnp.int32, (NB, 1), 0).astype(F32)
        grp = sum((k >= float(m * PAIRS_PER_GROUP)).astype(F32) for m in range(1, N_GROUPS))
        p = k - grp * PAIRS_PER_GROUP
        pair_start = [i * (2 * G - 1 - i) // 2 for i in range(G - 1)]
        ia = sum((p >= float(s)).astype(F32) for s in pair_start[1:])
        ib = p - ia * (2 * G - 1 - ia) * 0.5 + ia + 1.0
        n_pad = meta_ref.shape[1]
        blk = lax.broadcasted_iota(jnp.int32, (NB, n_pad), 1).astype(F32)
        first = jnp.broadcast_to(bstart[:, 0:1], (NB, n_pad))
        last = jnp.broadcast_to((bstart + nblk_b)[:, 0:1], (NB, n_pad))
        member = (first <= blk) & (blk < last)
        ea = jnp.sum(jnp.where(member, grp * G + ia, 0.0), axis=0, keepdims=True)
        eb = jnp.sum(jnp.where(member, grp * G + ib, 0.0), axis=0, keepdims=True)
        meta_ref[...] = jnp.zeros_like(meta_ref)
        meta_ref[0:1, :] = ea.astype(jnp.int32)
        meta_ref[1:2, :] = eb.astype(jnp.int32)
        meta_ref[2:3, :] = jnp.max(last, axis=0, keepdims=True).astype(jnp.int32)
        pad0 = jnp.broadcast_to(bstart[:, 0:1] * R + counts, (NB, NB)).T[0:1, :]
        padn = jnp.broadcast_to(nblk * R - counts, (NB, NB)).T[0:1, :]
        meta_ref[3:4, 0:NB] = pad0.astype(jnp.int32)
        meta_ref[4:5, 0:NB] = padn.astype(jnp.int32)

    @pl.when(ph == 1)
    def _():
        ri = lax.broadcasted_iota(jnp.int32, (LANES, 2 * LANES), 0)
        ci = lax.broadcasted_iota(jnp.int32, (LANES, 2 * LANES), 1)
        w = ((ci >= ri) | (ci >= LANES)).astype(BF16)
        base = base_scr[...]
        slot0 = bstart_scr[...] * R
        ds = []
        for j in range(TR // LANES):
            sl = slice(j * LANES, (j + 1) * LANES)
            blk = cnt[:, sl]
            res = _mm(blk.astype(BF16), w)
            val = slot0 + base + (res[:, 0:LANES] - blk)
            ds.append(jnp.sum(jnp.where(oh[:, sl], val, 0.0), axis=0, keepdims=True))
            base = base + res[:, LANES:2 * LANES]
        base_scr[...] = base
        ri_ref[...] = jnp.zeros_like(ri_ref)
        ri_ref[0:1, :] = jnp.concatenate(ds, axis=1).astype(jnp.int32)
        rf_ref[...] = jnp.zeros_like(rf_ref)
        rf_ref[0:1, :] = gate_a
        rf_ref[1:2, :] = gate_b


def _route_call(lgt, n_blocks):
    rows, N = lgt.shape
    TR = ROUTE_TILE
    E = NB
    n_pad = -(-n_blocks // LANES) * LANES
    return pl.pallas_call(
        _route_kernel,
        grid=(2, N // TR),
        in_specs=[pl.BlockSpec((rows, TR), lambda ph, t: (0, t))],
        out_specs=[
            pl.BlockSpec((SUBLANES, TR), lambda ph, t: (0, ph * t)),
            pl.BlockSpec((SUBLANES, TR), lambda ph, t: (0, ph * t)),
            pl.BlockSpec((SUBLANES, n_pad), lambda ph, t: (0, 0)),
        ],
        out_shape=[
            jax.ShapeDtypeStruct((SUBLANES, N), F32),
            jax.ShapeDtypeStruct((SUBLANES, N), jnp.int32),
            jax.ShapeDtypeStruct((SUBLANES, n_pad), jnp.int32),
        ],
        scratch_shapes=[pltpu.VMEM((E, LANES), F32), pltpu.VMEM((E, LANES), F32), pltpu.VMEM((E, LANES), F32)],
        compiler_params=pltpu.CompilerParams(dimension_semantics=("arbitrary", "arbitrary")),
        name="route",
    )(lgt)


def kernel(x, c, positions, w_ada, b_ada, w_in, w_out, hgrn_lb, hgrn_norm_w, ret_norm_w, post_ln1_w,
           post_ln1_b, w_rg, b_rg, w_re, b_re, w_gate, w_up, w_down, post_ln2_w, post_ln2_b):
    B, S, D = x.shape
    N = B * S
    hw = N_HEADS * D_HEAD

    mod3 = _ada_call(c, w_ada[0], b_ada[0]).reshape(B, 6, D)

    win = w_in[0].astype(BF16)
    wout = w_out[0].astype(BF16)
    pad = LANES - N_EXPERTS - N_GROUPS
    wr = jnp.concatenate([w_re[0], w_rg[0], jnp.zeros((D, pad), F32)], axis=1).astype(BF16)
    br = jnp.concatenate([b_re[0], b_rg[0], jnp.zeros((pad,), F32)]).reshape(1, LANES)

    x1, h2t, lgt, wg16, wu16, wd16 = _mixer_call(
        x, positions, mod3, win, wout, hgrn_lb, hgrn_norm_w[0].reshape(1, hw),
        ret_norm_w[0].reshape(1, hw), post_ln1_w[0].reshape(1, D), post_ln1_b[0].reshape(1, D), wr, br,
        w_gate[0], w_up[0], w_down[0])

    n_blocks = N // SLOT_BLOCK + N_GROUPS * PAIRS_PER_GROUP
    rf, ri, meta = _route_call(lgt, n_blocks)
    dest = ri[0]
    gates = rf[0:2].T
    xs = _dispatch_call(dest, meta[3, :NB], meta[4, :NB], meta[2, 0:1], h2t, n_blocks * SLOT_BLOCK)
    yt = _expert_call(meta[0, :n_blocks], meta[1, :n_blocks], meta[2, 0:1], xs, wg16, wu16, wd16)
    out = _final_call(dest, yt, x1.reshape(N, D), gates, mod3, post_ln2_w[0].reshape(1, D),
                      post_ln2_b[0].reshape(1, D), S)
    return out.reshape(B, S, D)
```

```python
import functools
import math

import numpy as np
import jax
import jax.numpy as jnp
from jax import lax
from jax.experimental import pallas as pl
from jax.experimental.pallas import tpu as pltpu

F32 = jnp.float32
BF16 = jnp.bfloat16

D_MODEL = 1024
N_HEADS = 4
D_HEAD = 128
HGRN_CHUNK = 64
RET_CHUNK = 256
ROPE_BASE = 10000.0
N_EXPERTS = 32
EXPERTS_PER_GROUP = 8
N_GROUPS = 4
D_EXPERT = 512
DEEPNORM_ALPHA = 2.0 ** 0.25
LN_EPS = 1e-5

SEQ_TILE = 512
ROW_BLOCK = 128
PAIRS_PER_GROUP = EXPERTS_PER_GROUP * (EXPERTS_PER_GROUP - 1) // 2
NB = 128
SLOT_BLOCK = 256
BLOCK_ROWS = (64, 128, 192, 256)
DISPATCH_TILE = 512
DISPATCH_RING = 3
PAD_COPY_SIZES = (128, 64, 32, 16, 8, 4, 2, 1)
TOK_TILE = 512
ROUTE_TILE = 2048
N_LOGIT_ROWS = 40
LANES = 128
SUBLANES = 8
VMEM_LIMIT = 56 * 1024 * 1024


def _mm(a, b):
    return jnp.dot(a, b, preferred_element_type=F32)


def _mm_nt(a, b):
    return lax.dot_general(a, b, (((1,), (1,)), ((), ())), preferred_element_type=F32)


def _mm_tn(a, b):
    return lax.dot_general(a, b, (((0,), (0,)), ((), ())), preferred_element_type=F32)


def _sigmoid(x):
    return 1.0 / (1.0 + jnp.exp(-x))


def _silu(x):
    return x * _sigmoid(x)


def _ln(x):
    mu = jnp.mean(x, axis=-1, keepdims=True)
    xc = x - mu
    var = jnp.mean(xc * xc, axis=-1, keepdims=True)
    return xc * lax.rsqrt(var + LN_EPS)


def _const_spec(shape):
    nd = len(shape)
    return pl.BlockSpec(shape, lambda *_: (0,) * nd, pipeline_mode=pl.Buffered(1))


def _ada_kernel(c_ref, w_ref, b_ref, o_ref):
    ca = _silu(c_ref[...])
    o_ref[...] = _mm(ca.astype(BF16), w_ref[...].astype(BF16)) + b_ref[...]


def _ada_call(c, w_ada, b_ada):
    B, D = c.shape
    n_out = w_ada.shape[1]
    return pl.pallas_call(
        _ada_kernel,
        grid=(n_out // D,),
        in_specs=[pl.BlockSpec((B, D), lambda j: (0, 0)),
                  pl.BlockSpec((D, D), lambda j: (0, j)),
                  pl.BlockSpec((1, D), lambda j: (0, j))],
        out_specs=pl.BlockSpec((B, D), lambda j: (0, j)),
        out_shape=jax.ShapeDtypeStruct((B, n_out), F32),
        name="ada_mod",
    )(c, w_ada, b_ada.reshape(1, n_out))


def _mixer_kernel(x_ref, pos_ref, mod_ref, win_ref, wout_ref, lb_ref, hnw_ref, rnw_ref,
                  ln1w_ref, ln1b_ref, invf_ref, dmat_ref, qdec_ref, kdec_ref, wr_ref, br_ref,
                  xp_ref, modp_ref, wg32_ref, wu32_ref, wd32_ref,
                  x1_ref, h2t_ref, lg_ref, wg16_ref, wu16_ref, wd16_ref,
                  h_scr, proj_scr, proj2_scr, o_scr, cos_scr, sin_scr, sh_scr, sr_scr, y_scr, h2_scr,
                  *, ret_chunk_decay, tiles_per_seq):
    T = SEQ_TILE
    D = D_MODEL
    dh = D_HEAD
    hw = N_HEADS * dh
    hw4 = 4 * hw
    step = pl.program_id(0)
    n_tiles = pl.num_programs(0) - 1

    wg16_ref[...] = wg32_ref[...].astype(BF16)
    wu16_ref[...] = wu32_ref[...].astype(BF16)
    wd16_ref[...] = wd32_ref[...].astype(BF16)

    @pl.when(step == 0)
    def _():
        y_scr[...] = jnp.zeros_like(y_scr)

    @pl.when((step < n_tiles) & (step % tiles_per_seq == 0))
    def _():
        sh_scr[...] = jnp.zeros_like(sh_scr)
        sr_scr[...] = jnp.zeros_like(sr_scr)

    gate1p = modp_ref[0, 2:3, :]
    shift2p = modp_ref[0, 3:4, :]
    scale2p = modp_ref[0, 4:5, :]

    def post_block(r):
        rows = slice(r * ROW_BLOCK, (r + 1) * ROW_BLOCK)
        u = DEEPNORM_ALPHA * xp_ref[0, rows, :] + gate1p * y_scr[rows, :]
        x1 = _ln(u) * ln1w_ref[...] + ln1b_ref[...]
        x1_ref[0, rows, :] = x1
        h2 = (_ln(x1) * (1.0 + scale2p) + shift2p).astype(BF16)
        h2_scr[rows, :] = h2
        h2f = h2.astype(F32)
        for s in range(SUBLANES):
            h2t_ref[pl.ds(r * ROW_BLOCK * SUBLANES + s, ROW_BLOCK, stride=SUBLANES), :] = (
                h2f[:, s * LANES:(s + 1) * LANES])

    shift1 = mod_ref[0, 0:1, :]
    scale1 = mod_ref[0, 1:2, :]

    for r in range(T // ROW_BLOCK):
        rows = slice(r * ROW_BLOCK, (r + 1) * ROW_BLOCK)
        h = _ln(x_ref[0, rows, :]) * (1.0 + scale1) + shift1
        h_scr[rows, :] = h.astype(BF16)

    n_post = T // ROW_BLOCK
    proj_scr[...] = _mm(h_scr[...], win_ref[:, 0:hw4])
    for r in range(n_post // 2):
        post_block(r)
    proj2_scr[...] = _mm(h_scr[...], win_ref[:, hw4:2 * hw4])
    for r in range(n_post // 2, n_post):
        post_block(r)
    lg_ref[...] = (_mm(h2_scr[...], wr_ref[...]) + br_ref[...]).T[0:N_LOGIT_ROWS, :]

    pos = pos_ref[0].astype(F32)
    ang_t = invf_ref[...] * pos
    cos_t = jnp.cos(ang_t)
    sin_t = jnp.sin(ang_t)
    cos_scr[...] = jnp.concatenate([cos_t, cos_t], axis=0).T
    sin_scr[...] = jnp.concatenate([-sin_t, sin_t], axis=0).T

    l0 = lb_ref[0:1, :]
    l1 = lb_ref[1:2, :]
    lmax = jnp.maximum(l0, l1)
    e0 = jnp.exp(l0 - lmax)
    e1 = jnp.exp(l1 - lmax)
    lb = e0 / (e0 + e1)
    hnw = hnw_ref[...]

    C = HGRN_CHUNK
    rr = lax.broadcasted_iota(jnp.int32, (C, C), 0)
    cc = lax.broadcasted_iota(jnp.int32, (C, C), 1)
    causal = rr >= cc
    tri = causal.astype(BF16)

    def hgrn_chunk(c):
        rows = slice(c * C, (c + 1) * C)
        logfs, ks = [], []
        for hd in range(N_HEADS):
            z = proj_scr[rows, hw + hd * dh: hw + (hd + 1) * dh]
            e = jnp.exp(-jnp.abs(z))
            r = 1.0 / (1.0 + e)
            er = e * r
            zp = z >= 0
            sig_pos = jnp.where(zp, r, er)
            sig_neg = jnp.where(zp, er, r)
            lbh = lb[:, hd * dh:(hd + 1) * dh]
            logfs.append(jnp.log(lbh + (1.0 - lbh) * sig_pos))
            ks.append((1.0 - lbh) * sig_neg)
        lf = jnp.concatenate(logfs, axis=1)
        hi = lf.astype(BF16)
        r1 = lf - hi.astype(F32)
        mid = r1.astype(BF16)
        lo = (r1 - mid.astype(F32)).astype(BF16)
        bsum = _mm(tri, hi) + _mm(tri, mid) + _mm(tri, lo)
        for hd in range(N_HEADS):
            cols = slice(hd * dh, (hd + 1) * dh)
            b = bsum[:, cols]
            b_last = b[C - 1:C, :]
            q = proj_scr[rows, cols]
            v = proj_scr[rows, 2 * hw + hd * dh: 2 * hw + (hd + 1) * dh].astype(BF16)
            g = proj_scr[rows, 3 * hw + hd * dh: 3 * hw + (hd + 1) * dh]
            k = ks[hd]
            q_in = (_silu(q) * jnp.exp(b)).astype(BF16)
            k_in = (k * jnp.exp(-b)).astype(BF16)
            k_dec = (k * jnp.exp(b_last - b)).astype(BF16)
            st = sh_scr[hd]
            a = jnp.where(causal, _mm_nt(q_in, k_in), 0.0)
            o = _mm(a.astype(BF16), v) + _mm_nt(q_in, st.astype(BF16))
            sh_scr[hd] = st * jnp.exp(b_last) + _mm_tn(v, k_dec)
            ms = jnp.mean(o * o, axis=-1, keepdims=True)
            on = o * lax.rsqrt(ms + LN_EPS) * hnw[:, hd * dh:(hd + 1) * dh] * _silu(g)
            o_scr[rows, hd * dh:(hd + 1) * dh] = on.astype(BF16)

    rnw = rnw_ref[...]
    RC = RET_CHUNK
    q_scale = dh ** -0.5

    def retention_unit(sub, hd):
        rows = slice(sub * RC, (sub + 1) * RC)
        cs = cos_scr[rows, :]
        sn = sin_scr[rows, :]
        rq = proj2_scr[rows, hd * dh: (hd + 1) * dh]
        rk = proj2_scr[rows, hw + hd * dh: hw + (hd + 1) * dh]
        v = proj2_scr[rows, 2 * hw + hd * dh: 2 * hw + (hd + 1) * dh].astype(BF16)
        g = proj2_scr[rows, 3 * hw + hd * dh: 3 * hw + (hd + 1) * dh]
        q = (rq * cs + pltpu.roll(rq, dh // 2, 1) * sn) * q_scale
        k = rk * cs + pltpu.roll(rk, dh // 2, 1) * sn
        qb = q.astype(BF16)
        st = sr_scr[hd]
        s = _mm_nt(qb, k.astype(BF16)) * dmat_ref[hd]
        o = _mm(s.astype(BF16), v) + _mm_nt(qb, st.astype(BF16)) * qdec_ref[hd]
        sr_scr[hd] = st * ret_chunk_decay[hd] + _mm_tn(v, (k * kdec_ref[hd]).astype(BF16))
        mu = jnp.mean(o, axis=-1, keepdims=True)
        oc = o - mu
        var = jnp.mean(oc * oc, axis=-1, keepdims=True)
        on = oc * lax.rsqrt(var + LN_EPS) * rnw[:, hd * dh:(hd + 1) * dh] * _silu(g)
        o_scr[rows, hw + hd * dh: hw + (hd + 1) * dh] = on.astype(BF16)

    n_chunks = T // C
    units = [(sub, hd) for sub in range(T // RC) for hd in range(N_HEADS)]
    for c in range(n_chunks):
        hgrn_chunk(c)
        for sub, hd in units[c * len(units) // n_chunks:(c + 1) * len(units) // n_chunks]:
            retention_unit(sub, hd)

    y_scr[...] = _mm(o_scr[...], wout_ref[...])


def _retention_tables():
    h = np.arange(N_HEADS, dtype=np.float64)
    log_gamma = np.log(1.0 - np.exp2(-5.0 - h))
    idx = np.arange(RET_CHUNK, dtype=np.float64)
    rel = idx[:, None] - idx[None, :]
    dmat = np.where(rel >= 0, np.exp(np.maximum(rel, 0.0)[None] * log_gamma[:, None, None]), 0.0)
    qdec = np.exp((idx + 1.0)[None, :] * log_gamma[:, None])
    kdec = np.exp((RET_CHUNK - 1.0 - idx)[None, :] * log_gamma[:, None])
    cdec = np.exp(RET_CHUNK * log_gamma)
    bc = lambda t: np.broadcast_to(t[:, :, None], (N_HEADS, RET_CHUNK, D_HEAD))
    return (jnp.asarray(dmat, F32), jnp.asarray(bc(qdec), F32), jnp.asarray(bc(kdec), F32),
            tuple(float(np.float32(v)) for v in cdec))


def _mixer_call(x, positions, mod3, win, wout, hgrn_lb, hnw, rnw, ln1w, ln1b, wr, br, wg, wu, wd):
    B, S, D = x.shape
    T = SEQ_TILE
    nj = S // T
    dmat, qdec, kdec, cdec = _retention_tables()
    inv_freq = np.power(ROPE_BASE, -np.arange(0, D_HEAD, 2, dtype=np.float64) / D_HEAD)
    invf = jnp.asarray(inv_freq[:, None], F32)
    pos3 = positions.reshape(B, 1, S)
    kern = functools.partial(_mixer_kernel, ret_chunk_decay=cdec, tiles_per_seq=nj)
    n_tiles = B * nj
    cur = lambda s: jnp.minimum(s, n_tiles - 1)
    prev = lambda s: jnp.maximum(s - 1, 0)
    E, _, F = wg.shape
    wg2, wu2, wd2 = wg.reshape(E * D, F), wu.reshape(E * D, F), wd.reshape(E * F, D)
    rg, rd = (E * D) // n_tiles, (E * F) // n_tiles
    assert rg * n_tiles == E * D and rd * n_tiles == E * F and rd % 16 == 0
    wspec_g = pl.BlockSpec((rg, F), lambda s: (cur(s), 0))
    wspec_d = pl.BlockSpec((rd, D), lambda s: (cur(s), 0))
    outs = pl.pallas_call(
        kern,
        grid=(n_tiles + 1,),
        in_specs=[
            pl.BlockSpec((1, T, D), lambda s: (cur(s) // nj, cur(s) % nj, 0)),
            pl.BlockSpec((1, 1, T), lambda s: (cur(s) // nj, 0, cur(s) % nj)),
            pl.BlockSpec((1, 6, D), lambda s: (cur(s) // nj, 0, 0)),
            _const_spec(win.shape), _const_spec(wout.shape), _const_spec(hgrn_lb.shape),
            _const_spec(hnw.shape), _const_spec(rnw.shape), _const_spec(ln1w.shape),
            _const_spec(ln1b.shape), _const_spec(invf.shape), _const_spec(dmat.shape),
            _const_spec(qdec.shape), _const_spec(kdec.shape), _const_spec(wr.shape),
            _const_spec(br.shape),
            pl.BlockSpec((1, T, D), lambda s: (prev(s) // nj, prev(s) % nj, 0)),
            pl.BlockSpec((1, 6, D), lambda s: (prev(s) // nj, 0, 0)),
            wspec_g, wspec_g, wspec_d,
        ],
        out_specs=[
            pl.BlockSpec((1, T, D), lambda s: (prev(s) // nj, prev(s) % nj, 0)),
            pl.BlockSpec((T * SUBLANES, LANES), lambda s: (prev(s), 0)),
            pl.BlockSpec((N_LOGIT_ROWS, T), lambda s: (0, prev(s))),
            wspec_g, wspec_g, wspec_d,
        ],
        out_shape=[
            jax.ShapeDtypeStruct((B, S, D), F32),
            jax.ShapeDtypeStruct((B * S * SUBLANES, LANES), F32),
            jax.ShapeDtypeStruct((N_LOGIT_ROWS, B * S), F32),
            jax.ShapeDtypeStruct(wg2.shape, BF16),
            jax.ShapeDtypeStruct(wu2.shape, BF16),
            jax.ShapeDtypeStruct(wd2.shape, BF16),
        ],
        scratch_shapes=[
            pltpu.VMEM((T, D), BF16),
            pltpu.VMEM((T, 4 * N_HEADS * D_HEAD), F32),
            pltpu.VMEM((T, 4 * N_HEADS * D_HEAD), F32),
            pltpu.VMEM((T, D), BF16),
            pltpu.VMEM((T, D_HEAD), F32),
            pltpu.VMEM((T, D_HEAD), F32),
            pltpu.VMEM((N_HEADS, D_HEAD, D_HEAD), F32),
            pltpu.VMEM((N_HEADS, D_HEAD, D_HEAD), F32),
            pltpu.VMEM((T, D), F32),
            pltpu.VMEM((T, D), BF16),
        ],
        compiler_params=pltpu.CompilerParams(
            dimension_semantics=("arbitrary",), vmem_limit_bytes=VMEM_LIMIT),
        name="mixer",
    )(x, pos3, mod3, win, wout, hgrn_lb, hnw, rnw, ln1w, ln1b, invf, dmat, qdec, kdec, wr, br, x, mod3,
      wg2, wu2, wd2)
    x1, h2t, lgt, wg16, wu16, wd16 = outs
    return x1, h2t, lgt, wg16.reshape(E, D, F), wu16.reshape(E, D, F), wd16.reshape(E, F, D)


def _row_copy(src_ref, dst_ref, src_row, dst_row, sem):
    s0 = pl.multiple_of(src_row * SUBLANES, SUBLANES)
    d0 = pl.multiple_of(dst_row * SUBLANES, SUBLANES)
    return pltpu.make_async_copy(src_ref.at[pl.ds(s0, SUBLANES)], dst_ref.at[pl.ds(d0, SUBLANES)], sem)


def _from_token_tiles(buf_ref, first_row, n_rows):
    return jnp.concatenate(
        [buf_ref[pl.ds(first_row * SUBLANES + s, n_rows, stride=SUBLANES), :] for s in range(SUBLANES)],
        axis=1)


def _start_rows(src_ref, dst_ref, idx_ref, idx_base, dst_base, n_rows, sem):
    def issue(r2, carry):
        for u in range(2):
            r = r2 * 2 + u
            _row_copy(src_ref, dst_ref, idx_ref[idx_base + r], dst_base + r, sem).start(priority=u)
        return carry
    lax.fori_loop(0, n_rows // 2, issue, 0)


def _wait_rows(src_ref, dst_ref, n_rows, sem):
    pltpu.make_async_copy(src_ref.at[pl.ds(0, n_rows * SUBLANES)], dst_ref, sem).wait()


def _bf16_bits(x):
    return lax.bitcast_convert_type(x.astype(BF16).astype(F32), jnp.uint32)


def _dispatch_kernel(dest_ref, pad0_ref, padn_ref, nv_ref, h2t_ref, xs_ref,
                     buf, zbuf, load_sems, scat_sems, zsem):
    i = pl.program_id(0)
    n = pl.num_programs(0)
    TT = DISPATCH_TILE
    rows = TT * SUBLANES

    def pad_copies(wait):
        def bucket(b, carry):
            pos = pad0_ref[b]
            left = padn_ref[b]
            for size in PAD_COPY_SIZES:
                hit = (left & size) != 0
                first = 0 if wait else pl.multiple_of(pos * SUBLANES, SUBLANES)
                copy = pltpu.make_async_copy(
                    zbuf.at[pl.ds(0, size * SUBLANES)], xs_ref.at[pl.ds(first, size * SUBLANES)], zsem)

                @pl.when(hit)
                def _():
                    copy.wait() if wait else copy.start()
                pos = pos + jnp.where(hit, size, 0)
            return carry
        lax.fori_loop(0, pad0_ref.shape[0], bucket, 0)

        zrows = PAD_COPY_SIZES[0] * SUBLANES

        def unused_block(blk, carry):
            for part in range(SLOT_BLOCK // PAD_COPY_SIZES[0]):
                first = 0 if wait else pl.multiple_of(blk * SLOT_BLOCK * SUBLANES + part * zrows, zrows)
                copy = pltpu.make_async_copy(zbuf, xs_ref.at[pl.ds(first, zrows)], zsem)
                copy.wait() if wait else copy.start()
            return carry
        lax.fori_loop(nv_ref[0], xs_ref.shape[0] // (SLOT_BLOCK * SUBLANES), unused_block, 0)

    @pl.when(i == 0)
    def _():
        zbuf[...] = jnp.zeros_like(zbuf)
        pad_copies(wait=False)

    def load(tile, slot):
        src = h2t_ref.at[pl.ds(pl.multiple_of(tile * rows, rows), rows)]
        return pltpu.make_async_copy(src, buf.at[slot], load_sems.at[slot])

    def wait_scatter(slot):
        _wait_rows(buf.at[slot], xs_ref.at[pl.ds(0, rows)], TT, scat_sems.at[slot])

    @pl.when(i == 0)
    def _():
        load(0, 0).start()

    for slot in range(DISPATCH_RING):
        @pl.when((i + 1 < n) & ((i + 1) % DISPATCH_RING == slot))
        def _():
            @pl.when(i + 1 >= DISPATCH_RING)
            def _():
                wait_scatter(slot)
            load(i + 1, slot).start()

    for slot in range(DISPATCH_RING):
        @pl.when(i % DISPATCH_RING == slot)
        def _():
            load(i, slot).wait()

            def issue(r2, carry):
                for u in range(2):
                    r = r2 * 2 + u
                    _row_copy(buf.at[slot], xs_ref, r, dest_ref[i * TT + r],
                              scat_sems.at[slot]).start(priority=u)
                return carry
            lax.fori_loop(0, TT // 2, issue, 0)

    @pl.when(i == n - 1)
    def _():
        for slot in range(DISPATCH_RING):
            wait_scatter(slot)
        pad_copies(wait=True)


def _dispatch_call(dest, pad_start, pad_len, n_valid, h2t, n_slots):
    N = dest.shape[0]
    TT = DISPATCH_TILE
    assert N // TT >= DISPATCH_RING and SLOT_BLOCK <= 2 * PAD_COPY_SIZES[0]
    assert SLOT_BLOCK % PAD_COPY_SIZES[0] == 0
    grid_spec = pltpu.PrefetchScalarGridSpec(
        num_scalar_prefetch=4,
        grid=(N // TT,),
        in_specs=[pl.BlockSpec(memory_space=pl.ANY)],
        out_specs=pl.BlockSpec(memory_space=pl.ANY),
        scratch_shapes=[pltpu.VMEM((DISPATCH_RING, TT * SUBLANES, LANES), F32),
                        pltpu.VMEM((PAD_COPY_SIZES[0] * SUBLANES, LANES), F32),
                        pltpu.SemaphoreType.DMA((DISPATCH_RING,)),
                        pltpu.SemaphoreType.DMA((DISPATCH_RING,)),
                        pltpu.SemaphoreType.DMA],
    )
    return pl.pallas_call(
        _dispatch_kernel,
        grid_spec=grid_spec,
        out_shape=jax.ShapeDtypeStruct((n_slots * SUBLANES, LANES), F32),
        compiler_params=pltpu.CompilerParams(dimension_semantics=("arbitrary",)),
        name="dispatch",
    )(dest, pad_start, pad_len, n_valid, h2t)


def _expert_kernel(ea_ref, eb_ref, rows_ref, nv_ref, xs_ref, wg_ref, wu_ref, wd_ref, y_ref):
    i = pl.program_id(0)
    R = SLOT_BLOCK
    G = EXPERTS_PER_GROUP
    used = i < nv_ref[0]
    n_rows = rows_ref[i]

    def ffn(xb, e):
        a = _mm(xb, wg_ref[e])
        u = _mm(xb, wu_ref[e])
        return _mm((_silu(a) * u).astype(BF16), wd_ref[e])

    for k, m in enumerate(BLOCK_ROWS):
        fewer = BLOCK_ROWS[k - 1] if k else 0

        @pl.when(used & (n_rows > fewer) & (n_rows <= m))
        def _():
            xb = _from_token_tiles(xs_ref, 0, m).astype(BF16)
            ya = ffn(xb, ea_ref[i] & (G - 1))
            yb = ffn(xb, eb_ref[i] & (G - 1))
            packed = _bf16_bits(ya) | (_bf16_bits(yb) >> 16)
            for s in range(SUBLANES):
                y_ref[pl.ds(s, m, stride=SUBLANES), :] = packed[:, s * LANES:(s + 1) * LANES]
            if m < R:
                y_ref[pl.ds(m * SUBLANES, (R - m) * SUBLANES), :] = jnp.zeros(
                    ((R - m) * SUBLANES, LANES), y_ref.dtype)

    @pl.when(jnp.logical_not(used))
    def _():
        y_ref[...] = jnp.zeros_like(y_ref)


def _expert_call(block_ea, block_eb, block_rows, n_valid, xs, wg, wu, wd):
    assert BLOCK_ROWS[-1] == SLOT_BLOCK
    R = SLOT_BLOCK
    n_blocks = xs.shape[0] // (R * SUBLANES)
    D, F = wg.shape[1], wg.shape[2]
    G = EXPERTS_PER_GROUP
    last = lambda i, nv: jnp.minimum(i, nv[0] - 1)
    grp = lambda i, ea, eb, rows, nv: (ea[last(i, nv)] // G, 0, 0)
    blk = lambda i, ea, eb, rows, nv: (last(i, nv), 0)
    grid_spec = pltpu.PrefetchScalarGridSpec(
        num_scalar_prefetch=4,
        grid=(n_blocks,),
        in_specs=[
            pl.BlockSpec((R * SUBLANES, LANES), blk),
            pl.BlockSpec((G, D, F), grp, pipeline_mode=pl.Buffered(1)),
            pl.BlockSpec((G, D, F), grp, pipeline_mode=pl.Buffered(1)),
            pl.BlockSpec((G, F, D), grp, pipeline_mode=pl.Buffered(1)),
        ],
        out_specs=pl.BlockSpec((R * SUBLANES, LANES), lambda i, ea, eb, rows, nv: (i, 0)),
    )
    return pl.pallas_call(
        _expert_kernel,
        grid_spec=grid_spec,
        out_shape=jax.ShapeDtypeStruct(xs.shape, jnp.uint32),
        compiler_params=pltpu.CompilerParams(
            dimension_semantics=("arbitrary",), vmem_limit_bytes=VMEM_LIMIT),
        name="experts",
    )(block_ea, block_eb, block_rows, n_valid, xs, wg, wu, wd)


def _final_kernel(dest_ref, yt_ref, x1_ref, g_ref, mod_ref, w_ref, b_ref, o_ref, buf, sems):
    i = pl.program_id(0)
    n_steps = pl.num_programs(0)
    TT = TOK_TILE

    def start_tile(tile, slot):
        _start_rows(yt_ref, buf.at[slot], dest_ref, tile * TT, 0, TT, sems.at[slot])

    @pl.when(i == 0)
    def _():
        start_tile(0, 0)

    for slot in range(2):
        @pl.when((i + 1 < n_steps) & ((i + 1) % 2 == slot))
        def _():
            start_tile(i + 1, slot)

    gate2 = mod_ref[0, 5:6, :]
    g = g_ref[...]
    for slot in range(2):
        @pl.when(i % 2 == slot)
        def _():
            _wait_rows(yt_ref, buf.at[slot], TT, sems.at[slot])
            for r in range(TT // ROW_BLOCK):
                rows = slice(r * ROW_BLOCK, (r + 1) * ROW_BLOCK)
                packed = _from_token_tiles(buf.at[slot], r * ROW_BLOCK, ROW_BLOCK)
                ya = lax.bitcast_convert_type(packed & jnp.uint32(0xFFFF0000), F32)
                yb = lax.bitcast_convert_type(packed << 16, F32)
                y = ya * g[rows, 0:1] + yb * g[rows, 1:2]
                u = DEEPNORM_ALPHA * x1_ref[rows, :] + gate2 * y
                o_ref[rows, :] = _ln(u) * w_ref[...] + b_ref[...]


def _final_call(dest, yt, x1, gates, mod3, w, b, seq_len):
    N, D = x1.shape
    TT = TOK_TILE
    per_seq = seq_len // TT
    grid_spec = pltpu.PrefetchScalarGridSpec(
        num_scalar_prefetch=1,
        grid=(N // TT,),
        in_specs=[
            pl.BlockSpec(memory_space=pl.ANY),
            pl.BlockSpec((TT, D), lambda i, d: (i, 0)),
            pl.BlockSpec((TT, 2), lambda i, d: (i, 0)),
            pl.BlockSpec((1, 6, D), lambda i, d: (i // per_seq, 0, 0)),
            pl.BlockSpec((1, D), lambda i, d: (0, 0)),
            pl.BlockSpec((1, D), lambda i, d: (0, 0)),
        ],
        out_specs=pl.BlockSpec((TT, D), lambda i, d: (i, 0)),
        scratch_shapes=[pltpu.VMEM((2, TT * SUBLANES, LANES), jnp.uint32), pltpu.SemaphoreType.DMA((2,))],
    )
    return pl.pallas_call(
        _final_kernel,
        grid_spec=grid_spec,
        out_shape=jax.ShapeDtypeStruct((N, D), F32),
        compiler_params=pltpu.CompilerParams(
            dimension_semantics=("arbitrary",), vmem_limit_bytes=VMEM_LIMIT),
        name="combine_ln",
    )(dest, yt, x1, gates, mod3, w, b)


def _route_kernel(lgt_ref, rf_ref, ri_ref, meta_ref, cnt_scr, base_scr, bstart_scr):
    ph = pl.program_id(0)
    t = pl.program_id(1)
    TR = ROUTE_TILE
    E = N_EXPERTS
    R = SLOT_BLOCK
    G = EXPERTS_PER_GROUP
    lg = lgt_ref[...]

    g = [lg[E + k:E + k + 1, :] for k in range(N_GROUPS)]
    gmax = jnp.maximum(jnp.maximum(g[0], g[1]), jnp.maximum(g[2], g[3]))
    gs = jnp.where(g[0] == gmax, 0.0, jnp.where(g[1] == gmax, 1.0, jnp.where(g[2] == gmax, 2.0, 3.0)))
    psum = (jnp.exp(g[0] - gmax) + jnp.exp(g[1] - gmax)) + (jnp.exp(g[2] - gmax) + jnp.exp(g[3] - gmax))
    p_star = 1.0 / psum
    esel = jnp.where(gs == 0.0, lg[0:G], jnp.where(gs == 1.0, lg[G:2 * G],
                                                   jnp.where(gs == 2.0, lg[2 * G:3 * G], lg[3 * G:4 * G])))
    sub = lax.broadcasted_iota(jnp.int32, (G, TR), 0).astype(F32)
    m1 = jnp.max(esel, axis=0, keepdims=True)
    i1 = jnp.min(jnp.where(esel == m1, sub, float(G)), axis=0, keepdims=True)
    es2 = jnp.where(sub == i1, -jnp.inf, esel)
    m2 = jnp.max(es2, axis=0, keepdims=True)
    i2 = jnp.min(jnp.where(es2 == m2, sub, float(G)), axis=0, keepdims=True)
    d = jnp.exp(m2 - m1)
    w1 = 1.0 / (1.0 + d)
    w2 = d * w1
    lo = jnp.minimum(i1, i2)
    hi = jnp.maximum(i1, i2)
    bucket = gs * PAIRS_PER_GROUP + (lo * (2 * G - 1 - lo) * 0.5 + (hi - lo - 1.0))
    first_is_lo = i1 < i2
    gate_a = p_star * jnp.where(first_is_lo, w1, w2)
    gate_b = p_star * jnp.where(first_is_lo, w2, w1)
    row = lax.broadcasted_iota(jnp.int32, (NB, TR), 0).astype(F32)
    oh = row == bucket
    cnt = oh.astype(F32)

    @pl.when(ph == 0)
    def _():
        @pl.when(t == 0)
        def _():
            cnt_scr[...] = jnp.zeros_like(cnt_scr)
        acc = cnt_scr[...]
        for j in range(TR // LANES):
            acc = acc + cnt[:, j * LANES:(j + 1) * LANES]
        cnt_scr[...] = acc

    @pl.when((ph == 1) & (t == 0))
    def _():
        counts = jnp.sum(cnt_scr[...], axis=1, keepdims=True)
        nblk = jnp.floor((counts + (R - 1)) * (1.0 / R))
        nblk_b = jnp.broadcast_to(nblk, (NB, LANES))
        rb = lax.broadcasted_iota(jnp.int32, (NB, NB), 0)
        cb = lax.broadcasted_iota(jnp.int32, (NB, NB), 1)
        bstart = _mm((cb < rb).astype(BF16), nblk_b.astype(BF16))
        bstart_scr[...] = bstart
        base_scr[...] = jnp.zeros_like(base_scr)
        k = lax.broadcasted_iota(jnp.int32, (NB, 1), 0).astype(F32)
        grp = sum((k >= float(m * PAIRS_PER_GROUP)).astype(F32) for m in range(1, N_GROUPS))
        p = k - grp * PAIRS_PER_GROUP
        pair_start = [i * (2 * G - 1 - i) // 2 for i in range(G - 1)]
        ia = sum((p >= float(s)).astype(F32) for s in pair_start[1:])
        ib = p - ia * (2 * G - 1 - ia) * 0.5 + ia + 1.0
        n_pad = meta_ref.shape[1]
        blk = lax.broadcasted_iota(jnp.int32, (NB, n_pad), 1).astype(F32)
        first = jnp.broadcast_to(bstart[:, 0:1], (NB, n_pad))
        last = jnp.broadcast_to((bstart + nblk_b)[:, 0:1], (NB, n_pad))
        member = (first <= blk) & (blk < last)
        ea = jnp.sum(jnp.where(member, grp * G + ia, 0.0), axis=0, keepdims=True)
        eb = jnp.sum(jnp.where(member, grp * G + ib, 0.0), axis=0, keepdims=True)
        meta_ref[...] = jnp.zeros_like(meta_ref)
        meta_ref[0:1, :] = ea.astype(jnp.int32)
        meta_ref[1:2, :] = eb.astype(jnp.int32)
        meta_ref[2:3, :] = jnp.max(last, axis=0, keepdims=True).astype(jnp.int32)
        in_blk = jnp.clip(counts - (blk - first) * R, 0.0, float(R))
        meta_ref[5:6, :] = jnp.sum(jnp.where(member, in_blk, 0.0), axis=0, keepdims=True).astype(jnp.int32)
        pad0 = jnp.broadcast_to(bstart[:, 0:1] * R + counts, (NB, NB)).T[0:1, :]
        padn = jnp.broadcast_to(nblk * R - counts, (NB, NB)).T[0:1, :]
        meta_ref[3:4, 0:NB] = pad0.astype(jnp.int32)
        meta_ref[4:5, 0:NB] = padn.astype(jnp.int32)

    @pl.when(ph == 1)
    def _():
        ri = lax.broadcasted_iota(jnp.int32, (LANES, 2 * LANES), 0)
        ci = lax.broadcasted_iota(jnp.int32, (LANES, 2 * LANES), 1)
        w = ((ci >= ri) | (ci >= LANES)).astype(BF16)
        base = base_scr[...]
        slot0 = bstart_scr[...] * R
        ds = []
        for j in range(TR // LANES):
            sl = slice(j * LANES, (j + 1) * LANES)
            blk = cnt[:, sl]
            res = _mm(blk.astype(BF16), w)
            val = slot0 + base + (res[:, 0:LANES] - blk)
            ds.append(jnp.sum(jnp.where(oh[:, sl], val, 0.0), axis=0, keepdims=True))
            base = base + res[:, LANES:2 * LANES]
        base_scr[...] = base
        ri_ref[...] = jnp.zeros_like(ri_ref)
        ri_ref[0:1, :] = jnp.concatenate(ds, axis=1).astype(jnp.int32)
        rf_ref[...] = jnp.zeros_like(rf_ref)
        rf_ref[0:1, :] = gate_a
        rf_ref[1:2, :] = gate_b


def _route_call(lgt, n_blocks):
    rows, N = lgt.shape
    TR = ROUTE_TILE
    E = NB
    n_pad = -(-n_blocks // LANES) * LANES
    return pl.pallas_call(
        _route_kernel,
        grid=(2, N // TR),
        in_specs=[pl.BlockSpec((rows, TR), lambda ph, t: (0, t))],
        out_specs=[
            pl.BlockSpec((SUBLANES, TR), lambda ph, t: (0, ph * t)),
            pl.BlockSpec((SUBLANES, TR), lambda ph, t: (0, ph * t)),
            pl.BlockSpec((SUBLANES, n_pad), lambda ph, t: (0, 0)),
        ],
        out_shape=[
            jax.ShapeDtypeStruct((SUBLANES, N), F32),
            jax.ShapeDtypeStruct((SUBLANES, N), jnp.int32),
            jax.ShapeDtypeStruct((SUBLANES, n_pad), jnp.int32),
        ],
        scratch_shapes=[pltpu.VMEM((E, LANES), F32), pltpu.VMEM((E, LANES), F32), pltpu.VMEM((E, LANES), F32)],
        compiler_params=pltpu.CompilerParams(dimension_semantics=("arbitrary", "arbitrary")),
        name="route",
    )(lgt)


def kernel(x, c, positions, w_ada, b_ada, w_in, w_out, hgrn_lb, hgrn_norm_w, ret_norm_w, post_ln1_w,
           post_ln1_b, w_rg, b_rg, w_re, b_re, w_gate, w_up, w_down, post_ln2_w, post_ln2_b):
    B, S, D = x.shape
    N = B * S
    hw = N_HEADS * D_HEAD

    mod3 = _ada_call(c, w_ada[0], b_ada[0]).reshape(B, 6, D)

    win = w_in[0].astype(BF16)
    wout = w_out[0].astype(BF16)
    pad = LANES - N_EXPERTS - N_GROUPS
    wr = jnp.concatenate([w_re[0], w_rg[0], jnp.zeros((D, pad), F32)], axis=1).astype(BF16)
    br = jnp.concatenate([b_re[0], b_rg[0], jnp.zeros((pad,), F32)]).reshape(1, LANES)

    x1, h2t, lgt, wg16, wu16, wd16 = _mixer_call(
        x, positions, mod3, win, wout, hgrn_lb, hgrn_norm_w[0].reshape(1, hw),
        ret_norm_w[0].reshape(1, hw), post_ln1_w[0].reshape(1, D), post_ln1_b[0].reshape(1, D), wr, br,
        w_gate[0], w_up[0], w_down[0])

    n_blocks = N // SLOT_BLOCK + N_GROUPS * PAIRS_PER_GROUP
    rf, ri, meta = _route_call(lgt, n_blocks)
    dest = ri[0]
    gates = rf[0:2].T
    xs = _dispatch_call(dest, meta[3, :NB], meta[4, :NB], meta[2, 0:1], h2t, n_blocks * SLOT_BLOCK)
    yt = _expert_call(meta[0, :n_blocks], meta[1, :n_blocks], meta[5, :n_blocks], meta[2, 0:1], xs,
                      wg16, wu16, wd16)
    out = _final_call(dest, yt, x1.reshape(N, D), gates, mod3, post_ln2_w[0].reshape(1, D),
                      post_ln2_b[0].reshape(1, D), S)
    return out.reshape(B, S, D)
```

```python
import functools
import math

import numpy as np
import jax
import jax.numpy as jnp
from jax import lax
from jax.experimental import pallas as pl
from jax.experimental.pallas import tpu as pltpu

F32 = jnp.float32
BF16 = jnp.bfloat16

D_MODEL = 1024
N_HEADS = 4
D_HEAD = 128
HGRN_CHUNK = 64
RET_CHUNK = 256
ROPE_BASE = 10000.0
N_EXPERTS = 32
EXPERTS_PER_GROUP = 8
N_GROUPS = 4
D_EXPERT = 512
DEEPNORM_ALPHA = 2.0 ** 0.25
LN_EPS = 1e-5

SEQ_TILE = 512
ROW_BLOCK = 128
PAIRS_PER_GROUP = EXPERTS_PER_GROUP * (EXPERTS_PER_GROUP - 1) // 2
NB = 128
SLOT_BLOCK = 256
BLOCK_ROWS = (64, 128, 192, 256)
DISPATCH_TILE = 512
DISPATCH_RING = 3
PAD_COPY_SIZES = (128, 64, 32, 16, 8, 4, 2, 1)
TOK_TILE = 512
ROUTE_TILE = 2048
N_LOGIT_ROWS = 40
LANES = 128
SUBLANES = 8
VMEM_LIMIT = 56 * 1024 * 1024


def _mm(a, b):
    return jnp.dot(a, b, preferred_element_type=F32)


def _mm_nt(a, b):
    return lax.dot_general(a, b, (((1,), (1,)), ((), ())), preferred_element_type=F32)


def _mm_tn(a, b):
    return lax.dot_general(a, b, (((0,), (0,)), ((), ())), preferred_element_type=F32)


def _sigmoid(x):
    return 1.0 / (1.0 + jnp.exp(-x))


def _silu(x):
    return x * _sigmoid(x)


def _ln(x):
    mu = jnp.mean(x, axis=-1, keepdims=True)
    xc = x - mu
    var = jnp.mean(xc * xc, axis=-1, keepdims=True)
    return xc * lax.rsqrt(var + LN_EPS)


def _const_spec(shape):
    nd = len(shape)
    return pl.BlockSpec(shape, lambda *_: (0,) * nd, pipeline_mode=pl.Buffered(1))


def _ada_kernel(c_ref, w_ref, b_ref, o_ref):
    ca = _silu(c_ref[...])
    o_ref[...] = _mm(ca.astype(BF16), w_ref[...].astype(BF16)) + b_ref[...]


def _ada_call(c, w_ada, b_ada):
    B, D = c.shape
    n_out = w_ada.shape[1]
    return pl.pallas_call(
        _ada_kernel,
        grid=(n_out // D,),
        in_specs=[pl.BlockSpec((B, D), lambda j: (0, 0)),
                  pl.BlockSpec((D, D), lambda j: (0, j)),
                  pl.BlockSpec((1, D), lambda j: (0, j))],
        out_specs=pl.BlockSpec((B, D), lambda j: (0, j)),
        out_shape=jax.ShapeDtypeStruct((B, n_out), F32),
        name="ada_mod",
    )(c, w_ada, b_ada.reshape(1, n_out))


def _mixer_kernel(x_ref, pos_ref, mod_ref, win_ref, wout_ref, lb_ref, hnw_ref, rnw_ref,
                  ln1w_ref, ln1b_ref, invf_ref, dmat_ref, qdec_ref, kdec_ref, wr_ref, br_ref,
                  xp_ref, modp_ref, wg32_ref, wu32_ref, wd32_ref,
                  x1_ref, h2t_ref, lg_ref, wg16_ref, wu16_ref, wd16_ref,
                  h_scr, proj_scr, proj2_scr, o_scr, cos_scr, sin_scr, sh_scr, sr_scr, y_scr, h2_scr,
                  *, ret_chunk_decay, tiles_per_seq):
    T = SEQ_TILE
    D = D_MODEL
    dh = D_HEAD
    hw = N_HEADS * dh
    hw4 = 4 * hw
    step = pl.program_id(0)
    n_tiles = pl.num_programs(0) - 1

    wg16_ref[...] = wg32_ref[...].astype(BF16)
    wu16_ref[...] = wu32_ref[...].astype(BF16)
    wd16_ref[...] = wd32_ref[...].astype(BF16)

    @pl.when(step == 0)
    def _():
        y_scr[...] = jnp.zeros_like(y_scr)

    @pl.when((step < n_tiles) & (step % tiles_per_seq == 0))
    def _():
        sh_scr[...] = jnp.zeros_like(sh_scr)
        sr_scr[...] = jnp.zeros_like(sr_scr)

    gate1p = modp_ref[0, 2:3, :]
    shift2p = modp_ref[0, 3:4, :]
    scale2p = modp_ref[0, 4:5, :]

    def post_block(r):
        rows = slice(r * ROW_BLOCK, (r + 1) * ROW_BLOCK)
        u = DEEPNORM_ALPHA * xp_ref[0, rows, :] + gate1p * y_scr[rows, :]
        x1 = _ln(u) * ln1w_ref[...] + ln1b_ref[...]
        x1_ref[0, rows, :] = x1
        h2 = (_ln(x1) * (1.0 + scale2p) + shift2p).astype(BF16)
        h2_scr[rows, :] = h2
        h2f = h2.astype(F32)
        for s in range(SUBLANES):
            h2t_ref[pl.ds(r * ROW_BLOCK * SUBLANES + s, ROW_BLOCK, stride=SUBLANES), :] = (
                h2f[:, s * LANES:(s + 1) * LANES])

    shift1 = mod_ref[0, 0:1, :]
    scale1 = mod_ref[0, 1:2, :]

    for r in range(T // ROW_BLOCK):
        rows = slice(r * ROW_BLOCK, (r + 1) * ROW_BLOCK)
        h = _ln(x_ref[0, rows, :]) * (1.0 + scale1) + shift1
        h_scr[rows, :] = h.astype(BF16)

    n_post = T // ROW_BLOCK
    proj_scr[...] = _mm(h_scr[...], win_ref[:, 0:hw4])
    for r in range(n_post // 2):
        post_block(r)
    proj2_scr[...] = _mm(h_scr[...], win_ref[:, hw4:2 * hw4])
    for r in range(n_post // 2, n_post):
        post_block(r)
    lg_ref[...] = (_mm(h2_scr[...], wr_ref[...]) + br_ref[...]).T[0:N_LOGIT_ROWS, :]

    pos = pos_ref[0].astype(F32)
    ang_t = invf_ref[...] * pos
    cos_t = jnp.cos(ang_t)
    sin_t = jnp.sin(ang_t)
    cos_scr[...] = jnp.concatenate([cos_t, cos_t], axis=0).T
    sin_scr[...] = jnp.concatenate([-sin_t, sin_t], axis=0).T

    l0 = lb_ref[0:1, :]
    l1 = lb_ref[1:2, :]
    lmax = jnp.maximum(l0, l1)
    e0 = jnp.exp(l0 - lmax)
    e1 = jnp.exp(l1 - lmax)
    lb = e0 / (e0 + e1)
    hnw = hnw_ref[...]

    C = HGRN_CHUNK
    rr = lax.broadcasted_iota(jnp.int32, (C, C), 0)
    cc = lax.broadcasted_iota(jnp.int32, (C, C), 1)
    causal = rr >= cc
    row_id = lax.broadcasted_iota(jnp.int32, (C, hw), 0)

    def hgrn_chunk(c):
        rows = slice(c * C, (c + 1) * C)
        logfs, ks = [], []
        for hd in range(N_HEADS):
            z = proj_scr[rows, hw + hd * dh: hw + (hd + 1) * dh]
            e = jnp.exp(-jnp.abs(z))
            r = 1.0 / (1.0 + e)
            er = e * r
            zp = z >= 0
            sig_pos = jnp.where(zp, r, er)
            sig_neg = jnp.where(zp, er, r)
            lbh = lb[:, hd * dh:(hd + 1) * dh]
            logfs.append(jnp.log(lbh + (1.0 - lbh) * sig_pos))
            ks.append((1.0 - lbh) * sig_neg)
        bsum = jnp.concatenate(logfs, axis=1)
        for k in (1, 2, 4):
            bsum = bsum + jnp.where(row_id >= k, pltpu.roll(bsum, k, 0), 0.0)
        for k in (8, 16, 32):
            bsum = bsum + jnp.concatenate([jnp.zeros((k, hw), F32), bsum[0:C - k, :]], axis=0)
        for hd in range(N_HEADS):
            cols = slice(hd * dh, (hd + 1) * dh)
            b = bsum[:, cols]
            b_last = b[C - 1:C, :]
            q = proj_scr[rows, cols]
            v = proj_scr[rows, 2 * hw + hd * dh: 2 * hw + (hd + 1) * dh].astype(BF16)
            g = proj_scr[rows, 3 * hw + hd * dh: 3 * hw + (hd + 1) * dh]
            k = ks[hd]
            q_in = (_silu(q) * jnp.exp(b)).astype(BF16)
            k_in = (k * jnp.exp(-b)).astype(BF16)
            k_dec = (k * jnp.exp(b_last - b)).astype(BF16)
            st = sh_scr[hd]
            a = jnp.where(causal, _mm_nt(q_in, k_in), 0.0)
            o = _mm(a.astype(BF16), v) + _mm_nt(q_in, st.astype(BF16))
            sh_scr[hd] = st * jnp.exp(b_last) + _mm_tn(v, k_dec)
            ms = jnp.mean(o * o, axis=-1, keepdims=True)
            on = o * lax.rsqrt(ms + LN_EPS) * hnw[:, hd * dh:(hd + 1) * dh] * _silu(g)
            o_scr[rows, hd * dh:(hd + 1) * dh] = on.astype(BF16)

    rnw = rnw_ref[...]
    RC = RET_CHUNK
    q_scale = dh ** -0.5

    def retention_unit(sub, hd):
        rows = slice(sub * RC, (sub + 1) * RC)
        cs = cos_scr[rows, :]
        sn = sin_scr[rows, :]
        rq = proj2_scr[rows, hd * dh: (hd + 1) * dh]
        rk = proj2_scr[rows, hw + hd * dh: hw + (hd + 1) * dh]
        v = proj2_scr[rows, 2 * hw + hd * dh: 2 * hw + (hd + 1) * dh].astype(BF16)
        g = proj2_scr[rows, 3 * hw + hd * dh: 3 * hw + (hd + 1) * dh]
        q = (rq * cs + pltpu.roll(rq, dh // 2, 1) * sn) * q_scale
        k = rk * cs + pltpu.roll(rk, dh // 2, 1) * sn
        qb = q.astype(BF16)
        st = sr_scr[hd]
        s = _mm_nt(qb, k.astype(BF16)) * dmat_ref[hd]
        o = _mm(s.astype(BF16), v) + _mm_nt(qb, st.astype(BF16)) * qdec_ref[hd]
        sr_scr[hd] = st * ret_chunk_decay[hd] + _mm_tn(v, (k * kdec_ref[hd]).astype(BF16))
        mu = jnp.mean(o, axis=-1, keepdims=True)
        oc = o - mu
        var = jnp.mean(oc * oc, axis=-1, keepdims=True)
        on = oc * lax.rsqrt(var + LN_EPS) * rnw[:, hd * dh:(hd + 1) * dh] * _silu(g)
        o_scr[rows, hw + hd * dh: hw + (hd + 1) * dh] = on.astype(BF16)

    n_chunks = T // C
    units = [(sub, hd) for sub in range(T // RC) for hd in range(N_HEADS)]
    for c in range(n_chunks):
        hgrn_chunk(c)
        for sub, hd in units[c * len(units) // n_chunks:(c + 1) * len(units) // n_chunks]:
            retention_unit(sub, hd)

    y_scr[...] = _mm(o_scr[...], wout_ref[...])


def _retention_tables():
    h = np.arange(N_HEADS, dtype=np.float64)
    log_gamma = np.log(1.0 - np.exp2(-5.0 - h))
    idx = np.arange(RET_CHUNK, dtype=np.float64)
    rel = idx[:, None] - idx[None, :]
    dmat = np.where(rel >= 0, np.exp(np.maximum(rel, 0.0)[None] * log_gamma[:, None, None]), 0.0)
    qdec = np.exp((idx + 1.0)[None, :] * log_gamma[:, None])
    kdec = np.exp((RET_CHUNK - 1.0 - idx)[None, :] * log_gamma[:, None])
    cdec = np.exp(RET_CHUNK * log_gamma)
    bc = lambda t: np.broadcast_to(t[:, :, None], (N_HEADS, RET_CHUNK, D_HEAD))
    return (jnp.asarray(dmat, F32), jnp.asarray(bc(qdec), F32), jnp.asarray(bc(kdec), F32),
            tuple(float(np.float32(v)) for v in cdec))


def _mixer_call(x, positions, mod3, win, wout, hgrn_lb, hnw, rnw, ln1w, ln1b, wr, br, wg, wu, wd):
    B, S, D = x.shape
    T = SEQ_TILE
    nj = S // T
    dmat, qdec, kdec, cdec = _retention_tables()
    inv_freq = np.power(ROPE_BASE, -np.arange(0, D_HEAD, 2, dtype=np.float64) / D_HEAD)
    invf = jnp.asarray(inv_freq[:, None], F32)
    pos3 = positions.reshape(B, 1, S)
    kern = functools.partial(_mixer_kernel, ret_chunk_decay=cdec, tiles_per_seq=nj)
    n_tiles = B * nj
    cur = lambda s: jnp.minimum(s, n_tiles - 1)
    prev = lambda s: jnp.maximum(s - 1, 0)
    E, _, F = wg.shape
    wg2, wu2, wd2 = wg.reshape(E * D, F), wu.reshape(E * D, F), wd.reshape(E * F, D)
    rg, rd = (E * D) // n_tiles, (E * F) // n_tiles
    assert rg * n_tiles == E * D and rd * n_tiles == E * F and rd % 16 == 0
    wspec_g = pl.BlockSpec((rg, F), lambda s: (cur(s), 0))
    wspec_d = pl.BlockSpec((rd, D), lambda s: (cur(s), 0))
    outs = pl.pallas_call(
        kern,
        grid=(n_tiles + 1,),
        in_specs=[
            pl.BlockSpec((1, T, D), lambda s: (cur(s) // nj, cur(s) % nj, 0)),
            pl.BlockSpec((1, 1, T), lambda s: (cur(s) // nj, 0, cur(s) % nj)),
            pl.BlockSpec((1, 6, D), lambda s: (cur(s) // nj, 0, 0)),
            _const_spec(win.shape), _const_spec(wout.shape), _const_spec(hgrn_lb.shape),
            _const_spec(hnw.shape), _const_spec(rnw.shape), _const_spec(ln1w.shape),
            _const_spec(ln1b.shape), _const_spec(invf.shape), _const_spec(dmat.shape),
            _const_spec(qdec.shape), _const_spec(kdec.shape), _const_spec(wr.shape),
            _const_spec(br.shape),
            pl.BlockSpec((1, T, D), lambda s: (prev(s) // nj, prev(s) % nj, 0)),
            pl.BlockSpec((1, 6, D), lambda s: (prev(s) // nj, 0, 0)),
            wspec_g, wspec_g, wspec_d,
        ],
        out_specs=[
            pl.BlockSpec((1, T, D), lambda s: (prev(s) // nj, prev(s) % nj, 0)),
            pl.BlockSpec((T * SUBLANES, LANES), lambda s: (prev(s), 0)),
            pl.BlockSpec((N_LOGIT_ROWS, T), lambda s: (0, prev(s))),
            wspec_g, wspec_g, wspec_d,
        ],
        out_shape=[
            jax.ShapeDtypeStruct((B, S, D), F32),
            jax.ShapeDtypeStruct((B * S * SUBLANES, LANES), F32),
            jax.ShapeDtypeStruct((N_LOGIT_ROWS, B * S), F32),
            jax.ShapeDtypeStruct(wg2.shape, BF16),
            jax.ShapeDtypeStruct(wu2.shape, BF16),
            jax.ShapeDtypeStruct(wd2.shape, BF16),
        ],
        scratch_shapes=[
            pltpu.VMEM((T, D), BF16),
            pltpu.VMEM((T, 4 * N_HEADS * D_HEAD), F32),
            pltpu.VMEM((T, 4 * N_HEADS * D_HEAD), F32),
            pltpu.VMEM((T, D), BF16),
            pltpu.VMEM((T, D_HEAD), F32),
            pltpu.VMEM((T, D_HEAD), F32),
            pltpu.VMEM((N_HEADS, D_HEAD, D_HEAD), F32),
            pltpu.VMEM((N_HEADS, D_HEAD, D_HEAD), F32),
            pltpu.VMEM((T, D), F32),
            pltpu.VMEM((T, D), BF16),
        ],
        compiler_params=pltpu.CompilerParams(
            dimension_semantics=("arbitrary",), vmem_limit_bytes=VMEM_LIMIT),
        name="mixer",
    )(x, pos3, mod3, win, wout, hgrn_lb, hnw, rnw, ln1w, ln1b, invf, dmat, qdec, kdec, wr, br, x, mod3,
      wg2, wu2, wd2)
    x1, h2t, lgt, wg16, wu16, wd16 = outs
    return x1, h2t, lgt, wg16.reshape(E, D, F), wu16.reshape(E, D, F), wd16.reshape(E, F, D)


def _row_copy(src_ref, dst_ref, src_row, dst_row, sem):
    s0 = pl.multiple_of(src_row * SUBLANES, SUBLANES)
    d0 = pl.multiple_of(dst_row * SUBLANES, SUBLANES)
    return pltpu.make_async_copy(src_ref.at[pl.ds(s0, SUBLANES)], dst_ref.at[pl.ds(d0, SUBLANES)], sem)


def _from_token_tiles(buf_ref, first_row, n_rows):
    return jnp.concatenate(
        [buf_ref[pl.ds(first_row * SUBLANES + s, n_rows, stride=SUBLANES), :] for s in range(SUBLANES)],
        axis=1)


def _start_rows(src_ref, dst_ref, idx_ref, idx_base, dst_base, n_rows, sem):
    def issue(r2, carry):
        for u in range(2):
            r = r2 * 2 + u
            _row_copy(src_ref, dst_ref, idx_ref[idx_base + r], dst_base + r, sem).start(priority=u)
        return carry
    lax.fori_loop(0, n_rows // 2, issue, 0)


def _wait_rows(src_ref, dst_ref, n_rows, sem):
    pltpu.make_async_copy(src_ref.at[pl.ds(0, n_rows * SUBLANES)], dst_ref, sem).wait()


def _bf16_bits(x):
    return lax.bitcast_convert_type(x.astype(BF16).astype(F32), jnp.uint32)


def _dispatch_kernel(dest_ref, pad0_ref, padn_ref, nv_ref, h2t_ref, xs_ref,
                     buf, zbuf, load_sems, scat_sems, zsem):
    i = pl.program_id(0)
    n = pl.num_programs(0)
    TT = DISPATCH_TILE
    rows = TT * SUBLANES

    def pad_copies(wait):
        def bucket(b, carry):
            pos = pad0_ref[b]
            left = padn_ref[b]
            for size in PAD_COPY_SIZES:
                hit = (left & size) != 0
                first = 0 if wait else pl.multiple_of(pos * SUBLANES, SUBLANES)
                copy = pltpu.make_async_copy(
                    zbuf.at[pl.ds(0, size * SUBLANES)], xs_ref.at[pl.ds(first, size * SUBLANES)], zsem)

                @pl.when(hit)
                def _():
                    copy.wait() if wait else copy.start()
                pos = pos + jnp.where(hit, size, 0)
            return carry
        lax.fori_loop(0, pad0_ref.shape[0], bucket, 0)

        zrows = PAD_COPY_SIZES[0] * SUBLANES

        def unused_block(blk, carry):
            for part in range(SLOT_BLOCK // PAD_COPY_SIZES[0]):
                first = 0 if wait else pl.multiple_of(blk * SLOT_BLOCK * SUBLANES + part * zrows, zrows)
                copy = pltpu.make_async_copy(zbuf, xs_ref.at[pl.ds(first, zrows)], zsem)
                copy.wait() if wait else copy.start()
            return carry
        lax.fori_loop(nv_ref[0], xs_ref.shape[0] // (SLOT_BLOCK * SUBLANES), unused_block, 0)

    @pl.when(i == 0)
    def _():
        zbuf[...] = jnp.zeros_like(zbuf)
        pad_copies(wait=False)

    def load(tile, slot):
        src = h2t_ref.at[pl.ds(pl.multiple_of(tile * rows, rows), rows)]
        return pltpu.make_async_copy(src, buf.at[slot], load_sems.at[slot])

    def wait_scatter(slot):
        _wait_rows(buf.at[slot], xs_ref.at[pl.ds(0, rows)], TT, scat_sems.at[slot])

    @pl.when(i == 0)
    def _():
        load(0, 0).start()

    for slot in range(DISPATCH_RING):
        @pl.when((i + 1 < n) & ((i + 1) % DISPATCH_RING == slot))
        def _():
            @pl.when(i + 1 >= DISPATCH_RING)
            def _():
                wait_scatter(slot)
            load(i + 1, slot).start()

    for slot in range(DISPATCH_RING):
        @pl.when(i % DISPATCH_RING == slot)
        def _():
            load(i, slot).wait()

            def issue(r2, carry):
                for u in range(2):
                    r = r2 * 2 + u
                    _row_copy(buf.at[slot], xs_ref, r, dest_ref[i * TT + r],
                              scat_sems.at[slot]).start(priority=u)
                return carry
            lax.fori_loop(0, TT // 2, issue, 0)

    @pl.when(i == n - 1)
    def _():
        for slot in range(DISPATCH_RING):
            wait_scatter(slot)
        pad_copies(wait=True)


def _dispatch_call(dest, pad_start, pad_len, n_valid, h2t, n_slots):
    N = dest.shape[0]
    TT = DISPATCH_TILE
    assert N // TT >= DISPATCH_RING and SLOT_BLOCK <= 2 * PAD_COPY_SIZES[0]
    assert SLOT_BLOCK % PAD_COPY_SIZES[0] == 0
    grid_spec = pltpu.PrefetchScalarGridSpec(
        num_scalar_prefetch=4,
        grid=(N // TT,),
        in_specs=[pl.BlockSpec(memory_space=pl.ANY)],
        out_specs=pl.BlockSpec(memory_space=pl.ANY),
        scratch_shapes=[pltpu.VMEM((DISPATCH_RING, TT * SUBLANES, LANES), F32),
                        pltpu.VMEM((PAD_COPY_SIZES[0] * SUBLANES, LANES), F32),
                        pltpu.SemaphoreType.DMA((DISPATCH_RING,)),
                        pltpu.SemaphoreType.DMA((DISPATCH_RING,)),
                        pltpu.SemaphoreType.DMA],
    )
    return pl.pallas_call(
        _dispatch_kernel,
        grid_spec=grid_spec,
        out_shape=jax.ShapeDtypeStruct((n_slots * SUBLANES, LANES), F32),
        compiler_params=pltpu.CompilerParams(dimension_semantics=("arbitrary",)),
        name="dispatch",
    )(dest, pad_start, pad_len, n_valid, h2t)


def _expert_kernel(ea_ref, eb_ref, rows_ref, nv_ref, xs_ref, wg_ref, wu_ref, wd_ref, y_ref):
    i = pl.program_id(0)
    R = SLOT_BLOCK
    G = EXPERTS_PER_GROUP
    used = i < nv_ref[0]
    n_rows = rows_ref[i]

    def ffn(xb, e):
        a = _mm(xb, wg_ref[e])
        u = _mm(xb, wu_ref[e])
        return _mm((_silu(a) * u).astype(BF16), wd_ref[e])

    for k, m in enumerate(BLOCK_ROWS):
        fewer = BLOCK_ROWS[k - 1] if k else 0

        @pl.when(used & (n_rows > fewer) & (n_rows <= m))
        def _():
            xb = _from_token_tiles(xs_ref, 0, m).astype(BF16)
            ya = ffn(xb, ea_ref[i] & (G - 1))
            yb = ffn(xb, eb_ref[i] & (G - 1))
            packed = _bf16_bits(ya) | (_bf16_bits(yb) >> 16)
            for s in range(SUBLANES):
                y_ref[pl.ds(s, m, stride=SUBLANES), :] = packed[:, s * LANES:(s + 1) * LANES]
            if m < R:
                y_ref[pl.ds(m * SUBLANES, (R - m) * SUBLANES), :] = jnp.zeros(
                    ((R - m) * SUBLANES, LANES), y_ref.dtype)

    @pl.when(jnp.logical_not(used))
    def _():
        y_ref[...] = jnp.zeros_like(y_ref)


def _expert_call(block_ea, block_eb, block_rows, n_valid, xs, wg, wu, wd):
    assert BLOCK_ROWS[-1] == SLOT_BLOCK
    R = SLOT_BLOCK
    n_blocks = xs.shape[0] // (R * SUBLANES)
    D, F = wg.shape[1], wg.shape[2]
    G = EXPERTS_PER_GROUP
    last = lambda i, nv: jnp.minimum(i, nv[0] - 1)
    grp = lambda i, ea, eb, rows, nv: (ea[last(i, nv)] // G, 0, 0)
    blk = lambda i, ea, eb, rows, nv: (last(i, nv), 0)
    grid_spec = pltpu.PrefetchScalarGridSpec(
        num_scalar_prefetch=4,
        grid=(n_blocks,),
        in_specs=[
            pl.BlockSpec((R * SUBLANES, LANES), blk),
            pl.BlockSpec((G, D, F), grp, pipeline_mode=pl.Buffered(1)),
            pl.BlockSpec((G, D, F), grp, pipeline_mode=pl.Buffered(1)),
            pl.BlockSpec((G, F, D), grp, pipeline_mode=pl.Buffered(1)),
        ],
        out_specs=pl.BlockSpec((R * SUBLANES, LANES), lambda i, ea, eb, rows, nv: (i, 0)),
    )
    return pl.pallas_call(
        _expert_kernel,
        grid_spec=grid_spec,
        out_shape=jax.ShapeDtypeStruct(xs.shape, jnp.uint32),
        compiler_params=pltpu.CompilerParams(
            dimension_semantics=("arbitrary",), vmem_limit_bytes=VMEM_LIMIT),
        name="experts",
    )(block_ea, block_eb, block_rows, n_valid, xs, wg, wu, wd)


def _final_kernel(dest_ref, yt_ref, x1_ref, g_ref, mod_ref, w_ref, b_ref, o_ref, buf, sems):
    i = pl.program_id(0)
    n_steps = pl.num_programs(0)
    TT = TOK_TILE

    def start_tile(tile, slot):
        _start_rows(yt_ref, buf.at[slot], dest_ref, tile * TT, 0, TT, sems.at[slot])

    @pl.when(i == 0)
    def _():
        start_tile(0, 0)

    for slot in range(2):
        @pl.when((i + 1 < n_steps) & ((i + 1) % 2 == slot))
        def _():
            start_tile(i + 1, slot)

    gate2 = mod_ref[0, 5:6, :]
    g = g_ref[...]
    for slot in range(2):
        @pl.when(i % 2 == slot)
        def _():
            _wait_rows(yt_ref, buf.at[slot], TT, sems.at[slot])
            for r in range(TT // ROW_BLOCK):
                rows = slice(r * ROW_BLOCK, (r + 1) * ROW_BLOCK)
                packed = _from_token_tiles(buf.at[slot], r * ROW_BLOCK, ROW_BLOCK)
                ya = lax.bitcast_convert_type(packed & jnp.uint32(0xFFFF0000), F32)
                yb = lax.bitcast_convert_type(packed << 16, F32)
                y = ya * g[rows, 0:1] + yb * g[rows, 1:2]
                u = DEEPNORM_ALPHA * x1_ref[rows, :] + gate2 * y
                o_ref[rows, :] = _ln(u) * w_ref[...] + b_ref[...]


def _final_call(dest, yt, x1, gates, mod3, w, b, seq_len):
    N, D = x1.shape
    TT = TOK_TILE
    per_seq = seq_len // TT
    grid_spec = pltpu.PrefetchScalarGridSpec(
        num_scalar_prefetch=1,
        grid=(N // TT,),
        in_specs=[
            pl.BlockSpec(memory_space=pl.ANY),
            pl.BlockSpec((TT, D), lambda i, d: (i, 0)),
            pl.BlockSpec((TT, 2), lambda i, d: (i, 0)),
            pl.BlockSpec((1, 6, D), lambda i, d: (i // per_seq, 0, 0)),
            pl.BlockSpec((1, D), lambda i, d: (0, 0)),
            pl.BlockSpec((1, D), lambda i, d: (0, 0)),
        ],
        out_specs=pl.BlockSpec((TT, D), lambda i, d: (i, 0)),
        scratch_shapes=[pltpu.VMEM((2, TT * SUBLANES, LANES), jnp.uint32), pltpu.SemaphoreType.DMA((2,))],
    )
    return pl.pallas_call(
        _final_kernel,
        grid_spec=grid_spec,
        out_shape=jax.ShapeDtypeStruct((N, D), F32),
        compiler_params=pltpu.CompilerParams(
            dimension_semantics=("arbitrary",), vmem_limit_bytes=VMEM_LIMIT),
        name="combine_ln",
    )(dest, yt, x1, gates, mod3, w, b)


def _route_kernel(lgt_ref, rf_ref, ri_ref, meta_ref, cnt_scr, base_scr, bstart_scr):
    ph = pl.program_id(0)
    t = pl.program_id(1)
    TR = ROUTE_TILE
    E = N_EXPERTS
    R = SLOT_BLOCK
    G = EXPERTS_PER_GROUP
    lg = lgt_ref[...]

    g = [lg[E + k:E + k + 1, :] for k in range(N_GROUPS)]
    gmax = jnp.maximum(jnp.maximum(g[0], g[1]), jnp.maximum(g[2], g[3]))
    gs = jnp.where(g[0] == gmax, 0.0, jnp.where(g[1] == gmax, 1.0, jnp.where(g[2] == gmax, 2.0, 3.0)))
    psum = (jnp.exp(g[0] - gmax) + jnp.exp(g[1] - gmax)) + (jnp.exp(g[2] - gmax) + jnp.exp(g[3] - gmax))
    p_star = 1.0 / psum
    esel = jnp.where(gs == 0.0, lg[0:G], jnp.where(gs == 1.0, lg[G:2 * G],
                                                   jnp.where(gs == 2.0, lg[2 * G:3 * G], lg[3 * G:4 * G])))
    sub = lax.broadcasted_iota(jnp.int32, (G, TR), 0).astype(F32)
    m1 = jnp.max(esel, axis=0, keepdims=True)
    i1 = jnp.min(jnp.where(esel == m1, sub, float(G)), axis=0, keepdims=True)
    es2 = jnp.where(sub == i1, -jnp.inf, esel)
    m2 = jnp.max(es2, axis=0, keepdims=True)
    i2 = jnp.min(jnp.where(es2 == m2, sub, float(G)), axis=0, keepdims=True)
    d = jnp.exp(m2 - m1)
    w1 = 1.0 / (1.0 + d)
    w2 = d * w1
    lo = jnp.minimum(i1, i2)
    hi = jnp.maximum(i1, i2)
    bucket = gs * PAIRS_PER_GROUP + (lo * (2 * G - 1 - lo) * 0.5 + (hi - lo - 1.0))
    first_is_lo = i1 < i2
    gate_a = p_star * jnp.where(first_is_lo, w1, w2)
    gate_b = p_star * jnp.where(first_is_lo, w2, w1)
    row = lax.broadcasted_iota(jnp.int32, (NB, TR), 0).astype(F32)
    oh = row == bucket
    cnt = oh.astype(F32)

    @pl.when(ph == 0)
    def _():
        @pl.when(t == 0)
        def _():
            cnt_scr[...] = jnp.zeros_like(cnt_scr)
        acc = cnt_scr[...]
        for j in range(TR // LANES):
            acc = acc + cnt[:, j * LANES:(j + 1) * LANES]
        cnt_scr[...] = acc

    @pl.when((ph == 1) & (t == 0))
    def _():
        counts = jnp.sum(cnt_scr[...], axis=1, keepdims=True)
        nblk = jnp.floor((counts + (R - 1)) * (1.0 / R))
        nblk_b = jnp.broadcast_to(nblk, (NB, LANES))
        rb = lax.broadcasted_iota(jnp.int32, (NB, NB), 0)
        cb = lax.broadcasted_iota(jnp.int32, (NB, NB), 1)
        bstart = _mm((cb < rb).astype(BF16), nblk_b.astype(BF16))
        bstart_scr[...] = bstart
        base_scr[...] = jnp.zeros_like(base_scr)
        k = lax.broadcasted_iota(jnp.int32, (NB, 1), 0).astype(F32)
        grp = sum((k >= float(m * PAIRS_PER_GROUP)).astype(F32) for m in range(1, N_GROUPS))
        p = k - grp * PAIRS_PER_GROUP
        pair_start = [i * (2 * G - 1 - i) // 2 for i in range(G - 1)]
        ia = sum((p >= float(s)).astype(F32) for s in pair_start[1:])
        ib = p - ia * (2 * G - 1 - ia) * 0.5 + ia + 1.0
        n_pad = meta_ref.shape[1]
        blk = lax.broadcasted_iota(jnp.int32, (NB, n_pad), 1).astype(F32)
        first = jnp.broadcast_to(bstart[:, 0:1], (NB, n_pad))
        last = jnp.broadcast_to((bstart + nblk_b)[:, 0:1], (NB, n_pad))
        member = (first <= blk) & (blk < last)
        ea = jnp.sum(jnp.where(member, grp * G + ia, 0.0), axis=0, keepdims=True)
        eb = jnp.sum(jnp.where(member, grp * G + ib, 0.0), axis=0, keepdims=True)
        meta_ref[...] = jnp.zeros_like(meta_ref)
        meta_ref[0:1, :] = ea.astype(jnp.int32)
        meta_ref[1:2, :] = eb.astype(jnp.int32)
        meta_ref[2:3, :] = jnp.max(last, axis=0, keepdims=True).astype(jnp.int32)
        in_blk = jnp.clip(counts - (blk - first) * R, 0.0, float(R))
        meta_ref[5:6, :] = jnp.sum(jnp.where(member, in_blk, 0.0), axis=0, keepdims=True).astype(jnp.int32)
        pad0 = jnp.broadcast_to(bstart[:, 0:1] * R + counts, (NB, NB)).T[0:1, :]
        padn = jnp.broadcast_to(nblk * R - counts, (NB, NB)).T[0:1, :]
        meta_ref[3:4, 0:NB] = pad0.astype(jnp.int32)
        meta_ref[4:5, 0:NB] = padn.astype(jnp.int32)

    @pl.when(ph == 1)
    def _():
        ri = lax.broadcasted_iota(jnp.int32, (LANES, 2 * LANES), 0)
        ci = lax.broadcasted_iota(jnp.int32, (LANES, 2 * LANES), 1)
        w = ((ci >= ri) | (ci >= LANES)).astype(BF16)
        base = base_scr[...]
        slot0 = bstart_scr[...] * R
        ds = []
        for j in range(TR // LANES):
            sl = slice(j * LANES, (j + 1) * LANES)
            blk = cnt[:, sl]
            res = _mm(blk.astype(BF16), w)
            val = slot0 + base + (res[:, 0:LANES] - blk)
            ds.append(jnp.sum(jnp.where(oh[:, sl], val, 0.0), axis=0, keepdims=True))
            base = base + res[:, LANES:2 * LANES]
        base_scr[...] = base
        ri_ref[...] = jnp.zeros_like(ri_ref)
        ri_ref[0:1, :] = jnp.concatenate(ds, axis=1).astype(jnp.int32)
        rf_ref[...] = jnp.zeros_like(rf_ref)
        rf_ref[0:1, :] = gate_a
        rf_ref[1:2, :] = gate_b


def _route_call(lgt, n_blocks):
    rows, N = lgt.shape
    TR = ROUTE_TILE
    E = NB
    n_pad = -(-n_blocks // LANES) * LANES
    return pl.pallas_call(
        _route_kernel,
        grid=(2, N // TR),
        in_specs=[pl.BlockSpec((rows, TR), lambda ph, t: (0, t))],
        out_specs=[
            pl.BlockSpec((SUBLANES, TR), lambda ph, t: (0, ph * t)),
            pl.BlockSpec((SUBLANES, TR), lambda ph, t: (0, ph * t)),
            pl.BlockSpec((SUBLANES, n_pad), lambda ph, t: (0, 0)),
        ],
        out_shape=[
            jax.ShapeDtypeStruct((SUBLANES, N), F32),
            jax.ShapeDtypeStruct((SUBLANES, N), jnp.int32),
            jax.ShapeDtypeStruct((SUBLANES, n_pad), jnp.int32),
        ],
        scratch_shapes=[pltpu.VMEM((E, LANES), F32), pltpu.VMEM((E, LANES), F32), pltpu.VMEM((E, LANES), F32)],
        compiler_params=pltpu.CompilerParams(dimension_semantics=("arbitrary", "arbitrary")),
        name="route",
    )(lgt)


def kernel(x, c, positions, w_ada, b_ada, w_in, w_out, hgrn_lb, hgrn_norm_w, ret_norm_w, post_ln1_w,
           post_ln1_b, w_rg, b_rg, w_re, b_re, w_gate, w_up, w_down, post_ln2_w, post_ln2_b):
    B, S, D = x.shape
    N = B * S
    hw = N_HEADS * D_HEAD

    mod3 = _ada_call(c, w_ada[0], b_ada[0]).reshape(B, 6, D)

    win = w_in[0].astype(BF16)
    wout = w_out[0].astype(BF16)
    pad = LANES - N_EXPERTS - N_GROUPS
    wr = jnp.concatenate([w_re[0], w_rg[0], jnp.zeros((D, pad), F32)], axis=1).astype(BF16)
    br = jnp.concatenate([b_re[0], b_rg[0], jnp.zeros((pad,), F32)]).reshape(1, LANES)

    x1, h2t, lgt, wg16, wu16, wd16 = _mixer_call(
        x, positions, mod3, win, wout, hgrn_lb, hgrn_norm_w[0].reshape(1, hw),
        ret_norm_w[0].reshape(1, hw), post_ln1_w[0].reshape(1, D), post_ln1_b[0].reshape(1, D), wr, br,
        w_gate[0], w_up[0], w_down[0])

    n_blocks = N // SLOT_BLOCK + N_GROUPS * PAIRS_PER_GROUP
    rf, ri, meta = _route_call(lgt, n_blocks)
    dest = ri[0]
    gates = rf[0:2].T
    xs = _dispatch_call(dest, meta[3, :NB], meta[4, :NB], meta[2, 0:1], h2t, n_blocks * SLOT_BLOCK)
    yt = _expert_call(meta[0, :n_blocks], meta[1, :n_blocks], meta[5, :n_blocks], meta[2, 0:1], xs,
                      wg16, wu16, wd16)
    out = _final_call(dest, yt, x1.reshape(N, D), gates, mod3, post_ln2_w[0].reshape(1, D),
                      post_ln2_b[0].reshape(1, D), S)
    return out.reshape(B, S, D)
```

```python
import functools
import math

import numpy as np
import jax
import jax.numpy as jnp
from jax import lax
from jax.experimental import pallas as pl
from jax.experimental.pallas import tpu as pltpu

F32 = jnp.float32
BF16 = jnp.bfloat16

D_MODEL = 1024
N_HEADS = 4
D_HEAD = 128
HGRN_CHUNK = 64
RET_CHUNK = 256
ROPE_BASE = 10000.0
N_EXPERTS = 32
EXPERTS_PER_GROUP = 8
N_GROUPS = 4
D_EXPERT = 512
DEEPNORM_ALPHA = 2.0 ** 0.25
LN_EPS = 1e-5

SEQ_TILE = 512
ROW_BLOCK = 128
PAIRS_PER_GROUP = EXPERTS_PER_GROUP * (EXPERTS_PER_GROUP - 1) // 2
NB = 128
SLOT_BLOCK = 256
BLOCK_ROWS = (64, 128, 192, 256)
DISPATCH_TILE = 512
DISPATCH_RING = 3
PAD_COPY_SIZES = (128, 64, 32, 16, 8, 4, 2, 1)
TOK_TILE = 512
WIN_ROWS = 8
WIN_LANES = 256
MAX_WINDOWS = N_GROUPS * PAIRS_PER_GROUP + TOK_TILE // WIN_ROWS
ROUTE_TILE = 2048
N_LOGIT_ROWS = 40
LANES = 128
SUBLANES = 8
VMEM_LIMIT = 56 * 1024 * 1024


def _mm(a, b):
    return jnp.dot(a, b, preferred_element_type=F32)


def _mm_nt(a, b):
    return lax.dot_general(a, b, (((1,), (1,)), ((), ())), preferred_element_type=F32)


def _mm_tn(a, b):
    return lax.dot_general(a, b, (((0,), (0,)), ((), ())), preferred_element_type=F32)


def _sigmoid(x):
    return 1.0 / (1.0 + jnp.exp(-x))


def _silu(x):
    return x * _sigmoid(x)


def _ln(x):
    mu = jnp.mean(x, axis=-1, keepdims=True)
    xc = x - mu
    var = jnp.mean(xc * xc, axis=-1, keepdims=True)
    return xc * lax.rsqrt(var + LN_EPS)


def _const_spec(shape):
    nd = len(shape)
    return pl.BlockSpec(shape, lambda *_: (0,) * nd, pipeline_mode=pl.Buffered(1))


def _ada_kernel(c_ref, w_ref, b_ref, o_ref):
    ca = _silu(c_ref[...])
    o_ref[...] = _mm(ca.astype(BF16), w_ref[...].astype(BF16)) + b_ref[...]


def _ada_call(c, w_ada, b_ada):
    B, D = c.shape
    n_out = w_ada.shape[1]
    return pl.pallas_call(
        _ada_kernel,
        grid=(n_out // D,),
        in_specs=[pl.BlockSpec((B, D), lambda j: (0, 0)),
                  pl.BlockSpec((D, D), lambda j: (0, j)),
                  pl.BlockSpec((1, D), lambda j: (0, j))],
        out_specs=pl.BlockSpec((B, D), lambda j: (0, j)),
        out_shape=jax.ShapeDtypeStruct((B, n_out), F32),
        name="ada_mod",
    )(c, w_ada, b_ada.reshape(1, n_out))


def _mixer_kernel(x_ref, pos_ref, mod_ref, win_ref, wout_ref, lb_ref, hnw_ref, rnw_ref,
                  ln1w_ref, ln1b_ref, invf_ref, dmat_ref, qdec_ref, kdec_ref, wr_ref, br_ref,
                  xp_ref, modp_ref, wg32_ref, wu32_ref, wd32_ref,
                  x1_ref, h2t_ref, lg_ref, wg16_ref, wu16_ref, wd16_ref,
                  h_scr, proj_scr, proj2_scr, o_scr, cos_scr, sin_scr, sh_scr, sr_scr, y_scr, h2_scr,
                  *, ret_chunk_decay, tiles_per_seq):
    T = SEQ_TILE
    D = D_MODEL
    dh = D_HEAD
    hw = N_HEADS * dh
    hw4 = 4 * hw
    step = pl.program_id(0)
    n_tiles = pl.num_programs(0) - 1

    wg16_ref[...] = wg32_ref[...].astype(BF16)
    wu16_ref[...] = wu32_ref[...].astype(BF16)
    wd16_ref[...] = wd32_ref[...].astype(BF16)

    @pl.when(step == 0)
    def _():
        y_scr[...] = jnp.zeros_like(y_scr)

    @pl.when((step < n_tiles) & (step % tiles_per_seq == 0))
    def _():
        sh_scr[...] = jnp.zeros_like(sh_scr)
        sr_scr[...] = jnp.zeros_like(sr_scr)

    gate1p = modp_ref[0, 2:3, :]
    shift2p = modp_ref[0, 3:4, :]
    scale2p = modp_ref[0, 4:5, :]

    def post_block(r):
        rows = slice(r * ROW_BLOCK, (r + 1) * ROW_BLOCK)
        u = DEEPNORM_ALPHA * xp_ref[0, rows, :] + gate1p * y_scr[rows, :]
        x1 = _ln(u) * ln1w_ref[...] + ln1b_ref[...]
        x1_ref[0, rows, :] = x1
        h2 = (_ln(x1) * (1.0 + scale2p) + shift2p).astype(BF16)
        h2_scr[rows, :] = h2
        h2f = h2.astype(F32)
        for s in range(SUBLANES):
            h2t_ref[pl.ds(r * ROW_BLOCK * SUBLANES + s, ROW_BLOCK, stride=SUBLANES), :] = (
                h2f[:, s * LANES:(s + 1) * LANES])

    shift1 = mod_ref[0, 0:1, :]
    scale1 = mod_ref[0, 1:2, :]

    for r in range(T // ROW_BLOCK):
        rows = slice(r * ROW_BLOCK, (r + 1) * ROW_BLOCK)
        h = _ln(x_ref[0, rows, :]) * (1.0 + scale1) + shift1
        h_scr[rows, :] = h.astype(BF16)

    n_post = T // ROW_BLOCK
    proj_scr[...] = _mm(h_scr[...], win_ref[:, 0:hw4])
    for r in range(n_post // 2):
        post_block(r)
    proj2_scr[...] = _mm(h_scr[...], win_ref[:, hw4:2 * hw4])
    for r in range(n_post // 2, n_post):
        post_block(r)
    lg_ref[...] = (_mm(h2_scr[...], wr_ref[...]) + br_ref[...]).T[0:N_LOGIT_ROWS, :]

    pos = pos_ref[0].astype(F32)
    ang_t = invf_ref[...] * pos
    cos_t = jnp.cos(ang_t)
    sin_t = jnp.sin(ang_t)
    cos_scr[...] = jnp.concatenate([cos_t, cos_t], axis=0).T
    sin_scr[...] = jnp.concatenate([-sin_t, sin_t], axis=0).T

    l0 = lb_ref[0:1, :]
    l1 = lb_ref[1:2, :]
    lmax = jnp.maximum(l0, l1)
    e0 = jnp.exp(l0 - lmax)
    e1 = jnp.exp(l1 - lmax)
    lb = e0 / (e0 + e1)
    hnw = hnw_ref[...]

    C = HGRN_CHUNK
    rr = lax.broadcasted_iota(jnp.int32, (C, C), 0)
    cc = lax.broadcasted_iota(jnp.int32, (C, C), 1)
    causal = rr >= cc
    row_id = lax.broadcasted_iota(jnp.int32, (C, hw), 0)

    def hgrn_chunk(c):
        rows = slice(c * C, (c + 1) * C)
        logfs, ks = [], []
        for hd in range(N_HEADS):
            z = proj_scr[rows, hw + hd * dh: hw + (hd + 1) * dh]
            e = jnp.exp(-jnp.abs(z))
            r = 1.0 / (1.0 + e)
            er = e * r
            zp = z >= 0
            sig_pos = jnp.where(zp, r, er)
            sig_neg = jnp.where(zp, er, r)
            lbh = lb[:, hd * dh:(hd + 1) * dh]
            logfs.append(jnp.log(lbh + (1.0 - lbh) * sig_pos))
            ks.append((1.0 - lbh) * sig_neg)
        bsum = jnp.concatenate(logfs, axis=1)
        for k in (1, 2, 4):
            bsum = bsum + jnp.where(row_id >= k, pltpu.roll(bsum, k, 0), 0.0)
        for k in (8, 16, 32):
            bsum = bsum + jnp.concatenate([jnp.zeros((k, hw), F32), bsum[0:C - k, :]], axis=0)
        for hd in range(N_HEADS):
            cols = slice(hd * dh, (hd + 1) * dh)
            b = bsum[:, cols]
            b_last = b[C - 1:C, :]
            q = proj_scr[rows, cols]
            v = proj_scr[rows, 2 * hw + hd * dh: 2 * hw + (hd + 1) * dh].astype(BF16)
            g = proj_scr[rows, 3 * hw + hd * dh: 3 * hw + (hd + 1) * dh]
            k = ks[hd]
            q_in = (_silu(q) * jnp.exp(b)).astype(BF16)
            k_in = (k * jnp.exp(-b)).astype(BF16)
            k_dec = (k * jnp.exp(b_last - b)).astype(BF16)
            st = sh_scr[hd]
            a = jnp.where(causal, _mm_nt(q_in, k_in), 0.0)
            o = _mm(a.astype(BF16), v) + _mm_nt(q_in, st.astype(BF16))
            sh_scr[hd] = st * jnp.exp(b_last) + _mm_tn(v, k_dec)
            ms = jnp.mean(o * o, axis=-1, keepdims=True)
            on = o * lax.rsqrt(ms + LN_EPS) * hnw[:, hd * dh:(hd + 1) * dh] * _silu(g)
            o_scr[rows, hd * dh:(hd + 1) * dh] = on.astype(BF16)

    rnw = rnw_ref[...]
    RC = RET_CHUNK
    q_scale = dh ** -0.5

    def retention_unit(sub, hd):
        rows = slice(sub * RC, (sub + 1) * RC)
        cs = cos_scr[rows, :]
        sn = sin_scr[rows, :]
        rq = proj2_scr[rows, hd * dh: (hd + 1) * dh]
        rk = proj2_scr[rows, hw + hd * dh: hw + (hd + 1) * dh]
        v = proj2_scr[rows, 2 * hw + hd * dh: 2 * hw + (hd + 1) * dh].astype(BF16)
        g = proj2_scr[rows, 3 * hw + hd * dh: 3 * hw + (hd + 1) * dh]
        q = (rq * cs + pltpu.roll(rq, dh // 2, 1) * sn) * q_scale
        k = rk * cs + pltpu.roll(rk, dh // 2, 1) * sn
        qb = q.astype(BF16)
        st = sr_scr[hd]
        s = _mm_nt(qb, k.astype(BF16)) * dmat_ref[hd]
        o = _mm(s.astype(BF16), v) + _mm_nt(qb, st.astype(BF16)) * qdec_ref[hd]
        sr_scr[hd] = st * ret_chunk_decay[hd] + _mm_tn(v, (k * kdec_ref[hd]).astype(BF16))
        mu = jnp.mean(o, axis=-1, keepdims=True)
        oc = o - mu
        var = jnp.mean(oc * oc, axis=-1, keepdims=True)
        on = oc * lax.rsqrt(var + LN_EPS) * rnw[:, hd * dh:(hd + 1) * dh] * _silu(g)
        o_scr[rows, hw + hd * dh: hw + (hd + 1) * dh] = on.astype(BF16)

    n_chunks = T // C
    units = [(sub, hd) for sub in range(T // RC) for hd in range(N_HEADS)]
    for c in range(n_chunks):
        hgrn_chunk(c)
        for sub, hd in units[c * len(units) // n_chunks:(c + 1) * len(units) // n_chunks]:
            retention_unit(sub, hd)

    y_scr[...] = _mm(o_scr[...], wout_ref[...])


def _retention_tables():
    h = np.arange(N_HEADS, dtype=np.float64)
    log_gamma = np.log(1.0 - np.exp2(-5.0 - h))
    idx = np.arange(RET_CHUNK, dtype=np.float64)
    rel = idx[:, None] - idx[None, :]
    dmat = np.where(rel >= 0, np.exp(np.maximum(rel, 0.0)[None] * log_gamma[:, None, None]), 0.0)
    qdec = np.exp((idx + 1.0)[None, :] * log_gamma[:, None])
    kdec = np.exp((RET_CHUNK - 1.0 - idx)[None, :] * log_gamma[:, None])
    cdec = np.exp(RET_CHUNK * log_gamma)
    bc = lambda t: np.broadcast_to(t[:, :, None], (N_HEADS, RET_CHUNK, D_HEAD))
    return (jnp.asarray(dmat, F32), jnp.asarray(bc(qdec), F32), jnp.asarray(bc(kdec), F32),
            tuple(float(np.float32(v)) for v in cdec))


def _mixer_call(x, positions, mod3, win, wout, hgrn_lb, hnw, rnw, ln1w, ln1b, wr, br, wg, wu, wd):
    B, S, D = x.shape
    T = SEQ_TILE
    nj = S // T
    dmat, qdec, kdec, cdec = _retention_tables()
    inv_freq = np.power(ROPE_BASE, -np.arange(0, D_HEAD, 2, dtype=np.float64) / D_HEAD)
    invf = jnp.asarray(inv_freq[:, None], F32)
    pos3 = positions.reshape(B, 1, S)
    kern = functools.partial(_mixer_kernel, ret_chunk_decay=cdec, tiles_per_seq=nj)
    n_tiles = B * nj
    cur = lambda s: jnp.minimum(s, n_tiles - 1)
    prev = lambda s: jnp.maximum(s - 1, 0)
    E, _, F = wg.shape
    wg2, wu2, wd2 = wg.reshape(E * D, F), wu.reshape(E * D, F), wd.reshape(E * F, D)
    rg, rd = (E * D) // n_tiles, (E * F) // n_tiles
    assert rg * n_tiles == E * D and rd * n_tiles == E * F and rd % 16 == 0
    wspec_g = pl.BlockSpec((rg, F), lambda s: (cur(s), 0))
    wspec_d = pl.BlockSpec((rd, D), lambda s: (cur(s), 0))
    outs = pl.pallas_call(
        kern,
        grid=(n_tiles + 1,),
        in_specs=[
            pl.BlockSpec((1, T, D), lambda s: (cur(s) // nj, cur(s) % nj, 0)),
            pl.BlockSpec((1, 1, T), lambda s: (cur(s) // nj, 0, cur(s) % nj)),
            pl.BlockSpec((1, 6, D), lambda s: (cur(s) // nj, 0, 0)),
            _const_spec(win.shape), _const_spec(wout.shape), _const_spec(hgrn_lb.shape),
            _const_spec(hnw.shape), _const_spec(rnw.shape), _const_spec(ln1w.shape),
            _const_spec(ln1b.shape), _const_spec(invf.shape), _const_spec(dmat.shape),
            _const_spec(qdec.shape), _const_spec(kdec.shape), _const_spec(wr.shape),
            _const_spec(br.shape),
            pl.BlockSpec((1, T, D), lambda s: (prev(s) // nj, prev(s) % nj, 0)),
            pl.BlockSpec((1, 6, D), lambda s: (prev(s) // nj, 0, 0)),
            wspec_g, wspec_g, wspec_d,
        ],
        out_specs=[
            pl.BlockSpec((1, T, D), lambda s: (prev(s) // nj, prev(s) % nj, 0)),
            pl.BlockSpec((T * SUBLANES, LANES), lambda s: (prev(s), 0)),
            pl.BlockSpec((N_LOGIT_ROWS, T), lambda s: (0, prev(s))),
            wspec_g, wspec_g, wspec_d,
        ],
        out_shape=[
            jax.ShapeDtypeStruct((B, S, D), F32),
            jax.ShapeDtypeStruct((B * S * SUBLANES, LANES), F32),
            jax.ShapeDtypeStruct((N_LOGIT_ROWS, B * S), F32),
            jax.ShapeDtypeStruct(wg2.shape, BF16),
            jax.ShapeDtypeStruct(wu2.shape, BF16),
            jax.ShapeDtypeStruct(wd2.shape, BF16),
        ],
        scratch_shapes=[
            pltpu.VMEM((T, D), BF16),
            pltpu.VMEM((T, 4 * N_HEADS * D_HEAD), F32),
            pltpu.VMEM((T, 4 * N_HEADS * D_HEAD), F32),
            pltpu.VMEM((T, D), BF16),
            pltpu.VMEM((T, D_HEAD), F32),
            pltpu.VMEM((T, D_HEAD), F32),
            pltpu.VMEM((N_HEADS, D_HEAD, D_HEAD), F32),
            pltpu.VMEM((N_HEADS, D_HEAD, D_HEAD), F32),
            pltpu.VMEM((T, D), F32),
            pltpu.VMEM((T, D), BF16),
        ],
        compiler_params=pltpu.CompilerParams(
            dimension_semantics=("arbitrary",), vmem_limit_bytes=VMEM_LIMIT),
        name="mixer",
    )(x, pos3, mod3, win, wout, hgrn_lb, hnw, rnw, ln1w, ln1b, invf, dmat, qdec, kdec, wr, br, x, mod3,
      wg2, wu2, wd2)
    x1, h2t, lgt, wg16, wu16, wd16 = outs
    return x1, h2t, lgt, wg16.reshape(E, D, F), wu16.reshape(E, D, F), wd16.reshape(E, F, D)


def _row_copy(src_ref, dst_ref, src_row, dst_row, sem):
    s0 = pl.multiple_of(src_row * SUBLANES, SUBLANES)
    d0 = pl.multiple_of(dst_row * SUBLANES, SUBLANES)
    return pltpu.make_async_copy(src_ref.at[pl.ds(s0, SUBLANES)], dst_ref.at[pl.ds(d0, SUBLANES)], sem)


def _from_token_tiles(buf_ref, first_row, n_rows):
    return jnp.concatenate(
        [buf_ref[pl.ds(first_row * SUBLANES + s, n_rows, stride=SUBLANES), :] for s in range(SUBLANES)],
        axis=1)


def _start_rows(src_ref, dst_ref, idx_ref, idx_base, dst_base, n_rows, sem):
    def issue(r2, carry):
        for u in range(2):
            r = r2 * 2 + u
            _row_copy(src_ref, dst_ref, idx_ref[idx_base + r], dst_base + r, sem).start(priority=u)
        return carry
    lax.fori_loop(0, n_rows // 2, issue, 0)


def _wait_rows(src_ref, dst_ref, n_rows, sem):
    pltpu.make_async_copy(src_ref.at[pl.ds(0, n_rows * SUBLANES)], dst_ref, sem).wait()


def _bf16_bits(x):
    return lax.bitcast_convert_type(x.astype(BF16).astype(F32), jnp.uint32)


def _dispatch_kernel(dest_ref, pad0_ref, padn_ref, nv_ref, h2t_ref, xs_ref,
                     buf, zbuf, load_sems, scat_sems, zsem):
    i = pl.program_id(0)
    n = pl.num_programs(0)
    TT = DISPATCH_TILE
    rows = TT * SUBLANES

    def pad_copies(wait):
        def bucket(b, carry):
            pos = pad0_ref[b]
            left = padn_ref[b]
            for size in PAD_COPY_SIZES:
                hit = (left & size) != 0
                first = 0 if wait else pl.multiple_of(pos * SUBLANES, SUBLANES)
                copy = pltpu.make_async_copy(
                    zbuf.at[pl.ds(0, size * SUBLANES)], xs_ref.at[pl.ds(first, size * SUBLANES)], zsem)

                @pl.when(hit)
                def _():
                    copy.wait() if wait else copy.start()
                pos = pos + jnp.where(hit, size, 0)
            return carry
        lax.fori_loop(0, pad0_ref.shape[0], bucket, 0)

        zrows = PAD_COPY_SIZES[0] * SUBLANES

        def unused_block(blk, carry):
            for part in range(SLOT_BLOCK // PAD_COPY_SIZES[0]):
                first = 0 if wait else pl.multiple_of(blk * SLOT_BLOCK * SUBLANES + part * zrows, zrows)
                copy = pltpu.make_async_copy(zbuf, xs_ref.at[pl.ds(first, zrows)], zsem)
                copy.wait() if wait else copy.start()
            return carry
        lax.fori_loop(nv_ref[0], xs_ref.shape[0] // (SLOT_BLOCK * SUBLANES), unused_block, 0)

    @pl.when(i == 0)
    def _():
        zbuf[...] = jnp.zeros_like(zbuf)
        pad_copies(wait=False)

    def load(tile, slot):
        src = h2t_ref.at[pl.ds(pl.multiple_of(tile * rows, rows), rows)]
        return pltpu.make_async_copy(src, buf.at[slot], load_sems.at[slot])

    def wait_scatter(slot):
        _wait_rows(buf.at[slot], xs_ref.at[pl.ds(0, rows)], TT, scat_sems.at[slot])

    @pl.when(i == 0)
    def _():
        load(0, 0).start()

    for slot in range(DISPATCH_RING):
        @pl.when((i + 1 < n) & ((i + 1) % DISPATCH_RING == slot))
        def _():
            @pl.when(i + 1 >= DISPATCH_RING)
            def _():
                wait_scatter(slot)
            load(i + 1, slot).start()

    for slot in range(DISPATCH_RING):
        @pl.when(i % DISPATCH_RING == slot)
        def _():
            load(i, slot).wait()

            def issue(r2, carry):
                for u in range(2):
                    r = r2 * 2 + u
                    _row_copy(buf.at[slot], xs_ref, r, dest_ref[i * TT + r],
                              scat_sems.at[slot]).start(priority=u)
                return carry
            lax.fori_loop(0, TT // 2, issue, 0)

    @pl.when(i == n - 1)
    def _():
        for slot in range(DISPATCH_RING):
            wait_scatter(slot)
        pad_copies(wait=True)


def _dispatch_call(dest, pad_start, pad_len, n_valid, h2t, n_slots):
    N = dest.shape[0]
    TT = DISPATCH_TILE
    assert N // TT >= DISPATCH_RING and SLOT_BLOCK <= 2 * PAD_COPY_SIZES[0]
    assert SLOT_BLOCK % PAD_COPY_SIZES[0] == 0
    grid_spec = pltpu.PrefetchScalarGridSpec(
        num_scalar_prefetch=4,
        grid=(N // TT,),
        in_specs=[pl.BlockSpec(memory_space=pl.ANY)],
        out_specs=pl.BlockSpec(memory_space=pl.ANY),
        scratch_shapes=[pltpu.VMEM((DISPATCH_RING, TT * SUBLANES, LANES), F32),
                        pltpu.VMEM((PAD_COPY_SIZES[0] * SUBLANES, LANES), F32),
                        pltpu.SemaphoreType.DMA((DISPATCH_RING,)),
                        pltpu.SemaphoreType.DMA((DISPATCH_RING,)),
                        pltpu.SemaphoreType.DMA],
    )
    return pl.pallas_call(
        _dispatch_kernel,
        grid_spec=grid_spec,
        out_shape=jax.ShapeDtypeStruct((n_slots * SUBLANES, LANES), F32),
        compiler_params=pltpu.CompilerParams(dimension_semantics=("arbitrary",)),
        name="dispatch",
    )(dest, pad_start, pad_len, n_valid, h2t)


def _expert_kernel(ea_ref, eb_ref, rows_ref, nv_ref, xs_ref, wg_ref, wu_ref, wd_ref, y_ref):
    i = pl.program_id(0)
    R = SLOT_BLOCK
    G = EXPERTS_PER_GROUP
    used = i < nv_ref[0]
    n_rows = rows_ref[i]

    def ffn(xb, e):
        a = _mm(xb, wg_ref[e])
        u = _mm(xb, wu_ref[e])
        return _mm((_silu(a) * u).astype(BF16), wd_ref[e])

    for k, m in enumerate(BLOCK_ROWS):
        fewer = BLOCK_ROWS[k - 1] if k else 0

        @pl.when(used & (n_rows > fewer) & (n_rows <= m))
        def _():
            xb = _from_token_tiles(xs_ref, 0, m).astype(BF16)
            ya = ffn(xb, ea_ref[i] & (G - 1))
            yb = ffn(xb, eb_ref[i] & (G - 1))
            packed = _bf16_bits(ya) | (_bf16_bits(yb) >> 16)
            for s in range(SUBLANES):
                y_ref[pl.ds(s, m, stride=SUBLANES), :] = packed[:, s * LANES:(s + 1) * LANES]
            if m < R:
                y_ref[pl.ds(m * SUBLANES, (R - m) * SUBLANES), :] = jnp.zeros(
                    ((R - m) * SUBLANES, LANES), y_ref.dtype)

    @pl.when(jnp.logical_not(used))
    def _():
        y_ref[...] = jnp.zeros_like(y_ref)


def _expert_call(block_ea, block_eb, block_rows, n_valid, xs, wg, wu, wd):
    assert BLOCK_ROWS[-1] == SLOT_BLOCK
    R = SLOT_BLOCK
    n_blocks = xs.shape[0] // (R * SUBLANES)
    D, F = wg.shape[1], wg.shape[2]
    G = EXPERTS_PER_GROUP
    last = lambda i, nv: jnp.minimum(i, nv[0] - 1)
    grp = lambda i, ea, eb, rows, nv: (ea[last(i, nv)] // G, 0, 0)
    blk = lambda i, ea, eb, rows, nv: (last(i, nv), 0)
    grid_spec = pltpu.PrefetchScalarGridSpec(
        num_scalar_prefetch=4,
        grid=(n_blocks,),
        in_specs=[
            pl.BlockSpec((R * SUBLANES, LANES), blk),
            pl.BlockSpec((G, D, F), grp, pipeline_mode=pl.Buffered(1)),
            pl.BlockSpec((G, D, F), grp, pipeline_mode=pl.Buffered(1)),
            pl.BlockSpec((G, F, D), grp, pipeline_mode=pl.Buffered(1)),
        ],
        out_specs=pl.BlockSpec((R * SUBLANES, LANES), lambda i, ea, eb, rows, nv: (i, 0)),
    )
    return pl.pallas_call(
        _expert_kernel,
        grid_spec=grid_spec,
        out_shape=jax.ShapeDtypeStruct(xs.shape, jnp.uint32),
        compiler_params=pltpu.CompilerParams(
            dimension_semantics=("arbitrary",), vmem_limit_bytes=VMEM_LIMIT),
        name="experts",
    )(block_ea, block_eb, block_rows, n_valid, xs, wg, wu, wd)


def _final_kernel(stage_ref, win_ref, yt_ref, x1_ref, g_ref, mod_ref, w_ref, b_ref, o_ref,
                  stage, tok, sems):
    i = pl.program_id(0)
    n_steps = pl.num_programs(0)
    TT = TOK_TILE
    wrows = WIN_ROWS * SUBLANES

    def window(tile, slot, w):
        first = pl.multiple_of(win_ref[2 * tile * WIN_LANES + w] * SUBLANES, SUBLANES)
        return pltpu.make_async_copy(yt_ref.at[pl.ds(first, wrows)],
                                     stage.at[slot].at[pl.ds(pl.multiple_of(w * wrows, wrows), wrows)],
                                     sems.at[slot])

    def n_windows(tile):
        return win_ref[2 * tile * WIN_LANES + WIN_LANES - 1]

    def start_tile(tile, slot):
        def issue(w, carry):
            window(tile, slot, w).start()
            return carry
        lax.fori_loop(0, n_windows(tile), issue, 0)

    @pl.when(i == 0)
    def _():
        start_tile(0, 0)

    for slot in range(2):
        @pl.when((i + 1 < n_steps) & ((i + 1) % 2 == slot))
        def _():
            start_tile(i + 1, slot)

    gate2 = mod_ref[0, 5:6, :]
    g = g_ref[...]
    for slot in range(2):
        @pl.when(i % 2 == slot)
        def _():
            def wait(w, carry):
                window(i, slot, w).wait()
                return carry
            lax.fori_loop(0, n_windows(i), wait, 0)

            def place(r8, carry):
                for u in range(8):
                    r = r8 * 8 + u
                    q = pl.multiple_of(stage_ref[i * TT + r] * SUBLANES, SUBLANES)
                    tok[pl.ds(pl.multiple_of(r * SUBLANES, SUBLANES), SUBLANES), :] = (
                        stage[slot, pl.ds(q, SUBLANES), :])
                return carry
            lax.fori_loop(0, TT // 8, place, 0)

            for r in range(TT // ROW_BLOCK):
                rows = slice(r * ROW_BLOCK, (r + 1) * ROW_BLOCK)
                packed = _from_token_tiles(tok, r * ROW_BLOCK, ROW_BLOCK)
                ya = lax.bitcast_convert_type(packed & jnp.uint32(0xFFFF0000), F32)
                yb = lax.bitcast_convert_type(packed << 16, F32)
                y = ya * g[rows, 0:1] + yb * g[rows, 1:2]
                u = DEEPNORM_ALPHA * x1_ref[rows, :] + gate2 * y
                o_ref[rows, :] = _ln(u) * w_ref[...] + b_ref[...]


def _final_call(staged_row, windows, yt, x1, gates, mod3, w, b, seq_len):
    N, D = x1.shape
    TT = TOK_TILE
    per_seq = seq_len // TT
    grid_spec = pltpu.PrefetchScalarGridSpec(
        num_scalar_prefetch=2,
        grid=(N // TT,),
        in_specs=[
            pl.BlockSpec(memory_space=pl.ANY),
            pl.BlockSpec((TT, D), lambda i, q, wn: (i, 0)),
            pl.BlockSpec((TT, 2), lambda i, q, wn: (i, 0)),
            pl.BlockSpec((1, 6, D), lambda i, q, wn: (i // per_seq, 0, 0)),
            pl.BlockSpec((1, D), lambda i, q, wn: (0, 0)),
            pl.BlockSpec((1, D), lambda i, q, wn: (0, 0)),
        ],
        out_specs=pl.BlockSpec((TT, D), lambda i, q, wn: (i, 0)),
        scratch_shapes=[pltpu.VMEM((2, MAX_WINDOWS * WIN_ROWS * SUBLANES, LANES), jnp.uint32),
                        pltpu.VMEM((TT * SUBLANES, LANES), jnp.uint32),
                        pltpu.SemaphoreType.DMA((2,))],
    )
    return pl.pallas_call(
        _final_kernel,
        grid_spec=grid_spec,
        out_shape=jax.ShapeDtypeStruct((N, D), F32),
        compiler_params=pltpu.CompilerParams(
            dimension_semantics=("arbitrary",), vmem_limit_bytes=VMEM_LIMIT),
        name="combine_ln",
    )(staged_row, windows, yt, x1, gates, mod3, w, b)


def _route_kernel(lgt_ref, rf_ref, ri_ref, meta_ref, win_ref, cnt_scr, base_scr, bstart_scr):
    ph = pl.program_id(0)
    t = pl.program_id(1)
    TR = ROUTE_TILE
    E = N_EXPERTS
    R = SLOT_BLOCK
    G = EXPERTS_PER_GROUP
    lg = lgt_ref[...]

    g = [lg[E + k:E + k + 1, :] for k in range(N_GROUPS)]
    gmax = jnp.maximum(jnp.maximum(g[0], g[1]), jnp.maximum(g[2], g[3]))
    gs = jnp.where(g[0] == gmax, 0.0, jnp.where(g[1] == gmax, 1.0, jnp.where(g[2] == gmax, 2.0, 3.0)))
    psum = (jnp.exp(g[0] - gmax) + jnp.exp(g[1] - gmax)) + (jnp.exp(g[2] - gmax) + jnp.exp(g[3] - gmax))
    p_star = 1.0 / psum
    esel = jnp.where(gs == 0.0, lg[0:G], jnp.where(gs == 1.0, lg[G:2 * G],
                                                   jnp.where(gs == 2.0, lg[2 * G:3 * G], lg[3 * G:4 * G])))
    sub = lax.broadcasted_iota(jnp.int32, (G, TR), 0).astype(F32)
    m1 = jnp.max(esel, axis=0, keepdims=True)
    i1 = jnp.min(jnp.where(esel == m1, sub, float(G)), axis=0, keepdims=True)
    es2 = jnp.where(sub == i1, -jnp.inf, esel)
    m2 = jnp.max(es2, axis=0, keepdims=True)
    i2 = jnp.min(jnp.where(es2 == m2, sub, float(G)), axis=0, keepdims=True)
    d = jnp.exp(m2 - m1)
    w1 = 1.0 / (1.0 + d)
    w2 = d * w1
    lo = jnp.minimum(i1, i2)
    hi = jnp.maximum(i1, i2)
    bucket = gs * PAIRS_PER_GROUP + (lo * (2 * G - 1 - lo) * 0.5 + (hi - lo - 1.0))
    first_is_lo = i1 < i2
    gate_a = p_star * jnp.where(first_is_lo, w1, w2)
    gate_b = p_star * jnp.where(first_is_lo, w2, w1)
    row = lax.broadcasted_iota(jnp.int32, (NB, TR), 0).astype(F32)
    oh = row == bucket
    cnt = oh.astype(F32)

    @pl.when(ph == 0)
    def _():
        @pl.when(t == 0)
        def _():
            cnt_scr[...] = jnp.zeros_like(cnt_scr)
        acc = cnt_scr[...]
        for j in range(TR // LANES):
            acc = acc + cnt[:, j * LANES:(j + 1) * LANES]
        cnt_scr[...] = acc

    @pl.when((ph == 1) & (t == 0))
    def _():
        counts = jnp.sum(cnt_scr[...], axis=1, keepdims=True)
        nblk = jnp.floor((counts + (R - 1)) * (1.0 / R))
        nblk_b = jnp.broadcast_to(nblk, (NB, LANES))
        rb = lax.broadcasted_iota(jnp.int32, (NB, NB), 0)
        cb = lax.broadcasted_iota(jnp.int32, (NB, NB), 1)
        bstart = _mm((cb < rb).astype(BF16), nblk_b.astype(BF16))
        bstart_scr[...] = bstart
        base_scr[...] = jnp.zeros_like(base_scr)
        k = lax.broadcasted_iota(jnp.int32, (NB, 1), 0).astype(F32)
        grp = sum((k >= float(m * PAIRS_PER_GROUP)).astype(F32) for m in range(1, N_GROUPS))
        p = k - grp * PAIRS_PER_GROUP
        pair_start = [i * (2 * G - 1 - i) // 2 for i in range(G - 1)]
        ia = sum((p >= float(s)).astype(F32) for s in pair_start[1:])
        ib = p - ia * (2 * G - 1 - ia) * 0.5 + ia + 1.0
        n_pad = meta_ref.shape[1]
        blk = lax.broadcasted_iota(jnp.int32, (NB, n_pad), 1).astype(F32)
        first = jnp.broadcast_to(bstart[:, 0:1], (NB, n_pad))
        last = jnp.broadcast_to((bstart + nblk_b)[:, 0:1], (NB, n_pad))
        member = (first <= blk) & (blk < last)
        ea = jnp.sum(jnp.where(member, grp * G + ia, 0.0), axis=0, keepdims=True)
        eb = jnp.sum(jnp.where(member, grp * G + ib, 0.0), axis=0, keepdims=True)
        meta_ref[...] = jnp.zeros_like(meta_ref)
        meta_ref[0:1, :] = ea.astype(jnp.int32)
        meta_ref[1:2, :] = eb.astype(jnp.int32)
        meta_ref[2:3, :] = jnp.max(last, axis=0, keepdims=True).astype(jnp.int32)
        in_blk = jnp.clip(counts - (blk - first) * R, 0.0, float(R))
        meta_ref[5:6, :] = jnp.sum(jnp.where(member, in_blk, 0.0), axis=0, keepdims=True).astype(jnp.int32)
        pad0 = jnp.broadcast_to(bstart[:, 0:1] * R + counts, (NB, NB)).T[0:1, :]
        padn = jnp.broadcast_to(nblk * R - counts, (NB, NB)).T[0:1, :]
        meta_ref[3:4, 0:NB] = pad0.astype(jnp.int32)
        meta_ref[4:5, 0:NB] = padn.astype(jnp.int32)

    @pl.when(ph == 1)
    def _():
        ri = lax.broadcasted_iota(jnp.int32, (LANES, 2 * LANES), 0)
        ci = lax.broadcasted_iota(jnp.int32, (LANES, 2 * LANES), 1)
        w = ((ci >= ri) | (ci >= LANES)).astype(BF16)
        base = base_scr[...]
        slot0 = bstart_scr[...] * R
        rb = lax.broadcasted_iota(jnp.int32, (NB, NB), 0)
        cb = lax.broadcasted_iota(jnp.int32, (NB, NB), 1)
        lower = (cb < rb).astype(BF16)
        wlane = lax.broadcasted_iota(jnp.int32, (NB, WIN_LANES), 1).astype(F32)
        groups_per_tile = TOK_TILE // LANES
        ds, qs = [], []
        for tile in range(TR // TOK_TILE):
            first_slot = slot0 + base
            vals = []
            for j in range(tile * groups_per_tile, (tile + 1) * groups_per_tile):
                sl = slice(j * LANES, (j + 1) * LANES)
                blk = cnt[:, sl]
                res = _mm(blk.astype(BF16), w)
                val = slot0 + base + (res[:, 0:LANES] - blk)
                vals.append(val)
                ds.append(jnp.sum(jnp.where(oh[:, sl], val, 0.0), axis=0, keepdims=True))
                base = base + res[:, LANES:2 * LANES]
            in_tile = slot0 + base - first_slot
            nwin = jnp.floor((in_tile + (WIN_ROWS - 1)) * (1.0 / WIN_ROWS))
            wbase = _mm(lower, nwin.astype(BF16))
            for jj, j in enumerate(range(tile * groups_per_tile, (tile + 1) * groups_per_tile)):
                sl = slice(j * LANES, (j + 1) * LANES)
                stage = wbase * WIN_ROWS + (vals[jj] - first_slot)
                qs.append(jnp.sum(jnp.where(oh[:, sl], stage, 0.0), axis=0, keepdims=True))
            wb = jnp.broadcast_to(wbase[:, 0:1], (NB, WIN_LANES))
            nw = jnp.broadcast_to(nwin[:, 0:1], (NB, WIN_LANES))
            fs = jnp.broadcast_to(first_slot[:, 0:1], (NB, WIN_LANES))
            mine = (wb <= wlane) & (wlane < wb + nw)
            starts = jnp.sum(jnp.where(mine, fs + (wlane - wb) * WIN_ROWS, 0.0), axis=0, keepdims=True)
            total = jnp.sum(nw, axis=0, keepdims=True)
            row = jnp.where(wlane[0:1, :] == float(WIN_LANES - 1), total, starts)
            win_ref[2 * tile:2 * tile + 1, :] = row.astype(jnp.int32)
            win_ref[2 * tile + 1:2 * tile + 2, :] = jnp.zeros((1, WIN_LANES), jnp.int32)
        base_scr[...] = base
        ri_ref[...] = jnp.zeros_like(ri_ref)
        ri_ref[0:1, :] = jnp.concatenate(ds, axis=1).astype(jnp.int32)
        ri_ref[1:2, :] = jnp.concatenate(qs, axis=1).astype(jnp.int32)
        rf_ref[...] = jnp.zeros_like(rf_ref)
        rf_ref[0:1, :] = gate_a
        rf_ref[1:2, :] = gate_b


def _route_call(lgt, n_blocks):
    rows, N = lgt.shape
    TR = ROUTE_TILE
    E = NB
    assert 2 * (TR // TOK_TILE) == SUBLANES and MAX_WINDOWS < WIN_LANES
    n_pad = -(-n_blocks // LANES) * LANES
    return pl.pallas_call(
        _route_kernel,
        grid=(2, N // TR),
        in_specs=[pl.BlockSpec((rows, TR), lambda ph, t: (0, t))],
        out_specs=[
            pl.BlockSpec((SUBLANES, TR), lambda ph, t: (0, ph * t)),
            pl.BlockSpec((SUBLANES, TR), lambda ph, t: (0, ph * t)),
            pl.BlockSpec((SUBLANES, n_pad), lambda ph, t: (0, 0)),
            pl.BlockSpec((SUBLANES, WIN_LANES), lambda ph, t: (ph * t, 0)),
        ],
        out_shape=[
            jax.ShapeDtypeStruct((SUBLANES, N), F32),
            jax.ShapeDtypeStruct((SUBLANES, N), jnp.int32),
            jax.ShapeDtypeStruct((SUBLANES, n_pad), jnp.int32),
            jax.ShapeDtypeStruct((2 * N // TOK_TILE, WIN_LANES), jnp.int32),
        ],
        scratch_shapes=[pltpu.VMEM((E, LANES), F32), pltpu.VMEM((E, LANES), F32), pltpu.VMEM((E, LANES), F32)],
        compiler_params=pltpu.CompilerParams(dimension_semantics=("arbitrary", "arbitrary")),
        name="route",
    )(lgt)


def kernel(x, c, positions, w_ada, b_ada, w_in, w_out, hgrn_lb, hgrn_norm_w, ret_norm_w, post_ln1_w,
           post_ln1_b, w_rg, b_rg, w_re, b_re, w_gate, w_up, w_down, post_ln2_w, post_ln2_b):
    B, S, D = x.shape
    N = B * S
    hw = N_HEADS * D_HEAD

    mod3 = _ada_call(c, w_ada[0], b_ada[0]).reshape(B, 6, D)

    win = w_in[0].astype(BF16)
    wout = w_out[0].astype(BF16)
    pad = LANES - N_EXPERTS - N_GROUPS
    wr = jnp.concatenate([w_re[0], w_rg[0], jnp.zeros((D, pad), F32)], axis=1).astype(BF16)
    br = jnp.concatenate([b_re[0], b_rg[0], jnp.zeros((pad,), F32)]).reshape(1, LANES)

    x1, h2t, lgt, wg16, wu16, wd16 = _mixer_call(
        x, positions, mod3, win, wout, hgrn_lb, hgrn_norm_w[0].reshape(1, hw),
        ret_norm_w[0].reshape(1, hw), post_ln1_w[0].reshape(1, D), post_ln1_b[0].reshape(1, D), wr, br,
        w_gate[0], w_up[0], w_down[0])

    n_blocks = N // SLOT_BLOCK + N_GROUPS * PAIRS_PER_GROUP
    rf, ri, meta, windows = _route_call(lgt, n_blocks)
    dest = ri[0]
    gates = rf[0:2].T
    xs = _dispatch_call(dest, meta[3, :NB], meta[4, :NB], meta[2, 0:1], h2t, n_blocks * SLOT_BLOCK)
    yt = _expert_call(meta[0, :n_blocks], meta[1, :n_blocks], meta[5, :n_blocks], meta[2, 0:1], xs,
                      wg16, wu16, wd16)
    out = _final_call(ri[1], windows.reshape(-1), yt, x1.reshape(N, D), gates, mod3,
                      post_ln2_w[0].reshape(1, D), post_ln2_b[0].reshape(1, D), S)
    return out.reshape(B, S, D)
```

```python
import functools

import numpy as np
import jax
import jax.numpy as jnp
from jax import lax
from jax.experimental import pallas as pl
from jax.experimental.pallas import tpu as pltpu

F32 = jnp.float32
BF16 = jnp.bfloat16

D_MODEL = 1024
N_HEADS = 4
D_HEAD = 128
HGRN_CHUNK = 64
RET_CHUNK = 256
ROPE_BASE = 10000.0
N_EXPERTS = 32
EXPERTS_PER_GROUP = 8
N_GROUPS = 4
DEEPNORM_ALPHA = 2.0 ** 0.25
LN_EPS = 1e-5

SEQ_TILE = 512
ROW_BLOCK = 128
PAIRS_PER_GROUP = EXPERTS_PER_GROUP * (EXPERTS_PER_GROUP - 1) // 2
NB = 128
SLOT_BLOCK = 256
BLOCK_ROWS = (64, 128, 192, 256)
DISPATCH_TILE = 512
DISPATCH_RING = 3
PAD_COPY_SIZES = (128, 64, 32, 16, 8, 4, 2, 1)
TOK_TILE = 512
ROUTE_TILE = 2048
N_LOGIT_ROWS = 40
LANES = 128
SUBLANES = 8
V7X_VMEM_BYTES = 64 * 1024 * 1024
VMEM_LIMIT = V7X_VMEM_BYTES - 8 * 1024 * 1024


def _mm(a, b):
    return jnp.dot(a, b, preferred_element_type=F32)


def _mm_nt(a, b):
    return lax.dot_general(a, b, (((1,), (1,)), ((), ())), preferred_element_type=F32)


def _mm_tn(a, b):
    return lax.dot_general(a, b, (((0,), (0,)), ((), ())), preferred_element_type=F32)


def _sigmoid(x):
    return 1.0 / (1.0 + jnp.exp(-x))


def _silu(x):
    return x * _sigmoid(x)


def _ln(x):
    mu = jnp.mean(x, axis=-1, keepdims=True)
    xc = x - mu
    var = jnp.mean(xc * xc, axis=-1, keepdims=True)
    return xc * lax.rsqrt(var + LN_EPS)


def _const_spec(shape):
    nd = len(shape)
    return pl.BlockSpec(shape, lambda *_: (0,) * nd, pipeline_mode=pl.Buffered(1))


def _ada_kernel(c_ref, w_ref, b_ref, o_ref):
    ca = _silu(c_ref[...])
    o_ref[...] = _mm(ca.astype(BF16), w_ref[...].astype(BF16)) + b_ref[...]


def _ada_call(c, w_ada, b_ada):
    B, D = c.shape
    n_out = w_ada.shape[1]
    return pl.pallas_call(
        _ada_kernel,
        grid=(n_out // D,),
        in_specs=[pl.BlockSpec((B, D), lambda j: (0, 0)),
                  pl.BlockSpec((D, D), lambda j: (0, j)),
                  pl.BlockSpec((1, D), lambda j: (0, j))],
        out_specs=pl.BlockSpec((B, D), lambda j: (0, j)),
        out_shape=jax.ShapeDtypeStruct((B, n_out), F32),
        name="ada_mod",
    )(c, w_ada, b_ada.reshape(1, n_out))


def _mixer_kernel(x_ref, pos_ref, mod_ref, win_ref, wout_ref, lb_ref, hnw_ref, rnw_ref,
                  ln1w_ref, ln1b_ref, invf_ref, dmat_ref, qdec_ref, kdec_ref, wr_ref, br_ref,
                  xp_ref, modp_ref, wg32_ref, wu32_ref, wd32_ref,
                  x1_ref, h2t_ref, lg_ref, wg16_ref, wu16_ref, wd16_ref,
                  h_scr, proj_scr, proj2_scr, o_scr, cos_scr, sin_scr, sh_scr, sr_scr, y_scr, h2_scr,
                  *, ret_chunk_decay, tiles_per_seq):
    T = SEQ_TILE
    D = D_MODEL
    dh = D_HEAD
    hw = N_HEADS * dh
    hw4 = 4 * hw
    step = pl.program_id(0)
    n_tiles = pl.num_programs(0) - 1

    wg16_ref[...] = wg32_ref[...].astype(BF16)
    wu16_ref[...] = wu32_ref[...].astype(BF16)
    wd16_ref[...] = wd32_ref[...].astype(BF16)

    @pl.when(step == 0)
    def _():
        y_scr[...] = jnp.zeros_like(y_scr)

    @pl.when((step < n_tiles) & (step % tiles_per_seq == 0))
    def _():
        sh_scr[...] = jnp.zeros_like(sh_scr)
        sr_scr[...] = jnp.zeros_like(sr_scr)

    gate1p = modp_ref[0, 2:3, :]
    shift2p = modp_ref[0, 3:4, :]
    scale2p = modp_ref[0, 4:5, :]

    def post_block(r):
        rows = slice(r * ROW_BLOCK, (r + 1) * ROW_BLOCK)
        u = DEEPNORM_ALPHA * xp_ref[0, rows, :] + gate1p * y_scr[rows, :]
        x1 = _ln(u) * ln1w_ref[...] + ln1b_ref[...]
        x1_ref[0, rows, :] = x1
        h2 = (_ln(x1) * (1.0 + scale2p) + shift2p).astype(BF16)
        h2_scr[rows, :] = h2
        h2f = h2.astype(F32)
        for s in range(SUBLANES):
            h2t_ref[pl.ds(r * ROW_BLOCK * SUBLANES + s, ROW_BLOCK, stride=SUBLANES), :] = (
                h2f[:, s * LANES:(s + 1) * LANES])

    shift1 = mod_ref[0, 0:1, :]
    scale1 = mod_ref[0, 1:2, :]

    for r in range(T // ROW_BLOCK):
        rows = slice(r * ROW_BLOCK, (r + 1) * ROW_BLOCK)
        h = _ln(x_ref[0, rows, :]) * (1.0 + scale1) + shift1
        h_scr[rows, :] = h.astype(BF16)

    n_post = T // ROW_BLOCK
    proj_scr[...] = _mm(h_scr[...], win_ref[:, 0:hw4])
    for r in range(n_post // 2):
        post_block(r)
    proj2_scr[...] = _mm(h_scr[...], win_ref[:, hw4:2 * hw4])
    for r in range(n_post // 2, n_post):
        post_block(r)
    lg_ref[...] = (_mm(h2_scr[...], wr_ref[...]) + br_ref[...]).T[0:N_LOGIT_ROWS, :]

    pos = pos_ref[0].astype(F32)
    ang_t = invf_ref[...] * pos
    cos_t = jnp.cos(ang_t)
    sin_t = jnp.sin(ang_t)
    cos_scr[...] = jnp.concatenate([cos_t, cos_t], axis=0).T
    sin_scr[...] = jnp.concatenate([-sin_t, sin_t], axis=0).T

    l0 = lb_ref[0:1, :]
    l1 = lb_ref[1:2, :]
    lmax = jnp.maximum(l0, l1)
    e0 = jnp.exp(l0 - lmax)
    e1 = jnp.exp(l1 - lmax)
    lb = e0 / (e0 + e1)
    hnw = hnw_ref[...]

    C = HGRN_CHUNK
    rr = lax.broadcasted_iota(jnp.int32, (C, C), 0)
    cc = lax.broadcasted_iota(jnp.int32, (C, C), 1)
    causal = rr >= cc
    row_id = lax.broadcasted_iota(jnp.int32, (C, hw), 0)

    def hgrn_chunk(c):
        rows = slice(c * C, (c + 1) * C)
        logfs, ks = [], []
        for hd in range(N_HEADS):
            z = proj_scr[rows, hw + hd * dh: hw + (hd + 1) * dh]
            e = jnp.exp(-jnp.abs(z))
            r = 1.0 / (1.0 + e)
            er = e * r
            zp = z >= 0
            sig_pos = jnp.where(zp, r, er)
            sig_neg = jnp.where(zp, er, r)
            lbh = lb[:, hd * dh:(hd + 1) * dh]
            logfs.append(jnp.log(lbh + (1.0 - lbh) * sig_pos))
            ks.append((1.0 - lbh) * sig_neg)
        bsum = jnp.concatenate(logfs, axis=1)
        for k in (1, 2, 4):
            bsum = bsum + jnp.where(row_id >= k, pltpu.roll(bsum, k, 0), 0.0)
        for k in (8, 16, 32):
            bsum = bsum + jnp.concatenate([jnp.zeros((k, hw), F32), bsum[0:C - k, :]], axis=0)
        for hd in range(N_HEADS):
            cols = slice(hd * dh, (hd + 1) * dh)
            b = bsum[:, cols]
            b_last = b[C - 1:C, :]
            q = proj_scr[rows, cols]
            v = proj_scr[rows, 2 * hw + hd * dh: 2 * hw + (hd + 1) * dh].astype(BF16)
            g = proj_scr[rows, 3 * hw + hd * dh: 3 * hw + (hd + 1) * dh]
            k = ks[hd]
            q_in = (_silu(q) * jnp.exp(b)).astype(BF16)
            k_in = (k * jnp.exp(-b)).astype(BF16)
            k_dec = (k * jnp.exp(b_last - b)).astype(BF16)
            st = sh_scr[hd]
            a = jnp.where(causal, _mm_nt(q_in, k_in), 0.0)
            o = _mm(a.astype(BF16), v) + _mm_nt(q_in, st.astype(BF16))
            sh_scr[hd] = st * jnp.exp(b_last) + _mm_tn(v, k_dec)
            ms = jnp.mean(o * o, axis=-1, keepdims=True)
            on = o * lax.rsqrt(ms + LN_EPS) * hnw[:, hd * dh:(hd + 1) * dh] * _silu(g)
            o_scr[rows, hd * dh:(hd + 1) * dh] = on.astype(BF16)

    rnw = rnw_ref[...]
    RC = RET_CHUNK
    q_scale = dh ** -0.5

    def retention_unit(sub, hd):
        rows = slice(sub * RC, (sub + 1) * RC)
        cs = cos_scr[rows, :]
        sn = sin_scr[rows, :]
        rq = proj2_scr[rows, hd * dh: (hd + 1) * dh]
        rk = proj2_scr[rows, hw + hd * dh: hw + (hd + 1) * dh]
        v = proj2_scr[rows, 2 * hw + hd * dh: 2 * hw + (hd + 1) * dh].astype(BF16)
        g = proj2_scr[rows, 3 * hw + hd * dh: 3 * hw + (hd + 1) * dh]
        q = (rq * cs + pltpu.roll(rq, dh // 2, 1) * sn) * q_scale
        k = rk * cs + pltpu.roll(rk, dh // 2, 1) * sn
        qb = q.astype(BF16)
        st = sr_scr[hd]
        s = _mm_nt(qb, k.astype(BF16)) * dmat_ref[hd]
        o = _mm(s.astype(BF16), v) + _mm_nt(qb, st.astype(BF16)) * qdec_ref[hd]
        sr_scr[hd] = st * ret_chunk_decay[hd] + _mm_tn(v, (k * kdec_ref[hd]).astype(BF16))
        mu = jnp.mean(o, axis=-1, keepdims=True)
        oc = o - mu
        var = jnp.mean(oc * oc, axis=-1, keepdims=True)
        on = oc * lax.rsqrt(var + LN_EPS) * rnw[:, hd * dh:(hd + 1) * dh] * _silu(g)
        o_scr[rows, hw + hd * dh: hw + (hd + 1) * dh] = on.astype(BF16)

    n_chunks = T // C
    units = [(sub, hd) for sub in range(T // RC) for hd in range(N_HEADS)]
    for c in range(n_chunks):
        hgrn_chunk(c)
        for sub, hd in units[c * len(units) // n_chunks:(c + 1) * len(units) // n_chunks]:
            retention_unit(sub, hd)

    y_scr[...] = _mm(o_scr[...], wout_ref[...])


def _retention_tables():
    h = np.arange(N_HEADS, dtype=np.float64)
    log_gamma = np.log(1.0 - np.exp2(-5.0 - h))
    idx = np.arange(RET_CHUNK, dtype=np.float64)
    rel = idx[:, None] - idx[None, :]
    dmat = np.where(rel >= 0, np.exp(np.maximum(rel, 0.0)[None] * log_gamma[:, None, None]), 0.0)
    qdec = np.exp((idx + 1.0)[None, :] * log_gamma[:, None])
    kdec = np.exp((RET_CHUNK - 1.0 - idx)[None, :] * log_gamma[:, None])
    cdec = np.exp(RET_CHUNK * log_gamma)
    bc = lambda t: np.broadcast_to(t[:, :, None], (N_HEADS, RET_CHUNK, D_HEAD))
    return (jnp.asarray(dmat, F32), jnp.asarray(bc(qdec), F32), jnp.asarray(bc(kdec), F32),
            tuple(float(np.float32(v)) for v in cdec))


def _mixer_call(x, positions, mod3, win, wout, hgrn_lb, hnw, rnw, ln1w, ln1b, wr, br, wg, wu, wd):
    B, S, D = x.shape
    T = SEQ_TILE
    nj = S // T
    dmat, qdec, kdec, cdec = _retention_tables()
    inv_freq = np.power(ROPE_BASE, -np.arange(0, D_HEAD, 2, dtype=np.float64) / D_HEAD)
    invf = jnp.asarray(inv_freq[:, None], F32)
    pos3 = positions.reshape(B, 1, S)
    kern = functools.partial(_mixer_kernel, ret_chunk_decay=cdec, tiles_per_seq=nj)
    n_tiles = B * nj
    cur = lambda s: jnp.minimum(s, n_tiles - 1)
    prev = lambda s: jnp.maximum(s - 1, 0)
    E, _, F = wg.shape
    wg2, wu2, wd2 = wg.reshape(E * D, F), wu.reshape(E * D, F), wd.reshape(E * F, D)
    rg, rd = (E * D) // n_tiles, (E * F) // n_tiles
    assert rg * n_tiles == E * D and rd * n_tiles == E * F and rd % 16 == 0
    wspec_g = pl.BlockSpec((rg, F), lambda s: (cur(s), 0))
    wspec_d = pl.BlockSpec((rd, D), lambda s: (cur(s), 0))
    outs = pl.pallas_call(
        kern,
        grid=(n_tiles + 1,),
        in_specs=[
            pl.BlockSpec((1, T, D), lambda s: (cur(s) // nj, cur(s) % nj, 0)),
            pl.BlockSpec((1, 1, T), lambda s: (cur(s) // nj, 0, cur(s) % nj)),
            pl.BlockSpec((1, 6, D), lambda s: (cur(s) // nj, 0, 0)),
            _const_spec(win.shape), _const_spec(wout.shape), _const_spec(hgrn_lb.shape),
            _const_spec(hnw.shape), _const_spec(rnw.shape), _const_spec(ln1w.shape),
            _const_spec(ln1b.shape), _const_spec(invf.shape), _const_spec(dmat.shape),
            _const_spec(qdec.shape), _const_spec(kdec.shape), _const_spec(wr.shape),
            _const_spec(br.shape),
            pl.BlockSpec((1, T, D), lambda s: (prev(s) // nj, prev(s) % nj, 0)),
            pl.BlockSpec((1, 6, D), lambda s: (prev(s) // nj, 0, 0)),
            wspec_g, wspec_g, wspec_d,
        ],
        out_specs=[
            pl.BlockSpec((1, T, D), lambda s: (prev(s) // nj, prev(s) % nj, 0)),
            pl.BlockSpec((T * SUBLANES, LANES), lambda s: (prev(s), 0)),
            pl.BlockSpec((N_LOGIT_ROWS, T), lambda s: (0, prev(s))),
            wspec_g, wspec_g, wspec_d,
        ],
        out_shape=[
            jax.ShapeDtypeStruct((B, S, D), F32),
            jax.ShapeDtypeStruct((B * S * SUBLANES, LANES), F32),
            jax.ShapeDtypeStruct((N_LOGIT_ROWS, B * S), F32),
            jax.ShapeDtypeStruct(wg2.shape, BF16),
            jax.ShapeDtypeStruct(wu2.shape, BF16),
            jax.ShapeDtypeStruct(wd2.shape, BF16),
        ],
        scratch_shapes=[
            pltpu.VMEM((T, D), BF16),
            pltpu.VMEM((T, 4 * N_HEADS * D_HEAD), F32),
            pltpu.VMEM((T, 4 * N_HEADS * D_HEAD), F32),
            pltpu.VMEM((T, D), BF16),
            pltpu.VMEM((T, D_HEAD), F32),
            pltpu.VMEM((T, D_HEAD), F32),
            pltpu.VMEM((N_HEADS, D_HEAD, D_HEAD), F32),
            pltpu.VMEM((N_HEADS, D_HEAD, D_HEAD), F32),
            pltpu.VMEM((T, D), F32),
            pltpu.VMEM((T, D), BF16),
        ],
        compiler_params=pltpu.CompilerParams(
            dimension_semantics=("arbitrary",), vmem_limit_bytes=VMEM_LIMIT),
        name="mixer",
    )(x, pos3, mod3, win, wout, hgrn_lb, hnw, rnw, ln1w, ln1b, invf, dmat, qdec, kdec, wr, br, x, mod3,
      wg2, wu2, wd2)
    x1, h2t, lgt, wg16, wu16, wd16 = outs
    return x1, h2t, lgt, wg16.reshape(E, D, F), wu16.reshape(E, D, F), wd16.reshape(E, F, D)


def _row_copy(src_ref, dst_ref, src_row, dst_row, sem):
    s0 = pl.multiple_of(src_row * SUBLANES, SUBLANES)
    d0 = pl.multiple_of(dst_row * SUBLANES, SUBLANES)
    return pltpu.make_async_copy(src_ref.at[pl.ds(s0, SUBLANES)], dst_ref.at[pl.ds(d0, SUBLANES)], sem)


def _from_token_tiles(buf_ref, first_row, n_rows):
    return jnp.concatenate(
        [buf_ref[pl.ds(first_row * SUBLANES + s, n_rows, stride=SUBLANES), :] for s in range(SUBLANES)],
        axis=1)


def _start_rows(src_ref, dst_ref, idx_ref, idx_base, dst_base, n_rows, sem):
    def issue(r2, carry):
        for u in range(2):
            r = r2 * 2 + u
            _row_copy(src_ref, dst_ref, idx_ref[idx_base + r], dst_base + r, sem).start(priority=u)
        return carry
    lax.fori_loop(0, n_rows // 2, issue, 0)


def _wait_rows(src_ref, dst_ref, n_rows, sem):
    pltpu.make_async_copy(src_ref.at[pl.ds(0, n_rows * SUBLANES)], dst_ref, sem).wait()


def _bf16_bits(x):
    return lax.bitcast_convert_type(x.astype(BF16).astype(F32), jnp.uint32)


def _dispatch_kernel(dest_ref, pad0_ref, padn_ref, nv_ref, h2t_ref, xs_ref,
                     buf, zbuf, load_sems, scat_sems, zsem):
    i = pl.program_id(0)
    n = pl.num_programs(0)
    TT = DISPATCH_TILE
    rows = TT * SUBLANES

    def pad_copies(wait):
        def bucket(b, carry):
            pos = pad0_ref[b]
            left = padn_ref[b]
            for size in PAD_COPY_SIZES:
                hit = (left & size) != 0
                first = 0 if wait else pl.multiple_of(pos * SUBLANES, SUBLANES)
                copy = pltpu.make_async_copy(
                    zbuf.at[pl.ds(0, size * SUBLANES)], xs_ref.at[pl.ds(first, size * SUBLANES)], zsem)

                @pl.when(hit)
                def _():
                    copy.wait() if wait else copy.start()
                pos = pos + jnp.where(hit, size, 0)
            return carry
        lax.fori_loop(0, pad0_ref.shape[0], bucket, 0)

        zrows = PAD_COPY_SIZES[0] * SUBLANES

        def unused_block(blk, carry):
            for part in range(SLOT_BLOCK // PAD_COPY_SIZES[0]):
                first = 0 if wait else pl.multiple_of(blk * SLOT_BLOCK * SUBLANES + part * zrows, zrows)
                copy = pltpu.make_async_copy(zbuf, xs_ref.at[pl.ds(first, zrows)], zsem)
                copy.wait() if wait else copy.start()
            return carry
        lax.fori_loop(nv_ref[0], xs_ref.shape[0] // (SLOT_BLOCK * SUBLANES), unused_block, 0)

    @pl.when(i == 0)
    def _():
        zbuf[...] = jnp.zeros_like(zbuf)
        pad_copies(wait=False)

    def load(tile, slot):
        src = h2t_ref.at[pl.ds(pl.multiple_of(tile * rows, rows), rows)]
        return pltpu.make_async_copy(src, buf.at[slot], load_sems.at[slot])

    def wait_scatter(slot):
        _wait_rows(buf.at[slot], xs_ref.at[pl.ds(0, rows)], TT, scat_sems.at[slot])

    @pl.when(i == 0)
    def _():
        load(0, 0).start()

    for slot in range(DISPATCH_RING):
        @pl.when((i + 1 < n) & ((i + 1) % DISPATCH_RING == slot))
        def _():
            @pl.when(i + 1 >= DISPATCH_RING)
            def _():
                wait_scatter(slot)
            load(i + 1, slot).start()

    for slot in range(DISPATCH_RING):
        @pl.when(i % DISPATCH_RING == slot)
        def _():
            load(i, slot).wait()

            def issue(r2, carry):
                for u in range(4):
                    r = r2 * 4 + u
                    _row_copy(buf.at[slot], xs_ref, r, dest_ref[i * TT + r],
                              scat_sems.at[slot]).start(priority=u % 2)
                return carry
            lax.fori_loop(0, TT // 4, issue, 0)

    @pl.when(i == n - 1)
    def _():
        for slot in range(DISPATCH_RING):
            wait_scatter(slot)
        pad_copies(wait=True)


def _dispatch_call(dest, pad_start, pad_len, n_valid, h2t, n_slots):
    N = dest.shape[0]
    TT = DISPATCH_TILE
    assert N // TT >= DISPATCH_RING and SLOT_BLOCK <= 2 * PAD_COPY_SIZES[0]
    assert SLOT_BLOCK % PAD_COPY_SIZES[0] == 0
    grid_spec = pltpu.PrefetchScalarGridSpec(
        num_scalar_prefetch=4,
        grid=(N // TT,),
        in_specs=[pl.BlockSpec(memory_space=pl.ANY)],
        out_specs=pl.BlockSpec(memory_space=pl.ANY),
        scratch_shapes=[pltpu.VMEM((DISPATCH_RING, TT * SUBLANES, LANES), F32),
                        pltpu.VMEM((PAD_COPY_SIZES[0] * SUBLANES, LANES), F32),
                        pltpu.SemaphoreType.DMA((DISPATCH_RING,)),
                        pltpu.SemaphoreType.DMA((DISPATCH_RING,)),
                        pltpu.SemaphoreType.DMA],
    )
    return pl.pallas_call(
        _dispatch_kernel,
        grid_spec=grid_spec,
        out_shape=jax.ShapeDtypeStruct((n_slots * SUBLANES, LANES), F32),
        compiler_params=pltpu.CompilerParams(dimension_semantics=("arbitrary",)),
        name="dispatch",
    )(dest, pad_start, pad_len, n_valid, h2t)


def _expert_kernel(ea_ref, eb_ref, rows_ref, nv_ref, xs_ref, wg_ref, wu_ref, wd_ref, y_ref):
    i = pl.program_id(0)
    R = SLOT_BLOCK
    G = EXPERTS_PER_GROUP
    used = i < nv_ref[0]
    n_rows = rows_ref[i]

    def ffn(xb, e):
        a = _mm(xb, wg_ref[e])
        u = _mm(xb, wu_ref[e])
        return _mm((_silu(a) * u).astype(BF16), wd_ref[e])

    for k, m in enumerate(BLOCK_ROWS):
        fewer = BLOCK_ROWS[k - 1] if k else 0

        @pl.when(used & (n_rows > fewer) & (n_rows <= m))
        def _():
            xb = _from_token_tiles(xs_ref, 0, m).astype(BF16)
            ya = ffn(xb, ea_ref[i] & (G - 1))
            yb = ffn(xb, eb_ref[i] & (G - 1))
            packed = _bf16_bits(ya) | (_bf16_bits(yb) >> 16)
            for s in range(SUBLANES):
                y_ref[pl.ds(s, m, stride=SUBLANES), :] = packed[:, s * LANES:(s + 1) * LANES]
            if m < R:
                y_ref[pl.ds(m * SUBLANES, (R - m) * SUBLANES), :] = jnp.zeros(
                    ((R - m) * SUBLANES, LANES), y_ref.dtype)

    @pl.when(jnp.logical_not(used))
    def _():
        y_ref[...] = jnp.zeros_like(y_ref)


def _expert_call(block_ea, block_eb, block_rows, n_valid, xs, wg, wu, wd):
    assert BLOCK_ROWS[-1] == SLOT_BLOCK
    R = SLOT_BLOCK
    n_blocks = xs.shape[0] // (R * SUBLANES)
    D, F = wg.shape[1], wg.shape[2]
    G = EXPERTS_PER_GROUP
    last = lambda i, nv: jnp.minimum(i, nv[0] - 1)
    grp = lambda i, ea, eb, rows, nv: (ea[last(i, nv)] // G, 0, 0)
    blk = lambda i, ea, eb, rows, nv: (last(i, nv), 0)
    grid_spec = pltpu.PrefetchScalarGridSpec(
        num_scalar_prefetch=4,
        grid=(n_blocks,),
        in_specs=[
            pl.BlockSpec((R * SUBLANES, LANES), blk),
            pl.BlockSpec((G, D, F), grp, pipeline_mode=pl.Buffered(1)),
            pl.BlockSpec((G, D, F), grp, pipeline_mode=pl.Buffered(1)),
            pl.BlockSpec((G, F, D), grp, pipeline_mode=pl.Buffered(1)),
        ],
        out_specs=pl.BlockSpec((R * SUBLANES, LANES), lambda i, ea, eb, rows, nv: (i, 0)),
    )
    return pl.pallas_call(
        _expert_kernel,
        grid_spec=grid_spec,
        out_shape=jax.ShapeDtypeStruct(xs.shape, jnp.uint32),
        compiler_params=pltpu.CompilerParams(
            dimension_semantics=("arbitrary",), vmem_limit_bytes=VMEM_LIMIT),
        name="experts",
    )(block_ea, block_eb, block_rows, n_valid, xs, wg, wu, wd)


def _final_kernel(dest_ref, yt_ref, x1_ref, g_ref, mod_ref, w_ref, b_ref, o_ref, buf, sems):
    i = pl.program_id(0)
    n_steps = pl.num_programs(0)
    TT = TOK_TILE

    def start_tile(tile, slot):
        _start_rows(yt_ref, buf.at[slot], dest_ref, tile * TT, 0, TT, sems.at[slot])

    @pl.when(i == 0)
    def _():
        start_tile(0, 0)

    for slot in range(2):
        @pl.when((i + 1 < n_steps) & ((i + 1) % 2 == slot))
        def _():
            start_tile(i + 1, slot)

    gate2 = mod_ref[0, 5:6, :]
    g = g_ref[...]
    for slot in range(2):
        @pl.when(i % 2 == slot)
        def _():
            _wait_rows(yt_ref, buf.at[slot], TT, sems.at[slot])
            for r in range(TT // ROW_BLOCK):
                rows = slice(r * ROW_BLOCK, (r + 1) * ROW_BLOCK)
                packed = _from_token_tiles(buf.at[slot], r * ROW_BLOCK, ROW_BLOCK)
                ya = lax.bitcast_convert_type(packed & jnp.uint32(0xFFFF0000), F32)
                yb = lax.bitcast_convert_type(packed << 16, F32)
                y = ya * g[rows, 0:1] + yb * g[rows, 1:2]
                u = DEEPNORM_ALPHA * x1_ref[rows, :] + gate2 * y
                o_ref[rows, :] = _ln(u) * w_ref[...] + b_ref[...]


def _final_call(dest, yt, x1, gates, mod3, w, b, seq_len):
    N, D = x1.shape
    TT = TOK_TILE
    per_seq = seq_len // TT
    grid_spec = pltpu.PrefetchScalarGridSpec(
        num_scalar_prefetch=1,
        grid=(N // TT,),
        in_specs=[
            pl.BlockSpec(memory_space=pl.ANY),
            pl.BlockSpec((TT, D), lambda i, d: (i, 0)),
            pl.BlockSpec((TT, 2), lambda i, d: (i, 0)),
            pl.BlockSpec((1, 6, D), lambda i, d: (i // per_seq, 0, 0)),
            pl.BlockSpec((1, D), lambda i, d: (0, 0)),
            pl.BlockSpec((1, D), lambda i, d: (0, 0)),
        ],
        out_specs=pl.BlockSpec((TT, D), lambda i, d: (i, 0)),
        scratch_shapes=[pltpu.VMEM((2, TT * SUBLANES, LANES), jnp.uint32), pltpu.SemaphoreType.DMA((2,))],
    )
    return pl.pallas_call(
        _final_kernel,
        grid_spec=grid_spec,
        out_shape=jax.ShapeDtypeStruct((N, D), F32),
        compiler_params=pltpu.CompilerParams(
            dimension_semantics=("arbitrary",), vmem_limit_bytes=VMEM_LIMIT),
        name="combine_ln",
    )(dest, yt, x1, gates, mod3, w, b)


def _route_kernel(lgt_ref, rf_ref, ri_ref, meta_ref, cnt_scr, base_scr, bstart_scr):
    ph = pl.program_id(0)
    t = pl.program_id(1)
    TR = ROUTE_TILE
    E = N_EXPERTS
    R = SLOT_BLOCK
    G = EXPERTS_PER_GROUP
    lg = lgt_ref[...]

    g = [lg[E + k:E + k + 1, :] for k in range(N_GROUPS)]
    gmax = jnp.maximum(jnp.maximum(g[0], g[1]), jnp.maximum(g[2], g[3]))
    gs = jnp.where(g[0] == gmax, 0.0, jnp.where(g[1] == gmax, 1.0, jnp.where(g[2] == gmax, 2.0, 3.0)))
    psum = (jnp.exp(g[0] - gmax) + jnp.exp(g[1] - gmax)) + (jnp.exp(g[2] - gmax) + jnp.exp(g[3] - gmax))
    p_star = 1.0 / psum
    esel = jnp.where(gs == 0.0, lg[0:G], jnp.where(gs == 1.0, lg[G:2 * G],
                                                   jnp.where(gs == 2.0, lg[2 * G:3 * G], lg[3 * G:4 * G])))
    sub = lax.broadcasted_iota(jnp.int32, (G, TR), 0).astype(F32)
    m1 = jnp.max(esel, axis=0, keepdims=True)
    i1 = jnp.min(jnp.where(esel == m1, sub, float(G)), axis=0, keepdims=True)
    es2 = jnp.where(sub == i1, -jnp.inf, esel)
    m2 = jnp.max(es2, axis=0, keepdims=True)
    i2 = jnp.min(jnp.where(es2 == m2, sub, float(G)), axis=0, keepdims=True)
    d = jnp.exp(m2 - m1)
    w1 = 1.0 / (1.0 + d)
    w2 = d * w1
    lo = jnp.minimum(i1, i2)
    hi = jnp.maximum(i1, i2)
    bucket = gs * PAIRS_PER_GROUP + (lo * (2 * G - 1 - lo) * 0.5 + (hi - lo - 1.0))
    first_is_lo = i1 < i2
    gate_a = p_star * jnp.where(first_is_lo, w1, w2)
    gate_b = p_star * jnp.where(first_is_lo, w2, w1)
    row = lax.broadcasted_iota(jnp.int32, (NB, TR), 0).astype(F32)
    oh = row == bucket
    cnt = oh.astype(F32)

    @pl.when(ph == 0)
    def _():
        @pl.when(t == 0)
        def _():
            cnt_scr[...] = jnp.zeros_like(cnt_scr)
        acc = cnt_scr[...]
        for j in range(TR // LANES):
            acc = acc + cnt[:, j * LANES:(j + 1) * LANES]
        cnt_scr[...] = acc

    @pl.when((ph == 1) & (t == 0))
    def _():
        counts = jnp.sum(cnt_scr[...], axis=1, keepdims=True)
        nblk = jnp.floor((counts + (R - 1)) * (1.0 / R))
        nblk_b = jnp.broadcast_to(nblk, (NB, LANES))
        rb = lax.broadcasted_iota(jnp.int32, (NB, NB), 0)
        cb = lax.broadcasted_iota(jnp.int32, (NB, NB), 1)
        bstart = _mm((cb < rb).astype(BF16), nblk_b.astype(BF16))
        bstart_scr[...] = bstart
        base_scr[...] = jnp.zeros_like(base_scr)
        k = lax.broadcasted_iota(jnp.int32, (NB, 1), 0).astype(F32)
        grp = sum((k >= float(m * PAIRS_PER_GROUP)).astype(F32) for m in range(1, N_GROUPS))
        p = k - grp * PAIRS_PER_GROUP
        pair_start = [i * (2 * G - 1 - i) // 2 for i in range(G - 1)]
        ia = sum((p >= float(s)).astype(F32) for s in pair_start[1:])
        ib = p - ia * (2 * G - 1 - ia) * 0.5 + ia + 1.0
        n_pad = meta_ref.shape[1]
        blk = lax.broadcasted_iota(jnp.int32, (NB, n_pad), 1).astype(F32)
        first = jnp.broadcast_to(bstart[:, 0:1], (NB, n_pad))
        last = jnp.broadcast_to((bstart + nblk_b)[:, 0:1], (NB, n_pad))
        member = (first <= blk) & (blk < last)
        ea = jnp.sum(jnp.where(member, grp * G + ia, 0.0), axis=0, keepdims=True)
        eb = jnp.sum(jnp.where(member, grp * G + ib, 0.0), axis=0, keepdims=True)
        meta_ref[...] = jnp.zeros_like(meta_ref)
        meta_ref[0:1, :] = ea.astype(jnp.int32)
        meta_ref[1:2, :] = eb.astype(jnp.int32)
        meta_ref[2:3, :] = jnp.max(last, axis=0, keepdims=True).astype(jnp.int32)
        in_blk = jnp.clip(counts - (blk - first) * R, 0.0, float(R))
        meta_ref[5:6, :] = jnp.sum(jnp.where(member, in_blk, 0.0), axis=0, keepdims=True).astype(jnp.int32)
        pad0 = jnp.broadcast_to(bstart[:, 0:1] * R + counts, (NB, NB)).T[0:1, :]
        padn = jnp.broadcast_to(nblk * R - counts, (NB, NB)).T[0:1, :]
        meta_ref[3:4, 0:NB] = pad0.astype(jnp.int32)
        meta_ref[4:5, 0:NB] = padn.astype(jnp.int32)

    @pl.when(ph == 1)
    def _():
        ri = lax.broadcasted_iota(jnp.int32, (LANES, 2 * LANES), 0)
        ci = lax.broadcasted_iota(jnp.int32, (LANES, 2 * LANES), 1)
        w = ((ci >= ri) | (ci >= LANES)).astype(BF16)
        base = base_scr[...]
        slot0 = bstart_scr[...] * R
        ds = []
        for j in range(TR // LANES):
            sl = slice(j * LANES, (j + 1) * LANES)
            blk = cnt[:, sl]
            res = _mm(blk.astype(BF16), w)
            val = slot0 + base + (res[:, 0:LANES] - blk)
            ds.append(jnp.sum(jnp.where(oh[:, sl], val, 0.0), axis=0, keepdims=True))
            base = base + res[:, LANES:2 * LANES]
        base_scr[...] = base
        ri_ref[...] = jnp.zeros_like(ri_ref)
        ri_ref[0:1, :] = jnp.concatenate(ds, axis=1).astype(jnp.int32)
        rf_ref[...] = jnp.zeros_like(rf_ref)
        rf_ref[0:1, :] = gate_a
        rf_ref[1:2, :] = gate_b


def _route_call(lgt, n_blocks):
    rows, N = lgt.shape
    TR = ROUTE_TILE
    E = NB
    n_pad = -(-n_blocks // LANES) * LANES
    return pl.pallas_call(
        _route_kernel,
        grid=(2, N // TR),
        in_specs=[pl.BlockSpec((rows, TR), lambda ph, t: (0, t))],
        out_specs=[
            pl.BlockSpec((SUBLANES, TR), lambda ph, t: (0, ph * t)),
            pl.BlockSpec((SUBLANES, TR), lambda ph, t: (0, ph * t)),
            pl.BlockSpec((SUBLANES, n_pad), lambda ph, t: (0, 0)),
        ],
        out_shape=[
            jax.ShapeDtypeStruct((SUBLANES, N), F32),
            jax.ShapeDtypeStruct((SUBLANES, N), jnp.int32),
            jax.ShapeDtypeStruct((SUBLANES, n_pad), jnp.int32),
        ],
        scratch_shapes=[pltpu.VMEM((E, LANES), F32), pltpu.VMEM((E, LANES), F32), pltpu.VMEM((E, LANES), F32)],
        compiler_params=pltpu.CompilerParams(dimension_semantics=("arbitrary", "arbitrary")),
        name="route",
    )(lgt)


def kernel(x, c, positions, w_ada, b_ada, w_in, w_out, hgrn_lb, hgrn_norm_w, ret_norm_w, post_ln1_w,
           post_ln1_b, w_rg, b_rg, w_re, b_re, w_gate, w_up, w_down, post_ln2_w, post_ln2_b):
    B, S, D = x.shape
    N = B * S
    hw = N_HEADS * D_HEAD

    mod3 = _ada_call(c, w_ada[0], b_ada[0]).reshape(B, 6, D)

    win = w_in[0].astype(BF16)
    wout = w_out[0].astype(BF16)
    pad = LANES - N_EXPERTS - N_GROUPS
    wr = jnp.concatenate([w_re[0], w_rg[0], jnp.zeros((D, pad), F32)], axis=1).astype(BF16)
    br = jnp.concatenate([b_re[0], b_rg[0], jnp.zeros((pad,), F32)]).reshape(1, LANES)

    x1, h2t, lgt, wg16, wu16, wd16 = _mixer_call(
        x, positions, mod3, win, wout, hgrn_lb, hgrn_norm_w[0].reshape(1, hw),
        ret_norm_w[0].reshape(1, hw), post_ln1_w[0].reshape(1, D), post_ln1_b[0].reshape(1, D), wr, br,
        w_gate[0], w_up[0], w_down[0])

    n_blocks = N // SLOT_BLOCK + N_GROUPS * PAIRS_PER_GROUP
    rf, ri, meta = _route_call(lgt, n_blocks)
    dest = ri[0]
    gates = rf[0:2].T
    xs = _dispatch_call(dest, meta[3, :NB], meta[4, :NB], meta[2, 0:1], h2t, n_blocks * SLOT_BLOCK)
    yt = _expert_call(meta[0, :n_blocks], meta[1, :n_blocks], meta[5, :n_blocks], meta[2, 0:1], xs,
                      wg16, wu16, wd16)
    out = _final_call(dest, yt, x1.reshape(N, D), gates, mod3, post_ln2_w[0].reshape(1, D),
                      post_ln2_b[0].reshape(1, D), S)
    return out.reshape(B, S, D)
```

```python
import functools

import numpy as np
import jax
import jax.numpy as jnp
from jax import lax
from jax.experimental import pallas as pl
from jax.experimental.pallas import tpu as pltpu

F32 = jnp.float32
BF16 = jnp.bfloat16

D_MODEL = 1024
N_HEADS = 4
D_HEAD = 128
HGRN_CHUNK = 64
RET_CHUNK = 256
ROPE_BASE = 10000.0
N_EXPERTS = 32
EXPERTS_PER_GROUP = 8
N_GROUPS = 4
DEEPNORM_ALPHA = 2.0 ** 0.25
LN_EPS = 1e-5

SEQ_TILE = 512
ROW_BLOCK = 128
PAIRS_PER_GROUP = EXPERTS_PER_GROUP * (EXPERTS_PER_GROUP - 1) // 2
NB = 128
SLOT_BLOCK = 256
BLOCK_ROWS = (64, 128, 192, 256)
DISPATCH_TILE = 512
DISPATCH_RING = 3
PAD_COPY_SIZES = (128, 64, 32, 16, 8, 4, 2, 1)
TOK_TILE = 512
ROUTE_TILE = 2048
N_LOGIT_ROWS = 40
LANES = 128
SUBLANES = 8
V7X_VMEM_BYTES = 64 * 1024 * 1024
VMEM_LIMIT = V7X_VMEM_BYTES - 8 * 1024 * 1024


def _mm(a, b):
    return jnp.dot(a, b, preferred_element_type=F32)


def _mm_nt(a, b):
    return lax.dot_general(a, b, (((1,), (1,)), ((), ())), preferred_element_type=F32)


def _mm_tn(a, b):
    return lax.dot_general(a, b, (((0,), (0,)), ((), ())), preferred_element_type=F32)


def _sigmoid(x):
    return 1.0 / (1.0 + jnp.exp(-x))


def _silu(x):
    return x * _sigmoid(x)


def _ln(x):
    mu = jnp.mean(x, axis=-1, keepdims=True)
    xc = x - mu
    var = jnp.mean(xc * xc, axis=-1, keepdims=True)
    return xc * lax.rsqrt(var + LN_EPS)


def _const_spec(shape):
    nd = len(shape)
    return pl.BlockSpec(shape, lambda *_: (0,) * nd, pipeline_mode=pl.Buffered(1))


def _ada_kernel(c_ref, w_ref, b_ref, o_ref):
    ca = _silu(c_ref[...])
    o_ref[...] = _mm(ca.astype(BF16), w_ref[...].astype(BF16)) + b_ref[...]


def _ada_call(c, w_ada, b_ada):
    B, D = c.shape
    n_out = w_ada.shape[1]
    return pl.pallas_call(
        _ada_kernel,
        grid=(n_out // D,),
        in_specs=[pl.BlockSpec((B, D), lambda j: (0, 0)),
                  pl.BlockSpec((D, D), lambda j: (0, j)),
                  pl.BlockSpec((1, D), lambda j: (0, j))],
        out_specs=pl.BlockSpec((B, D), lambda j: (0, j)),
        out_shape=jax.ShapeDtypeStruct((B, n_out), F32),
        name="ada_mod",
    )(c, w_ada, b_ada.reshape(1, n_out))


def _mixer_kernel(x_ref, pos_ref, mod_ref, win_ref, wout_ref, lb_ref, hnw_ref, rnw_ref,
                  ln1w_ref, ln1b_ref, invf_ref, dmat_ref, qdec_ref, kdec_ref, wr_ref, br_ref,
                  xp_ref, modp_ref, wg32_ref, wu32_ref, wd32_ref,
                  x1_ref, h2t_ref, lg_ref, wg16_ref, wu16_ref, wd16_ref,
                  h_scr, proj_scr, proj2_scr, o_scr, cos_scr, sin_scr, sh_scr, sr_scr, y_scr, h2_scr,
                  *, ret_chunk_decay, tiles_per_seq):
    T = SEQ_TILE
    D = D_MODEL
    dh = D_HEAD
    hw = N_HEADS * dh
    hw4 = 4 * hw
    step = pl.program_id(0)
    n_tiles = pl.num_programs(0) - 1

    wg16_ref[...] = wg32_ref[...].astype(BF16)
    wu16_ref[...] = wu32_ref[...].astype(BF16)
    wd16_ref[...] = wd32_ref[...].astype(BF16)

    @pl.when(step == 0)
    def _():
        y_scr[...] = jnp.zeros_like(y_scr)

    @pl.when((step < n_tiles) & (step % tiles_per_seq == 0))
    def _():
        sh_scr[...] = jnp.zeros_like(sh_scr)
        sr_scr[...] = jnp.zeros_like(sr_scr)

    gate1p = modp_ref[0, 2:3, :]
    shift2p = modp_ref[0, 3:4, :]
    scale2p = modp_ref[0, 4:5, :]

    def post_block(r):
        rows = slice(r * ROW_BLOCK, (r + 1) * ROW_BLOCK)
        u = DEEPNORM_ALPHA * xp_ref[0, rows, :] + gate1p * y_scr[rows, :]
        x1 = _ln(u) * ln1w_ref[...] + ln1b_ref[...]
        x1_ref[0, rows, :] = x1
        h2 = (_ln(x1) * (1.0 + scale2p) + shift2p).astype(BF16)
        h2_scr[rows, :] = h2
        h2f = h2.astype(F32)
        for s in range(SUBLANES):
            h2t_ref[pl.ds(r * ROW_BLOCK * SUBLANES + s, ROW_BLOCK, stride=SUBLANES), :] = (
                h2f[:, s * LANES:(s + 1) * LANES])

    shift1 = mod_ref[0, 0:1, :]
    scale1 = mod_ref[0, 1:2, :]

    for r in range(T // ROW_BLOCK):
        rows = slice(r * ROW_BLOCK, (r + 1) * ROW_BLOCK)
        h = _ln(x_ref[0, rows, :]) * (1.0 + scale1) + shift1
        h_scr[rows, :] = h.astype(BF16)

    n_post = T // ROW_BLOCK
    proj_scr[...] = _mm(h_scr[...], win_ref[:, 0:hw4])
    for r in range(n_post // 2):
        post_block(r)
    proj2_scr[...] = _mm(h_scr[...], win_ref[:, hw4:2 * hw4])
    for r in range(n_post // 2, n_post):
        post_block(r)
    lg_ref[...] = (_mm(h2_scr[...], wr_ref[...]) + br_ref[...]).T[0:N_LOGIT_ROWS, :]

    pos = pos_ref[0].astype(F32)
    ang_t = invf_ref[...] * pos
    cos_t = jnp.cos(ang_t)
    sin_t = jnp.sin(ang_t)
    cos_scr[...] = jnp.concatenate([cos_t, cos_t], axis=0).T
    sin_scr[...] = jnp.concatenate([-sin_t, sin_t], axis=0).T

    l0 = lb_ref[0:1, :]
    l1 = lb_ref[1:2, :]
    lmax = jnp.maximum(l0, l1)
    e0 = jnp.exp(l0 - lmax)
    e1 = jnp.exp(l1 - lmax)
    lb = e0 / (e0 + e1)
    hnw = hnw_ref[...]

    C = HGRN_CHUNK
    rr = lax.broadcasted_iota(jnp.int32, (C, C), 0)
    cc = lax.broadcasted_iota(jnp.int32, (C, C), 1)
    causal = rr >= cc
    row_id = lax.broadcasted_iota(jnp.int32, (C, hw), 0)

    def hgrn_chunk(c):
        rows = slice(c * C, (c + 1) * C)
        logfs, ks = [], []
        for hd in range(N_HEADS):
            z = proj_scr[rows, hw + hd * dh: hw + (hd + 1) * dh]
            e = jnp.exp(-jnp.abs(z))
            r = 1.0 / (1.0 + e)
            er = e * r
            zp = z >= 0
            sig_pos = jnp.where(zp, r, er)
            sig_neg = jnp.where(zp, er, r)
            lbh = lb[:, hd * dh:(hd + 1) * dh]
            logfs.append(jnp.log(lbh + (1.0 - lbh) * sig_pos))
            ks.append((1.0 - lbh) * sig_neg)
        bsum = jnp.concatenate(logfs, axis=1)
        for k in (1, 2, 4):
            bsum = bsum + jnp.where(row_id >= k, pltpu.roll(bsum, k, 0), 0.0)
        for k in (8, 16, 32):
            bsum = bsum + jnp.concatenate([jnp.zeros((k, hw), F32), bsum[0:C - k, :]], axis=0)
        for hd in range(N_HEADS):
            cols = slice(hd * dh, (hd + 1) * dh)
            b = bsum[:, cols]
            b_last = b[C - 1:C, :]
            q = proj_scr[rows, cols]
            v = proj_scr[rows, 2 * hw + hd * dh: 2 * hw + (hd + 1) * dh].astype(BF16)
            g = proj_scr[rows, 3 * hw + hd * dh: 3 * hw + (hd + 1) * dh]
            k = ks[hd]
            q_in = (_silu(q) * jnp.exp(b)).astype(BF16)
            k_in = (k * jnp.exp(-b)).astype(BF16)
            k_dec = (k * jnp.exp(b_last - b)).astype(BF16)
            st = sh_scr[hd]
            a = jnp.where(causal, _mm_nt(q_in, k_in), 0.0)
            o = _mm(a.astype(BF16), v) + _mm_nt(q_in, st.astype(BF16))
            sh_scr[hd] = st * jnp.exp(b_last) + _mm_tn(v, k_dec)
            ms = jnp.mean(o * o, axis=-1, keepdims=True)
            on = o * lax.rsqrt(ms + LN_EPS) * hnw[:, hd * dh:(hd + 1) * dh] * _silu(g)
            o_scr[rows, hd * dh:(hd + 1) * dh] = on.astype(BF16)

    rnw = rnw_ref[...]
    RC = RET_CHUNK
    q_scale = dh ** -0.5

    def retention_unit(sub, hd):
        rows = slice(sub * RC, (sub + 1) * RC)
        cs = cos_scr[rows, :]
        sn = sin_scr[rows, :]
        rq = proj2_scr[rows, hd * dh: (hd + 1) * dh]
        rk = proj2_scr[rows, hw + hd * dh: hw + (hd + 1) * dh]
        v = proj2_scr[rows, 2 * hw + hd * dh: 2 * hw + (hd + 1) * dh].astype(BF16)
        g = proj2_scr[rows, 3 * hw + hd * dh: 3 * hw + (hd + 1) * dh]
        q = (rq * cs + pltpu.roll(rq, dh // 2, 1) * sn) * q_scale
        k = rk * cs + pltpu.roll(rk, dh // 2, 1) * sn
        qb = q.astype(BF16)
        st = sr_scr[hd]
        s = _mm_nt(qb, k.astype(BF16)) * dmat_ref[hd]
        o = _mm(s.astype(BF16), v) + _mm_nt(qb, st.astype(BF16)) * qdec_ref[hd]
        sr_scr[hd] = st * ret_chunk_decay[hd] + _mm_tn(v, (k * kdec_ref[hd]).astype(BF16))
        mu = jnp.mean(o, axis=-1, keepdims=True)
        oc = o - mu
        var = jnp.mean(oc * oc, axis=-1, keepdims=True)
        on = oc * lax.rsqrt(var + LN_EPS) * rnw[:, hd * dh:(hd + 1) * dh] * _silu(g)
        o_scr[rows, hw + hd * dh: hw + (hd + 1) * dh] = on.astype(BF16)

    n_chunks = T // C
    units = [(sub, hd) for sub in range(T // RC) for hd in range(N_HEADS)]
    for c in range(n_chunks):
        hgrn_chunk(c)
        for sub, hd in units[c * len(units) // n_chunks:(c + 1) * len(units) // n_chunks]:
            retention_unit(sub, hd)

    y_scr[...] = _mm(o_scr[...], wout_ref[...])


def _retention_tables():
    h = np.arange(N_HEADS, dtype=np.float64)
    log_gamma = np.log(1.0 - np.exp2(-5.0 - h))
    idx = np.arange(RET_CHUNK, dtype=np.float64)
    rel = idx[:, None] - idx[None, :]
    dmat = np.where(rel >= 0, np.exp(np.maximum(rel, 0.0)[None] * log_gamma[:, None, None]), 0.0)
    qdec = np.exp((idx + 1.0)[None, :] * log_gamma[:, None])
    kdec = np.exp((RET_CHUNK - 1.0 - idx)[None, :] * log_gamma[:, None])
    cdec = np.exp(RET_CHUNK * log_gamma)
    bc = lambda t: np.broadcast_to(t[:, :, None], (N_HEADS, RET_CHUNK, D_HEAD))
    return (jnp.asarray(dmat, F32), jnp.asarray(bc(qdec), F32), jnp.asarray(bc(kdec), F32),
            tuple(float(np.float32(v)) for v in cdec))


def _mixer_call(x, positions, mod3, win, wout, hgrn_lb, hnw, rnw, ln1w, ln1b, wr, br, wg, wu, wd):
    B, S, D = x.shape
    T = SEQ_TILE
    nj = S // T
    dmat, qdec, kdec, cdec = _retention_tables()
    inv_freq = np.power(ROPE_BASE, -np.arange(0, D_HEAD, 2, dtype=np.float64) / D_HEAD)
    invf = jnp.asarray(inv_freq[:, None], F32)
    pos3 = positions.reshape(B, 1, S)
    kern = functools.partial(_mixer_kernel, ret_chunk_decay=cdec, tiles_per_seq=nj)
    n_tiles = B * nj
    cur = lambda s: jnp.minimum(s, n_tiles - 1)
    prev = lambda s: jnp.maximum(s - 1, 0)
    E, _, F = wg.shape
    wg2, wu2, wd2 = wg.reshape(E * D, F), wu.reshape(E * D, F), wd.reshape(E * F, D)
    rg, rd = (E * D) // n_tiles, (E * F) // n_tiles
    assert rg * n_tiles == E * D and rd * n_tiles == E * F and rd % 16 == 0
    wspec_g = pl.BlockSpec((rg, F), lambda s: (cur(s), 0))
    wspec_d = pl.BlockSpec((rd, D), lambda s: (cur(s), 0))
    outs = pl.pallas_call(
        kern,
        grid=(n_tiles + 1,),
        in_specs=[
            pl.BlockSpec((1, T, D), lambda s: (cur(s) // nj, cur(s) % nj, 0)),
            pl.BlockSpec((1, 1, T), lambda s: (cur(s) // nj, 0, cur(s) % nj)),
            pl.BlockSpec((1, 6, D), lambda s: (cur(s) // nj, 0, 0)),
            _const_spec(win.shape), _const_spec(wout.shape), _const_spec(hgrn_lb.shape),
            _const_spec(hnw.shape), _const_spec(rnw.shape), _const_spec(ln1w.shape),
            _const_spec(ln1b.shape), _const_spec(invf.shape), _const_spec(dmat.shape),
            _const_spec(qdec.shape), _const_spec(kdec.shape), _const_spec(wr.shape),
            _const_spec(br.shape),
            pl.BlockSpec((1, T, D), lambda s: (prev(s) // nj, prev(s) % nj, 0)),
            pl.BlockSpec((1, 6, D), lambda s: (prev(s) // nj, 0, 0)),
            wspec_g, wspec_g, wspec_d,
        ],
        out_specs=[
            pl.BlockSpec((1, T, D), lambda s: (prev(s) // nj, prev(s) % nj, 0)),
            pl.BlockSpec((T * SUBLANES, LANES), lambda s: (prev(s), 0)),
            pl.BlockSpec((N_LOGIT_ROWS, T), lambda s: (0, prev(s))),
            wspec_g, wspec_g, wspec_d,
        ],
        out_shape=[
            jax.ShapeDtypeStruct((B, S, D), F32),
            jax.ShapeDtypeStruct((B * S * SUBLANES, LANES), F32),
            jax.ShapeDtypeStruct((N_LOGIT_ROWS, B * S), F32),
            jax.ShapeDtypeStruct(wg2.shape, BF16),
            jax.ShapeDtypeStruct(wu2.shape, BF16),
            jax.ShapeDtypeStruct(wd2.shape, BF16),
        ],
        scratch_shapes=[
            pltpu.VMEM((T, D), BF16),
            pltpu.VMEM((T, 4 * N_HEADS * D_HEAD), F32),
            pltpu.VMEM((T, 4 * N_HEADS * D_HEAD), F32),
            pltpu.VMEM((T, D), BF16),
            pltpu.VMEM((T, D_HEAD), F32),
            pltpu.VMEM((T, D_HEAD), F32),
            pltpu.VMEM((N_HEADS, D_HEAD, D_HEAD), F32),
            pltpu.VMEM((N_HEADS, D_HEAD, D_HEAD), F32),
            pltpu.VMEM((T, D), F32),
            pltpu.VMEM((T, D), BF16),
        ],
        compiler_params=pltpu.CompilerParams(
            dimension_semantics=("arbitrary",), vmem_limit_bytes=VMEM_LIMIT),
        name="mixer",
    )(x, pos3, mod3, win, wout, hgrn_lb, hnw, rnw, ln1w, ln1b, invf, dmat, qdec, kdec, wr, br, x, mod3,
      wg2, wu2, wd2)
    x1, h2t, lgt, wg16, wu16, wd16 = outs
    return x1, h2t, lgt, wg16.reshape(E, D, F), wu16.reshape(E, D, F), wd16.reshape(E, F, D)


def _row_copy(src_ref, dst_ref, src_row, dst_row, sem):
    s0 = pl.multiple_of(src_row * SUBLANES, SUBLANES)
    d0 = pl.multiple_of(dst_row * SUBLANES, SUBLANES)
    return pltpu.make_async_copy(src_ref.at[pl.ds(s0, SUBLANES)], dst_ref.at[pl.ds(d0, SUBLANES)], sem)


def _from_token_tiles(buf_ref, first_row, n_rows):
    return jnp.concatenate(
        [buf_ref[pl.ds(first_row * SUBLANES + s, n_rows, stride=SUBLANES), :] for s in range(SUBLANES)],
        axis=1)


def _start_rows(src_ref, dst_ref, idx_ref, idx_base, dst_base, n_rows, sem):
    def issue(r2, carry):
        for u in range(2):
            r = r2 * 2 + u
            _row_copy(src_ref, dst_ref, idx_ref[idx_base + r], dst_base + r, sem).start(priority=u)
        return carry
    lax.fori_loop(0, n_rows // 2, issue, 0)


def _wait_rows(src_ref, dst_ref, n_rows, sem):
    pltpu.make_async_copy(src_ref.at[pl.ds(0, n_rows * SUBLANES)], dst_ref, sem).wait()


def _bf16_bits(x):
    return lax.bitcast_convert_type(x.astype(BF16).astype(F32), jnp.uint32)


def _dispatch_kernel(dest_ref, pad0_ref, padn_ref, nv_ref, h2t_ref, xs_ref,
                     buf, zbuf, load_sems, scat_sems, zsem):
    i = pl.program_id(0)
    n = pl.num_programs(0)
    TT = DISPATCH_TILE
    rows = TT * SUBLANES

    def pad_copies(wait):
        def bucket(b, carry):
            pos = pad0_ref[b]
            left = padn_ref[b]
            for size in PAD_COPY_SIZES:
                hit = (left & size) != 0
                first = 0 if wait else pl.multiple_of(pos * SUBLANES, SUBLANES)
                copy = pltpu.make_async_copy(
                    zbuf.at[pl.ds(0, size * SUBLANES)], xs_ref.at[pl.ds(first, size * SUBLANES)], zsem)

                @pl.when(hit)
                def _():
                    copy.wait() if wait else copy.start()
                pos = pos + jnp.where(hit, size, 0)
            return carry
        lax.fori_loop(0, pad0_ref.shape[0], bucket, 0)

        zrows = PAD_COPY_SIZES[0] * SUBLANES

        def unused_block(blk, carry):
            for part in range(SLOT_BLOCK // PAD_COPY_SIZES[0]):
                first = 0 if wait else pl.multiple_of(blk * SLOT_BLOCK * SUBLANES + part * zrows, zrows)
                copy = pltpu.make_async_copy(zbuf, xs_ref.at[pl.ds(first, zrows)], zsem)
                copy.wait() if wait else copy.start()
            return carry
        lax.fori_loop(nv_ref[0], xs_ref.shape[0] // (SLOT_BLOCK * SUBLANES), unused_block, 0)

    @pl.when(i == 0)
    def _():
        zbuf[...] = jnp.zeros_like(zbuf)
        pad_copies(wait=False)

    def load(tile, slot):
        src = h2t_ref.at[pl.ds(pl.multiple_of(tile * rows, rows), rows)]
        return pltpu.make_async_copy(src, buf.at[slot], load_sems.at[slot])

    def wait_scatter(slot):
        _wait_rows(buf.at[slot], xs_ref.at[pl.ds(0, rows)], TT, scat_sems.at[slot])

    @pl.when(i == 0)
    def _():
        load(0, 0).start()

    for slot in range(DISPATCH_RING):
        @pl.when((i + 1 < n) & ((i + 1) % DISPATCH_RING == slot))
        def _():
            @pl.when(i + 1 >= DISPATCH_RING)
            def _():
                wait_scatter(slot)
            load(i + 1, slot).start()

    for slot in range(DISPATCH_RING):
        @pl.when(i % DISPATCH_RING == slot)
        def _():
            load(i, slot).wait()

            def issue(r2, carry):
                for u in range(4):
                    r = r2 * 4 + u
                    _row_copy(buf.at[slot], xs_ref, r, dest_ref[i * TT + r],
                              scat_sems.at[slot]).start(priority=u % 2)
                return carry
            lax.fori_loop(0, TT // 4, issue, 0)

    @pl.when(i == n - 1)
    def _():
        for slot in range(DISPATCH_RING):
            wait_scatter(slot)
        pad_copies(wait=True)


def _dispatch_call(dest, pad_start, pad_len, n_valid, h2t, n_slots):
    N = dest.shape[0]
    TT = DISPATCH_TILE
    assert N // TT >= DISPATCH_RING and SLOT_BLOCK <= 2 * PAD_COPY_SIZES[0]
    assert SLOT_BLOCK % PAD_COPY_SIZES[0] == 0
    grid_spec = pltpu.PrefetchScalarGridSpec(
        num_scalar_prefetch=4,
        grid=(N // TT,),
        in_specs=[pl.BlockSpec(memory_space=pl.ANY)],
        out_specs=pl.BlockSpec(memory_space=pl.ANY),
        scratch_shapes=[pltpu.VMEM((DISPATCH_RING, TT * SUBLANES, LANES), F32),
                        pltpu.VMEM((PAD_COPY_SIZES[0] * SUBLANES, LANES), F32),
                        pltpu.SemaphoreType.DMA((DISPATCH_RING,)),
                        pltpu.SemaphoreType.DMA((DISPATCH_RING,)),
                        pltpu.SemaphoreType.DMA],
    )
    return pl.pallas_call(
        _dispatch_kernel,
        grid_spec=grid_spec,
        out_shape=jax.ShapeDtypeStruct((n_slots * SUBLANES, LANES), F32),
        compiler_params=pltpu.CompilerParams(dimension_semantics=("arbitrary",)),
        name="dispatch",
    )(dest, pad_start, pad_len, n_valid, h2t)


def _expert_kernel(ea_ref, eb_ref, rows_ref, nv_ref, xs_ref, wg_ref, wu_ref, wd_ref, y_ref):
    i = pl.program_id(0)
    R = SLOT_BLOCK
    G = EXPERTS_PER_GROUP
    used = i < nv_ref[0]
    n_rows = rows_ref[i]

    def ffn(xb, e):
        a = _mm(xb, wg_ref[e])
        u = _mm(xb, wu_ref[e])
        return _mm((_silu(a) * u).astype(BF16), wd_ref[e])

    for k, m in enumerate(BLOCK_ROWS):
        fewer = BLOCK_ROWS[k - 1] if k else 0

        @pl.when(used & (n_rows > fewer) & (n_rows <= m))
        def _():
            xb = _from_token_tiles(xs_ref, 0, m).astype(BF16)
            ya = ffn(xb, ea_ref[i] & (G - 1))
            yb = ffn(xb, eb_ref[i] & (G - 1))
            packed = _bf16_bits(ya) | (_bf16_bits(yb) >> 16)
            for s in range(SUBLANES):
                y_ref[pl.ds(s, m, stride=SUBLANES), :] = packed[:, s * LANES:(s + 1) * LANES]
            if m < R:
                y_ref[pl.ds(m * SUBLANES, (R - m) * SUBLANES), :] = jnp.zeros(
                    ((R - m) * SUBLANES, LANES), y_ref.dtype)

    @pl.when(jnp.logical_not(used))
    def _():
        y_ref[...] = jnp.zeros_like(y_ref)


def _expert_call(block_ea, block_eb, block_rows, n_valid, xs, wg, wu, wd):
    assert BLOCK_ROWS[-1] == SLOT_BLOCK
    R = SLOT_BLOCK
    n_blocks = xs.shape[0] // (R * SUBLANES)
    D, F = wg.shape[1], wg.shape[2]
    G = EXPERTS_PER_GROUP
    last = lambda i, nv: jnp.minimum(i, nv[0] - 1)
    grp = lambda i, ea, eb, rows, nv: (ea[last(i, nv)] // G, 0, 0)
    blk = lambda i, ea, eb, rows, nv: (last(i, nv), 0)
    grid_spec = pltpu.PrefetchScalarGridSpec(
        num_scalar_prefetch=4,
        grid=(n_blocks,),
        in_specs=[
            pl.BlockSpec((R * SUBLANES, LANES), blk),
            pl.BlockSpec((G, D, F), grp, pipeline_mode=pl.Buffered(1)),
            pl.BlockSpec((G, D, F), grp, pipeline_mode=pl.Buffered(1)),
            pl.BlockSpec((G, F, D), grp, pipeline_mode=pl.Buffered(1)),
        ],
        out_specs=pl.BlockSpec((R * SUBLANES, LANES), lambda i, ea, eb, rows, nv: (i, 0)),
    )
    return pl.pallas_call(
        _expert_kernel,
        grid_spec=grid_spec,
        out_shape=jax.ShapeDtypeStruct(xs.shape, jnp.uint32),
        compiler_params=pltpu.CompilerParams(
            dimension_semantics=("arbitrary",), vmem_limit_bytes=VMEM_LIMIT),
        name="experts",
    )(block_ea, block_eb, block_rows, n_valid, xs, wg, wu, wd)


def _final_kernel(dest_ref, yt_ref, x1_ref, g_ref, mod_ref, w_ref, b_ref, o_ref, buf, sems):
    i = pl.program_id(0)
    n_steps = pl.num_programs(0)
    TT = TOK_TILE

    @pl.when(i == 0)
    def _():
        _start_rows(yt_ref, buf.at[0], dest_ref, 0, 0, TT, sems.at[0])

    nxt = jnp.minimum(i + 1, n_steps - 1)
    gate2 = mod_ref[0, 5:6, :]
    g = g_ref[...]
    for slot in range(2):
        @pl.when(i % 2 == slot)
        def _():
            _wait_rows(yt_ref, buf.at[slot], TT, sems.at[slot])
            for r in range(TT // ROW_BLOCK):
                rows = slice(r * ROW_BLOCK, (r + 1) * ROW_BLOCK)
                packed = _from_token_tiles(buf.at[slot], r * ROW_BLOCK, ROW_BLOCK)
                ya = lax.bitcast_convert_type(packed & jnp.uint32(0xFFFF0000), F32)
                yb = lax.bitcast_convert_type(packed << 16, F32)
                y = ya * g[rows, 0:1] + yb * g[rows, 1:2]
                u = DEEPNORM_ALPHA * x1_ref[rows, :] + gate2 * y
                o_ref[rows, :] = _ln(u) * w_ref[...] + b_ref[...]
                for q in range(r * ROW_BLOCK, (r + 1) * ROW_BLOCK):
                    _row_copy(yt_ref, buf.at[1 - slot], dest_ref[nxt * TT + q], q,
                              sems.at[1 - slot]).start(priority=q % 2)

            @pl.when(i == n_steps - 1)
            def _():
                _wait_rows(yt_ref, buf.at[1 - slot], TT, sems.at[1 - slot])


def _final_call(dest, yt, x1, gates, mod3, w, b, seq_len):
    N, D = x1.shape
    TT = TOK_TILE
    per_seq = seq_len // TT
    grid_spec = pltpu.PrefetchScalarGridSpec(
        num_scalar_prefetch=1,
        grid=(N // TT,),
        in_specs=[
            pl.BlockSpec(memory_space=pl.ANY),
            pl.BlockSpec((TT, D), lambda i, d: (i, 0)),
            pl.BlockSpec((TT, 2), lambda i, d: (i, 0)),
            pl.BlockSpec((1, 6, D), lambda i, d: (i // per_seq, 0, 0)),
            pl.BlockSpec((1, D), lambda i, d: (0, 0)),
            pl.BlockSpec((1, D), lambda i, d: (0, 0)),
        ],
        out_specs=pl.BlockSpec((TT, D), lambda i, d: (i, 0)),
        scratch_shapes=[pltpu.VMEM((2, TT * SUBLANES, LANES), jnp.uint32), pltpu.SemaphoreType.DMA((2,))],
    )
    return pl.pallas_call(
        _final_kernel,
        grid_spec=grid_spec,
        out_shape=jax.ShapeDtypeStruct((N, D), F32),
        compiler_params=pltpu.CompilerParams(
            dimension_semantics=("arbitrary",), vmem_limit_bytes=VMEM_LIMIT),
        name="combine_ln",
    )(dest, yt, x1, gates, mod3, w, b)


def _route_kernel(lgt_ref, rf_ref, ri_ref, meta_ref, cnt_scr, base_scr, bstart_scr):
    ph = pl.program_id(0)
    t = pl.program_id(1)
    TR = ROUTE_TILE
    E = N_EXPERTS
    R = SLOT_BLOCK
    G = EXPERTS_PER_GROUP
    lg = lgt_ref[...]

    g = [lg[E + k:E + k + 1, :] for k in range(N_GROUPS)]
    gmax = jnp.maximum(jnp.maximum(g[0], g[1]), jnp.maximum(g[2], g[3]))
    gs = jnp.where(g[0] == gmax, 0.0, jnp.where(g[1] == gmax, 1.0, jnp.where(g[2] == gmax, 2.0, 3.0)))
    psum = (jnp.exp(g[0] - gmax) + jnp.exp(g[1] - gmax)) + (jnp.exp(g[2] - gmax) + jnp.exp(g[3] - gmax))
    p_star = 1.0 / psum
    esel = jnp.where(gs == 0.0, lg[0:G], jnp.where(gs == 1.0, lg[G:2 * G],
                                                   jnp.where(gs == 2.0, lg[2 * G:3 * G], lg[3 * G:4 * G])))
    sub = lax.broadcasted_iota(jnp.int32, (G, TR), 0).astype(F32)
    m1 = jnp.max(esel, axis=0, keepdims=True)
    i1 = jnp.min(jnp.where(esel == m1, sub, float(G)), axis=0, keepdims=True)
    es2 = jnp.where(sub == i1, -jnp.inf, esel)
    m2 = jnp.max(es2, axis=0, keepdims=True)
    i2 = jnp.min(jnp.where(es2 == m2, sub, float(G)), axis=0, keepdims=True)
    d = jnp.exp(m2 - m1)
    w1 = 1.0 / (1.0 + d)
    w2 = d * w1
    lo = jnp.minimum(i1, i2)
    hi = jnp.maximum(i1, i2)
    bucket = gs * PAIRS_PER_GROUP + (lo * (2 * G - 1 - lo) * 0.5 + (hi - lo - 1.0))
    first_is_lo = i1 < i2
    gate_a = p_star * jnp.where(first_is_lo, w1, w2)
    gate_b = p_star * jnp.where(first_is_lo, w2, w1)
    row = lax.broadcasted_iota(jnp.int32, (NB, TR), 0).astype(F32)
    oh = row == bucket
    cnt = oh.astype(F32)

    @pl.when(ph == 0)
    def _():
        @pl.when(t == 0)
        def _():
            cnt_scr[...] = jnp.zeros_like(cnt_scr)
        acc = cnt_scr[...]
        for j in range(TR // LANES):
            acc = acc + cnt[:, j * LANES:(j + 1) * LANES]
        cnt_scr[...] = acc

    @pl.when((ph == 1) & (t == 0))
    def _():
        counts = jnp.sum(cnt_scr[...], axis=1, keepdims=True)
        nblk = jnp.floor((counts + (R - 1)) * (1.0 / R))
        nblk_b = jnp.broadcast_to(nblk, (NB, LANES))
        rb = lax.broadcasted_iota(jnp.int32, (NB, NB), 0)
        cb = lax.broadcasted_iota(jnp.int32, (NB, NB), 1)
        bstart = _mm((cb < rb).astype(BF16), nblk_b.astype(BF16))
        bstart_scr[...] = bstart
        base_scr[...] = jnp.zeros_like(base_scr)
        k = lax.broadcasted_iota(jnp.int32, (NB, 1), 0).astype(F32)
        grp = sum((k >= float(m * PAIRS_PER_GROUP)).astype(F32) for m in range(1, N_GROUPS))
        p = k - grp * PAIRS_PER_GROUP
        pair_start = [i * (2 * G - 1 - i) // 2 for i in range(G - 1)]
        ia = sum((p >= float(s)).astype(F32) for s in pair_start[1:])
        ib = p - ia * (2 * G - 1 - ia) * 0.5 + ia + 1.0
        n_pad = meta_ref.shape[1]
        blk = lax.broadcasted_iota(jnp.int32, (NB, n_pad), 1).astype(F32)
        first = jnp.broadcast_to(bstart[:, 0:1], (NB, n_pad))
        last = jnp.broadcast_to((bstart + nblk_b)[:, 0:1], (NB, n_pad))
        member = (first <= blk) & (blk < last)
        ea = jnp.sum(jnp.where(member, grp * G + ia, 0.0), axis=0, keepdims=True)
        eb = jnp.sum(jnp.where(member, grp * G + ib, 0.0), axis=0, keepdims=True)
        meta_ref[...] = jnp.zeros_like(meta_ref)
        meta_ref[0:1, :] = ea.astype(jnp.int32)
        meta_ref[1:2, :] = eb.astype(jnp.int32)
        meta_ref[2:3, :] = jnp.max(last, axis=0, keepdims=True).astype(jnp.int32)
        in_blk = jnp.clip(counts - (blk - first) * R, 0.0, float(R))
        meta_ref[5:6, :] = jnp.sum(jnp.where(member, in_blk, 0.0), axis=0, keepdims=True).astype(jnp.int32)
        pad0 = jnp.broadcast_to(bstart[:, 0:1] * R + counts, (NB, NB)).T[0:1, :]
        padn = jnp.broadcast_to(nblk * R - counts, (NB, NB)).T[0:1, :]
        meta_ref[3:4, 0:NB] = pad0.astype(jnp.int32)
        meta_ref[4:5, 0:NB] = padn.astype(jnp.int32)

    @pl.when(ph == 1)
    def _():
        ri = lax.broadcasted_iota(jnp.int32, (LANES, 2 * LANES), 0)
        ci = lax.broadcasted_iota(jnp.int32, (LANES, 2 * LANES), 1)
        w = ((ci >= ri) | (ci >= LANES)).astype(BF16)
        base = base_scr[...]
        slot0 = bstart_scr[...] * R
        ds = []
        for j in range(TR // LANES):
            sl = slice(j * LANES, (j + 1) * LANES)
            blk = cnt[:, sl]
            res = _mm(blk.astype(BF16), w)
            val = slot0 + base + (res[:, 0:LANES] - blk)
            ds.append(jnp.sum(jnp.where(oh[:, sl], val, 0.0), axis=0, keepdims=True))
            base = base + res[:, LANES:2 * LANES]
        base_scr[...] = base
        ri_ref[...] = jnp.zeros_like(ri_ref)
        ri_ref[0:1, :] = jnp.concatenate(ds, axis=1).astype(jnp.int32)
        rf_ref[...] = jnp.zeros_like(rf_ref)
        rf_ref[0:1, :] = gate_a
        rf_ref[1:2, :] = gate_b


def _route_call(lgt, n_blocks):
    rows, N = lgt.shape
    TR = ROUTE_TILE
    E = NB
    n_pad = -(-n_blocks // LANES) * LANES
    return pl.pallas_call(
        _route_kernel,
        grid=(2, N // TR),
        in_specs=[pl.BlockSpec((rows, TR), lambda ph, t: (0, t))],
        out_specs=[
            pl.BlockSpec((SUBLANES, TR), lambda ph, t: (0, ph * t)),
            pl.BlockSpec((SUBLANES, TR), lambda ph, t: (0, ph * t)),
            pl.BlockSpec((SUBLANES, n_pad), lambda ph, t: (0, 0)),
        ],
        out_shape=[
            jax.ShapeDtypeStruct((SUBLANES, N), F32),
            jax.ShapeDtypeStruct((SUBLANES, N), jnp.int32),
            jax.ShapeDtypeStruct((SUBLANES, n_pad), jnp.int32),
        ],
        scratch_shapes=[pltpu.VMEM((E, LANES), F32), pltpu.VMEM((E, LANES), F32), pltpu.VMEM((E, LANES), F32)],
        compiler_params=pltpu.CompilerParams(dimension_semantics=("arbitrary", "arbitrary")),
        name="route",
    )(lgt)


def kernel(x, c, positions, w_ada, b_ada, w_in, w_out, hgrn_lb, hgrn_norm_w, ret_norm_w, post_ln1_w,
           post_ln1_b, w_rg, b_rg, w_re, b_re, w_gate, w_up, w_down, post_ln2_w, post_ln2_b):
    B, S, D = x.shape
    N = B * S
    hw = N_HEADS * D_HEAD

    mod3 = _ada_call(c, w_ada[0], b_ada[0]).reshape(B, 6, D)

    win = w_in[0].astype(BF16)
    wout = w_out[0].astype(BF16)
    pad = LANES - N_EXPERTS - N_GROUPS
    wr = jnp.concatenate([w_re[0], w_rg[0], jnp.zeros((D, pad), F32)], axis=1).astype(BF16)
    br = jnp.concatenate([b_re[0], b_rg[0], jnp.zeros((pad,), F32)]).reshape(1, LANES)

    x1, h2t, lgt, wg16, wu16, wd16 = _mixer_call(
        x, positions, mod3, win, wout, hgrn_lb, hgrn_norm_w[0].reshape(1, hw),
        ret_norm_w[0].reshape(1, hw), post_ln1_w[0].reshape(1, D), post_ln1_b[0].reshape(1, D), wr, br,
        w_gate[0], w_up[0], w_down[0])

    n_blocks = N // SLOT_BLOCK + N_GROUPS * PAIRS_PER_GROUP
    rf, ri, meta = _route_call(lgt, n_blocks)
    dest = ri[0]
    gates = rf[0:2].T
    xs = _dispatch_call(dest, meta[3, :NB], meta[4, :NB], meta[2, 0:1], h2t, n_blocks * SLOT_BLOCK)
    yt = _expert_call(meta[0, :n_blocks], meta[1, :n_blocks], meta[5, :n_blocks], meta[2, 0:1], xs,
                      wg16, wu16, wd16)
    out = _final_call(dest, yt, x1.reshape(N, D), gates, mod3, post_ln2_w[0].reshape(1, D),
                      post_ln2_b[0].reshape(1, D), S)
    return out.reshape(B, S, D)
```

```python
import functools

import numpy as np
import jax
import jax.numpy as jnp
from jax import lax
from jax.experimental import pallas as pl
from jax.experimental.pallas import tpu as pltpu

F32 = jnp.float32
BF16 = jnp.bfloat16

D_MODEL = 1024
N_HEADS = 4
D_HEAD = 128
HGRN_CHUNK = 64
RET_CHUNK = 256
ROPE_BASE = 10000.0
N_EXPERTS = 32
EXPERTS_PER_GROUP = 8
N_GROUPS = 4
DEEPNORM_ALPHA = 2.0 ** 0.25
LN_EPS = 1e-5

SEQ_TILE = 512
ROW_BLOCK = 128
PAIRS_PER_GROUP = EXPERTS_PER_GROUP * (EXPERTS_PER_GROUP - 1) // 2
NB = 128
SLOT_BLOCK = 256
BLOCK_ROWS = (64, 128, 192, 256)
INVERT_STEPS = 8
TOK_TILE = 512
ROUTE_TILE = 2048
N_LOGIT_ROWS = 40
LANES = 128
SUBLANES = 8
V7X_VMEM_BYTES = 64 * 1024 * 1024
VMEM_LIMIT = V7X_VMEM_BYTES - 8 * 1024 * 1024


def _mm(a, b):
    return jnp.dot(a, b, preferred_element_type=F32)


def _mm_nt(a, b):
    return lax.dot_general(a, b, (((1,), (1,)), ((), ())), preferred_element_type=F32)


def _mm_tn(a, b):
    return lax.dot_general(a, b, (((0,), (0,)), ((), ())), preferred_element_type=F32)


def _sigmoid(x):
    return 1.0 / (1.0 + jnp.exp(-x))


def _silu(x):
    return x * _sigmoid(x)


def _ln(x):
    mu = jnp.mean(x, axis=-1, keepdims=True)
    xc = x - mu
    var = jnp.mean(xc * xc, axis=-1, keepdims=True)
    return xc * lax.rsqrt(var + LN_EPS)


def _const_spec(shape):
    nd = len(shape)
    return pl.BlockSpec(shape, lambda *_: (0,) * nd, pipeline_mode=pl.Buffered(1))


def _ada_kernel(c_ref, w_ref, b_ref, o_ref):
    ca = _silu(c_ref[...])
    o_ref[...] = _mm(ca.astype(BF16), w_ref[...].astype(BF16)) + b_ref[...]


def _ada_call(c, w_ada, b_ada):
    B, D = c.shape
    n_out = w_ada.shape[1]
    return pl.pallas_call(
        _ada_kernel,
        grid=(n_out // D,),
        in_specs=[pl.BlockSpec((B, D), lambda j: (0, 0)),
                  pl.BlockSpec((D, D), lambda j: (0, j)),
                  pl.BlockSpec((1, D), lambda j: (0, j))],
        out_specs=pl.BlockSpec((B, D), lambda j: (0, j)),
        out_shape=jax.ShapeDtypeStruct((B, n_out), F32),
        name="ada_mod",
    )(c, w_ada, b_ada.reshape(1, n_out))


def _mixer_kernel(x_ref, pos_ref, mod_ref, win_ref, wout_ref, lb_ref, hnw_ref, rnw_ref,
                  ln1w_ref, ln1b_ref, invf_ref, dmat_ref, qdec_ref, kdec_ref, wr_ref, br_ref,
                  xp_ref, modp_ref, wg32_ref, wu32_ref, wd32_ref,
                  x1_ref, h2t_ref, lg_ref, wg16_ref, wu16_ref, wd16_ref,
                  h_scr, proj_scr, proj2_scr, o_scr, cos_scr, sin_scr, sh_scr, sr_scr, y_scr, h2_scr,
                  *, ret_chunk_decay, tiles_per_seq):
    T = SEQ_TILE
    D = D_MODEL
    dh = D_HEAD
    hw = N_HEADS * dh
    hw4 = 4 * hw
    step = pl.program_id(0)
    n_tiles = pl.num_programs(0) - 1

    wg16_ref[...] = wg32_ref[...].astype(BF16)
    wu16_ref[...] = wu32_ref[...].astype(BF16)
    wd16_ref[...] = wd32_ref[...].astype(BF16)

    @pl.when(step == 0)
    def _():
        y_scr[...] = jnp.zeros_like(y_scr)

    @pl.when((step < n_tiles) & (step % tiles_per_seq == 0))
    def _():
        sh_scr[...] = jnp.zeros_like(sh_scr)
        sr_scr[...] = jnp.zeros_like(sr_scr)

    gate1p = modp_ref[0, 2:3, :]
    shift2p = modp_ref[0, 3:4, :]
    scale2p = modp_ref[0, 4:5, :]

    def post_block(r):
        rows = slice(r * ROW_BLOCK, (r + 1) * ROW_BLOCK)
        u = DEEPNORM_ALPHA * xp_ref[0, rows, :] + gate1p * y_scr[rows, :]
        x1 = _ln(u) * ln1w_ref[...] + ln1b_ref[...]
        x1_ref[0, rows, :] = x1
        h2 = (_ln(x1) * (1.0 + scale2p) + shift2p).astype(BF16)
        h2_scr[rows, :] = h2
        h2f = h2.astype(F32)
        for s in range(SUBLANES):
            h2t_ref[pl.ds(r * ROW_BLOCK * SUBLANES + s, ROW_BLOCK, stride=SUBLANES), :] = (
                h2f[:, s * LANES:(s + 1) * LANES])

    shift1 = mod_ref[0, 0:1, :]
    scale1 = mod_ref[0, 1:2, :]

    for r in range(T // ROW_BLOCK):
        rows = slice(r * ROW_BLOCK, (r + 1) * ROW_BLOCK)
        h = _ln(x_ref[0, rows, :]) * (1.0 + scale1) + shift1
        h_scr[rows, :] = h.astype(BF16)

    n_post = T // ROW_BLOCK
    proj_scr[...] = _mm(h_scr[...], win_ref[:, 0:hw4])
    for r in range(n_post // 2):
        post_block(r)
    proj2_scr[...] = _mm(h_scr[...], win_ref[:, hw4:2 * hw4])
    for r in range(n_post // 2, n_post):
        post_block(r)
    lg_ref[...] = (_mm(h2_scr[...], wr_ref[...]) + br_ref[...]).T[0:N_LOGIT_ROWS, :]

    pos = pos_ref[0].astype(F32)
    ang_t = invf_ref[...] * pos
    cos_t = jnp.cos(ang_t)
    sin_t = jnp.sin(ang_t)
    cos_scr[...] = jnp.concatenate([cos_t, cos_t], axis=0).T
    sin_scr[...] = jnp.concatenate([-sin_t, sin_t], axis=0).T

    l0 = lb_ref[0:1, :]
    l1 = lb_ref[1:2, :]
    lmax = jnp.maximum(l0, l1)
    e0 = jnp.exp(l0 - lmax)
    e1 = jnp.exp(l1 - lmax)
    lb = e0 / (e0 + e1)
    hnw = hnw_ref[...]

    C = HGRN_CHUNK
    rr = lax.broadcasted_iota(jnp.int32, (C, C), 0)
    cc = lax.broadcasted_iota(jnp.int32, (C, C), 1)
    causal = rr >= cc
    row_id = lax.broadcasted_iota(jnp.int32, (C, hw), 0)

    def hgrn_chunk(c):
        rows = slice(c * C, (c + 1) * C)
        logfs, ks = [], []
        for hd in range(N_HEADS):
            z = proj_scr[rows, hw + hd * dh: hw + (hd + 1) * dh]
            e = jnp.exp(-jnp.abs(z))
            r = 1.0 / (1.0 + e)
            er = e * r
            zp = z >= 0
            sig_pos = jnp.where(zp, r, er)
            sig_neg = jnp.where(zp, er, r)
            lbh = lb[:, hd * dh:(hd + 1) * dh]
            logfs.append(jnp.log(lbh + (1.0 - lbh) * sig_pos))
            ks.append((1.0 - lbh) * sig_neg)
        bsum = jnp.concatenate(logfs, axis=1)
        for k in (1, 2, 4):
            bsum = bsum + jnp.where(row_id >= k, pltpu.roll(bsum, k, 0), 0.0)
        for k in (8, 16, 32):
            bsum = bsum + jnp.concatenate([jnp.zeros((k, hw), F32), bsum[0:C - k, :]], axis=0)
        for hd in range(N_HEADS):
            cols = slice(hd * dh, (hd + 1) * dh)
            b = bsum[:, cols]
            b_last = b[C - 1:C, :]
            q = proj_scr[rows, cols]
            v = proj_scr[rows, 2 * hw + hd * dh: 2 * hw + (hd + 1) * dh].astype(BF16)
            g = proj_scr[rows, 3 * hw + hd * dh: 3 * hw + (hd + 1) * dh]
            k = ks[hd]
            q_in = (_silu(q) * jnp.exp(b)).astype(BF16)
            k_in = (k * jnp.exp(-b)).astype(BF16)
            k_dec = (k * jnp.exp(b_last - b)).astype(BF16)
            st = sh_scr[hd]
            a = jnp.where(causal, _mm_nt(q_in, k_in), 0.0)
            o = _mm(a.astype(BF16), v) + _mm_nt(q_in, st.astype(BF16))
            sh_scr[hd] = st * jnp.exp(b_last) + _mm_tn(v, k_dec)
            ms = jnp.mean(o * o, axis=-1, keepdims=True)
            on = o * lax.rsqrt(ms + LN_EPS) * hnw[:, hd * dh:(hd + 1) * dh] * _silu(g)
            o_scr[rows, hd * dh:(hd + 1) * dh] = on.astype(BF16)

    rnw = rnw_ref[...]
    RC = RET_CHUNK
    q_scale = dh ** -0.5

    def retention_unit(sub, hd):
        rows = slice(sub * RC, (sub + 1) * RC)
        cs = cos_scr[rows, :]
        sn = sin_scr[rows, :]
        rq = proj2_scr[rows, hd * dh: (hd + 1) * dh]
        rk = proj2_scr[rows, hw + hd * dh: hw + (hd + 1) * dh]
        v = proj2_scr[rows, 2 * hw + hd * dh: 2 * hw + (hd + 1) * dh].astype(BF16)
        g = proj2_scr[rows, 3 * hw + hd * dh: 3 * hw + (hd + 1) * dh]
        q = (rq * cs + pltpu.roll(rq, dh // 2, 1) * sn) * q_scale
        k = rk * cs + pltpu.roll(rk, dh // 2, 1) * sn
        qb = q.astype(BF16)
        st = sr_scr[hd]
        s = _mm_nt(qb, k.astype(BF16)) * dmat_ref[hd]
        o = _mm(s.astype(BF16), v) + _mm_nt(qb, st.astype(BF16)) * qdec_ref[hd]
        sr_scr[hd] = st * ret_chunk_decay[hd] + _mm_tn(v, (k * kdec_ref[hd]).astype(BF16))
        mu = jnp.mean(o, axis=-1, keepdims=True)
        oc = o - mu
        var = jnp.mean(oc * oc, axis=-1, keepdims=True)
        on = oc * lax.rsqrt(var + LN_EPS) * rnw[:, hd * dh:(hd + 1) * dh] * _silu(g)
        o_scr[rows, hw + hd * dh: hw + (hd + 1) * dh] = on.astype(BF16)

    n_chunks = T // C
    units = [(sub, hd) for sub in range(T // RC) for hd in range(N_HEADS)]
    for c in range(n_chunks):
        hgrn_chunk(c)
        for sub, hd in units[c * len(units) // n_chunks:(c + 1) * len(units) // n_chunks]:
            retention_unit(sub, hd)

    y_scr[...] = _mm(o_scr[...], wout_ref[...])


def _retention_tables():
    h = np.arange(N_HEADS, dtype=np.float64)
    log_gamma = np.log(1.0 - np.exp2(-5.0 - h))
    idx = np.arange(RET_CHUNK, dtype=np.float64)
    rel = idx[:, None] - idx[None, :]
    dmat = np.where(rel >= 0, np.exp(np.maximum(rel, 0.0)[None] * log_gamma[:, None, None]), 0.0)
    qdec = np.exp((idx + 1.0)[None, :] * log_gamma[:, None])
    kdec = np.exp((RET_CHUNK - 1.0 - idx)[None, :] * log_gamma[:, None])
    cdec = np.exp(RET_CHUNK * log_gamma)
    bc = lambda t: np.broadcast_to(t[:, :, None], (N_HEADS, RET_CHUNK, D_HEAD))
    return (jnp.asarray(dmat, F32), jnp.asarray(bc(qdec), F32), jnp.asarray(bc(kdec), F32),
            tuple(float(np.float32(v)) for v in cdec))


def _mixer_call(x, positions, mod3, win, wout, hgrn_lb, hnw, rnw, ln1w, ln1b, wr, br, wg, wu, wd):
    B, S, D = x.shape
    T = SEQ_TILE
    nj = S // T
    dmat, qdec, kdec, cdec = _retention_tables()
    inv_freq = np.power(ROPE_BASE, -np.arange(0, D_HEAD, 2, dtype=np.float64) / D_HEAD)
    invf = jnp.asarray(inv_freq[:, None], F32)
    pos3 = positions.reshape(B, 1, S)
    kern = functools.partial(_mixer_kernel, ret_chunk_decay=cdec, tiles_per_seq=nj)
    n_tiles = B * nj
    cur = lambda s: jnp.minimum(s, n_tiles - 1)
    prev = lambda s: jnp.maximum(s - 1, 0)
    E, _, F = wg.shape
    wg2, wu2, wd2 = wg.reshape(E * D, F), wu.reshape(E * D, F), wd.reshape(E * F, D)
    rg, rd = (E * D) // n_tiles, (E * F) // n_tiles
    assert rg * n_tiles == E * D and rd * n_tiles == E * F and rd % 16 == 0
    wspec_g = pl.BlockSpec((rg, F), lambda s: (cur(s), 0))
    wspec_d = pl.BlockSpec((rd, D), lambda s: (cur(s), 0))
    outs = pl.pallas_call(
        kern,
        grid=(n_tiles + 1,),
        in_specs=[
            pl.BlockSpec((1, T, D), lambda s: (cur(s) // nj, cur(s) % nj, 0)),
            pl.BlockSpec((1, 1, T), lambda s: (cur(s) // nj, 0, cur(s) % nj)),
            pl.BlockSpec((1, 6, D), lambda s: (cur(s) // nj, 0, 0)),
            _const_spec(win.shape), _const_spec(wout.shape), _const_spec(hgrn_lb.shape),
            _const_spec(hnw.shape), _const_spec(rnw.shape), _const_spec(ln1w.shape),
            _const_spec(ln1b.shape), _const_spec(invf.shape), _const_spec(dmat.shape),
            _const_spec(qdec.shape), _const_spec(kdec.shape), _const_spec(wr.shape),
            _const_spec(br.shape),
            pl.BlockSpec((1, T, D), lambda s: (prev(s) // nj, prev(s) % nj, 0)),
            pl.BlockSpec((1, 6, D), lambda s: (prev(s) // nj, 0, 0)),
            wspec_g, wspec_g, wspec_d,
        ],
        out_specs=[
            pl.BlockSpec((1, T, D), lambda s: (prev(s) // nj, prev(s) % nj, 0)),
            pl.BlockSpec((T * SUBLANES, LANES), lambda s: (prev(s), 0)),
            pl.BlockSpec((N_LOGIT_ROWS, T), lambda s: (0, prev(s))),
            wspec_g, wspec_g, wspec_d,
        ],
        out_shape=[
            jax.ShapeDtypeStruct((B, S, D), F32),
            jax.ShapeDtypeStruct((B * S * SUBLANES, LANES), F32),
            jax.ShapeDtypeStruct((N_LOGIT_ROWS, B * S), F32),
            jax.ShapeDtypeStruct(wg2.shape, BF16),
            jax.ShapeDtypeStruct(wu2.shape, BF16),
            jax.ShapeDtypeStruct(wd2.shape, BF16),
        ],
        scratch_shapes=[
            pltpu.VMEM((T, D), BF16),
            pltpu.VMEM((T, 4 * N_HEADS * D_HEAD), F32),
            pltpu.VMEM((T, 4 * N_HEADS * D_HEAD), F32),
            pltpu.VMEM((T, D), BF16),
            pltpu.VMEM((T, D_HEAD), F32),
            pltpu.VMEM((T, D_HEAD), F32),
            pltpu.VMEM((N_HEADS, D_HEAD, D_HEAD), F32),
            pltpu.VMEM((N_HEADS, D_HEAD, D_HEAD), F32),
            pltpu.VMEM((T, D), F32),
            pltpu.VMEM((T, D), BF16),
        ],
        compiler_params=pltpu.CompilerParams(
            dimension_semantics=("arbitrary",), vmem_limit_bytes=VMEM_LIMIT),
        name="mixer",
    )(x, pos3, mod3, win, wout, hgrn_lb, hnw, rnw, ln1w, ln1b, invf, dmat, qdec, kdec, wr, br, x, mod3,
      wg2, wu2, wd2)
    x1, h2t, lgt, wg16, wu16, wd16 = outs
    return x1, h2t, lgt, wg16.reshape(E, D, F), wu16.reshape(E, D, F), wd16.reshape(E, F, D)


def _row_copy(src_ref, dst_ref, src_row, dst_row, sem):
    s0 = pl.multiple_of(src_row * SUBLANES, SUBLANES)
    d0 = pl.multiple_of(dst_row * SUBLANES, SUBLANES)
    return pltpu.make_async_copy(src_ref.at[pl.ds(s0, SUBLANES)], dst_ref.at[pl.ds(d0, SUBLANES)], sem)


def _from_token_tiles(buf_ref, first_row, n_rows):
    return jnp.concatenate(
        [buf_ref[pl.ds(first_row * SUBLANES + s, n_rows, stride=SUBLANES), :] for s in range(SUBLANES)],
        axis=1)


def _start_rows(src_ref, dst_ref, idx_ref, idx_base, dst_base, n_rows, sem):
    def issue(r2, carry):
        for u in range(2):
            r = r2 * 2 + u
            _row_copy(src_ref, dst_ref, idx_ref[idx_base + r], dst_base + r, sem).start(priority=u)
        return carry
    lax.fori_loop(0, n_rows // 2, issue, 0)


def _wait_rows(src_ref, dst_ref, n_rows, sem):
    pltpu.make_async_copy(src_ref.at[pl.ds(0, n_rows * SUBLANES)], dst_ref, sem).wait()


def _bf16_bits(x):
    return lax.bitcast_convert_type(x.astype(BF16).astype(F32), jnp.uint32)


def _invert_kernel(dest_ref, fill_ref, tok_ref, sem):
    n_tok = dest_ref.shape[0]
    g = pl.program_id(0)
    U = 8
    toks_per_step = n_tok // INVERT_STEPS

    @pl.when(g == 0)
    def _():
        init = pltpu.make_async_copy(fill_ref, tok_ref, sem)
        init.start()
        init.wait()

    def fill(i, carry):
        p0 = g * toks_per_step + i * U
        slots = [dest_ref[p0 + u] for u in range(U)]
        for u in range(U):
            tok_ref[slots[u]] = p0 + u
        return carry
    lax.fori_loop(0, toks_per_step // U, fill, 0)


def _invert_call(dest, n_slots):
    n_tok = dest.shape[0]
    assert n_tok % (INVERT_STEPS * 8) == 0
    fill = jnp.arange(n_slots, dtype=jnp.int32) % n_tok
    return pl.pallas_call(
        _invert_kernel,
        grid=(INVERT_STEPS,),
        in_specs=[pl.BlockSpec(memory_space=pltpu.SMEM), pl.BlockSpec(memory_space=pl.ANY)],
        out_specs=pl.BlockSpec(memory_space=pltpu.SMEM),
        out_shape=jax.ShapeDtypeStruct((n_slots,), jnp.int32),
        scratch_shapes=[pltpu.SemaphoreType.DMA],
        compiler_params=pltpu.CompilerParams(dimension_semantics=("arbitrary",)),
        name="invert_slots",
    )(dest, fill)


def _expert_kernel(ea_ref, eb_ref, rows_ref, tok_ref, nv_ref, h2t_ref, wg_ref, wu_ref, wd_ref, y_ref,
                   buf, sems):
    i = pl.program_id(0)
    R = SLOT_BLOCK
    G = EXPERTS_PER_GROUP
    nv = nv_ref[0]
    used = i < nv
    n_rows = rows_ref[i]
    nxt = jnp.minimum(i + 1, nv - 1)

    @pl.when(i == 0)
    def _():
        _start_rows(h2t_ref, buf.at[0], tok_ref, 0, 0, R, sems.at[0])

    for slot in range(2):
        def fetch_next(first, last):
            for q in range(first, last):
                _row_copy(h2t_ref, buf.at[1 - slot], tok_ref[nxt * R + q], q,
                          sems.at[1 - slot]).start(priority=q % 2)

        for k, m in enumerate(BLOCK_ROWS):
            fewer = BLOCK_ROWS[k - 1] if k else 0

            @pl.when(used & (i % 2 == slot) & (n_rows > fewer) & (n_rows <= m))
            def _():
                _wait_rows(h2t_ref, buf.at[slot], R, sems.at[slot])
                xb = _from_token_tiles(buf.at[slot], 0, m).astype(BF16)
                ys = []
                for half, e_ref in enumerate((ea_ref, eb_ref)):
                    e = e_ref[i] & (G - 1)
                    a = _mm(xb, wg_ref[e])
                    u = _mm(xb, wu_ref[e])
                    fetch_next(half * R // 2, half * R // 2 + R // 4)
                    ys.append(_mm((_silu(a) * u).astype(BF16), wd_ref[e]))
                    fetch_next(half * R // 2 + R // 4, (half + 1) * R // 2)
                packed = _bf16_bits(ys[0]) | (_bf16_bits(ys[1]) >> 16)
                for s in range(SUBLANES):
                    y_ref[pl.ds(s, m, stride=SUBLANES), :] = packed[:, s * LANES:(s + 1) * LANES]
                if m < R:
                    y_ref[pl.ds(m * SUBLANES, (R - m) * SUBLANES), :] = jnp.zeros(
                        ((R - m) * SUBLANES, LANES), y_ref.dtype)

                @pl.when(i == nv - 1)
                def _():
                    _wait_rows(h2t_ref, buf.at[1 - slot], R, sems.at[1 - slot])

    @pl.when(jnp.logical_not(used))
    def _():
        y_ref[...] = jnp.zeros_like(y_ref)


def _expert_call(block_ea, block_eb, block_rows, slot_tok, n_valid, h2t, wg, wu, wd):
    assert BLOCK_ROWS[-1] == SLOT_BLOCK and SLOT_BLOCK % 4 == 0
    R = SLOT_BLOCK
    n_blocks = slot_tok.shape[0] // R
    D, F = wg.shape[1], wg.shape[2]
    G = EXPERTS_PER_GROUP
    last = lambda i, nv: jnp.minimum(i, nv[0] - 1)
    grp = lambda i, ea, eb, rows, tok, nv: (ea[last(i, nv)] // G, 0, 0)
    grid_spec = pltpu.PrefetchScalarGridSpec(
        num_scalar_prefetch=5,
        grid=(n_blocks,),
        in_specs=[
            pl.BlockSpec(memory_space=pl.ANY),
            pl.BlockSpec((G, D, F), grp, pipeline_mode=pl.Buffered(1)),
            pl.BlockSpec((G, D, F), grp, pipeline_mode=pl.Buffered(1)),
            pl.BlockSpec((G, F, D), grp, pipeline_mode=pl.Buffered(1)),
        ],
        out_specs=pl.BlockSpec((R * SUBLANES, LANES), lambda i, ea, eb, rows, tok, nv: (i, 0)),
        scratch_shapes=[pltpu.VMEM((2, R * SUBLANES, LANES), F32), pltpu.SemaphoreType.DMA((2,))],
    )
    return pl.pallas_call(
        _expert_kernel,
        grid_spec=grid_spec,
        out_shape=jax.ShapeDtypeStruct((slot_tok.shape[0] * SUBLANES, LANES), jnp.uint32),
        compiler_params=pltpu.CompilerParams(
            dimension_semantics=("arbitrary",), vmem_limit_bytes=VMEM_LIMIT),
        name="experts",
    )(block_ea, block_eb, block_rows, slot_tok, n_valid, h2t, wg, wu, wd)


def _final_kernel(dest_ref, yt_ref, x1_ref, g_ref, mod_ref, w_ref, b_ref, o_ref, buf, sems):
    i = pl.program_id(0)
    n_steps = pl.num_programs(0)
    TT = TOK_TILE

    @pl.when(i == 0)
    def _():
        _start_rows(yt_ref, buf.at[0], dest_ref, 0, 0, TT, sems.at[0])

    nxt = jnp.minimum(i + 1, n_steps - 1)
    gate2 = mod_ref[0, 5:6, :]
    g = g_ref[...]
    for slot in range(2):
        @pl.when(i % 2 == slot)
        def _():
            _wait_rows(yt_ref, buf.at[slot], TT, sems.at[slot])
            for r in range(TT // ROW_BLOCK):
                rows = slice(r * ROW_BLOCK, (r + 1) * ROW_BLOCK)
                packed = _from_token_tiles(buf.at[slot], r * ROW_BLOCK, ROW_BLOCK)
                ya = lax.bitcast_convert_type(packed & jnp.uint32(0xFFFF0000), F32)
                yb = lax.bitcast_convert_type(packed << 16, F32)
                y = ya * g[rows, 0:1] + yb * g[rows, 1:2]
                u = DEEPNORM_ALPHA * x1_ref[rows, :] + gate2 * y
                o_ref[rows, :] = _ln(u) * w_ref[...] + b_ref[...]
                for q in range(r * ROW_BLOCK, (r + 1) * ROW_BLOCK):
                    _row_copy(yt_ref, buf.at[1 - slot], dest_ref[nxt * TT + q], q,
                              sems.at[1 - slot]).start(priority=q % 2)

            @pl.when(i == n_steps - 1)
            def _():
                _wait_rows(yt_ref, buf.at[1 - slot], TT, sems.at[1 - slot])


def _final_call(dest, yt, x1, gates, mod3, w, b, seq_len):
    N, D = x1.shape
    TT = TOK_TILE
    per_seq = seq_len // TT
    grid_spec = pltpu.PrefetchScalarGridSpec(
        num_scalar_prefetch=1,
        grid=(N // TT,),
        in_specs=[
            pl.BlockSpec(memory_space=pl.ANY),
            pl.BlockSpec((TT, D), lambda i, d: (i, 0)),
            pl.BlockSpec((TT, 2), lambda i, d: (i, 0)),
            pl.BlockSpec((1, 6, D), lambda i, d: (i // per_seq, 0, 0)),
            pl.BlockSpec((1, D), lambda i, d: (0, 0)),
            pl.BlockSpec((1, D), lambda i, d: (0, 0)),
        ],
        out_specs=pl.BlockSpec((TT, D), lambda i, d: (i, 0)),
        scratch_shapes=[pltpu.VMEM((2, TT * SUBLANES, LANES), jnp.uint32), pltpu.SemaphoreType.DMA((2,))],
    )
    return pl.pallas_call(
        _final_kernel,
        grid_spec=grid_spec,
        out_shape=jax.ShapeDtypeStruct((N, D), F32),
        compiler_params=pltpu.CompilerParams(
            dimension_semantics=("arbitrary",), vmem_limit_bytes=VMEM_LIMIT),
        name="combine_ln",
    )(dest, yt, x1, gates, mod3, w, b)


def _route_kernel(lgt_ref, rf_ref, ri_ref, meta_ref, cnt_scr, base_scr, bstart_scr):
    ph = pl.program_id(0)
    t = pl.program_id(1)
    TR = ROUTE_TILE
    E = N_EXPERTS
    R = SLOT_BLOCK
    G = EXPERTS_PER_GROUP
    lg = lgt_ref[...]

    g = [lg[E + k:E + k + 1, :] for k in range(N_GROUPS)]
    gmax = jnp.maximum(jnp.maximum(g[0], g[1]), jnp.maximum(g[2], g[3]))
    gs = jnp.where(g[0] == gmax, 0.0, jnp.where(g[1] == gmax, 1.0, jnp.where(g[2] == gmax, 2.0, 3.0)))
    psum = (jnp.exp(g[0] - gmax) + jnp.exp(g[1] - gmax)) + (jnp.exp(g[2] - gmax) + jnp.exp(g[3] - gmax))
    p_star = 1.0 / psum
    esel = jnp.where(gs == 0.0, lg[0:G], jnp.where(gs == 1.0, lg[G:2 * G],
                                                   jnp.where(gs == 2.0, lg[2 * G:3 * G], lg[3 * G:4 * G])))
    sub = lax.broadcasted_iota(jnp.int32, (G, TR), 0).astype(F32)
    m1 = jnp.max(esel, axis=0, keepdims=True)
    i1 = jnp.min(jnp.where(esel == m1, sub, float(G)), axis=0, keepdims=True)
    es2 = jnp.where(sub == i1, -jnp.inf, esel)
    m2 = jnp.max(es2, axis=0, keepdims=True)
    i2 = jnp.min(jnp.where(es2 == m2, sub, float(G)), axis=0, keepdims=True)
    d = jnp.exp(m2 - m1)
    w1 = 1.0 / (1.0 + d)
    w2 = d * w1
    lo = jnp.minimum(i1, i2)
    hi = jnp.maximum(i1, i2)
    bucket = gs * PAIRS_PER_GROUP + (lo * (2 * G - 1 - lo) * 0.5 + (hi - lo - 1.0))
    first_is_lo = i1 < i2
    gate_a = p_star * jnp.where(first_is_lo, w1, w2)
    gate_b = p_star * jnp.where(first_is_lo, w2, w1)
    row = lax.broadcasted_iota(jnp.int32, (NB, TR), 0).astype(F32)
    oh = row == bucket
    cnt = oh.astype(F32)

    @pl.when(ph == 0)
    def _():
        @pl.when(t == 0)
        def _():
            cnt_scr[...] = jnp.zeros_like(cnt_scr)
        acc = cnt_scr[...]
        for j in range(TR // LANES):
            acc = acc + cnt[:, j * LANES:(j + 1) * LANES]
        cnt_scr[...] = acc

    @pl.when((ph == 1) & (t == 0))
    def _():
        counts = jnp.sum(cnt_scr[...], axis=1, keepdims=True)
        nblk = jnp.floor((counts + (R - 1)) * (1.0 / R))
        nblk_b = jnp.broadcast_to(nblk, (NB, LANES))
        rb = lax.broadcasted_iota(jnp.int32, (NB, NB), 0)
        cb = lax.broadcasted_iota(jnp.int32, (NB, NB), 1)
        bstart = _mm((cb < rb).astype(BF16), nblk_b.astype(BF16))
        bstart_scr[...] = bstart
        base_scr[...] = jnp.zeros_like(base_scr)
        k = lax.broadcasted_iota(jnp.int32, (NB, 1), 0).astype(F32)
        grp = sum((k >= float(m * PAIRS_PER_GROUP)).astype(F32) for m in range(1, N_GROUPS))
        p = k - grp * PAIRS_PER_GROUP
        pair_start = [i * (2 * G - 1 - i) // 2 for i in range(G - 1)]
        ia = sum((p >= float(s)).astype(F32) for s in pair_start[1:])
        ib = p - ia * (2 * G - 1 - ia) * 0.5 + ia + 1.0
        n_pad = meta_ref.shape[1]
        blk = lax.broadcasted_iota(jnp.int32, (NB, n_pad), 1).astype(F32)
        first = jnp.broadcast_to(bstart[:, 0:1], (NB, n_pad))
        last = jnp.broadcast_to((bstart + nblk_b)[:, 0:1], (NB, n_pad))
        member = (first <= blk) & (blk < last)
        ea = jnp.sum(jnp.where(member, grp * G + ia, 0.0), axis=0, keepdims=True)
        eb = jnp.sum(jnp.where(member, grp * G + ib, 0.0), axis=0, keepdims=True)
        meta_ref[...] = jnp.zeros_like(meta_ref)
        meta_ref[0:1, :] = ea.astype(jnp.int32)
        meta_ref[1:2, :] = eb.astype(jnp.int32)
        meta_ref[2:3, :] = jnp.max(last, axis=0, keepdims=True).astype(jnp.int32)
        in_blk = jnp.clip(counts - (blk - first) * R, 0.0, float(R))
        meta_ref[5:6, :] = jnp.sum(jnp.where(member, in_blk, 0.0), axis=0, keepdims=True).astype(jnp.int32)
        pad0 = jnp.broadcast_to(bstart[:, 0:1] * R + counts, (NB, NB)).T[0:1, :]
        padn = jnp.broadcast_to(nblk * R - counts, (NB, NB)).T[0:1, :]
        meta_ref[3:4, 0:NB] = pad0.astype(jnp.int32)
        meta_ref[4:5, 0:NB] = padn.astype(jnp.int32)

    @pl.when(ph == 1)
    def _():
        ri = lax.broadcasted_iota(jnp.int32, (LANES, 2 * LANES), 0)
        ci = lax.broadcasted_iota(jnp.int32, (LANES, 2 * LANES), 1)
        w = ((ci >= ri) | (ci >= LANES)).astype(BF16)
        base = base_scr[...]
        slot0 = bstart_scr[...] * R
        ds = []
        for j in range(TR // LANES):
            sl = slice(j * LANES, (j + 1) * LANES)
            blk = cnt[:, sl]
            res = _mm(blk.astype(BF16), w)
            val = slot0 + base + (res[:, 0:LANES] - blk)
            ds.append(jnp.sum(jnp.where(oh[:, sl], val, 0.0), axis=0, keepdims=True))
            base = base + res[:, LANES:2 * LANES]
        base_scr[...] = base
        ri_ref[...] = jnp.zeros_like(ri_ref)
        ri_ref[0:1, :] = jnp.concatenate(ds, axis=1).astype(jnp.int32)
        rf_ref[...] = jnp.zeros_like(rf_ref)
        rf_ref[0:1, :] = gate_a
        rf_ref[1:2, :] = gate_b


def _route_call(lgt, n_blocks):
    rows, N = lgt.shape
    TR = ROUTE_TILE
    E = NB
    n_pad = -(-n_blocks // LANES) * LANES
    return pl.pallas_call(
        _route_kernel,
        grid=(2, N // TR),
        in_specs=[pl.BlockSpec((rows, TR), lambda ph, t: (0, t))],
        out_specs=[
            pl.BlockSpec((SUBLANES, TR), lambda ph, t: (0, ph * t)),
            pl.BlockSpec((SUBLANES, TR), lambda ph, t: (0, ph * t)),
            pl.BlockSpec((SUBLANES, n_pad), lambda ph, t: (0, 0)),
        ],
        out_shape=[
            jax.ShapeDtypeStruct((SUBLANES, N), F32),
            jax.ShapeDtypeStruct((SUBLANES, N), jnp.int32),
            jax.ShapeDtypeStruct((SUBLANES, n_pad), jnp.int32),
        ],
        scratch_shapes=[pltpu.VMEM((E, LANES), F32), pltpu.VMEM((E, LANES), F32), pltpu.VMEM((E, LANES), F32)],
        compiler_params=pltpu.CompilerParams(dimension_semantics=("arbitrary", "arbitrary")),
        name="route",
    )(lgt)


def kernel(x, c, positions, w_ada, b_ada, w_in, w_out, hgrn_lb, hgrn_norm_w, ret_norm_w, post_ln1_w,
           post_ln1_b, w_rg, b_rg, w_re, b_re, w_gate, w_up, w_down, post_ln2_w, post_ln2_b):
    B, S, D = x.shape
    N = B * S
    hw = N_HEADS * D_HEAD

    mod3 = _ada_call(c, w_ada[0], b_ada[0]).reshape(B, 6, D)

    win = w_in[0].astype(BF16)
    wout = w_out[0].astype(BF16)
    pad = LANES - N_EXPERTS - N_GROUPS
    wr = jnp.concatenate([w_re[0], w_rg[0], jnp.zeros((D, pad), F32)], axis=1).astype(BF16)
    br = jnp.concatenate([b_re[0], b_rg[0], jnp.zeros((pad,), F32)]).reshape(1, LANES)

    x1, h2t, lgt, wg16, wu16, wd16 = _mixer_call(
        x, positions, mod3, win, wout, hgrn_lb, hgrn_norm_w[0].reshape(1, hw),
        ret_norm_w[0].reshape(1, hw), post_ln1_w[0].reshape(1, D), post_ln1_b[0].reshape(1, D), wr, br,
        w_gate[0], w_up[0], w_down[0])

    n_blocks = N // SLOT_BLOCK + N_GROUPS * PAIRS_PER_GROUP
    rf, ri, meta = _route_call(lgt, n_blocks)
    dest = ri[0]
    gates = rf[0:2].T
    slot_tok = _invert_call(dest, n_blocks * SLOT_BLOCK)
    yt = _expert_call(meta[0, :n_blocks], meta[1, :n_blocks], meta[5, :n_blocks], slot_tok, meta[2, 0:1],
                      h2t, wg16, wu16, wd16)
    out = _final_call(dest, yt, x1.reshape(N, D), gates, mod3, post_ln2_w[0].reshape(1, D),
                      post_ln2_b[0].reshape(1, D), S)
    return out.reshape(B, S, D)
```

```python
import functools

import numpy as np
import jax
import jax.numpy as jnp
from jax import lax
from jax.experimental import pallas as pl
from jax.experimental.pallas import tpu as pltpu

F32 = jnp.float32
BF16 = jnp.bfloat16

D_MODEL = 1024
N_HEADS = 4
D_HEAD = 128
HGRN_CHUNK = 64
RET_CHUNK = 256
ROPE_BASE = 10000.0
N_EXPERTS = 32
EXPERTS_PER_GROUP = 8
N_GROUPS = 4
DEEPNORM_ALPHA = 2.0 ** 0.25
LN_EPS = 1e-5

SEQ_TILE = 512
ROW_BLOCK = 128
PAIRS_PER_GROUP = EXPERTS_PER_GROUP * (EXPERTS_PER_GROUP - 1) // 2
NB = 128
SLOT_BLOCK = 256
BLOCK_ROWS = (64, 128, 192, 256)
DISPATCH_TILE = 512
DISPATCH_RING = 3
PAD_COPY_SIZES = (128, 64, 32, 16, 8, 4, 2, 1)
TOK_TILE = 512
ROUTE_TILE = 2048
N_LOGIT_ROWS = 40
LANES = 128
SUBLANES = 8
V7X_VMEM_BYTES = 64 * 1024 * 1024
VMEM_LIMIT = V7X_VMEM_BYTES - 8 * 1024 * 1024


def _mm(a, b):
    return jnp.dot(a, b, preferred_element_type=F32)


def _mm_nt(a, b):
    return lax.dot_general(a, b, (((1,), (1,)), ((), ())), preferred_element_type=F32)


def _mm_tn(a, b):
    return lax.dot_general(a, b, (((0,), (0,)), ((), ())), preferred_element_type=F32)


def _sigmoid(x):
    return 1.0 / (1.0 + jnp.exp(-x))


def _silu(x):
    return x * _sigmoid(x)


def _ln(x):
    mu = jnp.mean(x, axis=-1, keepdims=True)
    xc = x - mu
    var = jnp.mean(xc * xc, axis=-1, keepdims=True)
    return xc * lax.rsqrt(var + LN_EPS)


def _const_spec(shape):
    nd = len(shape)
    return pl.BlockSpec(shape, lambda *_: (0,) * nd, pipeline_mode=pl.Buffered(1))


def _ada_kernel(c_ref, w_ref, b_ref, o_ref):
    ca = _silu(c_ref[...])
    o_ref[...] = _mm(ca.astype(BF16), w_ref[...].astype(BF16)) + b_ref[...]


def _ada_call(c, w_ada, b_ada):
    B, D = c.shape
    n_out = w_ada.shape[1]
    return pl.pallas_call(
        _ada_kernel,
        grid=(n_out // D,),
        in_specs=[pl.BlockSpec((B, D), lambda j: (0, 0)),
                  pl.BlockSpec((D, D), lambda j: (0, j)),
                  pl.BlockSpec((1, D), lambda j: (0, j))],
        out_specs=pl.BlockSpec((B, D), lambda j: (0, j)),
        out_shape=jax.ShapeDtypeStruct((B, n_out), F32),
        name="ada_mod",
    )(c, w_ada, b_ada.reshape(1, n_out))


def _mixer_kernel(x_ref, pos_ref, mod_ref, win_ref, wout_ref, lb_ref, hnw_ref, rnw_ref,
                  ln1w_ref, ln1b_ref, invf_ref, dmat_ref, qdec_ref, kdec_ref, wr_ref, br_ref,
                  xp_ref, modp_ref, wg32_ref, wu32_ref, wd32_ref,
                  x1_ref, h2t_ref, lg_ref, wg16_ref, wu16_ref, wd16_ref,
                  h_scr, proj_scr, proj2_scr, o_scr, cos_scr, sin_scr, sh_scr, sr_scr, y_scr, h2_scr,
                  *, ret_chunk_decay, tiles_per_seq):
    T = SEQ_TILE
    D = D_MODEL
    dh = D_HEAD
    hw = N_HEADS * dh
    hw4 = 4 * hw
    step = pl.program_id(0)
    n_tiles = pl.num_programs(0) - 1

    wg16_ref[...] = wg32_ref[...].astype(BF16)
    wu16_ref[...] = wu32_ref[...].astype(BF16)
    wd16_ref[...] = wd32_ref[...].astype(BF16)

    @pl.when(step == 0)
    def _():
        y_scr[...] = jnp.zeros_like(y_scr)

    @pl.when((step < n_tiles) & (step % tiles_per_seq == 0))
    def _():
        sh_scr[...] = jnp.zeros_like(sh_scr)
        sr_scr[...] = jnp.zeros_like(sr_scr)

    gate1p = modp_ref[0, 2:3, :]
    shift2p = modp_ref[0, 3:4, :]
    scale2p = modp_ref[0, 4:5, :]

    def post_block(r):
        rows = slice(r * ROW_BLOCK, (r + 1) * ROW_BLOCK)
        u = DEEPNORM_ALPHA * xp_ref[0, rows, :] + gate1p * y_scr[rows, :]
        x1 = _ln(u) * ln1w_ref[...] + ln1b_ref[...]
        x1_ref[0, rows, :] = x1
        h2 = (_ln(x1) * (1.0 + scale2p) + shift2p).astype(BF16)
        h2_scr[rows, :] = h2
        h2f = h2.astype(F32)
        for s in range(SUBLANES):
            h2t_ref[pl.ds(r * ROW_BLOCK * SUBLANES + s, ROW_BLOCK, stride=SUBLANES), :] = (
                h2f[:, s * LANES:(s + 1) * LANES])

    shift1 = mod_ref[0, 0:1, :]
    scale1 = mod_ref[0, 1:2, :]

    for r in range(T // ROW_BLOCK):
        rows = slice(r * ROW_BLOCK, (r + 1) * ROW_BLOCK)
        h = _ln(x_ref[0, rows, :]) * (1.0 + scale1) + shift1
        h_scr[rows, :] = h.astype(BF16)

    n_post = T // ROW_BLOCK
    proj_scr[...] = _mm(h_scr[...], win_ref[:, 0:hw4])
    for r in range(n_post // 2):
        post_block(r)
    proj2_scr[...] = _mm(h_scr[...], win_ref[:, hw4:2 * hw4])
    for r in range(n_post // 2, n_post):
        post_block(r)
    lg_ref[...] = (_mm(h2_scr[...], wr_ref[...]) + br_ref[...]).T[0:N_LOGIT_ROWS, :]

    pos = pos_ref[0].astype(F32)
    ang_t = invf_ref[...] * pos
    cos_t = jnp.cos(ang_t)
    sin_t = jnp.sin(ang_t)
    cos_scr[...] = jnp.concatenate([cos_t, cos_t], axis=0).T
    sin_scr[...] = jnp.concatenate([-sin_t, sin_t], axis=0).T

    l0 = lb_ref[0:1, :]
    l1 = lb_ref[1:2, :]
    lmax = jnp.maximum(l0, l1)
    e0 = jnp.exp(l0 - lmax)
    e1 = jnp.exp(l1 - lmax)
    lb = e0 / (e0 + e1)
    hnw = hnw_ref[...]

    C = HGRN_CHUNK
    rr = lax.broadcasted_iota(jnp.int32, (C, C), 0)
    cc = lax.broadcasted_iota(jnp.int32, (C, C), 1)
    causal = rr >= cc
    row_id = lax.broadcasted_iota(jnp.int32, (C, hw), 0)

    def hgrn_chunk(c):
        rows = slice(c * C, (c + 1) * C)
        logfs, ks = [], []
        for hd in range(N_HEADS):
            z = proj_scr[rows, hw + hd * dh: hw + (hd + 1) * dh]
            e = jnp.exp(-jnp.abs(z))
            r = 1.0 / (1.0 + e)
            er = e * r
            zp = z >= 0
            sig_pos = jnp.where(zp, r, er)
            sig_neg = jnp.where(zp, er, r)
            lbh = lb[:, hd * dh:(hd + 1) * dh]
            logfs.append(jnp.log(lbh + (1.0 - lbh) * sig_pos))
            ks.append((1.0 - lbh) * sig_neg)
        bsum = jnp.concatenate(logfs, axis=1)
        for k in (1, 2, 4):
            bsum = bsum + jnp.where(row_id >= k, pltpu.roll(bsum, k, 0), 0.0)
        for k in (8, 16, 32):
            bsum = bsum + jnp.concatenate([jnp.zeros((k, hw), F32), bsum[0:C - k, :]], axis=0)
        for hd in range(N_HEADS):
            cols = slice(hd * dh, (hd + 1) * dh)
            b = bsum[:, cols]
            b_last = b[C - 1:C, :]
            q = proj_scr[rows, cols]
            v = proj_scr[rows, 2 * hw + hd * dh: 2 * hw + (hd + 1) * dh].astype(BF16)
            g = proj_scr[rows, 3 * hw + hd * dh: 3 * hw + (hd + 1) * dh]
            k = ks[hd]
            q_in = (_silu(q) * jnp.exp(b)).astype(BF16)
            k_in = (k * jnp.exp(-b)).astype(BF16)
            k_dec = (k * jnp.exp(b_last - b)).astype(BF16)
            st = sh_scr[hd]
            a = jnp.where(causal, _mm_nt(q_in, k_in), 0.0)
            o = _mm(a.astype(BF16), v) + _mm_nt(q_in, st.astype(BF16))
            sh_scr[hd] = st * jnp.exp(b_last) + _mm_tn(v, k_dec)
            ms = jnp.mean(o * o, axis=-1, keepdims=True)
            on = o * lax.rsqrt(ms + LN_EPS) * hnw[:, hd * dh:(hd + 1) * dh] * _silu(g)
            o_scr[rows, hd * dh:(hd + 1) * dh] = on.astype(BF16)

    rnw = rnw_ref[...]
    RC = RET_CHUNK
    q_scale = dh ** -0.5

    def retention_unit(sub, hd):
        rows = slice(sub * RC, (sub + 1) * RC)
        cs = cos_scr[rows, :]
        sn = sin_scr[rows, :]
        rq = proj2_scr[rows, hd * dh: (hd + 1) * dh]
        rk = proj2_scr[rows, hw + hd * dh: hw + (hd + 1) * dh]
        v = proj2_scr[rows, 2 * hw + hd * dh: 2 * hw + (hd + 1) * dh].astype(BF16)
        g = proj2_scr[rows, 3 * hw + hd * dh: 3 * hw + (hd + 1) * dh]
        q = (rq * cs + pltpu.roll(rq, dh // 2, 1) * sn) * q_scale
        k = rk * cs + pltpu.roll(rk, dh // 2, 1) * sn
        qb = q.astype(BF16)
        st = sr_scr[hd]
        s = _mm_nt(qb, k.astype(BF16)) * dmat_ref[hd]
        o = _mm(s.astype(BF16), v) + _mm_nt(qb, st.astype(BF16)) * qdec_ref[hd]
        sr_scr[hd] = st * ret_chunk_decay[hd] + _mm_tn(v, (k * kdec_ref[hd]).astype(BF16))
        mu = jnp.mean(o, axis=-1, keepdims=True)
        oc = o - mu
        var = jnp.mean(oc * oc, axis=-1, keepdims=True)
        on = oc * lax.rsqrt(var + LN_EPS) * rnw[:, hd * dh:(hd + 1) * dh] * _silu(g)
        o_scr[rows, hw + hd * dh: hw + (hd + 1) * dh] = on.astype(BF16)

    n_chunks = T // C
    units = [(sub, hd) for sub in range(T // RC) for hd in range(N_HEADS)]
    for c in range(n_chunks):
        hgrn_chunk(c)
        for sub, hd in units[c * len(units) // n_chunks:(c + 1) * len(units) // n_chunks]:
            retention_unit(sub, hd)

    y_scr[...] = _mm(o_scr[...], wout_ref[...])


def _retention_tables():
    h = np.arange(N_HEADS, dtype=np.float64)
    log_gamma = np.log(1.0 - np.exp2(-5.0 - h))
    idx = np.arange(RET_CHUNK, dtype=np.float64)
    rel = idx[:, None] - idx[None, :]
    dmat = np.where(rel >= 0, np.exp(np.maximum(rel, 0.0)[None] * log_gamma[:, None, None]), 0.0)
    qdec = np.exp((idx + 1.0)[None, :] * log_gamma[:, None])
    kdec = np.exp((RET_CHUNK - 1.0 - idx)[None, :] * log_gamma[:, None])
    cdec = np.exp(RET_CHUNK * log_gamma)
    bc = lambda t: np.broadcast_to(t[:, :, None], (N_HEADS, RET_CHUNK, D_HEAD))
    return (jnp.asarray(dmat, F32), jnp.asarray(bc(qdec), F32), jnp.asarray(bc(kdec), F32),
            tuple(float(np.float32(v)) for v in cdec))


def _mixer_call(x, positions, mod3, win, wout, hgrn_lb, hnw, rnw, ln1w, ln1b, wr, br, wg, wu, wd):
    B, S, D = x.shape
    T = SEQ_TILE
    nj = S // T
    dmat, qdec, kdec, cdec = _retention_tables()
    inv_freq = np.power(ROPE_BASE, -np.arange(0, D_HEAD, 2, dtype=np.float64) / D_HEAD)
    invf = jnp.asarray(inv_freq[:, None], F32)
    pos3 = positions.reshape(B, 1, S)
    kern = functools.partial(_mixer_kernel, ret_chunk_decay=cdec, tiles_per_seq=nj)
    n_tiles = B * nj
    cur = lambda s: jnp.minimum(s, n_tiles - 1)
    prev = lambda s: jnp.maximum(s - 1, 0)
    E, _, F = wg.shape
    wg2, wu2, wd2 = wg.reshape(E * D, F), wu.reshape(E * D, F), wd.reshape(E * F, D)
    rg, rd = (E * D) // n_tiles, (E * F) // n_tiles
    assert rg * n_tiles == E * D and rd * n_tiles == E * F and rd % 16 == 0
    wspec_g = pl.BlockSpec((rg, F), lambda s: (cur(s), 0))
    wspec_d = pl.BlockSpec((rd, D), lambda s: (cur(s), 0))
    outs = pl.pallas_call(
        kern,
        grid=(n_tiles + 1,),
        in_specs=[
            pl.BlockSpec((1, T, D), lambda s: (cur(s) // nj, cur(s) % nj, 0)),
            pl.BlockSpec((1, 1, T), lambda s: (cur(s) // nj, 0, cur(s) % nj)),
            pl.BlockSpec((1, 6, D), lambda s: (cur(s) // nj, 0, 0)),
            _const_spec(win.shape), _const_spec(wout.shape), _const_spec(hgrn_lb.shape),
            _const_spec(hnw.shape), _const_spec(rnw.shape), _const_spec(ln1w.shape),
            _const_spec(ln1b.shape), _const_spec(invf.shape), _const_spec(dmat.shape),
            _const_spec(qdec.shape), _const_spec(kdec.shape), _const_spec(wr.shape),
            _const_spec(br.shape),
            pl.BlockSpec((1, T, D), lambda s: (prev(s) // nj, prev(s) % nj, 0)),
            pl.BlockSpec((1, 6, D), lambda s: (prev(s) // nj, 0, 0)),
            wspec_g, wspec_g, wspec_d,
        ],
        out_specs=[
            pl.BlockSpec((1, T, D), lambda s: (prev(s) // nj, prev(s) % nj, 0)),
            pl.BlockSpec((T * SUBLANES, LANES), lambda s: (prev(s), 0)),
            pl.BlockSpec((N_LOGIT_ROWS, T), lambda s: (0, prev(s))),
            wspec_g, wspec_g, wspec_d,
        ],
        out_shape=[
            jax.ShapeDtypeStruct((B, S, D), F32),
            jax.ShapeDtypeStruct((B * S * SUBLANES, LANES), F32),
            jax.ShapeDtypeStruct((N_LOGIT_ROWS, B * S), F32),
            jax.ShapeDtypeStruct(wg2.shape, BF16),
            jax.ShapeDtypeStruct(wu2.shape, BF16),
            jax.ShapeDtypeStruct(wd2.shape, BF16),
        ],
        scratch_shapes=[
            pltpu.VMEM((T, D), BF16),
            pltpu.VMEM((T, 4 * N_HEADS * D_HEAD), F32),
            pltpu.VMEM((T, 4 * N_HEADS * D_HEAD), F32),
            pltpu.VMEM((T, D), BF16),
            pltpu.VMEM((T, D_HEAD), F32),
            pltpu.VMEM((T, D_HEAD), F32),
            pltpu.VMEM((N_HEADS, D_HEAD, D_HEAD), F32),
            pltpu.VMEM((N_HEADS, D_HEAD, D_HEAD), F32),
            pltpu.VMEM((T, D), F32),
            pltpu.VMEM((T, D), BF16),
        ],
        compiler_params=pltpu.CompilerParams(
            dimension_semantics=("arbitrary",), vmem_limit_bytes=VMEM_LIMIT),
        name="mixer",
    )(x, pos3, mod3, win, wout, hgrn_lb, hnw, rnw, ln1w, ln1b, invf, dmat, qdec, kdec, wr, br, x, mod3,
      wg2, wu2, wd2)
    x1, h2t, lgt, wg16, wu16, wd16 = outs
    return x1, h2t, lgt, wg16.reshape(E, D, F), wu16.reshape(E, D, F), wd16.reshape(E, F, D)


def _row_copy(src_ref, dst_ref, src_row, dst_row, sem):
    s0 = pl.multiple_of(src_row * SUBLANES, SUBLANES)
    d0 = pl.multiple_of(dst_row * SUBLANES, SUBLANES)
    return pltpu.make_async_copy(src_ref.at[pl.ds(s0, SUBLANES)], dst_ref.at[pl.ds(d0, SUBLANES)], sem)


def _from_token_tiles(buf_ref, first_row, n_rows):
    return jnp.concatenate(
        [buf_ref[pl.ds(first_row * SUBLANES + s, n_rows, stride=SUBLANES), :] for s in range(SUBLANES)],
        axis=1)


def _start_rows(src_ref, dst_ref, idx_ref, idx_base, dst_base, n_rows, sem):
    def issue(r2, carry):
        for u in range(2):
            r = r2 * 2 + u
            _row_copy(src_ref, dst_ref, idx_ref[idx_base + r], dst_base + r, sem).start(priority=u)
        return carry
    lax.fori_loop(0, n_rows // 2, issue, 0)


def _wait_rows(src_ref, dst_ref, n_rows, sem):
    pltpu.make_async_copy(src_ref.at[pl.ds(0, n_rows * SUBLANES)], dst_ref, sem).wait()


def _bf16_bits(x):
    return lax.bitcast_convert_type(x.astype(BF16).astype(F32), jnp.uint32)


def _dispatch_kernel(dest_ref, pad0_ref, padn_ref, nv_ref, h2t_ref, xs_ref,
                     buf, zbuf, load_sems, scat_sems, zsem):
    i = pl.program_id(0)
    n = pl.num_programs(0)
    TT = DISPATCH_TILE
    rows = TT * SUBLANES

    def pad_copies(wait):
        def bucket(b, carry):
            pos = pad0_ref[b]
            left = padn_ref[b]
            for size in PAD_COPY_SIZES:
                hit = (left & size) != 0
                first = 0 if wait else pl.multiple_of(pos * SUBLANES, SUBLANES)
                copy = pltpu.make_async_copy(
                    zbuf.at[pl.ds(0, size * SUBLANES)], xs_ref.at[pl.ds(first, size * SUBLANES)], zsem)

                @pl.when(hit)
                def _():
                    copy.wait() if wait else copy.start()
                pos = pos + jnp.where(hit, size, 0)
            return carry
        lax.fori_loop(0, pad0_ref.shape[0], bucket, 0)

        zrows = PAD_COPY_SIZES[0] * SUBLANES

        def unused_block(blk, carry):
            for part in range(SLOT_BLOCK // PAD_COPY_SIZES[0]):
                first = 0 if wait else pl.multiple_of(blk * SLOT_BLOCK * SUBLANES + part * zrows, zrows)
                copy = pltpu.make_async_copy(zbuf, xs_ref.at[pl.ds(first, zrows)], zsem)
                copy.wait() if wait else copy.start()
            return carry
        lax.fori_loop(nv_ref[0], xs_ref.shape[0] // (SLOT_BLOCK * SUBLANES), unused_block, 0)

    @pl.when(i == 0)
    def _():
        zbuf[...] = jnp.zeros_like(zbuf)
        pad_copies(wait=False)

    def load(tile, slot):
        src = h2t_ref.at[pl.ds(pl.multiple_of(tile * rows, rows), rows)]
        return pltpu.make_async_copy(src, buf.at[slot], load_sems.at[slot])

    def wait_scatter(slot):
        _wait_rows(buf.at[slot], xs_ref.at[pl.ds(0, rows)], TT, scat_sems.at[slot])

    @pl.when(i == 0)
    def _():
        load(0, 0).start()

    for slot in range(DISPATCH_RING):
        @pl.when((i + 1 < n) & ((i + 1) % DISPATCH_RING == slot))
        def _():
            @pl.when(i + 1 >= DISPATCH_RING)
            def _():
                wait_scatter(slot)
            load(i + 1, slot).start()

    for slot in range(DISPATCH_RING):
        @pl.when(i % DISPATCH_RING == slot)
        def _():
            load(i, slot).wait()

            for r in range(TT):
                _row_copy(buf.at[slot], xs_ref, r, dest_ref[i * TT + r],
                          scat_sems.at[slot]).start(priority=r % 2)

    @pl.when(i == n - 1)
    def _():
        for slot in range(DISPATCH_RING):
            wait_scatter(slot)
        pad_copies(wait=True)


def _dispatch_call(dest, pad_start, pad_len, n_valid, h2t, n_slots):
    N = dest.shape[0]
    TT = DISPATCH_TILE
    assert N // TT >= DISPATCH_RING and SLOT_BLOCK <= 2 * PAD_COPY_SIZES[0]
    assert SLOT_BLOCK % PAD_COPY_SIZES[0] == 0
    grid_spec = pltpu.PrefetchScalarGridSpec(
        num_scalar_prefetch=4,
        grid=(N // TT,),
        in_specs=[pl.BlockSpec(memory_space=pl.ANY)],
        out_specs=pl.BlockSpec(memory_space=pl.ANY),
        scratch_shapes=[pltpu.VMEM((DISPATCH_RING, TT * SUBLANES, LANES), F32),
                        pltpu.VMEM((PAD_COPY_SIZES[0] * SUBLANES, LANES), F32),
                        pltpu.SemaphoreType.DMA((DISPATCH_RING,)),
                        pltpu.SemaphoreType.DMA((DISPATCH_RING,)),
                        pltpu.SemaphoreType.DMA],
    )
    return pl.pallas_call(
        _dispatch_kernel,
        grid_spec=grid_spec,
        out_shape=jax.ShapeDtypeStruct((n_slots * SUBLANES, LANES), F32),
        compiler_params=pltpu.CompilerParams(dimension_semantics=("arbitrary",)),
        name="dispatch",
    )(dest, pad_start, pad_len, n_valid, h2t)


def _expert_kernel(ea_ref, eb_ref, rows_ref, nv_ref, xs_ref, wg_ref, wu_ref, wd_ref, y_ref):
    i = pl.program_id(0)
    R = SLOT_BLOCK
    G = EXPERTS_PER_GROUP
    used = i < nv_ref[0]
    n_rows = rows_ref[i]

    def ffn(xb, e):
        a = _mm(xb, wg_ref[e])
        u = _mm(xb, wu_ref[e])
        return _mm((_silu(a) * u).astype(BF16), wd_ref[e])

    for k, m in enumerate(BLOCK_ROWS):
        fewer = BLOCK_ROWS[k - 1] if k else 0

        @pl.when(used & (n_rows > fewer) & (n_rows <= m))
        def _():
            xb = _from_token_tiles(xs_ref, 0, m).astype(BF16)
            ya = ffn(xb, ea_ref[i] & (G - 1))
            yb = ffn(xb, eb_ref[i] & (G - 1))
            packed = _bf16_bits(ya) | (_bf16_bits(yb) >> 16)
            for s in range(SUBLANES):
                y_ref[pl.ds(s, m, stride=SUBLANES), :] = packed[:, s * LANES:(s + 1) * LANES]
            if m < R:
                y_ref[pl.ds(m * SUBLANES, (R - m) * SUBLANES), :] = jnp.zeros(
                    ((R - m) * SUBLANES, LANES), y_ref.dtype)

    @pl.when(jnp.logical_not(used))
    def _():
        y_ref[...] = jnp.zeros_like(y_ref)


def _expert_call(block_ea, block_eb, block_rows, n_valid, xs, wg, wu, wd):
    assert BLOCK_ROWS[-1] == SLOT_BLOCK
    R = SLOT_BLOCK
    n_blocks = xs.shape[0] // (R * SUBLANES)
    D, F = wg.shape[1], wg.shape[2]
    G = EXPERTS_PER_GROUP
    last = lambda i, nv: jnp.minimum(i, nv[0] - 1)
    grp = lambda i, ea, eb, rows, nv: (ea[last(i, nv)] // G, 0, 0)
    blk = lambda i, ea, eb, rows, nv: (last(i, nv), 0)
    grid_spec = pltpu.PrefetchScalarGridSpec(
        num_scalar_prefetch=4,
        grid=(n_blocks,),
        in_specs=[
            pl.BlockSpec((R * SUBLANES, LANES), blk),
            pl.BlockSpec((G, D, F), grp, pipeline_mode=pl.Buffered(1)),
            pl.BlockSpec((G, D, F), grp, pipeline_mode=pl.Buffered(1)),
            pl.BlockSpec((G, F, D), grp, pipeline_mode=pl.Buffered(1)),
        ],
        out_specs=pl.BlockSpec((R * SUBLANES, LANES), lambda i, ea, eb, rows, nv: (i, 0)),
    )
    return pl.pallas_call(
        _expert_kernel,
        grid_spec=grid_spec,
        out_shape=jax.ShapeDtypeStruct(xs.shape, jnp.uint32),
        compiler_params=pltpu.CompilerParams(
            dimension_semantics=("arbitrary",), vmem_limit_bytes=VMEM_LIMIT),
        name="experts",
    )(block_ea, block_eb, block_rows, n_valid, xs, wg, wu, wd)


def _final_kernel(dest_ref, yt_ref, x1_ref, g_ref, mod_ref, w_ref, b_ref, o_ref, buf, sems):
    i = pl.program_id(0)
    n_steps = pl.num_programs(0)
    TT = TOK_TILE

    @pl.when(i == 0)
    def _():
        _start_rows(yt_ref, buf.at[0], dest_ref, 0, 0, TT, sems.at[0])

    nxt = jnp.minimum(i + 1, n_steps - 1)
    gate2 = mod_ref[0, 5:6, :]
    g = g_ref[...]
    for slot in range(2):
        @pl.when(i % 2 == slot)
        def _():
            _wait_rows(yt_ref, buf.at[slot], TT, sems.at[slot])
            for r in range(TT // ROW_BLOCK):
                rows = slice(r * ROW_BLOCK, (r + 1) * ROW_BLOCK)
                packed = _from_token_tiles(buf.at[slot], r * ROW_BLOCK, ROW_BLOCK)
                ya = lax.bitcast_convert_type(packed & jnp.uint32(0xFFFF0000), F32)
                yb = lax.bitcast_convert_type(packed << 16, F32)
                y = ya * g[rows, 0:1] + yb * g[rows, 1:2]
                u = DEEPNORM_ALPHA * x1_ref[rows, :] + gate2 * y
                o_ref[rows, :] = _ln(u) * w_ref[...] + b_ref[...]
                for q in range(r * ROW_BLOCK, (r + 1) * ROW_BLOCK):
                    _row_copy(yt_ref, buf.at[1 - slot], dest_ref[nxt * TT + q], q,
                              sems.at[1 - slot]).start(priority=q % 2)

            @pl.when(i == n_steps - 1)
            def _():
                _wait_rows(yt_ref, buf.at[1 - slot], TT, sems.at[1 - slot])


def _final_call(dest, yt, x1, gates, mod3, w, b, seq_len):
    N, D = x1.shape
    TT = TOK_TILE
    per_seq = seq_len // TT
    grid_spec = pltpu.PrefetchScalarGridSpec(
        num_scalar_prefetch=1,
        grid=(N // TT,),
        in_specs=[
            pl.BlockSpec(memory_space=pl.ANY),
            pl.BlockSpec((TT, D), lambda i, d: (i, 0)),
            pl.BlockSpec((TT, 2), lambda i, d: (i, 0)),
            pl.BlockSpec((1, 6, D), lambda i, d: (i // per_seq, 0, 0)),
            pl.BlockSpec((1, D), lambda i, d: (0, 0)),
            pl.BlockSpec((1, D), lambda i, d: (0, 0)),
        ],
        out_specs=pl.BlockSpec((TT, D), lambda i, d: (i, 0)),
        scratch_shapes=[pltpu.VMEM((2, TT * SUBLANES, LANES), jnp.uint32), pltpu.SemaphoreType.DMA((2,))],
    )
    return pl.pallas_call(
        _final_kernel,
        grid_spec=grid_spec,
        out_shape=jax.ShapeDtypeStruct((N, D), F32),
        compiler_params=pltpu.CompilerParams(
            dimension_semantics=("arbitrary",), vmem_limit_bytes=VMEM_LIMIT),
        name="combine_ln",
    )(dest, yt, x1, gates, mod3, w, b)


def _route_kernel(lgt_ref, rf_ref, ri_ref, meta_ref, cnt_scr, base_scr, bstart_scr):
    ph = pl.program_id(0)
    t = pl.program_id(1)
    TR = ROUTE_TILE
    E = N_EXPERTS
    R = SLOT_BLOCK
    G = EXPERTS_PER_GROUP
    lg = lgt_ref[...]

    g = [lg[E + k:E + k + 1, :] for k in range(N_GROUPS)]
    gmax = jnp.maximum(jnp.maximum(g[0], g[1]), jnp.maximum(g[2], g[3]))
    gs = jnp.where(g[0] == gmax, 0.0, jnp.where(g[1] == gmax, 1.0, jnp.where(g[2] == gmax, 2.0, 3.0)))
    psum = (jnp.exp(g[0] - gmax) + jnp.exp(g[1] - gmax)) + (jnp.exp(g[2] - gmax) + jnp.exp(g[3] - gmax))
    p_star = 1.0 / psum
    esel = jnp.where(gs == 0.0, lg[0:G], jnp.where(gs == 1.0, lg[G:2 * G],
                                                   jnp.where(gs == 2.0, lg[2 * G:3 * G], lg[3 * G:4 * G])))
    sub = lax.broadcasted_iota(jnp.int32, (G, TR), 0).astype(F32)
    m1 = jnp.max(esel, axis=0, keepdims=True)
    i1 = jnp.min(jnp.where(esel == m1, sub, float(G)), axis=0, keepdims=True)
    es2 = jnp.where(sub == i1, -jnp.inf, esel)
    m2 = jnp.max(es2, axis=0, keepdims=True)
    i2 = jnp.min(jnp.where(es2 == m2, sub, float(G)), axis=0, keepdims=True)
    d = jnp.exp(m2 - m1)
    w1 = 1.0 / (1.0 + d)
    w2 = d * w1
    lo = jnp.minimum(i1, i2)
    hi = jnp.maximum(i1, i2)
    bucket = gs * PAIRS_PER_GROUP + (lo * (2 * G - 1 - lo) * 0.5 + (hi - lo - 1.0))
    first_is_lo = i1 < i2
    gate_a = p_star * jnp.where(first_is_lo, w1, w2)
    gate_b = p_star * jnp.where(first_is_lo, w2, w1)
    row = lax.broadcasted_iota(jnp.int32, (NB, TR), 0).astype(F32)
    oh = row == bucket
    cnt = oh.astype(F32)

    @pl.when(ph == 0)
    def _():
        @pl.when(t == 0)
        def _():
            cnt_scr[...] = jnp.zeros_like(cnt_scr)
        acc = cnt_scr[...]
        for j in range(TR // LANES):
            acc = acc + cnt[:, j * LANES:(j + 1) * LANES]
        cnt_scr[...] = acc

    @pl.when((ph == 1) & (t == 0))
    def _():
        counts = jnp.sum(cnt_scr[...], axis=1, keepdims=True)
        nblk = jnp.floor((counts + (R - 1)) * (1.0 / R))
        nblk_b = jnp.broadcast_to(nblk, (NB, LANES))
        rb = lax.broadcasted_iota(jnp.int32, (NB, NB), 0)
        cb = lax.broadcasted_iota(jnp.int32, (NB, NB), 1)
        bstart = _mm((cb < rb).astype(BF16), nblk_b.astype(BF16))
        bstart_scr[...] = bstart
        base_scr[...] = jnp.zeros_like(base_scr)
        k = lax.broadcasted_iota(jnp.int32, (NB, 1), 0).astype(F32)
        grp = sum((k >= float(m * PAIRS_PER_GROUP)).astype(F32) for m in range(1, N_GROUPS))
        p = k - grp * PAIRS_PER_GROUP
        pair_start = [i * (2 * G - 1 - i) // 2 for i in range(G - 1)]
        ia = sum((p >= float(s)).astype(F32) for s in pair_start[1:])
        ib = p - ia * (2 * G - 1 - ia) * 0.5 + ia + 1.0
        n_pad = meta_ref.shape[1]
        blk = lax.broadcasted_iota(jnp.int32, (NB, n_pad), 1).astype(F32)
        first = jnp.broadcast_to(bstart[:, 0:1], (NB, n_pad))
        last = jnp.broadcast_to((bstart + nblk_b)[:, 0:1], (NB, n_pad))
        member = (first <= blk) & (blk < last)
        ea = jnp.sum(jnp.where(member, grp * G + ia, 0.0), axis=0, keepdims=True)
        eb = jnp.sum(jnp.where(member, grp * G + ib, 0.0), axis=0, keepdims=True)
        meta_ref[...] = jnp.zeros_like(meta_ref)
        meta_ref[0:1, :] = ea.astype(jnp.int32)
        meta_ref[1:2, :] = eb.astype(jnp.int32)
        meta_ref[2:3, :] = jnp.max(last, axis=0, keepdims=True).astype(jnp.int32)
        in_blk = jnp.clip(counts - (blk - first) * R, 0.0, float(R))
        meta_ref[5:6, :] = jnp.sum(jnp.where(member, in_blk, 0.0), axis=0, keepdims=True).astype(jnp.int32)
        pad0 = jnp.broadcast_to(bstart[:, 0:1] * R + counts, (NB, NB)).T[0:1, :]
        padn = jnp.broadcast_to(nblk * R - counts, (NB, NB)).T[0:1, :]
        meta_ref[3:4, 0:NB] = pad0.astype(jnp.int32)
        meta_ref[4:5, 0:NB] = padn.astype(jnp.int32)

    @pl.when(ph == 1)
    def _():
        ri = lax.broadcasted_iota(jnp.int32, (LANES, 2 * LANES), 0)
        ci = lax.broadcasted_iota(jnp.int32, (LANES, 2 * LANES), 1)
        w = ((ci >= ri) | (ci >= LANES)).astype(BF16)
        base = base_scr[...]
        slot0 = bstart_scr[...] * R
        ds = []
        for j in range(TR // LANES):
            sl = slice(j * LANES, (j + 1) * LANES)
            blk = cnt[:, sl]
            res = _mm(blk.astype(BF16), w)
            val = slot0 + base + (res[:, 0:LANES] - blk)
            ds.append(jnp.sum(jnp.where(oh[:, sl], val, 0.0), axis=0, keepdims=True))
            base = base + res[:, LANES:2 * LANES]
        base_scr[...] = base
        ri_ref[...] = jnp.zeros_like(ri_ref)
        ri_ref[0:1, :] = jnp.concatenate(ds, axis=1).astype(jnp.int32)
        rf_ref[...] = jnp.zeros_like(rf_ref)
        rf_ref[0:1, :] = gate_a
        rf_ref[1:2, :] = gate_b


def _route_call(lgt, n_blocks):
    rows, N = lgt.shape
    TR = ROUTE_TILE
    E = NB
    n_pad = -(-n_blocks // LANES) * LANES
    return pl.pallas_call(
        _route_kernel,
        grid=(2, N // TR),
        in_specs=[pl.BlockSpec((rows, TR), lambda ph, t: (0, t))],
        out_specs=[
            pl.BlockSpec((SUBLANES, TR), lambda ph, t: (0, ph * t)),
            pl.BlockSpec((SUBLANES, TR), lambda ph, t: (0, ph * t)),
            pl.BlockSpec((SUBLANES, n_pad), lambda ph, t: (0, 0)),
        ],
        out_shape=[
            jax.ShapeDtypeStruct((SUBLANES, N), F32),
            jax.ShapeDtypeStruct((SUBLANES, N), jnp.int32),
            jax.ShapeDtypeStruct((SUBLANES, n_pad), jnp.int32),
        ],
        scratch_shapes=[pltpu.VMEM((E, LANES), F32), pltpu.VMEM((E, LANES), F32), pltpu.VMEM((E, LANES), F32)],
        compiler_params=pltpu.CompilerParams(dimension_semantics=("arbitrary", "arbitrary")),
        name="route",
    )(lgt)


def kernel(x, c, positions, w_ada, b_ada, w_in, w_out, hgrn_lb, hgrn_norm_w, ret_norm_w, post_ln1_w,
           post_ln1_b, w_rg, b_rg, w_re, b_re, w_gate, w_up, w_down, post_ln2_w, post_ln2_b):
    B, S, D = x.shape
    N = B * S
    hw = N_HEADS * D_HEAD

    mod3 = _ada_call(c, w_ada[0], b_ada[0]).reshape(B, 6, D)

    win = w_in[0].astype(BF16)
    wout = w_out[0].astype(BF16)
    pad = LANES - N_EXPERTS - N_GROUPS
    wr = jnp.concatenate([w_re[0], w_rg[0], jnp.zeros((D, pad), F32)], axis=1).astype(BF16)
    br = jnp.concatenate([b_re[0], b_rg[0], jnp.zeros((pad,), F32)]).reshape(1, LANES)

    x1, h2t, lgt, wg16, wu16, wd16 = _mixer_call(
        x, positions, mod3, win, wout, hgrn_lb, hgrn_norm_w[0].reshape(1, hw),
        ret_norm_w[0].reshape(1, hw), post_ln1_w[0].reshape(1, D), post_ln1_b[0].reshape(1, D), wr, br,
        w_gate[0], w_up[0], w_down[0])

    n_blocks = N // SLOT_BLOCK + N_GROUPS * PAIRS_PER_GROUP
    rf, ri, meta = _route_call(lgt, n_blocks)
    dest = ri[0]
    gates = rf[0:2].T
    xs = _dispatch_call(dest, meta[3, :NB], meta[4, :NB], meta[2, 0:1], h2t, n_blocks * SLOT_BLOCK)
    yt = _expert_call(meta[0, :n_blocks], meta[1, :n_blocks], meta[5, :n_blocks], meta[2, 0:1], xs,
                      wg16, wu16, wd16)
    out = _final_call(dest, yt, x1.reshape(N, D), gates, mod3, post_ln2_w[0].reshape(1, D),
                      post_ln2_b[0].reshape(1, D), S)
    return out.reshape(B, S, D)
```

```python
import functools

import numpy as np
import jax
import jax.numpy as jnp
from jax import lax
from jax.experimental import pallas as pl
from jax.experimental.pallas import tpu as pltpu

F32 = jnp.float32
BF16 = jnp.bfloat16

D_MODEL = 1024
N_HEADS = 4
D_HEAD = 128
HGRN_CHUNK = 64
RET_CHUNK = 256
ROPE_BASE = 10000.0
N_EXPERTS = 32
EXPERTS_PER_GROUP = 8
N_GROUPS = 4
DEEPNORM_ALPHA = 2.0 ** 0.25
LN_EPS = 1e-5

SEQ_TILE = 512
ROW_BLOCK = 128
PAIRS_PER_GROUP = EXPERTS_PER_GROUP * (EXPERTS_PER_GROUP - 1) // 2
NB = 128
SLOT_BLOCK = 256
BLOCK_ROWS = (32, 64, 96, 128, 160, 192, 224, 256)
DISPATCH_TILE = 512
DISPATCH_RING = 3
PAD_COPY_SIZES = (128, 64, 32, 16, 8, 4, 2, 1)
TOK_TILE = 512
ROUTE_TILE = 2048
N_LOGIT_ROWS = 40
LANES = 128
SUBLANES = 8
V7X_VMEM_BYTES = 64 * 1024 * 1024
VMEM_LIMIT = V7X_VMEM_BYTES - 8 * 1024 * 1024


def _mm(a, b):
    return jnp.dot(a, b, preferred_element_type=F32)


def _mm_nt(a, b):
    return lax.dot_general(a, b, (((1,), (1,)), ((), ())), preferred_element_type=F32)


def _mm_tn(a, b):
    return lax.dot_general(a, b, (((0,), (0,)), ((), ())), preferred_element_type=F32)


def _sigmoid(x):
    return 1.0 / (1.0 + jnp.exp(-x))


def _silu(x):
    return x * _sigmoid(x)


def _ln(x):
    mu = jnp.mean(x, axis=-1, keepdims=True)
    xc = x - mu
    var = jnp.mean(xc * xc, axis=-1, keepdims=True)
    return xc * lax.rsqrt(var + LN_EPS)


def _const_spec(shape):
    nd = len(shape)
    return pl.BlockSpec(shape, lambda *_: (0,) * nd, pipeline_mode=pl.Buffered(1))


def _ada_kernel(c_ref, w_ref, b_ref, o_ref):
    ca = _silu(c_ref[...])
    o_ref[...] = _mm(ca.astype(BF16), w_ref[...].astype(BF16)) + b_ref[...]


def _ada_call(c, w_ada, b_ada):
    B, D = c.shape
    n_out = w_ada.shape[1]
    return pl.pallas_call(
        _ada_kernel,
        grid=(n_out // D,),
        in_specs=[pl.BlockSpec((B, D), lambda j: (0, 0)),
                  pl.BlockSpec((D, D), lambda j: (0, j)),
                  pl.BlockSpec((1, D), lambda j: (0, j))],
        out_specs=pl.BlockSpec((B, D), lambda j: (0, j)),
        out_shape=jax.ShapeDtypeStruct((B, n_out), F32),
        name="ada_mod",
    )(c, w_ada, b_ada.reshape(1, n_out))


def _mixer_kernel(x_ref, pos_ref, mod_ref, win_ref, wout_ref, lb_ref, hnw_ref, rnw_ref,
                  ln1w_ref, ln1b_ref, invf_ref, dmat_ref, qdec_ref, kdec_ref, wr_ref, br_ref,
                  xp_ref, modp_ref, wg32_ref, wu32_ref, wd32_ref,
                  x1_ref, h2t_ref, lg_ref, wg16_ref, wu16_ref, wd16_ref,
                  h_scr, proj_scr, proj2_scr, o_scr, cos_scr, sin_scr, sh_scr, sr_scr, y_scr, h2_scr,
                  *, ret_chunk_decay, tiles_per_seq):
    T = SEQ_TILE
    D = D_MODEL
    dh = D_HEAD
    hw = N_HEADS * dh
    hw4 = 4 * hw
    step = pl.program_id(0)
    n_tiles = pl.num_programs(0) - 1

    wg16_ref[...] = wg32_ref[...].astype(BF16)
    wu16_ref[...] = wu32_ref[...].astype(BF16)
    wd16_ref[...] = wd32_ref[...].astype(BF16)

    @pl.when(step == 0)
    def _():
        y_scr[...] = jnp.zeros_like(y_scr)

    @pl.when((step < n_tiles) & (step % tiles_per_seq == 0))
    def _():
        sh_scr[...] = jnp.zeros_like(sh_scr)
        sr_scr[...] = jnp.zeros_like(sr_scr)

    gate1p = modp_ref[0, 2:3, :]
    shift2p = modp_ref[0, 3:4, :]
    scale2p = modp_ref[0, 4:5, :]

    def post_block(r):
        rows = slice(r * ROW_BLOCK, (r + 1) * ROW_BLOCK)
        u = DEEPNORM_ALPHA * xp_ref[0, rows, :] + gate1p * y_scr[rows, :]
        x1 = _ln(u) * ln1w_ref[...] + ln1b_ref[...]
        x1_ref[0, rows, :] = x1
        h2 = (_ln(x1) * (1.0 + scale2p) + shift2p).astype(BF16)
        h2_scr[rows, :] = h2
        h2f = h2.astype(F32)
        for s in range(SUBLANES):
            h2t_ref[pl.ds(r * ROW_BLOCK * SUBLANES + s, ROW_BLOCK, stride=SUBLANES), :] = (
                h2f[:, s * LANES:(s + 1) * LANES])

    shift1 = mod_ref[0, 0:1, :]
    scale1 = mod_ref[0, 1:2, :]

    for r in range(T // ROW_BLOCK):
        rows = slice(r * ROW_BLOCK, (r + 1) * ROW_BLOCK)
        h = _ln(x_ref[0, rows, :]) * (1.0 + scale1) + shift1
        h_scr[rows, :] = h.astype(BF16)

    n_post = T // ROW_BLOCK
    proj_scr[...] = _mm(h_scr[...], win_ref[:, 0:hw4])
    for r in range(n_post // 2):
        post_block(r)
    proj2_scr[...] = _mm(h_scr[...], win_ref[:, hw4:2 * hw4])
    for r in range(n_post // 2, n_post):
        post_block(r)
    lg_ref[...] = (_mm(h2_scr[...], wr_ref[...]) + br_ref[...]).T[0:N_LOGIT_ROWS, :]

    pos = pos_ref[0].astype(F32)
    ang_t = invf_ref[...] * pos
    cos_t = jnp.cos(ang_t)
    sin_t = jnp.sin(ang_t)
    cos_scr[...] = jnp.concatenate([cos_t, cos_t], axis=0).T
    sin_scr[...] = jnp.concatenate([-sin_t, sin_t], axis=0).T

    l0 = lb_ref[0:1, :]
    l1 = lb_ref[1:2, :]
    lmax = jnp.maximum(l0, l1)
    e0 = jnp.exp(l0 - lmax)
    e1 = jnp.exp(l1 - lmax)
    lb = e0 / (e0 + e1)
    hnw = hnw_ref[...]

    C = HGRN_CHUNK
    rr = lax.broadcasted_iota(jnp.int32, (C, C), 0)
    cc = lax.broadcasted_iota(jnp.int32, (C, C), 1)
    causal = rr >= cc
    row_id = lax.broadcasted_iota(jnp.int32, (C, hw), 0)

    def hgrn_chunk(c):
        rows = slice(c * C, (c + 1) * C)
        logfs, ks = [], []
        for hd in range(N_HEADS):
            z = proj_scr[rows, hw + hd * dh: hw + (hd + 1) * dh]
            e = jnp.exp(-jnp.abs(z))
            r = 1.0 / (1.0 + e)
            er = e * r
            zp = z >= 0
            sig_pos = jnp.where(zp, r, er)
            sig_neg = jnp.where(zp, er, r)
            lbh = lb[:, hd * dh:(hd + 1) * dh]
            logfs.append(jnp.log(lbh + (1.0 - lbh) * sig_pos))
            ks.append((1.0 - lbh) * sig_neg)
        bsum = jnp.concatenate(logfs, axis=1)
        for k in (1, 2, 4):
            bsum = bsum + jnp.where(row_id >= k, pltpu.roll(bsum, k, 0), 0.0)
        for k in (8, 16, 32):
            bsum = bsum + jnp.concatenate([jnp.zeros((k, hw), F32), bsum[0:C - k, :]], axis=0)
        for hd in range(N_HEADS):
            cols = slice(hd * dh, (hd + 1) * dh)
            b = bsum[:, cols]
            b_last = b[C - 1:C, :]
            q = proj_scr[rows, cols]
            v = proj_scr[rows, 2 * hw + hd * dh: 2 * hw + (hd + 1) * dh].astype(BF16)
            g = proj_scr[rows, 3 * hw + hd * dh: 3 * hw + (hd + 1) * dh]
            k = ks[hd]
            q_in = (_silu(q) * jnp.exp(b)).astype(BF16)
            k_in = (k * jnp.exp(-b)).astype(BF16)
            k_dec = (k * jnp.exp(b_last - b)).astype(BF16)
            st = sh_scr[hd]
            a = jnp.where(causal, _mm_nt(q_in, k_in), 0.0)
            o = _mm(a.astype(BF16), v) + _mm_nt(q_in, st.astype(BF16))
            sh_scr[hd] = st * jnp.exp(b_last) + _mm_tn(v, k_dec)
            ms = jnp.mean(o * o, axis=-1, keepdims=True)
            on = o * lax.rsqrt(ms + LN_EPS) * hnw[:, hd * dh:(hd + 1) * dh] * _silu(g)
            o_scr[rows, hd * dh:(hd + 1) * dh] = on.astype(BF16)

    rnw = rnw_ref[...]
    RC = RET_CHUNK
    q_scale = dh ** -0.5

    def retention_unit(sub, hd):
        rows = slice(sub * RC, (sub + 1) * RC)
        cs = cos_scr[rows, :]
        sn = sin_scr[rows, :]
        rq = proj2_scr[rows, hd * dh: (hd + 1) * dh]
        rk = proj2_scr[rows, hw + hd * dh: hw + (hd + 1) * dh]
        v = proj2_scr[rows, 2 * hw + hd * dh: 2 * hw + (hd + 1) * dh].astype(BF16)
        g = proj2_scr[rows, 3 * hw + hd * dh: 3 * hw + (hd + 1) * dh]
        q = (rq * cs + pltpu.roll(rq, dh // 2, 1) * sn) * q_scale
        k = rk * cs + pltpu.roll(rk, dh // 2, 1) * sn
        qb = q.astype(BF16)
        st = sr_scr[hd]
        s = _mm_nt(qb, k.astype(BF16)) * dmat_ref[hd]
        o = _mm(s.astype(BF16), v) + _mm_nt(qb, st.astype(BF16)) * qdec_ref[hd]
        sr_scr[hd] = st * ret_chunk_decay[hd] + _mm_tn(v, (k * kdec_ref[hd]).astype(BF16))
        mu = jnp.mean(o, axis=-1, keepdims=True)
        oc = o - mu
        var = jnp.mean(oc * oc, axis=-1, keepdims=True)
        on = oc * lax.rsqrt(var + LN_EPS) * rnw[:, hd * dh:(hd + 1) * dh] * _silu(g)
        o_scr[rows, hw + hd * dh: hw + (hd + 1) * dh] = on.astype(BF16)

    n_chunks = T // C
    units = [(sub, hd) for sub in range(T // RC) for hd in range(N_HEADS)]
    for c in range(n_chunks):
        hgrn_chunk(c)
        for sub, hd in units[c * len(units) // n_chunks:(c + 1) * len(units) // n_chunks]:
            retention_unit(sub, hd)

    y_scr[...] = _mm(o_scr[...], wout_ref[...])


def _retention_tables():
    h = np.arange(N_HEADS, dtype=np.float64)
    log_gamma = np.log(1.0 - np.exp2(-5.0 - h))
    idx = np.arange(RET_CHUNK, dtype=np.float64)
    rel = idx[:, None] - idx[None, :]
    dmat = np.where(rel >= 0, np.exp(np.maximum(rel, 0.0)[None] * log_gamma[:, None, None]), 0.0)
    qdec = np.exp((idx + 1.0)[None, :] * log_gamma[:, None])
    kdec = np.exp((RET_CHUNK - 1.0 - idx)[None, :] * log_gamma[:, None])
    cdec = np.exp(RET_CHUNK * log_gamma)
    bc = lambda t: np.broadcast_to(t[:, :, None], (N_HEADS, RET_CHUNK, D_HEAD))
    return (jnp.asarray(dmat, F32), jnp.asarray(bc(qdec), F32), jnp.asarray(bc(kdec), F32),
            tuple(float(np.float32(v)) for v in cdec))


def _mixer_call(x, positions, mod3, win, wout, hgrn_lb, hnw, rnw, ln1w, ln1b, wr, br, wg, wu, wd):
    B, S, D = x.shape
    T = SEQ_TILE
    nj = S // T
    dmat, qdec, kdec, cdec = _retention_tables()
    inv_freq = np.power(ROPE_BASE, -np.arange(0, D_HEAD, 2, dtype=np.float64) / D_HEAD)
    invf = jnp.asarray(inv_freq[:, None], F32)
    pos3 = positions.reshape(B, 1, S)
    kern = functools.partial(_mixer_kernel, ret_chunk_decay=cdec, tiles_per_seq=nj)
    n_tiles = B * nj
    cur = lambda s: jnp.minimum(s, n_tiles - 1)
    prev = lambda s: jnp.maximum(s - 1, 0)
    E, _, F = wg.shape
    wg2, wu2, wd2 = wg.reshape(E * D, F), wu.reshape(E * D, F), wd.reshape(E * F, D)
    rg, rd = (E * D) // n_tiles, (E * F) // n_tiles
    assert rg * n_tiles == E * D and rd * n_tiles == E * F and rd % 16 == 0
    wspec_g = pl.BlockSpec((rg, F), lambda s: (cur(s), 0))
    wspec_d = pl.BlockSpec((rd, D), lambda s: (cur(s), 0))
    outs = pl.pallas_call(
        kern,
        grid=(n_tiles + 1,),
        in_specs=[
            pl.BlockSpec((1, T, D), lambda s: (cur(s) // nj, cur(s) % nj, 0)),
            pl.BlockSpec((1, 1, T), lambda s: (cur(s) // nj, 0, cur(s) % nj)),
            pl.BlockSpec((1, 6, D), lambda s: (cur(s) // nj, 0, 0)),
            _const_spec(win.shape), _const_spec(wout.shape), _const_spec(hgrn_lb.shape),
            _const_spec(hnw.shape), _const_spec(rnw.shape), _const_spec(ln1w.shape),
            _const_spec(ln1b.shape), _const_spec(invf.shape), _const_spec(dmat.shape),
            _const_spec(qdec.shape), _const_spec(kdec.shape), _const_spec(wr.shape),
            _const_spec(br.shape),
            pl.BlockSpec((1, T, D), lambda s: (prev(s) // nj, prev(s) % nj, 0)),
            pl.BlockSpec((1, 6, D), lambda s: (prev(s) // nj, 0, 0)),
            wspec_g, wspec_g, wspec_d,
        ],
        out_specs=[
            pl.BlockSpec((1, T, D), lambda s: (prev(s) // nj, prev(s) % nj, 0)),
            pl.BlockSpec((T * SUBLANES, LANES), lambda s: (prev(s), 0)),
            pl.BlockSpec((N_LOGIT_ROWS, T), lambda s: (0, prev(s))),
            wspec_g, wspec_g, wspec_d,
        ],
        out_shape=[
            jax.ShapeDtypeStruct((B, S, D), F32),
            jax.ShapeDtypeStruct((B * S * SUBLANES, LANES), F32),
            jax.ShapeDtypeStruct((N_LOGIT_ROWS, B * S), F32),
            jax.ShapeDtypeStruct(wg2.shape, BF16),
            jax.ShapeDtypeStruct(wu2.shape, BF16),
            jax.ShapeDtypeStruct(wd2.shape, BF16),
        ],
        scratch_shapes=[
            pltpu.VMEM((T, D), BF16),
            pltpu.VMEM((T, 4 * N_HEADS * D_HEAD), F32),
            pltpu.VMEM((T, 4 * N_HEADS * D_HEAD), F32),
            pltpu.VMEM((T, D), BF16),
            pltpu.VMEM((T, D_HEAD), F32),
            pltpu.VMEM((T, D_HEAD), F32),
            pltpu.VMEM((N_HEADS, D_HEAD, D_HEAD), F32),
            pltpu.VMEM((N_HEADS, D_HEAD, D_HEAD), F32),
            pltpu.VMEM((T, D), F32),
            pltpu.VMEM((T, D), BF16),
        ],
        compiler_params=pltpu.CompilerParams(
            dimension_semantics=("arbitrary",), vmem_limit_bytes=VMEM_LIMIT),
        name="mixer",
    )(x, pos3, mod3, win, wout, hgrn_lb, hnw, rnw, ln1w, ln1b, invf, dmat, qdec, kdec, wr, br, x, mod3,
      wg2, wu2, wd2)
    x1, h2t, lgt, wg16, wu16, wd16 = outs
    return x1, h2t, lgt, wg16.reshape(E, D, F), wu16.reshape(E, D, F), wd16.reshape(E, F, D)


def _row_copy(src_ref, dst_ref, src_row, dst_row, sem):
    s0 = pl.multiple_of(src_row * SUBLANES, SUBLANES)
    d0 = pl.multiple_of(dst_row * SUBLANES, SUBLANES)
    return pltpu.make_async_copy(src_ref.at[pl.ds(s0, SUBLANES)], dst_ref.at[pl.ds(d0, SUBLANES)], sem)


def _from_token_tiles(buf_ref, first_row, n_rows):
    return jnp.concatenate(
        [buf_ref[pl.ds(first_row * SUBLANES + s, n_rows, stride=SUBLANES), :] for s in range(SUBLANES)],
        axis=1)


def _start_rows(src_ref, dst_ref, idx_ref, idx_base, dst_base, n_rows, sem):
    def issue(r2, carry):
        for u in range(2):
            r = r2 * 2 + u
            _row_copy(src_ref, dst_ref, idx_ref[idx_base + r], dst_base + r, sem).start(priority=u)
        return carry
    lax.fori_loop(0, n_rows // 2, issue, 0)


def _wait_rows(src_ref, dst_ref, n_rows, sem):
    pltpu.make_async_copy(src_ref.at[pl.ds(0, n_rows * SUBLANES)], dst_ref, sem).wait()


def _bf16_bits(x):
    return lax.bitcast_convert_type(x.astype(BF16).astype(F32), jnp.uint32)


def _dispatch_kernel(dest_ref, pad0_ref, padn_ref, nv_ref, h2t_ref, xs_ref,
                     buf, zbuf, load_sems, scat_sems, zsem):
    i = pl.program_id(0)
    n = pl.num_programs(0)
    TT = DISPATCH_TILE
    rows = TT * SUBLANES

    def pad_copies(wait):
        def bucket(b, carry):
            pos = pad0_ref[b]
            left = padn_ref[b]
            for size in PAD_COPY_SIZES:
                hit = (left & size) != 0
                first = 0 if wait else pl.multiple_of(pos * SUBLANES, SUBLANES)
                copy = pltpu.make_async_copy(
                    zbuf.at[pl.ds(0, size * SUBLANES)], xs_ref.at[pl.ds(first, size * SUBLANES)], zsem)

                @pl.when(hit)
                def _():
                    copy.wait() if wait else copy.start()
                pos = pos + jnp.where(hit, size, 0)
            return carry
        lax.fori_loop(0, pad0_ref.shape[0], bucket, 0)

        zrows = PAD_COPY_SIZES[0] * SUBLANES

        def unused_block(blk, carry):
            for part in range(SLOT_BLOCK // PAD_COPY_SIZES[0]):
                first = 0 if wait else pl.multiple_of(blk * SLOT_BLOCK * SUBLANES + part * zrows, zrows)
                copy = pltpu.make_async_copy(zbuf, xs_ref.at[pl.ds(first, zrows)], zsem)
                copy.wait() if wait else copy.start()
            return carry
        lax.fori_loop(nv_ref[0], xs_ref.shape[0] // (SLOT_BLOCK * SUBLANES), unused_block, 0)

    @pl.when(i == 0)
    def _():
        zbuf[...] = jnp.zeros_like(zbuf)
        pad_copies(wait=False)

    def load(tile, slot):
        src = h2t_ref.at[pl.ds(pl.multiple_of(tile * rows, rows), rows)]
        return pltpu.make_async_copy(src, buf.at[slot], load_sems.at[slot])

    def wait_scatter(slot):
        _wait_rows(buf.at[slot], xs_ref.at[pl.ds(0, rows)], TT, scat_sems.at[slot])

    @pl.when(i == 0)
    def _():
        load(0, 0).start()

    for slot in range(DISPATCH_RING):
        @pl.when((i + 1 < n) & ((i + 1) % DISPATCH_RING == slot))
        def _():
            @pl.when(i + 1 >= DISPATCH_RING)
            def _():
                wait_scatter(slot)
            load(i + 1, slot).start()

    for slot in range(DISPATCH_RING):
        @pl.when(i % DISPATCH_RING == slot)
        def _():
            load(i, slot).wait()

            for r in range(TT):
                _row_copy(buf.at[slot], xs_ref, r, dest_ref[i * TT + r],
                          scat_sems.at[slot]).start(priority=r % 2)

    @pl.when(i == n - 1)
    def _():
        for slot in range(DISPATCH_RING):
            wait_scatter(slot)
        pad_copies(wait=True)


def _dispatch_call(dest, pad_start, pad_len, n_valid, h2t, n_slots):
    N = dest.shape[0]
    TT = DISPATCH_TILE
    assert N // TT >= DISPATCH_RING and SLOT_BLOCK <= 2 * PAD_COPY_SIZES[0]
    assert SLOT_BLOCK % PAD_COPY_SIZES[0] == 0
    grid_spec = pltpu.PrefetchScalarGridSpec(
        num_scalar_prefetch=4,
        grid=(N // TT,),
        in_specs=[pl.BlockSpec(memory_space=pl.ANY)],
        out_specs=pl.BlockSpec(memory_space=pl.ANY),
        scratch_shapes=[pltpu.VMEM((DISPATCH_RING, TT * SUBLANES, LANES), F32),
                        pltpu.VMEM((PAD_COPY_SIZES[0] * SUBLANES, LANES), F32),
                        pltpu.SemaphoreType.DMA((DISPATCH_RING,)),
                        pltpu.SemaphoreType.DMA((DISPATCH_RING,)),
                        pltpu.SemaphoreType.DMA],
    )
    return pl.pallas_call(
        _dispatch_kernel,
        grid_spec=grid_spec,
        out_shape=jax.ShapeDtypeStruct((n_slots * SUBLANES, LANES), F32),
        compiler_params=pltpu.CompilerParams(dimension_semantics=("arbitrary",)),
        name="dispatch",
    )(dest, pad_start, pad_len, n_valid, h2t)


def _expert_kernel(ea_ref, eb_ref, rows_ref, nv_ref, xs_ref, wg_ref, wu_ref, wd_ref, y_ref):
    i = pl.program_id(0)
    R = SLOT_BLOCK
    G = EXPERTS_PER_GROUP
    used = i < nv_ref[0]
    n_rows = rows_ref[i]

    def ffn(xb, e):
        a = _mm(xb, wg_ref[e])
        u = _mm(xb, wu_ref[e])
        return _mm((_silu(a) * u).astype(BF16), wd_ref[e])

    for k, m in enumerate(BLOCK_ROWS):
        fewer = BLOCK_ROWS[k - 1] if k else 0

        @pl.when(used & (n_rows > fewer) & (n_rows <= m))
        def _():
            xb = _from_token_tiles(xs_ref, 0, m).astype(BF16)
            ya = ffn(xb, ea_ref[i] & (G - 1))
            yb = ffn(xb, eb_ref[i] & (G - 1))
            packed = _bf16_bits(ya) | (_bf16_bits(yb) >> 16)
            for s in range(SUBLANES):
                y_ref[pl.ds(s, m, stride=SUBLANES), :] = packed[:, s * LANES:(s + 1) * LANES]
            if m < R:
                y_ref[pl.ds(m * SUBLANES, (R - m) * SUBLANES), :] = jnp.zeros(
                    ((R - m) * SUBLANES, LANES), y_ref.dtype)

    @pl.when(jnp.logical_not(used))
    def _():
        y_ref[...] = jnp.zeros_like(y_ref)


def _expert_call(block_ea, block_eb, block_rows, n_valid, xs, wg, wu, wd):
    assert BLOCK_ROWS[-1] == SLOT_BLOCK
    R = SLOT_BLOCK
    n_blocks = xs.shape[0] // (R * SUBLANES)
    D, F = wg.shape[1], wg.shape[2]
    G = EXPERTS_PER_GROUP
    last = lambda i, nv: jnp.minimum(i, nv[0] - 1)
    grp = lambda i, ea, eb, rows, nv: (ea[last(i, nv)] // G, 0, 0)
    blk = lambda i, ea, eb, rows, nv: (last(i, nv), 0)
    grid_spec = pltpu.PrefetchScalarGridSpec(
        num_scalar_prefetch=4,
        grid=(n_blocks,),
        in_specs=[
            pl.BlockSpec((R * SUBLANES, LANES), blk),
            pl.BlockSpec((G, D, F), grp, pipeline_mode=pl.Buffered(1)),
            pl.BlockSpec((G, D, F), grp, pipeline_mode=pl.Buffered(1)),
            pl.BlockSpec((G, F, D), grp, pipeline_mode=pl.Buffered(1)),
        ],
        out_specs=pl.BlockSpec((R * SUBLANES, LANES), lambda i, ea, eb, rows, nv: (i, 0)),
    )
    return pl.pallas_call(
        _expert_kernel,
        grid_spec=grid_spec,
        out_shape=jax.ShapeDtypeStruct(xs.shape, jnp.uint32),
        compiler_params=pltpu.CompilerParams(
            dimension_semantics=("arbitrary",), vmem_limit_bytes=VMEM_LIMIT),
        name="experts",
    )(block_ea, block_eb, block_rows, n_valid, xs, wg, wu, wd)


def _final_kernel(dest_ref, yt_ref, x1_ref, g_ref, mod_ref, w_ref, b_ref, o_ref, buf, sems):
    i = pl.program_id(0)
    n_steps = pl.num_programs(0)
    TT = TOK_TILE

    @pl.when(i == 0)
    def _():
        _start_rows(yt_ref, buf.at[0], dest_ref, 0, 0, TT, sems.at[0])

    nxt = jnp.minimum(i + 1, n_steps - 1)
    gate2 = mod_ref[0, 5:6, :]
    g = g_ref[...]
    for slot in range(2):
        @pl.when(i % 2 == slot)
        def _():
            _wait_rows(yt_ref, buf.at[slot], TT, sems.at[slot])
            for r in range(TT // ROW_BLOCK):
                rows = slice(r * ROW_BLOCK, (r + 1) * ROW_BLOCK)
                packed = _from_token_tiles(buf.at[slot], r * ROW_BLOCK, ROW_BLOCK)
                ya = lax.bitcast_convert_type(packed & jnp.uint32(0xFFFF0000), F32)
                yb = lax.bitcast_convert_type(packed << 16, F32)
                y = ya * g[rows, 0:1] + yb * g[rows, 1:2]
                u = DEEPNORM_ALPHA * x1_ref[rows, :] + gate2 * y
                o_ref[rows, :] = _ln(u) * w_ref[...] + b_ref[...]
                for q in range(r * ROW_BLOCK, (r + 1) * ROW_BLOCK):
                    _row_copy(yt_ref, buf.at[1 - slot], dest_ref[nxt * TT + q], q,
                              sems.at[1 - slot]).start(priority=q % 2)

            @pl.when(i == n_steps - 1)
            def _():
                _wait_rows(yt_ref, buf.at[1 - slot], TT, sems.at[1 - slot])


def _final_call(dest, yt, x1, gates, mod3, w, b, seq_len):
    N, D = x1.shape
    TT = TOK_TILE
    per_seq = seq_len // TT
    grid_spec = pltpu.PrefetchScalarGridSpec(
        num_scalar_prefetch=1,
        grid=(N // TT,),
        in_specs=[
            pl.BlockSpec(memory_space=pl.ANY),
            pl.BlockSpec((TT, D), lambda i, d: (i, 0)),
            pl.BlockSpec((TT, 2), lambda i, d: (i, 0)),
            pl.BlockSpec((1, 6, D), lambda i, d: (i // per_seq, 0, 0)),
            pl.BlockSpec((1, D), lambda i, d: (0, 0)),
            pl.BlockSpec((1, D), lambda i, d: (0, 0)),
        ],
        out_specs=pl.BlockSpec((TT, D), lambda i, d: (i, 0)),
        scratch_shapes=[pltpu.VMEM((2, TT * SUBLANES, LANES), jnp.uint32), pltpu.SemaphoreType.DMA((2,))],
    )
    return pl.pallas_call(
        _final_kernel,
        grid_spec=grid_spec,
        out_shape=jax.ShapeDtypeStruct((N, D), F32),
        compiler_params=pltpu.CompilerParams(
            dimension_semantics=("arbitrary",), vmem_limit_bytes=VMEM_LIMIT),
        name="combine_ln",
    )(dest, yt, x1, gates, mod3, w, b)


def _route_kernel(lgt_ref, rf_ref, ri_ref, meta_ref, cnt_scr, base_scr, bstart_scr):
    ph = pl.program_id(0)
    t = pl.program_id(1)
    TR = ROUTE_TILE
    E = N_EXPERTS
    R = SLOT_BLOCK
    G = EXPERTS_PER_GROUP
    lg = lgt_ref[...]

    g = [lg[E + k:E + k + 1, :] for k in range(N_GROUPS)]
    gmax = jnp.maximum(jnp.maximum(g[0], g[1]), jnp.maximum(g[2], g[3]))
    gs = jnp.where(g[0] == gmax, 0.0, jnp.where(g[1] == gmax, 1.0, jnp.where(g[2] == gmax, 2.0, 3.0)))
    psum = (jnp.exp(g[0] - gmax) + jnp.exp(g[1] - gmax)) + (jnp.exp(g[2] - gmax) + jnp.exp(g[3] - gmax))
    p_star = 1.0 / psum
    esel = jnp.where(gs == 0.0, lg[0:G], jnp.where(gs == 1.0, lg[G:2 * G],
                                                   jnp.where(gs == 2.0, lg[2 * G:3 * G], lg[3 * G:4 * G])))
    sub = lax.broadcasted_iota(jnp.int32, (G, TR), 0).astype(F32)
    m1 = jnp.max(esel, axis=0, keepdims=True)
    i1 = jnp.min(jnp.where(esel == m1, sub, float(G)), axis=0, keepdims=True)
    es2 = jnp.where(sub == i1, -jnp.inf, esel)
    m2 = jnp.max(es2, axis=0, keepdims=True)
    i2 = jnp.min(jnp.where(es2 == m2, sub, float(G)), axis=0, keepdims=True)
    d = jnp.exp(m2 - m1)
    w1 = 1.0 / (1.0 + d)
    w2 = d * w1
    lo = jnp.minimum(i1, i2)
    hi = jnp.maximum(i1, i2)
    bucket = gs * PAIRS_PER_GROUP + (lo * (2 * G - 1 - lo) * 0.5 + (hi - lo - 1.0))
    first_is_lo = i1 < i2
    gate_a = p_star * jnp.where(first_is_lo, w1, w2)
    gate_b = p_star * jnp.where(first_is_lo, w2, w1)
    row = lax.broadcasted_iota(jnp.int32, (NB, TR), 0).astype(F32)
    oh = row == bucket
    cnt = oh.astype(F32)

    @pl.when(ph == 0)
    def _():
        @pl.when(t == 0)
        def _():
            cnt_scr[...] = jnp.zeros_like(cnt_scr)
        acc = cnt_scr[...]
        for j in range(TR // LANES):
            acc = acc + cnt[:, j * LANES:(j + 1) * LANES]
        cnt_scr[...] = acc

    @pl.when((ph == 1) & (t == 0))
    def _():
        counts = jnp.sum(cnt_scr[...], axis=1, keepdims=True)
        nblk = jnp.floor((counts + (R - 1)) * (1.0 / R))
        nblk_b = jnp.broadcast_to(nblk, (NB, LANES))
        rb = lax.broadcasted_iota(jnp.int32, (NB, NB), 0)
        cb = lax.broadcasted_iota(jnp.int32, (NB, NB), 1)
        bstart = _mm((cb < rb).astype(BF16), nblk_b.astype(BF16))
        bstart_scr[...] = bstart
        base_scr[...] = jnp.zeros_like(base_scr)
        k = lax.broadcasted_iota(jnp.int32, (NB, 1), 0).astype(F32)
        grp = sum((k >= float(m * PAIRS_PER_GROUP)).astype(F32) for m in range(1, N_GROUPS))
        p = k - grp * PAIRS_PER_GROUP
        pair_start = [i * (2 * G - 1 - i) // 2 for i in range(G - 1)]
        ia = sum((p >= float(s)).astype(F32) for s in pair_start[1:])
        ib = p - ia * (2 * G - 1 - ia) * 0.5 + ia + 1.0
        n_pad = meta_ref.shape[1]
        blk = lax.broadcasted_iota(jnp.int32, (NB, n_pad), 1).astype(F32)
        first = jnp.broadcast_to(bstart[:, 0:1], (NB, n_pad))
        last = jnp.broadcast_to((bstart + nblk_b)[:, 0:1], (NB, n_pad))
        member = (first <= blk) & (blk < last)
        ea = jnp.sum(jnp.where(member, grp * G + ia, 0.0), axis=0, keepdims=True)
        eb = jnp.sum(jnp.where(member, grp * G + ib, 0.0), axis=0, keepdims=True)
        meta_ref[...] = jnp.zeros_like(meta_ref)
        meta_ref[0:1, :] = ea.astype(jnp.int32)
        meta_ref[1:2, :] = eb.astype(jnp.int32)
        meta_ref[2:3, :] = jnp.max(last, axis=0, keepdims=True).astype(jnp.int32)
        in_blk = jnp.clip(counts - (blk - first) * R, 0.0, float(R))
        meta_ref[5:6, :] = jnp.sum(jnp.where(member, in_blk, 0.0), axis=0, keepdims=True).astype(jnp.int32)
        pad0 = jnp.broadcast_to(bstart[:, 0:1] * R + counts, (NB, NB)).T[0:1, :]
        padn = jnp.broadcast_to(nblk * R - counts, (NB, NB)).T[0:1, :]
        meta_ref[3:4, 0:NB] = pad0.astype(jnp.int32)
        meta_ref[4:5, 0:NB] = padn.astype(jnp.int32)

    @pl.when(ph == 1)
    def _():
        ri = lax.broadcasted_iota(jnp.int32, (LANES, 2 * LANES), 0)
        ci = lax.broadcasted_iota(jnp.int32, (LANES, 2 * LANES), 1)
        w = ((ci >= ri) | (ci >= LANES)).astype(BF16)
        base = base_scr[...]
        slot0 = bstart_scr[...] * R
        ds = []
        for j in range(TR // LANES):
            sl = slice(j * LANES, (j + 1) * LANES)
            blk = cnt[:, sl]
            res = _mm(blk.astype(BF16), w)
            val = slot0 + base + (res[:, 0:LANES] - blk)
            ds.append(jnp.sum(jnp.where(oh[:, sl], val, 0.0), axis=0, keepdims=True))
            base = base + res[:, LANES:2 * LANES]
        base_scr[...] = base
        ri_ref[...] = jnp.zeros_like(ri_ref)
        ri_ref[0:1, :] = jnp.concatenate(ds, axis=1).astype(jnp.int32)
        rf_ref[...] = jnp.zeros_like(rf_ref)
        rf_ref[0:1, :] = gate_a
        rf_ref[1:2, :] = gate_b


def _route_call(lgt, n_blocks):
    rows, N = lgt.shape
    TR = ROUTE_TILE
    E = NB
    n_pad = -(-n_blocks // LANES) * LANES
    return pl.pallas_call(
        _route_kernel,
        grid=(2, N // TR),
        in_specs=[pl.BlockSpec((rows, TR), lambda ph, t: (0, t))],
        out_specs=[
            pl.BlockSpec((SUBLANES, TR), lambda ph, t: (0, ph * t)),
            pl.BlockSpec((SUBLANES, TR), lambda ph, t: (0, ph * t)),
            pl.BlockSpec((SUBLANES, n_pad), lambda ph, t: (0, 0)),
        ],
        out_shape=[
            jax.ShapeDtypeStruct((SUBLANES, N), F32),
            jax.ShapeDtypeStruct((SUBLANES, N), jnp.int32),
            jax.ShapeDtypeStruct((SUBLANES, n_pad), jnp.int32),
        ],
        scratch_shapes=[pltpu.VMEM((E, LANES), F32), pltpu.VMEM((E, LANES), F32), pltpu.VMEM((E, LANES), F32)],
        compiler_params=pltpu.CompilerParams(dimension_semantics=("arbitrary", "arbitrary")),
        name="route",
    )(lgt)


def kernel(x, c, positions, w_ada, b_ada, w_in, w_out, hgrn_lb, hgrn_norm_w, ret_norm_w, post_ln1_w,
           post_ln1_b, w_rg, b_rg, w_re, b_re, w_gate, w_up, w_down, post_ln2_w, post_ln2_b):
    B, S, D = x.shape
    N = B * S
    hw = N_HEADS * D_HEAD

    mod3 = _ada_call(c, w_ada[0], b_ada[0]).reshape(B, 6, D)

    win = w_in[0].astype(BF16)
    wout = w_out[0].astype(BF16)
    pad = LANES - N_EXPERTS - N_GROUPS
    wr = jnp.concatenate([w_re[0], w_rg[0], jnp.zeros((D, pad), F32)], axis=1).astype(BF16)
    br = jnp.concatenate([b_re[0], b_rg[0], jnp.zeros((pad,), F32)]).reshape(1, LANES)

    x1, h2t, lgt, wg16, wu16, wd16 = _mixer_call(
        x, positions, mod3, win, wout, hgrn_lb, hgrn_norm_w[0].reshape(1, hw),
        ret_norm_w[0].reshape(1, hw), post_ln1_w[0].reshape(1, D), post_ln1_b[0].reshape(1, D), wr, br,
        w_gate[0], w_up[0], w_down[0])

    n_blocks = N // SLOT_BLOCK + N_GROUPS * PAIRS_PER_GROUP
    rf, ri, meta = _route_call(lgt, n_blocks)
    dest = ri[0]
    gates = rf[0:2].T
    xs = _dispatch_call(dest, meta[3, :NB], meta[4, :NB], meta[2, 0:1], h2t, n_blocks * SLOT_BLOCK)
    yt = _expert_call(meta[0, :n_blocks], meta[1, :n_blocks], meta[5, :n_blocks], meta[2, 0:1], xs,
                      wg16, wu16, wd16)
    out = _final_call(dest, yt, x1.reshape(N, D), gates, mod3, post_ln2_w[0].reshape(1, D),
                      post_ln2_b[0].reshape(1, D), S)
    return out.reshape(B, S, D)
```

```python
import functools

import numpy as np
import jax
import jax.numpy as jnp
from jax import lax
from jax.experimental import pallas as pl
from jax.experimental.pallas import tpu as pltpu

F32 = jnp.float32
BF16 = jnp.bfloat16

D_MODEL = 1024
N_HEADS = 4
D_HEAD = 128
HGRN_CHUNK = 64
RET_CHUNK = 256
ROPE_BASE = 10000.0
N_EXPERTS = 32
EXPERTS_PER_GROUP = 8
N_GROUPS = 4
DEEPNORM_ALPHA = 2.0 ** 0.25
LN_EPS = 1e-5

SEQ_TILE = 512
ROW_BLOCK = 128
PAIRS_PER_GROUP = EXPERTS_PER_GROUP * (EXPERTS_PER_GROUP - 1) // 2
NB = 128
SLOT_BLOCK = 256
BLOCK_ROWS = (64, 128, 192, 256)
DISPATCH_TILE = 512
DISPATCH_RING = 3
PAD_COPY_SIZES = (128, 64, 32, 16, 8, 4, 2, 1)
TOK_TILE = 1024
ROUTE_TILE = 2048
N_LOGIT_ROWS = 40
LANES = 128
SUBLANES = 8
V7X_VMEM_BYTES = 64 * 1024 * 1024
VMEM_LIMIT = V7X_VMEM_BYTES - 8 * 1024 * 1024


def _mm(a, b):
    return jnp.dot(a, b, preferred_element_type=F32)


def _mm_nt(a, b):
    return lax.dot_general(a, b, (((1,), (1,)), ((), ())), preferred_element_type=F32)


def _mm_tn(a, b):
    return lax.dot_general(a, b, (((0,), (0,)), ((), ())), preferred_element_type=F32)


def _sigmoid(x):
    return 1.0 / (1.0 + jnp.exp(-x))


def _silu(x):
    return x * _sigmoid(x)


def _ln(x):
    mu = jnp.mean(x, axis=-1, keepdims=True)
    xc = x - mu
    var = jnp.mean(xc * xc, axis=-1, keepdims=True)
    return xc * lax.rsqrt(var + LN_EPS)


def _const_spec(shape):
    nd = len(shape)
    return pl.BlockSpec(shape, lambda *_: (0,) * nd, pipeline_mode=pl.Buffered(1))


def _ada_kernel(c_ref, w_ref, b_ref, o_ref):
    ca = _silu(c_ref[...])
    o_ref[...] = _mm(ca.astype(BF16), w_ref[...].astype(BF16)) + b_ref[...]


def _ada_call(c, w_ada, b_ada):
    B, D = c.shape
    n_out = w_ada.shape[1]
    return pl.pallas_call(
        _ada_kernel,
        grid=(n_out // D,),
        in_specs=[pl.BlockSpec((B, D), lambda j: (0, 0)),
                  pl.BlockSpec((D, D), lambda j: (0, j)),
                  pl.BlockSpec((1, D), lambda j: (0, j))],
        out_specs=pl.BlockSpec((B, D), lambda j: (0, j)),
        out_shape=jax.ShapeDtypeStruct((B, n_out), F32),
        name="ada_mod",
    )(c, w_ada, b_ada.reshape(1, n_out))


def _mixer_kernel(x_ref, pos_ref, mod_ref, win_ref, wout_ref, lb_ref, hnw_ref, rnw_ref,
                  ln1w_ref, ln1b_ref, invf_ref, dmat_ref, qdec_ref, kdec_ref, wr_ref, br_ref,
                  xp_ref, modp_ref, wg32_ref, wu32_ref, wd32_ref,
                  x1_ref, h2t_ref, lg_ref, wg16_ref, wu16_ref, wd16_ref,
                  h_scr, proj_scr, proj2_scr, o_scr, cos_scr, sin_scr, sh_scr, sr_scr, y_scr, h2_scr,
                  *, ret_chunk_decay, tiles_per_seq):
    T = SEQ_TILE
    D = D_MODEL
    dh = D_HEAD
    hw = N_HEADS * dh
    hw4 = 4 * hw
    step = pl.program_id(0)
    n_tiles = pl.num_programs(0) - 1

    wg16_ref[...] = wg32_ref[...].astype(BF16)
    wu16_ref[...] = wu32_ref[...].astype(BF16)
    wd16_ref[...] = wd32_ref[...].astype(BF16)

    @pl.when(step == 0)
    def _():
        y_scr[...] = jnp.zeros_like(y_scr)

    @pl.when((step < n_tiles) & (step % tiles_per_seq == 0))
    def _():
        sh_scr[...] = jnp.zeros_like(sh_scr)
        sr_scr[...] = jnp.zeros_like(sr_scr)

    gate1p = modp_ref[0, 2:3, :]
    shift2p = modp_ref[0, 3:4, :]
    scale2p = modp_ref[0, 4:5, :]

    def post_block(r):
        rows = slice(r * ROW_BLOCK, (r + 1) * ROW_BLOCK)
        u = DEEPNORM_ALPHA * xp_ref[0, rows, :] + gate1p * y_scr[rows, :]
        x1 = _ln(u) * ln1w_ref[...] + ln1b_ref[...]
        x1_ref[0, rows, :] = x1
        h2 = (_ln(x1) * (1.0 + scale2p) + shift2p).astype(BF16)
        h2_scr[rows, :] = h2
        h2f = h2.astype(F32)
        for s in range(SUBLANES):
            h2t_ref[pl.ds(r * ROW_BLOCK * SUBLANES + s, ROW_BLOCK, stride=SUBLANES), :] = (
                h2f[:, s * LANES:(s + 1) * LANES])

    shift1 = mod_ref[0, 0:1, :]
    scale1 = mod_ref[0, 1:2, :]

    for r in range(T // ROW_BLOCK):
        rows = slice(r * ROW_BLOCK, (r + 1) * ROW_BLOCK)
        h = _ln(x_ref[0, rows, :]) * (1.0 + scale1) + shift1
        h_scr[rows, :] = h.astype(BF16)

    n_post = T // ROW_BLOCK
    proj_scr[...] = _mm(h_scr[...], win_ref[:, 0:hw4])
    for r in range(n_post // 2):
        post_block(r)
    proj2_scr[...] = _mm(h_scr[...], win_ref[:, hw4:2 * hw4])
    for r in range(n_post // 2, n_post):
        post_block(r)
    lg_ref[...] = (_mm(h2_scr[...], wr_ref[...]) + br_ref[...]).T[0:N_LOGIT_ROWS, :]

    pos = pos_ref[0].astype(F32)
    ang_t = invf_ref[...] * pos
    cos_t = jnp.cos(ang_t)
    sin_t = jnp.sin(ang_t)
    cos_scr[...] = jnp.concatenate([cos_t, cos_t], axis=0).T
    sin_scr[...] = jnp.concatenate([-sin_t, sin_t], axis=0).T

    l0 = lb_ref[0:1, :]
    l1 = lb_ref[1:2, :]
    lmax = jnp.maximum(l0, l1)
    e0 = jnp.exp(l0 - lmax)
    e1 = jnp.exp(l1 - lmax)
    lb = e0 / (e0 + e1)
    hnw = hnw_ref[...]

    C = HGRN_CHUNK
    rr = lax.broadcasted_iota(jnp.int32, (C, C), 0)
    cc = lax.broadcasted_iota(jnp.int32, (C, C), 1)
    causal = rr >= cc
    row_id = lax.broadcasted_iota(jnp.int32, (C, hw), 0)

    def hgrn_chunk(c):
        rows = slice(c * C, (c + 1) * C)
        logfs, ks = [], []
        for hd in range(N_HEADS):
            z = proj_scr[rows, hw + hd * dh: hw + (hd + 1) * dh]
            e = jnp.exp(-jnp.abs(z))
            r = 1.0 / (1.0 + e)
            er = e * r
            zp = z >= 0
            sig_pos = jnp.where(zp, r, er)
            sig_neg = jnp.where(zp, er, r)
            lbh = lb[:, hd * dh:(hd + 1) * dh]
            logfs.append(jnp.log(lbh + (1.0 - lbh) * sig_pos))
            ks.append((1.0 - lbh) * sig_neg)
        bsum = jnp.concatenate(logfs, axis=1)
        for k in (1, 2, 4):
            bsum = bsum + jnp.where(row_id >= k, pltpu.roll(bsum, k, 0), 0.0)
        for k in (8, 16, 32):
            bsum = bsum + jnp.concatenate([jnp.zeros((k, hw), F32), bsum[0:C - k, :]], axis=0)
        for hd in range(N_HEADS):
            cols = slice(hd * dh, (hd + 1) * dh)
            b = bsum[:, cols]
            b_last = b[C - 1:C, :]
            q = proj_scr[rows, cols]
            v = proj_scr[rows, 2 * hw + hd * dh: 2 * hw + (hd + 1) * dh].astype(BF16)
            g = proj_scr[rows, 3 * hw + hd * dh: 3 * hw + (hd + 1) * dh]
            k = ks[hd]
            q_in = (_silu(q) * jnp.exp(b)).astype(BF16)
            k_in = (k * jnp.exp(-b)).astype(BF16)
            k_dec = (k * jnp.exp(b_last - b)).astype(BF16)
            st = sh_scr[hd]
            a = jnp.where(causal, _mm_nt(q_in, k_in), 0.0)
            o = _mm(a.astype(BF16), v) + _mm_nt(q_in, st.astype(BF16))
            sh_scr[hd] = st * jnp.exp(b_last) + _mm_tn(v, k_dec)
            ms = jnp.mean(o * o, axis=-1, keepdims=True)
            on = o * lax.rsqrt(ms + LN_EPS) * hnw[:, hd * dh:(hd + 1) * dh] * _silu(g)
            o_scr[rows, hd * dh:(hd + 1) * dh] = on.astype(BF16)

    rnw = rnw_ref[...]
    RC = RET_CHUNK
    q_scale = dh ** -0.5

    def retention_unit(sub, hd):
        rows = slice(sub * RC, (sub + 1) * RC)
        cs = cos_scr[rows, :]
        sn = sin_scr[rows, :]
        rq = proj2_scr[rows, hd * dh: (hd + 1) * dh]
        rk = proj2_scr[rows, hw + hd * dh: hw + (hd + 1) * dh]
        v = proj2_scr[rows, 2 * hw + hd * dh: 2 * hw + (hd + 1) * dh].astype(BF16)
        g = proj2_scr[rows, 3 * hw + hd * dh: 3 * hw + (hd + 1) * dh]
        q = (rq * cs + pltpu.roll(rq, dh // 2, 1) * sn) * q_scale
        k = rk * cs + pltpu.roll(rk, dh // 2, 1) * sn
        qb = q.astype(BF16)
        st = sr_scr[hd]
        s = _mm_nt(qb, k.astype(BF16)) * dmat_ref[hd]
        o = _mm(s.astype(BF16), v) + _mm_nt(qb, st.astype(BF16)) * qdec_ref[hd]
        sr_scr[hd] = st * ret_chunk_decay[hd] + _mm_tn(v, (k * kdec_ref[hd]).astype(BF16))
        mu = jnp.mean(o, axis=-1, keepdims=True)
        oc = o - mu
        var = jnp.mean(oc * oc, axis=-1, keepdims=True)
        on = oc * lax.rsqrt(var + LN_EPS) * rnw[:, hd * dh:(hd + 1) * dh] * _silu(g)
        o_scr[rows, hw + hd * dh: hw + (hd + 1) * dh] = on.astype(BF16)

    n_chunks = T // C
    units = [(sub, hd) for sub in range(T // RC) for hd in range(N_HEADS)]
    for c in range(n_chunks):
        hgrn_chunk(c)
        for sub, hd in units[c * len(units) // n_chunks:(c + 1) * len(units) // n_chunks]:
            retention_unit(sub, hd)

    y_scr[...] = _mm(o_scr[...], wout_ref[...])


def _retention_tables():
    h = np.arange(N_HEADS, dtype=np.float64)
    log_gamma = np.log(1.0 - np.exp2(-5.0 - h))
    idx = np.arange(RET_CHUNK, dtype=np.float64)
    rel = idx[:, None] - idx[None, :]
    dmat = np.where(rel >= 0, np.exp(np.maximum(rel, 0.0)[None] * log_gamma[:, None, None]), 0.0)
    qdec = np.exp((idx + 1.0)[None, :] * log_gamma[:, None])
    kdec = np.exp((RET_CHUNK - 1.0 - idx)[None, :] * log_gamma[:, None])
    cdec = np.exp(RET_CHUNK * log_gamma)
    bc = lambda t: np.broadcast_to(t[:, :, None], (N_HEADS, RET_CHUNK, D_HEAD))
    return (jnp.asarray(dmat, F32), jnp.asarray(bc(qdec), F32), jnp.asarray(bc(kdec), F32),
            tuple(float(np.float32(v)) for v in cdec))


def _mixer_call(x, positions, mod3, win, wout, hgrn_lb, hnw, rnw, ln1w, ln1b, wr, br, wg, wu, wd):
    B, S, D = x.shape
    T = SEQ_TILE
    nj = S // T
    dmat, qdec, kdec, cdec = _retention_tables()
    inv_freq = np.power(ROPE_BASE, -np.arange(0, D_HEAD, 2, dtype=np.float64) / D_HEAD)
    invf = jnp.asarray(inv_freq[:, None], F32)
    pos3 = positions.reshape(B, 1, S)
    kern = functools.partial(_mixer_kernel, ret_chunk_decay=cdec, tiles_per_seq=nj)
    n_tiles = B * nj
    cur = lambda s: jnp.minimum(s, n_tiles - 1)
    prev = lambda s: jnp.maximum(s - 1, 0)
    E, _, F = wg.shape
    wg2, wu2, wd2 = wg.reshape(E * D, F), wu.reshape(E * D, F), wd.reshape(E * F, D)
    rg, rd = (E * D) // n_tiles, (E * F) // n_tiles
    assert rg * n_tiles == E * D and rd * n_tiles == E * F and rd % 16 == 0
    wspec_g = pl.BlockSpec((rg, F), lambda s: (cur(s), 0))
    wspec_d = pl.BlockSpec((rd, D), lambda s: (cur(s), 0))
    outs = pl.pallas_call(
        kern,
        grid=(n_tiles + 1,),
        in_specs=[
            pl.BlockSpec((1, T, D), lambda s: (cur(s) // nj, cur(s) % nj, 0)),
            pl.BlockSpec((1, 1, T), lambda s: (cur(s) // nj, 0, cur(s) % nj)),
            pl.BlockSpec((1, 6, D), lambda s: (cur(s) // nj, 0, 0)),
            _const_spec(win.shape), _const_spec(wout.shape), _const_spec(hgrn_lb.shape),
            _const_spec(hnw.shape), _const_spec(rnw.shape), _const_spec(ln1w.shape),
            _const_spec(ln1b.shape), _const_spec(invf.shape), _const_spec(dmat.shape),
            _const_spec(qdec.shape), _const_spec(kdec.shape), _const_spec(wr.shape),
            _const_spec(br.shape),
            pl.BlockSpec((1, T, D), lambda s: (prev(s) // nj, prev(s) % nj, 0)),
            pl.BlockSpec((1, 6, D), lambda s: (prev(s) // nj, 0, 0)),
            wspec_g, wspec_g, wspec_d,
        ],
        out_specs=[
            pl.BlockSpec((1, T, D), lambda s: (prev(s) // nj, prev(s) % nj, 0)),
            pl.BlockSpec((T * SUBLANES, LANES), lambda s: (prev(s), 0)),
            pl.BlockSpec((N_LOGIT_ROWS, T), lambda s: (0, prev(s))),
            wspec_g, wspec_g, wspec_d,
        ],
        out_shape=[
            jax.ShapeDtypeStruct((B, S, D), F32),
            jax.ShapeDtypeStruct((B * S * SUBLANES, LANES), F32),
            jax.ShapeDtypeStruct((N_LOGIT_ROWS, B * S), F32),
            jax.ShapeDtypeStruct(wg2.shape, BF16),
            jax.ShapeDtypeStruct(wu2.shape, BF16),
            jax.ShapeDtypeStruct(wd2.shape, BF16),
        ],
        scratch_shapes=[
            pltpu.VMEM((T, D), BF16),
            pltpu.VMEM((T, 4 * N_HEADS * D_HEAD), F32),
            pltpu.VMEM((T, 4 * N_HEADS * D_HEAD), F32),
            pltpu.VMEM((T, D), BF16),
            pltpu.VMEM((T, D_HEAD), F32),
            pltpu.VMEM((T, D_HEAD), F32),
            pltpu.VMEM((N_HEADS, D_HEAD, D_HEAD), F32),
            pltpu.VMEM((N_HEADS, D_HEAD, D_HEAD), F32),
            pltpu.VMEM((T, D), F32),
            pltpu.VMEM((T, D), BF16),
        ],
        compiler_params=pltpu.CompilerParams(
            dimension_semantics=("arbitrary",), vmem_limit_bytes=VMEM_LIMIT),
        name="mixer",
    )(x, pos3, mod3, win, wout, hgrn_lb, hnw, rnw, ln1w, ln1b, invf, dmat, qdec, kdec, wr, br, x, mod3,
      wg2, wu2, wd2)
    x1, h2t, lgt, wg16, wu16, wd16 = outs
    return x1, h2t, lgt, wg16.reshape(E, D, F), wu16.reshape(E, D, F), wd16.reshape(E, F, D)


def _row_copy(src_ref, dst_ref, src_row, dst_row, sem):
    s0 = pl.multiple_of(src_row * SUBLANES, SUBLANES)
    d0 = pl.multiple_of(dst_row * SUBLANES, SUBLANES)
    return pltpu.make_async_copy(src_ref.at[pl.ds(s0, SUBLANES)], dst_ref.at[pl.ds(d0, SUBLANES)], sem)


def _from_token_tiles(buf_ref, first_row, n_rows):
    return jnp.concatenate(
        [buf_ref[pl.ds(first_row * SUBLANES + s, n_rows, stride=SUBLANES), :] for s in range(SUBLANES)],
        axis=1)


def _start_rows(src_ref, dst_ref, idx_ref, idx_base, dst_base, n_rows, sem):
    def issue(r2, carry):
        for u in range(2):
            r = r2 * 2 + u
            _row_copy(src_ref, dst_ref, idx_ref[idx_base + r], dst_base + r, sem).start(priority=u)
        return carry
    lax.fori_loop(0, n_rows // 2, issue, 0)


def _wait_rows(src_ref, dst_ref, n_rows, sem):
    pltpu.make_async_copy(src_ref.at[pl.ds(0, n_rows * SUBLANES)], dst_ref, sem).wait()


def _bf16_bits(x):
    return lax.bitcast_convert_type(x.astype(BF16).astype(F32), jnp.uint32)


def _dispatch_kernel(dest_ref, pad0_ref, padn_ref, nv_ref, h2t_ref, xs_ref,
                     buf, zbuf, load_sems, scat_sems, zsem):
    i = pl.program_id(0)
    n = pl.num_programs(0)
    TT = DISPATCH_TILE
    rows = TT * SUBLANES

    def pad_copies(wait):
        def bucket(b, carry):
            pos = pad0_ref[b]
            left = padn_ref[b]
            for size in PAD_COPY_SIZES:
                hit = (left & size) != 0
                first = 0 if wait else pl.multiple_of(pos * SUBLANES, SUBLANES)
                copy = pltpu.make_async_copy(
                    zbuf.at[pl.ds(0, size * SUBLANES)], xs_ref.at[pl.ds(first, size * SUBLANES)], zsem)

                @pl.when(hit)
                def _():
                    copy.wait() if wait else copy.start()
                pos = pos + jnp.where(hit, size, 0)
            return carry
        lax.fori_loop(0, pad0_ref.shape[0], bucket, 0)

        zrows = PAD_COPY_SIZES[0] * SUBLANES

        def unused_block(blk, carry):
            for part in range(SLOT_BLOCK // PAD_COPY_SIZES[0]):
                first = 0 if wait else pl.multiple_of(blk * SLOT_BLOCK * SUBLANES + part * zrows, zrows)
                copy = pltpu.make_async_copy(zbuf, xs_ref.at[pl.ds(first, zrows)], zsem)
                copy.wait() if wait else copy.start()
            return carry
        lax.fori_loop(nv_ref[0], xs_ref.shape[0] // (SLOT_BLOCK * SUBLANES), unused_block, 0)

    @pl.when(i == 0)
    def _():
        zbuf[...] = jnp.zeros_like(zbuf)
        pad_copies(wait=False)

    def load(tile, slot):
        src = h2t_ref.at[pl.ds(pl.multiple_of(tile * rows, rows), rows)]
        return pltpu.make_async_copy(src, buf.at[slot], load_sems.at[slot])

    def wait_scatter(slot):
        _wait_rows(buf.at[slot], xs_ref.at[pl.ds(0, rows)], TT, scat_sems.at[slot])

    @pl.when(i == 0)
    def _():
        load(0, 0).start()

    for slot in range(DISPATCH_RING):
        @pl.when((i + 1 < n) & ((i + 1) % DISPATCH_RING == slot))
        def _():
            @pl.when(i + 1 >= DISPATCH_RING)
            def _():
                wait_scatter(slot)
            load(i + 1, slot).start()

    for slot in range(DISPATCH_RING):
        @pl.when(i % DISPATCH_RING == slot)
        def _():
            load(i, slot).wait()

            for r in range(TT):
                _row_copy(buf.at[slot], xs_ref, r, dest_ref[i * TT + r],
                          scat_sems.at[slot]).start(priority=r % 2)

    @pl.when(i == n - 1)
    def _():
        for slot in range(DISPATCH_RING):
            wait_scatter(slot)
        pad_copies(wait=True)


def _dispatch_call(dest, pad_start, pad_len, n_valid, h2t, n_slots):
    N = dest.shape[0]
    TT = DISPATCH_TILE
    assert N // TT >= DISPATCH_RING and SLOT_BLOCK <= 2 * PAD_COPY_SIZES[0]
    assert SLOT_BLOCK % PAD_COPY_SIZES[0] == 0
    grid_spec = pltpu.PrefetchScalarGridSpec(
        num_scalar_prefetch=4,
        grid=(N // TT,),
        in_specs=[pl.BlockSpec(memory_space=pl.ANY)],
        out_specs=pl.BlockSpec(memory_space=pl.ANY),
        scratch_shapes=[pltpu.VMEM((DISPATCH_RING, TT * SUBLANES, LANES), F32),
                        pltpu.VMEM((PAD_COPY_SIZES[0] * SUBLANES, LANES), F32),
                        pltpu.SemaphoreType.DMA((DISPATCH_RING,)),
                        pltpu.SemaphoreType.DMA((DISPATCH_RING,)),
                        pltpu.SemaphoreType.DMA],
    )
    return pl.pallas_call(
        _dispatch_kernel,
        grid_spec=grid_spec,
        out_shape=jax.ShapeDtypeStruct((n_slots * SUBLANES, LANES), F32),
        compiler_params=pltpu.CompilerParams(dimension_semantics=("arbitrary",)),
        name="dispatch",
    )(dest, pad_start, pad_len, n_valid, h2t)


def _expert_kernel(ea_ref, eb_ref, rows_ref, nv_ref, xs_ref, wg_ref, wu_ref, wd_ref, y_ref):
    i = pl.program_id(0)
    R = SLOT_BLOCK
    G = EXPERTS_PER_GROUP
    used = i < nv_ref[0]
    n_rows = rows_ref[i]

    def ffn(xb, e):
        a = _mm(xb, wg_ref[e])
        u = _mm(xb, wu_ref[e])
        return _mm((_silu(a) * u).astype(BF16), wd_ref[e])

    for k, m in enumerate(BLOCK_ROWS):
        fewer = BLOCK_ROWS[k - 1] if k else 0

        @pl.when(used & (n_rows > fewer) & (n_rows <= m))
        def _():
            xb = _from_token_tiles(xs_ref, 0, m).astype(BF16)
            ya = ffn(xb, ea_ref[i] & (G - 1))
            yb = ffn(xb, eb_ref[i] & (G - 1))
            packed = _bf16_bits(ya) | (_bf16_bits(yb) >> 16)
            for s in range(SUBLANES):
                y_ref[pl.ds(s, m, stride=SUBLANES), :] = packed[:, s * LANES:(s + 1) * LANES]
            if m < R:
                y_ref[pl.ds(m * SUBLANES, (R - m) * SUBLANES), :] = jnp.zeros(
                    ((R - m) * SUBLANES, LANES), y_ref.dtype)

    @pl.when(jnp.logical_not(used))
    def _():
        y_ref[...] = jnp.zeros_like(y_ref)


def _expert_call(block_ea, block_eb, block_rows, n_valid, xs, wg, wu, wd):
    assert BLOCK_ROWS[-1] == SLOT_BLOCK
    R = SLOT_BLOCK
    n_blocks = xs.shape[0] // (R * SUBLANES)
    D, F = wg.shape[1], wg.shape[2]
    G = EXPERTS_PER_GROUP
    last = lambda i, nv: jnp.minimum(i, nv[0] - 1)
    grp = lambda i, ea, eb, rows, nv: (ea[last(i, nv)] // G, 0, 0)
    blk = lambda i, ea, eb, rows, nv: (last(i, nv), 0)
    grid_spec = pltpu.PrefetchScalarGridSpec(
        num_scalar_prefetch=4,
        grid=(n_blocks,),
        in_specs=[
            pl.BlockSpec((R * SUBLANES, LANES), blk),
            pl.BlockSpec((G, D, F), grp, pipeline_mode=pl.Buffered(1)),
            pl.BlockSpec((G, D, F), grp, pipeline_mode=pl.Buffered(1)),
            pl.BlockSpec((G, F, D), grp, pipeline_mode=pl.Buffered(1)),
        ],
        out_specs=pl.BlockSpec((R * SUBLANES, LANES), lambda i, ea, eb, rows, nv: (i, 0)),
    )
    return pl.pallas_call(
        _expert_kernel,
        grid_spec=grid_spec,
        out_shape=jax.ShapeDtypeStruct(xs.shape, jnp.uint32),
        compiler_params=pltpu.CompilerParams(
            dimension_semantics=("arbitrary",), vmem_limit_bytes=VMEM_LIMIT),
        name="experts",
    )(block_ea, block_eb, block_rows, n_valid, xs, wg, wu, wd)


def _final_kernel(dest_ref, yt_ref, x1_ref, g_ref, mod_ref, w_ref, b_ref, o_ref, buf, sems):
    i = pl.program_id(0)
    n_steps = pl.num_programs(0)
    TT = TOK_TILE

    @pl.when(i == 0)
    def _():
        _start_rows(yt_ref, buf.at[0], dest_ref, 0, 0, TT, sems.at[0])

    nxt = jnp.minimum(i + 1, n_steps - 1)
    gate2 = mod_ref[0, 5:6, :]
    g = g_ref[...]
    for slot in range(2):
        @pl.when(i % 2 == slot)
        def _():
            _wait_rows(yt_ref, buf.at[slot], TT, sems.at[slot])
            for r in range(TT // ROW_BLOCK):
                rows = slice(r * ROW_BLOCK, (r + 1) * ROW_BLOCK)
                packed = _from_token_tiles(buf.at[slot], r * ROW_BLOCK, ROW_BLOCK)
                ya = lax.bitcast_convert_type(packed & jnp.uint32(0xFFFF0000), F32)
                yb = lax.bitcast_convert_type(packed << 16, F32)
                y = ya * g[rows, 0:1] + yb * g[rows, 1:2]
                u = DEEPNORM_ALPHA * x1_ref[rows, :] + gate2 * y
                o_ref[rows, :] = _ln(u) * w_ref[...] + b_ref[...]
                for q in range(r * ROW_BLOCK, (r + 1) * ROW_BLOCK):
                    _row_copy(yt_ref, buf.at[1 - slot], dest_ref[nxt * TT + q], q,
                              sems.at[1 - slot]).start(priority=q % 2)

            @pl.when(i == n_steps - 1)
            def _():
                _wait_rows(yt_ref, buf.at[1 - slot], TT, sems.at[1 - slot])


def _final_call(dest, yt, x1, gates, mod3, w, b, seq_len):
    N, D = x1.shape
    TT = TOK_TILE
    per_seq = seq_len // TT
    grid_spec = pltpu.PrefetchScalarGridSpec(
        num_scalar_prefetch=1,
        grid=(N // TT,),
        in_specs=[
            pl.BlockSpec(memory_space=pl.ANY),
            pl.BlockSpec((TT, D), lambda i, d: (i, 0)),
            pl.BlockSpec((TT, 2), lambda i, d: (i, 0)),
            pl.BlockSpec((1, 6, D), lambda i, d: (i // per_seq, 0, 0)),
            pl.BlockSpec((1, D), lambda i, d: (0, 0)),
            pl.BlockSpec((1, D), lambda i, d: (0, 0)),
        ],
        out_specs=pl.BlockSpec((TT, D), lambda i, d: (i, 0)),
        scratch_shapes=[pltpu.VMEM((2, TT * SUBLANES, LANES), jnp.uint32), pltpu.SemaphoreType.DMA((2,))],
    )
    return pl.pallas_call(
        _final_kernel,
        grid_spec=grid_spec,
        out_shape=jax.ShapeDtypeStruct((N, D), F32),
        compiler_params=pltpu.CompilerParams(
            dimension_semantics=("arbitrary",), vmem_limit_bytes=VMEM_LIMIT),
        name="combine_ln",
    )(dest, yt, x1, gates, mod3, w, b)


def _route_kernel(lgt_ref, rf_ref, ri_ref, meta_ref, cnt_scr, base_scr, bstart_scr):
    ph = pl.program_id(0)
    t = pl.program_id(1)
    TR = ROUTE_TILE
    E = N_EXPERTS
    R = SLOT_BLOCK
    G = EXPERTS_PER_GROUP
    lg = lgt_ref[...]

    g = [lg[E + k:E + k + 1, :] for k in range(N_GROUPS)]
    gmax = jnp.maximum(jnp.maximum(g[0], g[1]), jnp.maximum(g[2], g[3]))
    gs = jnp.where(g[0] == gmax, 0.0, jnp.where(g[1] == gmax, 1.0, jnp.where(g[2] == gmax, 2.0, 3.0)))
    psum = (jnp.exp(g[0] - gmax) + jnp.exp(g[1] - gmax)) + (jnp.exp(g[2] - gmax) + jnp.exp(g[3] - gmax))
    p_star = 1.0 / psum
    esel = jnp.where(gs == 0.0, lg[0:G], jnp.where(gs == 1.0, lg[G:2 * G],
                                                   jnp.where(gs == 2.0, lg[2 * G:3 * G], lg[3 * G:4 * G])))
    sub = lax.broadcasted_iota(jnp.int32, (G, TR), 0).astype(F32)
    m1 = jnp.max(esel, axis=0, keepdims=True)
    i1 = jnp.min(jnp.where(esel == m1, sub, float(G)), axis=0, keepdims=True)
    es2 = jnp.where(sub == i1, -jnp.inf, esel)
    m2 = jnp.max(es2, axis=0, keepdims=True)
    i2 = jnp.min(jnp.where(es2 == m2, sub, float(G)), axis=0, keepdims=True)
    d = jnp.exp(m2 - m1)
    w1 = 1.0 / (1.0 + d)
    w2 = d * w1
    lo = jnp.minimum(i1, i2)
    hi = jnp.maximum(i1, i2)
    bucket = gs * PAIRS_PER_GROUP + (lo * (2 * G - 1 - lo) * 0.5 + (hi - lo - 1.0))
    first_is_lo = i1 < i2
    gate_a = p_star * jnp.where(first_is_lo, w1, w2)
    gate_b = p_star * jnp.where(first_is_lo, w2, w1)
    row = lax.broadcasted_iota(jnp.int32, (NB, TR), 0).astype(F32)
    oh = row == bucket
    cnt = oh.astype(F32)

    @pl.when(ph == 0)
    def _():
        @pl.when(t == 0)
        def _():
            cnt_scr[...] = jnp.zeros_like(cnt_scr)
        acc = cnt_scr[...]
        for j in range(TR // LANES):
            acc = acc + cnt[:, j * LANES:(j + 1) * LANES]
        cnt_scr[...] = acc

    @pl.when((ph == 1) & (t == 0))
    def _():
        counts = jnp.sum(cnt_scr[...], axis=1, keepdims=True)
        nblk = jnp.floor((counts + (R - 1)) * (1.0 / R))
        nblk_b = jnp.broadcast_to(nblk, (NB, LANES))
        rb = lax.broadcasted_iota(jnp.int32, (NB, NB), 0)
        cb = lax.broadcasted_iota(jnp.int32, (NB, NB), 1)
        bstart = _mm((cb < rb).astype(BF16), nblk_b.astype(BF16))
        bstart_scr[...] = bstart
        base_scr[...] = jnp.zeros_like(base_scr)
        k = lax.broadcasted_iota(jnp.int32, (NB, 1), 0).astype(F32)
        grp = sum((k >= float(m * PAIRS_PER_GROUP)).astype(F32) for m in range(1, N_GROUPS))
        p = k - grp * PAIRS_PER_GROUP
        pair_start = [i * (2 * G - 1 - i) // 2 for i in range(G - 1)]
        ia = sum((p >= float(s)).astype(F32) for s in pair_start[1:])
        ib = p - ia * (2 * G - 1 - ia) * 0.5 + ia + 1.0
        n_pad = meta_ref.shape[1]
        blk = lax.broadcasted_iota(jnp.int32, (NB, n_pad), 1).astype(F32)
        first = jnp.broadcast_to(bstart[:, 0:1], (NB, n_pad))
        last = jnp.broadcast_to((bstart + nblk_b)[:, 0:1], (NB, n_pad))
        member = (first <= blk) & (blk < last)
        ea = jnp.sum(jnp.where(member, grp * G + ia, 0.0), axis=0, keepdims=True)
        eb = jnp.sum(jnp.where(member, grp * G + ib, 0.0), axis=0, keepdims=True)
        meta_ref[...] = jnp.zeros_like(meta_ref)
        meta_ref[0:1, :] = ea.astype(jnp.int32)
        meta_ref[1:2, :] = eb.astype(jnp.int32)
        meta_ref[2:3, :] = jnp.max(last, axis=0, keepdims=True).astype(jnp.int32)
        in_blk = jnp.clip(counts - (blk - first) * R, 0.0, float(R))
        meta_ref[5:6, :] = jnp.sum(jnp.where(member, in_blk, 0.0), axis=0, keepdims=True).astype(jnp.int32)
        pad0 = jnp.broadcast_to(bstart[:, 0:1] * R + counts, (NB, NB)).T[0:1, :]
        padn = jnp.broadcast_to(nblk * R - counts, (NB, NB)).T[0:1, :]
        meta_ref[3:4, 0:NB] = pad0.astype(jnp.int32)
        meta_ref[4:5, 0:NB] = padn.astype(jnp.int32)

    @pl.when(ph == 1)
    def _():
        ri = lax.broadcasted_iota(jnp.int32, (LANES, 2 * LANES), 0)
        ci = lax.broadcasted_iota(jnp.int32, (LANES, 2 * LANES), 1)
        w = ((ci >= ri) | (ci >= LANES)).astype(BF16)
        base = base_scr[...]
        slot0 = bstart_scr[...] * R
        ds = []
        for j in range(TR // LANES):
            sl = slice(j * LANES, (j + 1) * LANES)
            blk = cnt[:, sl]
            res = _mm(blk.astype(BF16), w)
            val = slot0 + base + (res[:, 0:LANES] - blk)
            ds.append(jnp.sum(jnp.where(oh[:, sl], val, 0.0), axis=0, keepdims=True))
            base = base + res[:, LANES:2 * LANES]
        base_scr[...] = base
        ri_ref[...] = jnp.zeros_like(ri_ref)
        ri_ref[0:1, :] = jnp.concatenate(ds, axis=1).astype(jnp.int32)
        rf_ref[...] = jnp.zeros_like(rf_ref)
        rf_ref[0:1, :] = gate_a
        rf_ref[1:2, :] = gate_b


def _route_call(lgt, n_blocks):
    rows, N = lgt.shape
    TR = ROUTE_TILE
    E = NB
    n_pad = -(-n_blocks // LANES) * LANES
    return pl.pallas_call(
        _route_kernel,
        grid=(2, N // TR),
        in_specs=[pl.BlockSpec((rows, TR), lambda ph, t: (0, t))],
        out_specs=[
            pl.BlockSpec((SUBLANES, TR), lambda ph, t: (0, ph * t)),
            pl.BlockSpec((SUBLANES, TR), lambda ph, t: (0, ph * t)),
            pl.BlockSpec((SUBLANES, n_pad), lambda ph, t: (0, 0)),
        ],
        out_shape=[
            jax.ShapeDtypeStruct((SUBLANES, N), F32),
            jax.ShapeDtypeStruct((SUBLANES, N), jnp.int32),
            jax.ShapeDtypeStruct((SUBLANES, n_pad), jnp.int32),
        ],
        scratch_shapes=[pltpu.VMEM((E, LANES), F32), pltpu.VMEM((E, LANES), F32), pltpu.VMEM((E, LANES), F32)],
        compiler_params=pltpu.CompilerParams(dimension_semantics=("arbitrary", "arbitrary")),
        name="route",
    )(lgt)


def kernel(x, c, positions, w_ada, b_ada, w_in, w_out, hgrn_lb, hgrn_norm_w, ret_norm_w, post_ln1_w,
           post_ln1_b, w_rg, b_rg, w_re, b_re, w_gate, w_up, w_down, post_ln2_w, post_ln2_b):
    B, S, D = x.shape
    N = B * S
    hw = N_HEADS * D_HEAD

    mod3 = _ada_call(c, w_ada[0], b_ada[0]).reshape(B, 6, D)

    win = w_in[0].astype(BF16)
    wout = w_out[0].astype(BF16)
    pad = LANES - N_EXPERTS - N_GROUPS
    wr = jnp.concatenate([w_re[0], w_rg[0], jnp.zeros((D, pad), F32)], axis=1).astype(BF16)
    br = jnp.concatenate([b_re[0], b_rg[0], jnp.zeros((pad,), F32)]).reshape(1, LANES)

    x1, h2t, lgt, wg16, wu16, wd16 = _mixer_call(
        x, positions, mod3, win, wout, hgrn_lb, hgrn_norm_w[0].reshape(1, hw),
        ret_norm_w[0].reshape(1, hw), post_ln1_w[0].reshape(1, D), post_ln1_b[0].reshape(1, D), wr, br,
        w_gate[0], w_up[0], w_down[0])

    n_blocks = N // SLOT_BLOCK + N_GROUPS * PAIRS_PER_GROUP
    rf, ri, meta = _route_call(lgt, n_blocks)
    dest = ri[0]
    gates = rf[0:2].T
    xs = _dispatch_call(dest, meta[3, :NB], meta[4, :NB], meta[2, 0:1], h2t, n_blocks * SLOT_BLOCK)
    yt = _expert_call(meta[0, :n_blocks], meta[1, :n_blocks], meta[5, :n_blocks], meta[2, 0:1], xs,
                      wg16, wu16, wd16)
    out = _final_call(dest, yt, x1.reshape(N, D), gates, mod3, post_ln2_w[0].reshape(1, D),
                      post_ln2_b[0].reshape(1, D), S)
    return out.reshape(B, S, D)
```

```python
import functools

import numpy as np
import jax
import jax.numpy as jnp
from jax import lax
from jax.experimental import pallas as pl
from jax.experimental.pallas import tpu as pltpu

F32 = jnp.float32
BF16 = jnp.bfloat16

D_MODEL = 1024
N_HEADS = 4
D_HEAD = 128
HGRN_CHUNK = 64
RET_CHUNK = 256
ROPE_BASE = 10000.0
N_EXPERTS = 32
EXPERTS_PER_GROUP = 8
N_GROUPS = 4
DEEPNORM_ALPHA = 2.0 ** 0.25
LN_EPS = 1e-5

SEQ_TILE = 512
ROW_BLOCK = 128
PAIRS_PER_GROUP = EXPERTS_PER_GROUP * (EXPERTS_PER_GROUP - 1) // 2
NB = 128
SLOT_BLOCK = 256
BLOCK_ROWS = (64, 128, 192, 256)
DISPATCH_TILE = 1024
DISPATCH_RING = 3
PAD_COPY_SIZES = (128, 64, 32, 16, 8, 4, 2, 1)
TOK_TILE = 1024
ROUTE_TILE = 2048
N_LOGIT_ROWS = 40
LANES = 128
SUBLANES = 8
V7X_VMEM_BYTES = 64 * 1024 * 1024
VMEM_LIMIT = V7X_VMEM_BYTES - 8 * 1024 * 1024


def _mm(a, b):
    return jnp.dot(a, b, preferred_element_type=F32)


def _mm_nt(a, b):
    return lax.dot_general(a, b, (((1,), (1,)), ((), ())), preferred_element_type=F32)


def _mm_tn(a, b):
    return lax.dot_general(a, b, (((0,), (0,)), ((), ())), preferred_element_type=F32)


def _sigmoid(x):
    return 1.0 / (1.0 + jnp.exp(-x))


def _silu(x):
    return x * _sigmoid(x)


def _ln(x):
    mu = jnp.mean(x, axis=-1, keepdims=True)
    xc = x - mu
    var = jnp.mean(xc * xc, axis=-1, keepdims=True)
    return xc * lax.rsqrt(var + LN_EPS)


def _const_spec(shape):
    nd = len(shape)
    return pl.BlockSpec(shape, lambda *_: (0,) * nd, pipeline_mode=pl.Buffered(1))


def _ada_kernel(c_ref, w_ref, b_ref, o_ref):
    ca = _silu(c_ref[...])
    o_ref[...] = _mm(ca.astype(BF16), w_ref[...].astype(BF16)) + b_ref[...]


def _ada_call(c, w_ada, b_ada):
    B, D = c.shape
    n_out = w_ada.shape[1]
    return pl.pallas_call(
        _ada_kernel,
        grid=(n_out // D,),
        in_specs=[pl.BlockSpec((B, D), lambda j: (0, 0)),
                  pl.BlockSpec((D, D), lambda j: (0, j)),
                  pl.BlockSpec((1, D), lambda j: (0, j))],
        out_specs=pl.BlockSpec((B, D), lambda j: (0, j)),
        out_shape=jax.ShapeDtypeStruct((B, n_out), F32),
        name="ada_mod",
    )(c, w_ada, b_ada.reshape(1, n_out))


def _mixer_kernel(x_ref, pos_ref, mod_ref, win_ref, wout_ref, lb_ref, hnw_ref, rnw_ref,
                  ln1w_ref, ln1b_ref, invf_ref, dmat_ref, qdec_ref, kdec_ref, wr_ref, br_ref,
                  xp_ref, modp_ref, wg32_ref, wu32_ref, wd32_ref,
                  x1_ref, h2t_ref, lg_ref, wg16_ref, wu16_ref, wd16_ref,
                  h_scr, proj_scr, proj2_scr, o_scr, cos_scr, sin_scr, sh_scr, sr_scr, y_scr, h2_scr,
                  *, ret_chunk_decay, tiles_per_seq):
    T = SEQ_TILE
    D = D_MODEL
    dh = D_HEAD
    hw = N_HEADS * dh
    hw4 = 4 * hw
    step = pl.program_id(0)
    n_tiles = pl.num_programs(0) - 1

    wg16_ref[...] = wg32_ref[...].astype(BF16)
    wu16_ref[...] = wu32_ref[...].astype(BF16)
    wd16_ref[...] = wd32_ref[...].astype(BF16)

    @pl.when(step == 0)
    def _():
        y_scr[...] = jnp.zeros_like(y_scr)

    @pl.when((step < n_tiles) & (step % tiles_per_seq == 0))
    def _():
        sh_scr[...] = jnp.zeros_like(sh_scr)
        sr_scr[...] = jnp.zeros_like(sr_scr)

    gate1p = modp_ref[0, 2:3, :]
    shift2p = modp_ref[0, 3:4, :]
    scale2p = modp_ref[0, 4:5, :]

    def post_block(r):
        rows = slice(r * ROW_BLOCK, (r + 1) * ROW_BLOCK)
        u = DEEPNORM_ALPHA * xp_ref[0, rows, :] + gate1p * y_scr[rows, :]
        x1 = _ln(u) * ln1w_ref[...] + ln1b_ref[...]
        x1_ref[0, rows, :] = x1
        h2 = (_ln(x1) * (1.0 + scale2p) + shift2p).astype(BF16)
        h2_scr[rows, :] = h2
        h2f = h2.astype(F32)
        for s in range(SUBLANES):
            h2t_ref[pl.ds(r * ROW_BLOCK * SUBLANES + s, ROW_BLOCK, stride=SUBLANES), :] = (
                h2f[:, s * LANES:(s + 1) * LANES])

    shift1 = mod_ref[0, 0:1, :]
    scale1 = mod_ref[0, 1:2, :]

    for r in range(T // ROW_BLOCK):
        rows = slice(r * ROW_BLOCK, (r + 1) * ROW_BLOCK)
        h = _ln(x_ref[0, rows, :]) * (1.0 + scale1) + shift1
        h_scr[rows, :] = h.astype(BF16)

    n_post = T // ROW_BLOCK
    proj_scr[...] = _mm(h_scr[...], win_ref[:, 0:hw4])
    for r in range(n_post // 2):
        post_block(r)
    proj2_scr[...] = _mm(h_scr[...], win_ref[:, hw4:2 * hw4])
    for r in range(n_post // 2, n_post):
        post_block(r)
    lg_ref[...] = (_mm(h2_scr[...], wr_ref[...]) + br_ref[...]).T[0:N_LOGIT_ROWS, :]

    pos = pos_ref[0].astype(F32)
    ang_t = invf_ref[...] * pos
    cos_t = jnp.cos(ang_t)
    sin_t = jnp.sin(ang_t)
    cos_scr[...] = jnp.concatenate([cos_t, cos_t], axis=0).T
    sin_scr[...] = jnp.concatenate([-sin_t, sin_t], axis=0).T

    l0 = lb_ref[0:1, :]
    l1 = lb_ref[1:2, :]
    lmax = jnp.maximum(l0, l1)
    e0 = jnp.exp(l0 - lmax)
    e1 = jnp.exp(l1 - lmax)
    lb = e0 / (e0 + e1)
    hnw = hnw_ref[...]

    C = HGRN_CHUNK
    rr = lax.broadcasted_iota(jnp.int32, (C, C), 0)
    cc = lax.broadcasted_iota(jnp.int32, (C, C), 1)
    causal = rr >= cc
    row_id = lax.broadcasted_iota(jnp.int32, (C, hw), 0)

    def hgrn_chunk(c):
        rows = slice(c * C, (c + 1) * C)
        logfs, ks = [], []
        for hd in range(N_HEADS):
            z = proj_scr[rows, hw + hd * dh: hw + (hd + 1) * dh]
            e = jnp.exp(-jnp.abs(z))
            r = 1.0 / (1.0 + e)
            er = e * r
            zp = z >= 0
            sig_pos = jnp.where(zp, r, er)
            sig_neg = jnp.where(zp, er, r)
            lbh = lb[:, hd * dh:(hd + 1) * dh]
            logfs.append(jnp.log(lbh + (1.0 - lbh) * sig_pos))
            ks.append((1.0 - lbh) * sig_neg)
        bsum = jnp.concatenate(logfs, axis=1)
        for k in (1, 2, 4):
            bsum = bsum + jnp.where(row_id >= k, pltpu.roll(bsum, k, 0), 0.0)
        for k in (8, 16, 32):
            bsum = bsum + jnp.concatenate([jnp.zeros((k, hw), F32), bsum[0:C - k, :]], axis=0)
        for hd in range(N_HEADS):
            cols = slice(hd * dh, (hd + 1) * dh)
            b = bsum[:, cols]
            b_last = b[C - 1:C, :]
            q = proj_scr[rows, cols]
            v = proj_scr[rows, 2 * hw + hd * dh: 2 * hw + (hd + 1) * dh].astype(BF16)
            g = proj_scr[rows, 3 * hw + hd * dh: 3 * hw + (hd + 1) * dh]
            k = ks[hd]
            q_in = (_silu(q) * jnp.exp(b)).astype(BF16)
            k_in = (k * jnp.exp(-b)).astype(BF16)
            k_dec = (k * jnp.exp(b_last - b)).astype(BF16)
            st = sh_scr[hd]
            a = jnp.where(causal, _mm_nt(q_in, k_in), 0.0)
            o = _mm(a.astype(BF16), v) + _mm_nt(q_in, st.astype(BF16))
            sh_scr[hd] = st * jnp.exp(b_last) + _mm_tn(v, k_dec)
            ms = jnp.mean(o * o, axis=-1, keepdims=True)
            on = o * lax.rsqrt(ms + LN_EPS) * hnw[:, hd * dh:(hd + 1) * dh] * _silu(g)
            o_scr[rows, hd * dh:(hd + 1) * dh] = on.astype(BF16)

    rnw = rnw_ref[...]
    RC = RET_CHUNK
    q_scale = dh ** -0.5

    def retention_unit(sub, hd):
        rows = slice(sub * RC, (sub + 1) * RC)
        cs = cos_scr[rows, :]
        sn = sin_scr[rows, :]
        rq = proj2_scr[rows, hd * dh: (hd + 1) * dh]
        rk = proj2_scr[rows, hw + hd * dh: hw + (hd + 1) * dh]
        v = proj2_scr[rows, 2 * hw + hd * dh: 2 * hw + (hd + 1) * dh].astype(BF16)
        g = proj2_scr[rows, 3 * hw + hd * dh: 3 * hw + (hd + 1) * dh]
        q = (rq * cs + pltpu.roll(rq, dh // 2, 1) * sn) * q_scale
        k = rk * cs + pltpu.roll(rk, dh // 2, 1) * sn
        qb = q.astype(BF16)
        st = sr_scr[hd]
        s = _mm_nt(qb, k.astype(BF16)) * dmat_ref[hd]
        o = _mm(s.astype(BF16), v) + _mm_nt(qb, st.astype(BF16)) * qdec_ref[hd]
        sr_scr[hd] = st * ret_chunk_decay[hd] + _mm_tn(v, (k * kdec_ref[hd]).astype(BF16))
        mu = jnp.mean(o, axis=-1, keepdims=True)
        oc = o - mu
        var = jnp.mean(oc * oc, axis=-1, keepdims=True)
        on = oc * lax.rsqrt(var + LN_EPS) * rnw[:, hd * dh:(hd + 1) * dh] * _silu(g)
        o_scr[rows, hw + hd * dh: hw + (hd + 1) * dh] = on.astype(BF16)

    n_chunks = T // C
    units = [(sub, hd) for sub in range(T // RC) for hd in range(N_HEADS)]
    for c in range(n_chunks):
        hgrn_chunk(c)
        for sub, hd in units[c * len(units) // n_chunks:(c + 1) * len(units) // n_chunks]:
            retention_unit(sub, hd)

    y_scr[...] = _mm(o_scr[...], wout_ref[...])


def _retention_tables():
    h = np.arange(N_HEADS, dtype=np.float64)
    log_gamma = np.log(1.0 - np.exp2(-5.0 - h))
    idx = np.arange(RET_CHUNK, dtype=np.float64)
    rel = idx[:, None] - idx[None, :]
    dmat = np.where(rel >= 0, np.exp(np.maximum(rel, 0.0)[None] * log_gamma[:, None, None]), 0.0)
    qdec = np.exp((idx + 1.0)[None, :] * log_gamma[:, None])
    kdec = np.exp((RET_CHUNK - 1.0 - idx)[None, :] * log_gamma[:, None])
    cdec = np.exp(RET_CHUNK * log_gamma)
    bc = lambda t: np.broadcast_to(t[:, :, None], (N_HEADS, RET_CHUNK, D_HEAD))
    return (jnp.asarray(dmat, F32), jnp.asarray(bc(qdec), F32), jnp.asarray(bc(kdec), F32),
            tuple(float(np.float32(v)) for v in cdec))


def _mixer_call(x, positions, mod3, win, wout, hgrn_lb, hnw, rnw, ln1w, ln1b, wr, br, wg, wu, wd):
    B, S, D = x.shape
    T = SEQ_TILE
    nj = S // T
    dmat, qdec, kdec, cdec = _retention_tables()
    inv_freq = np.power(ROPE_BASE, -np.arange(0, D_HEAD, 2, dtype=np.float64) / D_HEAD)
    invf = jnp.asarray(inv_freq[:, None], F32)
    pos3 = positions.reshape(B, 1, S)
    kern = functools.partial(_mixer_kernel, ret_chunk_decay=cdec, tiles_per_seq=nj)
    n_tiles = B * nj
    cur = lambda s: jnp.minimum(s, n_tiles - 1)
    prev = lambda s: jnp.maximum(s - 1, 0)
    E, _, F = wg.shape
    wg2, wu2, wd2 = wg.reshape(E * D, F), wu.reshape(E * D, F), wd.reshape(E * F, D)
    rg, rd = (E * D) // n_tiles, (E * F) // n_tiles
    assert rg * n_tiles == E * D and rd * n_tiles == E * F and rd % 16 == 0
    wspec_g = pl.BlockSpec((rg, F), lambda s: (cur(s), 0))
    wspec_d = pl.BlockSpec((rd, D), lambda s: (cur(s), 0))
    outs = pl.pallas_call(
        kern,
        grid=(n_tiles + 1,),
        in_specs=[
            pl.BlockSpec((1, T, D), lambda s: (cur(s) // nj, cur(s) % nj, 0)),
            pl.BlockSpec((1, 1, T), lambda s: (cur(s) // nj, 0, cur(s) % nj)),
            pl.BlockSpec((1, 6, D), lambda s: (cur(s) // nj, 0, 0)),
            _const_spec(win.shape), _const_spec(wout.shape), _const_spec(hgrn_lb.shape),
            _const_spec(hnw.shape), _const_spec(rnw.shape), _const_spec(ln1w.shape),
            _const_spec(ln1b.shape), _const_spec(invf.shape), _const_spec(dmat.shape),
            _const_spec(qdec.shape), _const_spec(kdec.shape), _const_spec(wr.shape),
            _const_spec(br.shape),
            pl.BlockSpec((1, T, D), lambda s: (prev(s) // nj, prev(s) % nj, 0)),
            pl.BlockSpec((1, 6, D), lambda s: (prev(s) // nj, 0, 0)),
            wspec_g, wspec_g, wspec_d,
        ],
        out_specs=[
            pl.BlockSpec((1, T, D), lambda s: (prev(s) // nj, prev(s) % nj, 0)),
            pl.BlockSpec((T * SUBLANES, LANES), lambda s: (prev(s), 0)),
            pl.BlockSpec((N_LOGIT_ROWS, T), lambda s: (0, prev(s))),
            wspec_g, wspec_g, wspec_d,
        ],
        out_shape=[
            jax.ShapeDtypeStruct((B, S, D), F32),
            jax.ShapeDtypeStruct((B * S * SUBLANES, LANES), F32),
            jax.ShapeDtypeStruct((N_LOGIT_ROWS, B * S), F32),
            jax.ShapeDtypeStruct(wg2.shape, BF16),
            jax.ShapeDtypeStruct(wu2.shape, BF16),
            jax.ShapeDtypeStruct(wd2.shape, BF16),
        ],
        scratch_shapes=[
            pltpu.VMEM((T, D), BF16),
            pltpu.VMEM((T, 4 * N_HEADS * D_HEAD), F32),
            pltpu.VMEM((T, 4 * N_HEADS * D_HEAD), F32),
            pltpu.VMEM((T, D), BF16),
            pltpu.VMEM((T, D_HEAD), F32),
            pltpu.VMEM((T, D_HEAD), F32),
            pltpu.VMEM((N_HEADS, D_HEAD, D_HEAD), F32),
            pltpu.VMEM((N_HEADS, D_HEAD, D_HEAD), F32),
            pltpu.VMEM((T, D), F32),
            pltpu.VMEM((T, D), BF16),
        ],
        compiler_params=pltpu.CompilerParams(
            dimension_semantics=("arbitrary",), vmem_limit_bytes=VMEM_LIMIT),
        name="mixer",
    )(x, pos3, mod3, win, wout, hgrn_lb, hnw, rnw, ln1w, ln1b, invf, dmat, qdec, kdec, wr, br, x, mod3,
      wg2, wu2, wd2)
    x1, h2t, lgt, wg16, wu16, wd16 = outs
    return x1, h2t, lgt, wg16.reshape(E, D, F), wu16.reshape(E, D, F), wd16.reshape(E, F, D)


def _row_copy(src_ref, dst_ref, src_row, dst_row, sem):
    s0 = pl.multiple_of(src_row * SUBLANES, SUBLANES)
    d0 = pl.multiple_of(dst_row * SUBLANES, SUBLANES)
    return pltpu.make_async_copy(src_ref.at[pl.ds(s0, SUBLANES)], dst_ref.at[pl.ds(d0, SUBLANES)], sem)


def _from_token_tiles(buf_ref, first_row, n_rows):
    return jnp.concatenate(
        [buf_ref[pl.ds(first_row * SUBLANES + s, n_rows, stride=SUBLANES), :] for s in range(SUBLANES)],
        axis=1)


def _start_rows(src_ref, dst_ref, idx_ref, idx_base, dst_base, n_rows, sem):
    def issue(r2, carry):
        for u in range(2):
            r = r2 * 2 + u
            _row_copy(src_ref, dst_ref, idx_ref[idx_base + r], dst_base + r, sem).start(priority=u)
        return carry
    lax.fori_loop(0, n_rows // 2, issue, 0)


def _wait_rows(src_ref, dst_ref, n_rows, sem):
    pltpu.make_async_copy(src_ref.at[pl.ds(0, n_rows * SUBLANES)], dst_ref, sem).wait()


def _bf16_bits(x):
    return lax.bitcast_convert_type(x.astype(BF16).astype(F32), jnp.uint32)


def _dispatch_kernel(dest_ref, pad0_ref, padn_ref, nv_ref, h2t_ref, xs_ref,
                     buf, zbuf, load_sems, scat_sems, zsem):
    i = pl.program_id(0)
    n = pl.num_programs(0)
    TT = DISPATCH_TILE
    rows = TT * SUBLANES

    def pad_copies(wait):
        def bucket(b, carry):
            pos = pad0_ref[b]
            left = padn_ref[b]
            for size in PAD_COPY_SIZES:
                hit = (left & size) != 0
                first = 0 if wait else pl.multiple_of(pos * SUBLANES, SUBLANES)
                copy = pltpu.make_async_copy(
                    zbuf.at[pl.ds(0, size * SUBLANES)], xs_ref.at[pl.ds(first, size * SUBLANES)], zsem)

                @pl.when(hit)
                def _():
                    copy.wait() if wait else copy.start()
                pos = pos + jnp.where(hit, size, 0)
            return carry
        lax.fori_loop(0, pad0_ref.shape[0], bucket, 0)

        zrows = PAD_COPY_SIZES[0] * SUBLANES

        def unused_block(blk, carry):
            for part in range(SLOT_BLOCK // PAD_COPY_SIZES[0]):
                first = 0 if wait else pl.multiple_of(blk * SLOT_BLOCK * SUBLANES + part * zrows, zrows)
                copy = pltpu.make_async_copy(zbuf, xs_ref.at[pl.ds(first, zrows)], zsem)
                copy.wait() if wait else copy.start()
            return carry
        lax.fori_loop(nv_ref[0], xs_ref.shape[0] // (SLOT_BLOCK * SUBLANES), unused_block, 0)

    @pl.when(i == 0)
    def _():
        zbuf[...] = jnp.zeros_like(zbuf)
        pad_copies(wait=False)

    def load(tile, slot):
        src = h2t_ref.at[pl.ds(pl.multiple_of(tile * rows, rows), rows)]
        return pltpu.make_async_copy(src, buf.at[slot], load_sems.at[slot])

    def wait_scatter(slot):
        _wait_rows(buf.at[slot], xs_ref.at[pl.ds(0, rows)], TT, scat_sems.at[slot])

    @pl.when(i == 0)
    def _():
        load(0, 0).start()

    for slot in range(DISPATCH_RING):
        @pl.when((i + 1 < n) & ((i + 1) % DISPATCH_RING == slot))
        def _():
            @pl.when(i + 1 >= DISPATCH_RING)
            def _():
                wait_scatter(slot)
            load(i + 1, slot).start()

    for slot in range(DISPATCH_RING):
        @pl.when(i % DISPATCH_RING == slot)
        def _():
            load(i, slot).wait()

            for r in range(TT):
                _row_copy(buf.at[slot], xs_ref, r, dest_ref[i * TT + r],
                          scat_sems.at[slot]).start(priority=r % 2)

    @pl.when(i == n - 1)
    def _():
        for slot in range(DISPATCH_RING):
            wait_scatter(slot)
        pad_copies(wait=True)


def _dispatch_call(dest, pad_start, pad_len, n_valid, h2t, n_slots):
    N = dest.shape[0]
    TT = DISPATCH_TILE
    assert N // TT >= DISPATCH_RING and SLOT_BLOCK <= 2 * PAD_COPY_SIZES[0]
    assert SLOT_BLOCK % PAD_COPY_SIZES[0] == 0
    grid_spec = pltpu.PrefetchScalarGridSpec(
        num_scalar_prefetch=4,
        grid=(N // TT,),
        in_specs=[pl.BlockSpec(memory_space=pl.ANY)],
        out_specs=pl.BlockSpec(memory_space=pl.ANY),
        scratch_shapes=[pltpu.VMEM((DISPATCH_RING, TT * SUBLANES, LANES), F32),
                        pltpu.VMEM((PAD_COPY_SIZES[0] * SUBLANES, LANES), F32),
                        pltpu.SemaphoreType.DMA((DISPATCH_RING,)),
                        pltpu.SemaphoreType.DMA((DISPATCH_RING,)),
                        pltpu.SemaphoreType.DMA],
    )
    return pl.pallas_call(
        _dispatch_kernel,
        grid_spec=grid_spec,
        out_shape=jax.ShapeDtypeStruct((n_slots * SUBLANES, LANES), F32),
        compiler_params=pltpu.CompilerParams(dimension_semantics=("arbitrary",)),
        name="dispatch",
    )(dest, pad_start, pad_len, n_valid, h2t)


def _expert_kernel(ea_ref, eb_ref, rows_ref, nv_ref, xs_ref, wg_ref, wu_ref, wd_ref, y_ref):
    i = pl.program_id(0)
    R = SLOT_BLOCK
    G = EXPERTS_PER_GROUP
    used = i < nv_ref[0]
    n_rows = rows_ref[i]

    def ffn(xb, e):
        a = _mm(xb, wg_ref[e])
        u = _mm(xb, wu_ref[e])
        return _mm((_silu(a) * u).astype(BF16), wd_ref[e])

    for k, m in enumerate(BLOCK_ROWS):
        fewer = BLOCK_ROWS[k - 1] if k else 0

        @pl.when(used & (n_rows > fewer) & (n_rows <= m))
        def _():
            xb = _from_token_tiles(xs_ref, 0, m).astype(BF16)
            ya = ffn(xb, ea_ref[i] & (G - 1))
            yb = ffn(xb, eb_ref[i] & (G - 1))
            packed = _bf16_bits(ya) | (_bf16_bits(yb) >> 16)
            for s in range(SUBLANES):
                y_ref[pl.ds(s, m, stride=SUBLANES), :] = packed[:, s * LANES:(s + 1) * LANES]
            if m < R:
                y_ref[pl.ds(m * SUBLANES, (R - m) * SUBLANES), :] = jnp.zeros(
                    ((R - m) * SUBLANES, LANES), y_ref.dtype)

    @pl.when(jnp.logical_not(used))
    def _():
        y_ref[...] = jnp.zeros_like(y_ref)


def _expert_call(block_ea, block_eb, block_rows, n_valid, xs, wg, wu, wd):
    assert BLOCK_ROWS[-1] == SLOT_BLOCK
    R = SLOT_BLOCK
    n_blocks = xs.shape[0] // (R * SUBLANES)
    D, F = wg.shape[1], wg.shape[2]
    G = EXPERTS_PER_GROUP
    last = lambda i, nv: jnp.minimum(i, nv[0] - 1)
    grp = lambda i, ea, eb, rows, nv: (ea[last(i, nv)] // G, 0, 0)
    blk = lambda i, ea, eb, rows, nv: (last(i, nv), 0)
    grid_spec = pltpu.PrefetchScalarGridSpec(
        num_scalar_prefetch=4,
        grid=(n_blocks,),
        in_specs=[
            pl.BlockSpec((R * SUBLANES, LANES), blk),
            pl.BlockSpec((G, D, F), grp, pipeline_mode=pl.Buffered(1)),
            pl.BlockSpec((G, D, F), grp, pipeline_mode=pl.Buffered(1)),
            pl.BlockSpec((G, F, D), grp, pipeline_mode=pl.Buffered(1)),
        ],
        out_specs=pl.BlockSpec((R * SUBLANES, LANES), lambda i, ea, eb, rows, nv: (i, 0)),
    )
    return pl.pallas_call(
        _expert_kernel,
        grid_spec=grid_spec,
        out_shape=jax.ShapeDtypeStruct(xs.shape, jnp.uint32),
        compiler_params=pltpu.CompilerParams(
            dimension_semantics=("arbitrary",), vmem_limit_bytes=VMEM_LIMIT),
        name="experts",
    )(block_ea, block_eb, block_rows, n_valid, xs, wg, wu, wd)


def _final_kernel(dest_ref, yt_ref, x1_ref, g_ref, mod_ref, w_ref, b_ref, o_ref, buf, sems):
    i = pl.program_id(0)
    n_steps = pl.num_programs(0)
    TT = TOK_TILE

    @pl.when(i == 0)
    def _():
        _start_rows(yt_ref, buf.at[0], dest_ref, 0, 0, TT, sems.at[0])

    nxt = jnp.minimum(i + 1, n_steps - 1)
    gate2 = mod_ref[0, 5:6, :]
    g = g_ref[...]
    for slot in range(2):
        @pl.when(i % 2 == slot)
        def _():
            _wait_rows(yt_ref, buf.at[slot], TT, sems.at[slot])
            for r in range(TT // ROW_BLOCK):
                rows = slice(r * ROW_BLOCK, (r + 1) * ROW_BLOCK)
                packed = _from_token_tiles(buf.at[slot], r * ROW_BLOCK, ROW_BLOCK)
                ya = lax.bitcast_convert_type(packed & jnp.uint32(0xFFFF0000), F32)
                yb = lax.bitcast_convert_type(packed << 16, F32)
                y = ya * g[rows, 0:1] + yb * g[rows, 1:2]
                u = DEEPNORM_ALPHA * x1_ref[rows, :] + gate2 * y
                o_ref[rows, :] = _ln(u) * w_ref[...] + b_ref[...]
                for q in range(r * ROW_BLOCK, (r + 1) * ROW_BLOCK):
                    _row_copy(yt_ref, buf.at[1 - slot], dest_ref[nxt * TT + q], q,
                              sems.at[1 - slot]).start(priority=q % 2)

            @pl.when(i == n_steps - 1)
            def _():
                _wait_rows(yt_ref, buf.at[1 - slot], TT, sems.at[1 - slot])


def _final_call(dest, yt, x1, gates, mod3, w, b, seq_len):
    N, D = x1.shape
    TT = TOK_TILE
    per_seq = seq_len // TT
    grid_spec = pltpu.PrefetchScalarGridSpec(
        num_scalar_prefetch=1,
        grid=(N // TT,),
        in_specs=[
            pl.BlockSpec(memory_space=pl.ANY),
            pl.BlockSpec((TT, D), lambda i, d: (i, 0)),
            pl.BlockSpec((TT, 2), lambda i, d: (i, 0)),
            pl.BlockSpec((1, 6, D), lambda i, d: (i // per_seq, 0, 0)),
            pl.BlockSpec((1, D), lambda i, d: (0, 0)),
            pl.BlockSpec((1, D), lambda i, d: (0, 0)),
        ],
        out_specs=pl.BlockSpec((TT, D), lambda i, d: (i, 0)),
        scratch_shapes=[pltpu.VMEM((2, TT * SUBLANES, LANES), jnp.uint32), pltpu.SemaphoreType.DMA((2,))],
    )
    return pl.pallas_call(
        _final_kernel,
        grid_spec=grid_spec,
        out_shape=jax.ShapeDtypeStruct((N, D), F32),
        compiler_params=pltpu.CompilerParams(
            dimension_semantics=("arbitrary",), vmem_limit_bytes=VMEM_LIMIT),
        name="combine_ln",
    )(dest, yt, x1, gates, mod3, w, b)


def _route_kernel(lgt_ref, rf_ref, ri_ref, meta_ref, cnt_scr, base_scr, bstart_scr):
    ph = pl.program_id(0)
    t = pl.program_id(1)
    TR = ROUTE_TILE
    E = N_EXPERTS
    R = SLOT_BLOCK
    G = EXPERTS_PER_GROUP
    lg = lgt_ref[...]

    g = [lg[E + k:E + k + 1, :] for k in range(N_GROUPS)]
    gmax = jnp.maximum(jnp.maximum(g[0], g[1]), jnp.maximum(g[2], g[3]))
    gs = jnp.where(g[0] == gmax, 0.0, jnp.where(g[1] == gmax, 1.0, jnp.where(g[2] == gmax, 2.0, 3.0)))
    psum = (jnp.exp(g[0] - gmax) + jnp.exp(g[1] - gmax)) + (jnp.exp(g[2] - gmax) + jnp.exp(g[3] - gmax))
    p_star = 1.0 / psum
    esel = jnp.where(gs == 0.0, lg[0:G], jnp.where(gs == 1.0, lg[G:2 * G],
                                                   jnp.where(gs == 2.0, lg[2 * G:3 * G], lg[3 * G:4 * G])))
    sub = lax.broadcasted_iota(jnp.int32, (G, TR), 0).astype(F32)
    m1 = jnp.max(esel, axis=0, keepdims=True)
    i1 = jnp.min(jnp.where(esel == m1, sub, float(G)), axis=0, keepdims=True)
    es2 = jnp.where(sub == i1, -jnp.inf, esel)
    m2 = jnp.max(es2, axis=0, keepdims=True)
    i2 = jnp.min(jnp.where(es2 == m2, sub, float(G)), axis=0, keepdims=True)
    d = jnp.exp(m2 - m1)
    w1 = 1.0 / (1.0 + d)
    w2 = d * w1
    lo = jnp.minimum(i1, i2)
    hi = jnp.maximum(i1, i2)
    bucket = gs * PAIRS_PER_GROUP + (lo * (2 * G - 1 - lo) * 0.5 + (hi - lo - 1.0))
    first_is_lo = i1 < i2
    gate_a = p_star * jnp.where(first_is_lo, w1, w2)
    gate_b = p_star * jnp.where(first_is_lo, w2, w1)
    row = lax.broadcasted_iota(jnp.int32, (NB, TR), 0).astype(F32)
    oh = row == bucket
    cnt = oh.astype(F32)

    @pl.when(ph == 0)
    def _():
        @pl.when(t == 0)
        def _():
            cnt_scr[...] = jnp.zeros_like(cnt_scr)
        acc = cnt_scr[...]
        for j in range(TR // LANES):
            acc = acc + cnt[:, j * LANES:(j + 1) * LANES]
        cnt_scr[...] = acc

    @pl.when((ph == 1) & (t == 0))
    def _():
        counts = jnp.sum(cnt_scr[...], axis=1, keepdims=True)
        nblk = jnp.floor((counts + (R - 1)) * (1.0 / R))
        nblk_b = jnp.broadcast_to(nblk, (NB, LANES))
        rb = lax.broadcasted_iota(jnp.int32, (NB, NB), 0)
        cb = lax.broadcasted_iota(jnp.int32, (NB, NB), 1)
        bstart = _mm((cb < rb).astype(BF16), nblk_b.astype(BF16))
        bstart_scr[...] = bstart
        base_scr[...] = jnp.zeros_like(base_scr)
        k = lax.broadcasted_iota(jnp.int32, (NB, 1), 0).astype(F32)
        grp = sum((k >= float(m * PAIRS_PER_GROUP)).astype(F32) for m in range(1, N_GROUPS))
        p = k - grp * PAIRS_PER_GROUP
        pair_start = [i * (2 * G - 1 - i) // 2 for i in range(G - 1)]
        ia = sum((p >= float(s)).astype(F32) for s in pair_start[1:])
        ib = p - ia * (2 * G - 1 - ia) * 0.5 + ia + 1.0
        n_pad = meta_ref.shape[1]
        blk = lax.broadcasted_iota(jnp.int32, (NB, n_pad), 1).astype(F32)
        first = jnp.broadcast_to(bstart[:, 0:1], (NB, n_pad))
        last = jnp.broadcast_to((bstart + nblk_b)[:, 0:1], (NB, n_pad))
        member = (first <= blk) & (blk < last)
        ea = jnp.sum(jnp.where(member, grp * G + ia, 0.0), axis=0, keepdims=True)
        eb = jnp.sum(jnp.where(member, grp * G + ib, 0.0), axis=0, keepdims=True)
        meta_ref[...] = jnp.zeros_like(meta_ref)
        meta_ref[0:1, :] = ea.astype(jnp.int32)
        meta_ref[1:2, :] = eb.astype(jnp.int32)
        meta_ref[2:3, :] = jnp.max(last, axis=0, keepdims=True).astype(jnp.int32)
        in_blk = jnp.clip(counts - (blk - first) * R, 0.0, float(R))
        meta_ref[5:6, :] = jnp.sum(jnp.where(member, in_blk, 0.0), axis=0, keepdims=True).astype(jnp.int32)
        pad0 = jnp.broadcast_to(bstart[:, 0:1] * R + counts, (NB, NB)).T[0:1, :]
        padn = jnp.broadcast_to(nblk * R - counts, (NB, NB)).T[0:1, :]
        meta_ref[3:4, 0:NB] = pad0.astype(jnp.int32)
        meta_ref[4:5, 0:NB] = padn.astype(jnp.int32)

    @pl.when(ph == 1)
    def _():
        ri = lax.broadcasted_iota(jnp.int32, (LANES, 2 * LANES), 0)
        ci = lax.broadcasted_iota(jnp.int32, (LANES, 2 * LANES), 1)
        w = ((ci >= ri) | (ci >= LANES)).astype(BF16)
        base = base_scr[...]
        slot0 = bstart_scr[...] * R
        ds = []
        for j in range(TR // LANES):
            sl = slice(j * LANES, (j + 1) * LANES)
            blk = cnt[:, sl]
            res = _mm(blk.astype(BF16), w)
            val = slot0 + base + (res[:, 0:LANES] - blk)
            ds.append(jnp.sum(jnp.where(oh[:, sl], val, 0.0), axis=0, keepdims=True))
            base = base + res[:, LANES:2 * LANES]
        base_scr[...] = base
        ri_ref[...] = jnp.zeros_like(ri_ref)
        ri_ref[0:1, :] = jnp.concatenate(ds, axis=1).astype(jnp.int32)
        rf_ref[...] = jnp.zeros_like(rf_ref)
        rf_ref[0:1, :] = gate_a
        rf_ref[1:2, :] = gate_b


def _route_call(lgt, n_blocks):
    rows, N = lgt.shape
    TR = ROUTE_TILE
    E = NB
    n_pad = -(-n_blocks // LANES) * LANES
    return pl.pallas_call(
        _route_kernel,
        grid=(2, N // TR),
        in_specs=[pl.BlockSpec((rows, TR), lambda ph, t: (0, t))],
        out_specs=[
            pl.BlockSpec((SUBLANES, TR), lambda ph, t: (0, ph * t)),
            pl.BlockSpec((SUBLANES, TR), lambda ph, t: (0, ph * t)),
            pl.BlockSpec((SUBLANES, n_pad), lambda ph, t: (0, 0)),
        ],
        out_shape=[
            jax.ShapeDtypeStruct((SUBLANES, N), F32),
            jax.ShapeDtypeStruct((SUBLANES, N), jnp.int32),
            jax.ShapeDtypeStruct((SUBLANES, n_pad), jnp.int32),
        ],
        scratch_shapes=[pltpu.VMEM((E, LANES), F32), pltpu.VMEM((E, LANES), F32), pltpu.VMEM((E, LANES), F32)],
        compiler_params=pltpu.CompilerParams(dimension_semantics=("arbitrary", "arbitrary")),
        name="route",
    )(lgt)


def kernel(x, c, positions, w_ada, b_ada, w_in, w_out, hgrn_lb, hgrn_norm_w, ret_norm_w, post_ln1_w,
           post_ln1_b, w_rg, b_rg, w_re, b_re, w_gate, w_up, w_down, post_ln2_w, post_ln2_b):
    B, S, D = x.shape
    N = B * S
    hw = N_HEADS * D_HEAD

    mod3 = _ada_call(c, w_ada[0], b_ada[0]).reshape(B, 6, D)

    win = w_in[0].astype(BF16)
    wout = w_out[0].astype(BF16)
    pad = LANES - N_EXPERTS - N_GROUPS
    wr = jnp.concatenate([w_re[0], w_rg[0], jnp.zeros((D, pad), F32)], axis=1).astype(BF16)
    br = jnp.concatenate([b_re[0], b_rg[0], jnp.zeros((pad,), F32)]).reshape(1, LANES)

    x1, h2t, lgt, wg16, wu16, wd16 = _mixer_call(
        x, positions, mod3, win, wout, hgrn_lb, hgrn_norm_w[0].reshape(1, hw),
        ret_norm_w[0].reshape(1, hw), post_ln1_w[0].reshape(1, D), post_ln1_b[0].reshape(1, D), wr, br,
        w_gate[0], w_up[0], w_down[0])

    n_blocks = N // SLOT_BLOCK + N_GROUPS * PAIRS_PER_GROUP
    rf, ri, meta = _route_call(lgt, n_blocks)
    dest = ri[0]
    gates = rf[0:2].T
    xs = _dispatch_call(dest, meta[3, :NB], meta[4, :NB], meta[2, 0:1], h2t, n_blocks * SLOT_BLOCK)
    yt = _expert_call(meta[0, :n_blocks], meta[1, :n_blocks], meta[5, :n_blocks], meta[2, 0:1], xs,
                      wg16, wu16, wd16)
    out = _final_call(dest, yt, x1.reshape(N, D), gates, mod3, post_ln2_w[0].reshape(1, D),
                      post_ln2_b[0].reshape(1, D), S)
    return out.reshape(B, S, D)
```
